```python
import jax
import jax.numpy as jnp
from jax import lax
import numpy as np

D_MODEL = 1024
BATCH = 4
SEQ = 8192
DEPTH = 4

GRID_W = 64
CTX_LEN = 256
N_MIXERS = 3
NORM_EPS = 1e-6

ML_HEADS = 8
ML_HEAD_DIM = D_MODEL // ML_HEADS
ML_CHUNK = 64

RW_HEAD_DIM = 64
RW_HEADS = D_MODEL // RW_HEAD_DIM
RW_DECAY_LORA = 64
RW_ICLR_LORA = 64
RW_GATE_LORA = 128
RW_GN_EPS = 64e-5

HG_EXPAND = 128
HG_HEADS = D_MODEL // HG_EXPAND
HG_CHUNK = 32

N_EXPERTS = 16
N_GROUPS = 4
EXPERTS_PER_GROUP = N_EXPERTS // N_GROUPS
TOP_K = 2
D_EXPERT = 512
MOE_BLOCK = 256

kernel_name = 'hybrid_mlstm_rwkv7_hgrn2_moe_dit'


def rmsnorm(x, g):
    xf = x.astype(jnp.float32)
    y = xf * lax.rsqrt(jnp.mean(xf * xf, axis=-1, keepdims=True) + NORM_EPS)
    return (y * g.astype(jnp.float32)).astype(x.dtype)


def split_ctx(a, axis):
    return (lax.slice_in_dim(a, 0, CTX_LEN, axis=axis),
            lax.slice_in_dim(a, CTX_LEN, a.shape[axis], axis=axis))


def modulate(h, shift_c, scale_c, shift_x, scale_x):
    h_c, h_x = split_ctx(h, 1)
    return jnp.concatenate([h_c * (1 + scale_c) + shift_c, h_x * (1 + scale_x) + shift_x], axis=1)


def gated_residual(s, y, gate_c, gate_x):
    y_c, y_x = split_ctx(y, 1)
    return s + jnp.concatenate([y_c * gate_c, y_x * gate_x], axis=1)


def to_heads(a, n_heads):
    b, t, d = a.shape
    return a.reshape(b, t, n_heads, d // n_heads).transpose(0, 2, 1, 3)


def from_heads(a):
    b, h, t, dh = a.shape
    return a.transpose(0, 2, 1, 3).reshape(b, t, h * dh)


def head_rmsnorm(o, g):
    o = o * lax.rsqrt(jnp.mean(o * o, axis=-1, keepdims=True) + NORM_EPS)
    return from_heads(o) * g.astype(jnp.float32)


def to_chunks(a, size):
    b, h, t = a.shape[:3]
    return jnp.moveaxis(a.reshape(b, h, t // size, size, *a.shape[3:]), 2, 0)


def from_chunks(a):
    nc, b, h, size = a.shape[:4]
    return jnp.moveaxis(a, 0, 2).reshape(b, h, nc * size, *a.shape[4:])


def conv1d_centred(x, w):
    xp = jnp.pad(x, ((0, 0), (1, 1), (0, 0)))
    return xp[:, :-2] * w[0] + xp[:, 1:-1] * w[1] + xp[:, 2:] * w[2]


def conv_grid(x, w, rows):
    b, t, ch = x.shape
    y = lax.conv_general_dilated(x.reshape(b, rows, GRID_W, ch), w[:, :, None, :].astype(x.dtype),
                                 (1, 1), 'SAME', dimension_numbers=('NHWC', 'HWIO', 'NHWC'),
                                 feature_group_count=ch)
    return y.reshape(b, t, ch)


def centred_shift(a):
    ap = jnp.pad(a, ((0, 0), (1, 1), (0, 0)))
    return 0.5 * (ap[:, :-2] + ap[:, 2:]) - a


def bidirectional(scan_fn, fwd_in, bwd_in, init):
    def run(inputs, reverse):
        parts = [split_ctx(a, 2) for a in inputs]
        ctx_in = tuple(p[0] for p in parts)
        lat_in = tuple(p[1] for p in parts)
        if reverse:
            ctx_in = tuple(jnp.flip(a, 2) for a in ctx_in)
            lat_in = tuple(jnp.flip(a, 2) for a in lat_in)
        y_ctx, state = scan_fn(ctx_in, init)
        y_lat, _ = scan_fn(lat_in, state)
        if reverse:
            y_ctx, y_lat = jnp.flip(y_ctx, 2), jnp.flip(y_lat, 2)
        return jnp.concatenate([y_ctx, y_lat], axis=2)
    return run(fwd_in, False) + run(bwd_in, True)


def mlstm_chunk_scan(inputs, state):
    causal = jnp.tril(jnp.ones((ML_CHUNK, ML_CHUNK), bool))

    def step(carry, blk):
        C, n, m = carry
        qc, kc, vc, ic, fc = blk
        b = jnp.cumsum(fc, axis=-1)
        d = jnp.where(causal, b[..., :, None] - b[..., None, :] + ic[..., None, :], -jnp.inf)
        inter = b + m[..., None]
        m_t = jnp.maximum(inter, jnp.max(d, axis=-1))
        s = jnp.einsum('bhtd,bhsd->bhts', qc, kc) * jnp.exp(d - m_t[..., None])
        w_inter = jnp.exp(inter - m_t)
        num = (jnp.einsum('bhts,bhsv->bhtv', s, vc)
               + w_inter[..., None] * jnp.einsum('bhvd,bhtd->bhtv', C, qc))
        den = jnp.sum(s, axis=-1) + w_inter * jnp.einsum('bhd,bhtd->bht', n, qc)
        h = num / jnp.maximum(jnp.abs(den), jnp.exp(-m_t))[..., None]
        b_last = b[..., -1]
        d_last = b_last[..., None] - b + ic
        m_new = jnp.maximum(b_last + m, jnp.max(d_last, axis=-1))
        w_k = jnp.exp(d_last - m_new[..., None])
        w_prev = jnp.exp(b_last + m - m_new)
        C = w_prev[..., None, None] * C + jnp.einsum('bhs,bhsv,bhsd->bhvd', w_k, vc, kc)
        n = w_prev[..., None] * n + jnp.einsum('bhs,bhsd->bhd', w_k, kc)
        return (C, n, m_new), h

    state, h = lax.scan(step, state, tuple(to_chunks(a, ML_CHUNK) for a in inputs))
    return from_chunks(h), state


def mlstm_mixer(u, rows, w_in, w_gate, b_gate, conv, head_g, w_out):
    f32 = jnp.float32
    z = u @ w_in
    qk = z[..., :2 * D_MODEL]
    v = z[..., 2 * D_MODEL:3 * D_MODEL]
    o_gate = jax.nn.sigmoid(z[..., 3 * D_MODEL:].astype(f32))
    qk_c, qk_x = split_ctx(qk, 1)
    qk = jax.nn.silu(jnp.concatenate([conv1d_centred(qk_c, conv[1]), conv_grid(qk_x, conv, rows)], axis=1))
    q = to_heads(qk[..., :D_MODEL], ML_HEADS).astype(f32)
    k = to_heads(qk[..., D_MODEL:], ML_HEADS).astype(f32) * ML_HEAD_DIM ** -0.5
    vh = to_heads(v, ML_HEADS).astype(f32)
    gates = (jnp.einsum('btd,rdg->rbgt', u, w_gate) + b_gate[:, None, :, None]).astype(f32)
    ig = gates[:, :, :ML_HEADS]
    lf = jax.nn.log_sigmoid(gates[:, :, ML_HEADS:])
    bsz = u.shape[0]
    init = (jnp.zeros((bsz, ML_HEADS, ML_HEAD_DIM, ML_HEAD_DIM), f32),
            jnp.zeros((bsz, ML_HEADS, ML_HEAD_DIM), f32),
            jnp.zeros((bsz, ML_HEADS), f32))
    h = bidirectional(mlstm_chunk_scan, (q, k, vh, ig[0], lf[0]), (q, k, vh, ig[1], lf[1]), init)
    y = head_rmsnorm(h, head_g) * o_gate
    return y.astype(u.dtype) @ w_out


def rwkv7_step_scan(inputs, state):
    seq = tuple(jnp.moveaxis(a, 2, 0) for a in inputs)

    def step(S, t_in):
        w_t, r_t, k_t, v_t, kk_t, akk_t = t_in
        S = (S * w_t[:, :, None, :]
             + jnp.einsum('bhvk,bhk->bhv', S, -kk_t)[..., None] * akk_t[:, :, None, :]
             + v_t[..., None] * k_t[:, :, None, :])
        return S, jnp.einsum('bhvk,bhk->bhv', S, r_t)

    state, out = lax.scan(step, state, seq, unroll=8)
    return jnp.moveaxis(out, 0, 2), state


def rwkv7_mixer(u, mu, w_rkv, w0, w1, w2, a0, a1, a2, g1, g2, k_k, k_a, r_k, ln_w, ln_b, w_out):
    f32 = jnp.float32
    u_c, u_x = split_ctx(u, 1)
    du = jnp.concatenate([centred_shift(u_c), centred_shift(u_x)], axis=1)

    def mix(idx):
        return u + du * mu[idx]

    r = (mix(0) @ w_rkv[0]).astype(f32)
    k = (mix(1) @ w_rkv[1]).astype(f32)
    v = (mix(2) @ w_rkv[2]).astype(f32)
    w_pre = w0[:, None, None, :] + jnp.einsum(
        'rbtl,rle->rbte', jnp.tanh(jnp.einsum('btd,rdl->rbtl', mix(3), w1)), w2)
    decay = jnp.exp(-jnp.exp(-jax.nn.softplus(-w_pre.astype(f32)) - 0.5))
    a = jax.nn.sigmoid((a0[:, None, None, :] + jnp.einsum(
        'rbtl,rle->rbte', jnp.einsum('btd,rdl->rbtl', mix(4), a1), a2)).astype(f32))
    g = (jax.nn.sigmoid(mix(5) @ g1) @ g2).astype(f32)
    kk = to_heads(k * k_k, RW_HEADS)
    kk = kk * lax.rsqrt(jnp.maximum(jnp.sum(kk * kk, axis=-1, keepdims=True), 1e-24))
    k_dir = k[None] * (1.0 + (a - 1.0) * k_a)
    rh = to_heads(r, RW_HEADS)
    vh = to_heads(v, RW_HEADS)
    kf = to_heads(k_dir[0], RW_HEADS)
    kb = to_heads(k_dir[1], RW_HEADS)
    fwd = (to_heads(decay[0], RW_HEADS), rh, kf, vh, kk, kk * to_heads(a[0], RW_HEADS))
    bwd = (to_heads(decay[1], RW_HEADS), rh, kb, vh, kk, kk * to_heads(a[1], RW_HEADS))
    bsz = u.shape[0]
    init = jnp.zeros((bsz, RW_HEADS, RW_HEAD_DIM, RW_HEAD_DIM), f32)
    o = bidirectional(rwkv7_step_scan, fwd, bwd, init)
    mean = jnp.mean(o, axis=-1, keepdims=True)
    var = jnp.mean(jnp.square(o - mean), axis=-1, keepdims=True)
    o = from_heads((o - mean) * lax.rsqrt(var + RW_GN_EPS)) * ln_w + ln_b
    r_k_h = r_k.reshape(RW_HEADS, 1, RW_HEAD_DIM)
    bonus = from_heads(jnp.sum(rh * (kf + kb) * r_k_h, axis=-1, keepdims=True) * vh)
    y = (o + bonus) * g
    return y.astype(u.dtype) @ w_out


def hgrn2_chunk_scan(inputs, state):
    causal = jnp.tril(jnp.ones((HG_CHUNK, HG_CHUNK), bool))[:, :, None]

    def step(S, blk):
        qc, kc, vc, gc = blk
        b = jnp.cumsum(gc, axis=2)
        decay = jnp.exp(jnp.where(causal, b[:, :, :, None, :] - b[:, :, None, :, :], -jnp.inf))
        att = jnp.einsum('bhtk,bhtsk,bhsk->bhts', qc, decay, kc)
        o = (jnp.einsum('bhts,bhsv->bhtv', att, vc)
             + jnp.einsum('bhtk,bhkv->bhtv', qc * jnp.exp(b), S))
        b_last = b[:, :, -1:]
        S = (jnp.exp(b_last[:, :, 0])[..., None] * S
             + jnp.einsum('bhsk,bhsv->bhkv', kc * jnp.exp(b_last - b), vc))
        return S, o

    state, o = lax.scan(step, state, tuple(to_chunks(a, HG_CHUNK) for a in inputs))
    return from_chunks(o), state


def hgrn2_mixer(u, layer_idx, w_in, w_f, b_f, lb_logits, head_g, w_out):
    f32 = jnp.float32
    z = u @ w_in
    q = to_heads(jax.nn.silu(z[..., :D_MODEL]), HG_HEADS).astype(f32)
    v = to_heads(z[..., D_MODEL:2 * D_MODEL], HG_HEADS).astype(f32)
    out_gate = jax.nn.silu(z[..., 2 * D_MODEL:].astype(f32))
    p = jax.nn.softmax(lb_logits.astype(f32), axis=0)
    lb = jnp.cumsum(p, axis=0)[layer_idx] - p[0]
    f_pre = (jnp.einsum('btd,rde->rbte', u, w_f) + b_f[:, None, None, :]).astype(f32)
    log_f = jnp.logaddexp(jnp.log(lb), jnp.log1p(-lb) + jax.nn.log_sigmoid(f_pre))
    key = -jnp.expm1(log_f)
    fwd = (q, to_heads(key[0], HG_HEADS), v, to_heads(log_f[0], HG_HEADS))
    bwd = (q, to_heads(key[1], HG_HEADS), v, to_heads(log_f[1], HG_HEADS))
    bsz = u.shape[0]
    init = jnp.zeros((bsz, HG_HEADS, HG_EXPAND, D_MODEL // HG_HEADS), f32)
    o = bidirectional(hgrn2_chunk_scan, fwd, bwd, init)
    y = head_rmsnorm(o, head_g) * out_gate
    return y.astype(u.dtype) @ w_out


def expert_dispatch(h, expert_idx, gate, w1, w3, w2):
    n_tok, d = h.shape
    n_assign = n_tok * TOP_K
    flat_e = expert_idx.reshape(n_assign)
    order = jnp.argsort(flat_e)
    sorted_e = flat_e[order]
    counts = jnp.zeros((N_EXPERTS,), jnp.int32).at[flat_e].add(1)
    padded = (counts + MOE_BLOCK - 1) // MOE_BLOCK * MOE_BLOCK
    start_sorted = jnp.cumsum(counts) - counts
    end_pad = jnp.cumsum(padded)
    start_pad = end_pad - padded
    dest = start_pad[sorted_e] + jnp.arange(n_assign, dtype=jnp.int32) - start_sorted[sorted_e]
    n_blocks = -(-n_assign // MOE_BLOCK) + N_EXPERTS
    token_of_slot = jnp.full((n_blocks * MOE_BLOCK,), n_tok, jnp.int32).at[dest].set(
        (order // TOP_K).astype(jnp.int32))
    block_expert = jnp.minimum(jnp.searchsorted(
        end_pad, jnp.arange(n_blocks, dtype=jnp.int32) * MOE_BLOCK, side='right'), N_EXPERTS - 1)
    h_pad = jnp.concatenate([h, jnp.zeros((1, d), h.dtype)], axis=0)
    xb = h_pad[token_of_slot].reshape(n_blocks, MOE_BLOCK, d)

    def block_ffn(args):
        xe, e = args
        return (jax.nn.silu(xe @ w1[e]) * (xe @ w3[e])) @ w2[e]

    yb = lax.map(block_ffn, (xb, block_expert)).reshape(n_blocks * MOE_BLOCK, d)
    contrib = yb[dest].astype(jnp.float32) * gate.reshape(n_assign)[order][:, None]
    out = jnp.zeros((n_tok, d), jnp.float32).at[order // TOP_K].add(contrib)
    return out.astype(h.dtype)


def grouped_moe(h, router_w, router_b, w1, w3, w2):
    n_tok = h.shape[0]
    affinity = jax.nn.sigmoid((h @ router_w).astype(jnp.float32))
    sel = (affinity + router_b.astype(jnp.float32)).reshape(n_tok, N_GROUPS, EXPERTS_PER_GROUP)
    group_score = jnp.sum(lax.top_k(sel, 2)[0], axis=-1)
    best_group = jnp.argmax(group_score, axis=-1).astype(jnp.int32)
    in_group = jnp.take_along_axis(sel, best_group[:, None, None], axis=1)[:, 0]
    local = lax.top_k(in_group, TOP_K)[1]
    expert_idx = best_group[:, None] * EXPERTS_PER_GROUP + local
    chosen = jnp.take_along_axis(affinity, expert_idx, axis=1)
    gate = chosen / jnp.sum(chosen, axis=-1, keepdims=True)
    return expert_dispatch(h, expert_idx, gate, w1, w3, w2)


def setup_inputs(seed: int = 0) -> dict:
    key = jax.random.key(seed)
    keys = iter(jax.random.split(key, 64))

    def nrm(shape, scale):
        return scale * jax.random.normal(next(keys), shape, jnp.float32)

    D, E, F = D_MODEL, N_EXPERTS, D_EXPERT
    n_a = len(range(0, DEPTH, N_MIXERS))
    n_b = len(range(1, DEPTH, N_MIXERS))
    n_c = len(range(2, DEPTH, N_MIXERS))
    inv_d = D ** -0.5
    return {
        'x': nrm((BATCH, SEQ, D), 1.0),
        'c': nrm((BATCH, D), 1.0),
        'ctx': nrm((BATCH, CTX_LEN, D), 1.0),
        'c_ctx': nrm((D,), 1.0),
        'ada_w': nrm((DEPTH, D, 6 * D), 0.5 * inv_d),
        'ada_b': nrm((DEPTH, 6 * D), 0.02),
        'norm_mix': 1.0 + nrm((DEPTH, D), 0.05),
        'norm_ffn': 1.0 + nrm((DEPTH, D), 0.05),
        'norm_out': 1.0 + nrm((D,), 0.05),
        'ml_w_in': nrm((n_a, D, 4 * D), inv_d),
        'ml_w_gate': nrm((n_a, 2, D, 2 * ML_HEADS), 0.1 * inv_d),
        'ml_b_gate': jnp.concatenate([nrm((n_a, 2, ML_HEADS), 0.1),
                                      jnp.linspace(3.0, 6.0, ML_HEADS) + nrm((n_a, 2, ML_HEADS), 0.1)], axis=-1),
        'ml_conv': nrm((n_a, 3, 3, 2 * D), 1.0 / 3.0),
        'ml_head_g': 1.0 + nrm((n_a, D), 0.05),
        'ml_w_out': nrm((n_a, D, D), inv_d),
        'rw_mu': jax.random.uniform(next(keys), (n_b, 6, D), jnp.float32),
        'rw_w_rkv': nrm((n_b, 3, D, D), inv_d),
        'rw_w0': jnp.linspace(-6.0, 1.0, D) + nrm((n_b, 2, D), 0.3),
        'rw_w1': nrm((n_b, 2, D, RW_DECAY_LORA), inv_d),
        'rw_w2': nrm((n_b, 2, RW_DECAY_LORA, D), 0.1 * RW_DECAY_LORA ** -0.5),
        'rw_a0': nrm((n_b, 2, D), 0.1),
        'rw_a1': nrm((n_b, 2, D, RW_ICLR_LORA), inv_d),
        'rw_a2': nrm((n_b, 2, RW_ICLR_LORA, D), 0.1 * RW_ICLR_LORA ** -0.5),
        'rw_g1': nrm((n_b, D, RW_GATE_LORA), inv_d),
        'rw_g2': nrm((n_b, RW_GATE_LORA, D), RW_GATE_LORA ** -0.5),
        'rw_k_k': 0.85 + nrm((n_b, D), 0.05),
        'rw_k_a': 1.0 + nrm((n_b, D), 0.05),
        'rw_r_k': nrm((n_b, D), 0.1),
        'rw_ln_w': 1.0 + nrm((n_b, D), 0.05),
        'rw_ln_b': nrm((n_b, D), 0.02),
        'rw_w_out': nrm((n_b, D, D), inv_d),
        'hg_w_in': nrm((n_c, D, 3 * D), inv_d),
        'hg_w_f': nrm((n_c, 2, D, D), inv_d),
        'hg_b_f': nrm((n_c, 2, D), 0.1),
        'hg_lb_logits': nrm((DEPTH, D), 0.5),
        'hg_head_g': 1.0 + nrm((n_c, D), 0.05),
        'hg_w_out': nrm((n_c, D, D), inv_d),
        'router_w': nrm((D, E), inv_d),
        'router_b': nrm((E,), 0.01),
        'ex_w1': nrm((DEPTH, E, D, F), inv_d),
        'ex_w3': nrm((DEPTH, E, D, F), inv_d),
        'ex_w2': nrm((DEPTH, E, F, D), F ** -0.5),
    }


def reference(x, c, ctx, c_ctx, ada_w, ada_b, norm_mix, norm_ffn, norm_out,
              ml_w_in, ml_w_gate, ml_b_gate, ml_conv, ml_head_g, ml_w_out,
              rw_mu, rw_w_rkv, rw_w0, rw_w1, rw_w2, rw_a0, rw_a1, rw_a2, rw_g1, rw_g2,
              rw_k_k, rw_k_a, rw_r_k, rw_ln_w, rw_ln_b, rw_w_out,
              hg_w_in, hg_w_f, hg_b_f, hg_lb_logits, hg_head_g, hg_w_out,
              router_w, router_b, ex_w1, ex_w3, ex_w2):
    rows = x.shape[1] // GRID_W
    cond_x = jax.nn.silu(c)
    cond_c = jax.nn.silu(c_ctx)
    s = jnp.concatenate([ctx, x], axis=1)
    for i in range(DEPTH):
        mod_x = jnp.split((cond_x @ ada_w[i] + ada_b[i])[:, None, :], 6, axis=-1)
        mod_c = jnp.split(cond_c @ ada_w[i] + ada_b[i], 6, axis=-1)
        h = modulate(rmsnorm(s, norm_mix[i]), mod_c[0], mod_c[1], mod_x[0], mod_x[1])
        kind, j = i % N_MIXERS, i // N_MIXERS
        if kind == 0:
            y = mlstm_mixer(h, rows, ml_w_in[j], ml_w_gate[j], ml_b_gate[j], ml_conv[j],
                            ml_head_g[j], ml_w_out[j])
        elif kind == 1:
            y = rwkv7_mixer(h, rw_mu[j], rw_w_rkv[j], rw_w0[j], rw_w1[j], rw_w2[j], rw_a0[j],
                            rw_a1[j], rw_a2[j], rw_g1[j], rw_g2[j], rw_k_k[j], rw_k_a[j],
                            rw_r_k[j], rw_ln_w[j], rw_ln_b[j], rw_w_out[j])
        else:
            y = hgrn2_mixer(h, i, hg_w_in[j], hg_w_f[j], hg_b_f[j], hg_lb_logits,
                            hg_head_g[j], hg_w_out[j])
        s = gated_residual(s, y, mod_c[2], mod_x[2])
        h = modulate(rmsnorm(s, norm_ffn[i]), mod_c[3], mod_c[4], mod_x[3], mod_x[4])
        y = grouped_moe(h.reshape(-1, h.shape[-1]), router_w, router_b,
                        ex_w1[i], ex_w3[i], ex_w2[i]).reshape(h.shape)
        s = gated_residual(s, y, mod_c[5], mod_x[5])
    return rmsnorm(s[:, CTX_LEN:], norm_out)
```

```python
import functools

import jax
import jax.numpy as jnp
from jax import lax
from jax.experimental import pallas as pl
from jax.experimental.pallas import tpu as pltpu

F32 = jnp.float32
BF16 = jnp.bfloat16

GRID_W = 64
CTX_LEN = 256
N_MIXERS = 3
NORM_EPS = 1e-6
ML_HEADS = 8
RW_HEAD_DIM = 64
RW_GN_EPS = 64e-5
HG_EXPAND = 128
N_GROUPS = 4
TOP_K = 2
MOE_BLOCK = 256

LANES = 128
ML_CHUNK = 128
RW_CHUNK = 64
HG_CHUNK = 64
HG_SUB = 16
HG_EXP_CLAMP = 80.0
VMEM_LIMIT = 48 * 1024 * 1024

NT = (((1,), (1,)), ((), ()))
NN = (((1,), (0,)), ((), ()))


def _dot(a, b, dims=NN, passes=1):
    a_hi = a.astype(BF16)
    b_hi = b.astype(BF16)
    out = lax.dot_general(a_hi, b_hi, dims, preferred_element_type=F32)
    if passes == 3:
        a_lo = (a - a_hi.astype(F32)).astype(BF16)
        b_lo = (b - b_hi.astype(F32)).astype(BF16)
        out = out + lax.dot_general(a_hi, b_lo, dims, preferred_element_type=F32)
        out = out + lax.dot_general(a_lo, b_hi, dims, preferred_element_type=F32)
    return out


def _cumsum_rows(x, reverse):
    n = x.shape[0]
    row = lax.broadcasted_iota(jnp.int32, x.shape, 0)
    s = 1
    while s < n:
        if reverse:
            x = x + jnp.where(row < n - s, pltpu.roll(x, n - s, axis=0), 0.0)
        else:
            x = x + jnp.where(row >= s, pltpu.roll(x, s, axis=0), 0.0)
        s *= 2
    return x


def _pick_tile(n, candidates):
    for c in candidates:
        if n % c == 0:
            return c
    raise ValueError(f"no tile for {n}")


def _scan_chunk_index(d, p, nctx, nc):
    rev = jnp.where(p < nctx, nctx - 1 - p, nc - 1 - (p - nctx))
    return jnp.where(d == 0, p, rev)


def _mm_kernel(x_ref, w_ref, o_ref):
    o_ref[...] = jnp.dot(x_ref[...].astype(BF16), w_ref[...],
                         preferred_element_type=F32).astype(o_ref.dtype)


def _mm_precise_kernel(x_ref, w_ref, o_ref):
    o_ref[...] = _dot(x_ref[...], w_ref[...], passes=3).astype(o_ref.dtype)


def mm(x, w, out_dtype=F32, precise=False):
    n, k = x.shape
    m = w.shape[1]
    tm = _pick_tile(n, (512, 256, 128, 64, 32, 16, 8))
    tn = m if m <= 1024 else _pick_tile(m, (1024, 512, 256, 128))
    if not precise:
        w = w.astype(BF16)
    return pl.pallas_call(
        _mm_precise_kernel if precise else _mm_kernel,
        grid=(n // tm, m // tn),
        in_specs=[pl.BlockSpec((tm, k), lambda i, j: (i, 0)),
                  pl.BlockSpec((k, tn), lambda i, j: (0, j))],
        out_specs=pl.BlockSpec((tm, tn), lambda i, j: (i, j)),
        out_shape=jax.ShapeDtypeStruct((n, m), out_dtype),
        compiler_params=pltpu.CompilerParams(vmem_limit_bytes=VMEM_LIMIT),
        name="mm",
    )(x, w)


def _ml_scan_kernel(q_ref, k_ref, v_ref, gc_ref, gr_ref, o_ref, ct_scr, n_scr, m_scr, *, heads):
    L = q_ref.shape[0]
    d = pl.program_id(0)

    @pl.when(pl.program_id(2) == 0)
    def _():
        ct_scr[...] = jnp.zeros_like(ct_scr)
        n_scr[...] = jnp.zeros_like(n_scr)
        m_scr[...] = jnp.zeros_like(m_scr)

    row = lax.broadcasted_iota(jnp.int32, (L, L), 0)
    col = lax.broadcasted_iota(jnp.int32, (L, L), 1)

    def body(reverse):
        incl = (col >= row) if reverse else (col <= row)
        incl_t = (row >= col) if reverse else (row <= col)
        last = 0 if reverse else L - 1
        for h in range(heads):
            sl = slice(h * LANES, (h + 1) * LANES)
            q = q_ref[:, sl]
            k = k_ref[:, sl]
            v = v_ref[:, sl]
            ig_col = gc_ref[0, :, h:h + 1]
            lf_col = gc_ref[0, :, heads + h:heads + h + 1]
            ig_row = gr_ref[0, 0, h:h + 1, :]
            lf_row = gr_ref[0, 0, heads + h:heads + h + 1, :]
            b_col = jnp.sum(jnp.where(incl, lf_row, 0.0), axis=1, keepdims=True)
            b_row = jnp.sum(jnp.where(incl_t, lf_col, 0.0), axis=0, keepdims=True)
            m_prev = m_scr[h:h + 1, 0:1]
            dmat = jnp.where(incl, b_col - b_row + ig_row, -jnp.inf)
            inter = b_col + m_prev
            m_t = jnp.maximum(inter, jnp.max(dmat, axis=1, keepdims=True))
            s = _dot(q, k, NT) * jnp.exp(dmat - m_t)
            w_inter = jnp.exp(inter - m_t)
            n_row = n_scr[h:h + 1, :]
            num = _dot(s, v) + w_inter * _dot(q, ct_scr[h])
            den = jnp.sum(s, axis=1, keepdims=True) + w_inter * jnp.sum(q * n_row, axis=1, keepdims=True)
            o_ref[0, :, sl] = num / jnp.maximum(jnp.abs(den), jnp.exp(-m_t))
            b_last = b_col[last:last + 1, :]
            m_new = jnp.maximum(b_last + m_prev,
                                jnp.max(b_last - b_row + ig_row, axis=1, keepdims=True))
            w_k = jnp.exp(b_last - b_col + ig_col - m_new)
            w_prev = jnp.exp(b_last + m_prev - m_new)
            ct_scr[h] = w_prev * ct_scr[h] + _dot(k.T, w_k * v)
            n_scr[h:h + 1, :] = w_prev * n_row + jnp.sum(w_k * k, axis=0, keepdims=True)
            m_scr[h:h + 1, :] = jnp.broadcast_to(m_new, (1, LANES))

    @pl.when(d == 0)
    def _():
        body(False)

    @pl.when(d == 1)
    def _():
        body(True)


def mlstm_scan(q, k, v, gc, gr, bsz, t):
    n, dm = q.shape
    heads = dm // LANES
    L = ML_CHUNK
    nc, nctx = t // L, CTX_LEN // L

    def rows(d, b, p):
        return (b * nc + _scan_chunk_index(d, p, nctx, nc), 0)

    return pl.pallas_call(
        functools.partial(_ml_scan_kernel, heads=heads),
        grid=(2, bsz, nc),
        in_specs=[pl.BlockSpec((L, dm), rows), pl.BlockSpec((L, dm), rows), pl.BlockSpec((L, dm), rows),
                  pl.BlockSpec((1, L, 2 * heads), lambda d, b, p: (d,) + rows(d, b, p)),
                  pl.BlockSpec((1, 1, 2 * heads, L),
                               lambda d, b, p: (d, b, 0, _scan_chunk_index(d, p, nctx, nc)))],
        out_specs=pl.BlockSpec((1, L, dm), lambda d, b, p: (d,) + rows(d, b, p)),
        out_shape=jax.ShapeDtypeStruct((2, n, dm), F32),
        scratch_shapes=[pltpu.VMEM((heads, LANES, LANES), F32), pltpu.VMEM((heads, LANES), F32),
                        pltpu.VMEM((heads, LANES), F32)],
        compiler_params=pltpu.CompilerParams(
            dimension_semantics=("arbitrary", "arbitrary", "arbitrary"), vmem_limit_bytes=VMEM_LIMIT),
        name="mlstm_scan",
    )(q, k, v, gc, gr)


RW_INV_PASSES = 1
RW_STATE_PASSES = 3


def _rw_pre_kernel(lw_ref, kd_ref, a_ref, r_ref, v_ref, kk_ref, rdp_ref, o0_ref, m_ref, ha_ref):
    L = r_ref.shape[0]
    half = LANES // 2
    r = r_ref[...]
    v = v_ref[...]
    kk = kk_ref[...]
    row = lax.broadcasted_iota(jnp.int32, (L, L), 0)
    col = lax.broadcasted_iota(jnp.int32, (L, L), 1)
    eye = (row == col).astype(F32)
    lane = lax.broadcasted_iota(jnp.int32, (1, LANES), 1)
    head_masks = ((lane < half).astype(F32), (lane >= half).astype(F32))
    r2 = lax.broadcasted_iota(jnp.int32, (LANES, LANES), 0)
    c2 = lax.broadcasted_iota(jnp.int32, (LANES, LANES), 1)
    same_head = (r2 // half) == (c2 // half)
    for d in range(2):
        reverse = d == 1
        incl = (col >= row) if reverse else (col <= row)
        strict = (col > row) if reverse else (col < row)
        lw = lw_ref[d]
        k = kd_ref[d]
        akk = kk * a_ref[d]
        g = _cumsum_rows(lw, reverse)
        eg = jnp.exp(g)
        ieg = jnp.exp(-g)
        kd = kk * jnp.exp(g - lw)
        rd = r * eg
        ai = akk * ieg
        ki = k * ieg
        g_last = g[0:1] if reverse else g[L - 1:L]
        dl = jnp.exp(g_last - g)
        ad = akk * dl
        kdd = k * dl
        u0 = jnp.zeros((L, LANES), F32)
        kdp = jnp.zeros((L, LANES), F32)
        o0 = jnp.zeros((L, LANES), F32)
        rdp = jnp.zeros((L, LANES), F32)
        for hm in head_masks:
            kdm = kd * hm
            rdm = rd * hm
            vm = v * hm
            x = jnp.concatenate([kdm, rdm], axis=0)
            sa = _dot(x, ai, NT)
            sk = _dot(x, ki, NT)
            a_ab = jnp.where(strict, sa[:L], 0.0)
            b_ra = jnp.where(incl, sa[L:], 0.0)
            a_ak = jnp.where(strict, sk[:L], 0.0)
            b_rk = jnp.where(incl, sk[L:], 0.0)
            tinv = eye - a_ab
            pw = a_ab
            span = 2
            while span < L:
                pw = _dot(pw, pw, passes=RW_INV_PASSES)
                tinv = _dot(tinv, eye + pw, passes=RW_INV_PASSES)
                span *= 2
            w = -_dot(tinv, a_ak, passes=RW_INV_PASSES)
            u0j = _dot(w, vm)
            kdpj = _dot(tinv, kdm, passes=RW_INV_PASSES)
            o0 = o0 + _dot(b_ra, u0j) + _dot(b_rk, vm)
            rdp = rdp + rdm - _dot(b_ra, kdpj)
            u0 = u0 + u0j
            kdp = kdp + kdpj
        diag = jnp.where(r2 == c2, jnp.exp(g_last), 0.0)
        m_ref[d, 0, 0] = jnp.where(same_head, diag - _dot(ad.T, kdp), 0.0)
        ha_ref[d, 0, 0] = jnp.where(same_head, _dot(ad.T, u0) + _dot(kdd.T, v), 0.0)
        rdp_ref[d] = rdp
        o0_ref[d] = o0


def _rw_scan_kernel(rdp_ref, o0_ref, m_ref, ha_ref, o_ref, h_scr, *, pairs):
    @pl.when(pl.program_id(2) == 0)
    def _():
        h_scr[...] = jnp.zeros_like(h_scr)

    for p in range(pairs):
        sl = slice(p * LANES, (p + 1) * LANES)
        h = h_scr[p]
        o_ref[0, :, sl] = _dot(rdp_ref[0, :, sl], h, passes=RW_STATE_PASSES) + o0_ref[0, :, sl]
        h_scr[p] = _dot(m_ref[0, 0, p], h, passes=RW_STATE_PASSES) + ha_ref[0, 0, p]


def rwkv_scan(lw, kdir, a, r, v, kk, bsz, t):
    n, dm = r.shape
    pairs = dm // LANES
    L = RW_CHUNK
    nc, nctx = t // L, CTX_LEN // L
    nct = n // L
    dspec = pl.BlockSpec((2, L, LANES), lambda i, p: (0, i, p))
    sspec = pl.BlockSpec((L, LANES), lambda i, p: (i, p))
    mspec = pl.BlockSpec((2, 1, 1, LANES, LANES), lambda i, p: (0, i, p, 0, 0))
    rdp, o0, mm_, ha = pl.pallas_call(
        _rw_pre_kernel,
        grid=(nct, pairs),
        in_specs=[dspec, dspec, dspec, sspec, sspec, sspec],
        out_specs=[dspec, dspec, mspec, mspec],
        out_shape=[jax.ShapeDtypeStruct((2, n, dm), F32), jax.ShapeDtypeStruct((2, n, dm), F32),
                   jax.ShapeDtypeStruct((2, nct, pairs, LANES, LANES), F32),
                   jax.ShapeDtypeStruct((2, nct, pairs, LANES, LANES), F32)],
        compiler_params=pltpu.CompilerParams(
            dimension_semantics=("arbitrary", "arbitrary"), vmem_limit_bytes=VMEM_LIMIT),
        name="rwkv_pre",
    )(lw, kdir, a, r, v, kk)

    def chunk(d, b, p):
        return b * nc + _scan_chunk_index(d, p, nctx, nc)

    rspec = pl.BlockSpec((1, L, dm), lambda d, b, p: (d, chunk(d, b, p), 0))
    cspec = pl.BlockSpec((1, 1, pairs, LANES, LANES), lambda d, b, p: (d, chunk(d, b, p), 0, 0, 0))
    return pl.pallas_call(
        functools.partial(_rw_scan_kernel, pairs=pairs),
        grid=(2, bsz, nc),
        in_specs=[rspec, rspec, cspec, cspec],
        out_specs=rspec,
        out_shape=jax.ShapeDtypeStruct((2, n, dm), F32),
        scratch_shapes=[pltpu.VMEM((pairs, LANES, LANES), F32)],
        compiler_params=pltpu.CompilerParams(
            dimension_semantics=("arbitrary", "arbitrary", "arbitrary"), vmem_limit_bytes=VMEM_LIMIT),
        name="rwkv_scan",
    )(rdp, o0, mm_, ha)


def _hg_scan_kernel(q_ref, v_ref, lf_ref, o_ref, st_scr, *, heads):
    C = q_ref.shape[0]
    d = pl.program_id(0)

    @pl.when(pl.program_id(2) == 0)
    def _():
        st_scr[...] = jnp.zeros_like(st_scr)

    row = lax.broadcasted_iota(jnp.int32, (HG_SUB, C), 0)
    col = lax.broadcasted_iota(jnp.int32, (HG_SUB, C), 1)

    def body(reverse):
        last = 0 if reverse else C - 1
        for h in range(heads):
            sl = slice(h * LANES, (h + 1) * LANES)
            q = q_ref[:, sl]
            v = v_ref[:, sl]
            g = lf_ref[0, :, sl]
            k = -jnp.tanh(0.5 * g) * (jnp.exp(g) + 1.0)
            b = _cumsum_rows(g, reverse)
            bx = b - g
            st = st_scr[h]
            o_inter = _dot(q * jnp.exp(b), st, NT)
            parts = []
            for i in range(C // HG_SUB):
                r0 = i * HG_SUB
                first = r0 + HG_SUB - 1 if reverse else r0
                rho = bx[first:first + 1, :]
                qi = q[r0:r0 + HG_SUB] * jnp.exp(b[r0:r0 + HG_SUB] - rho)
                ki = k * jnp.exp(jnp.minimum(rho - b, HG_EXP_CLAMP))
                keep = (col >= row + r0) if reverse else (col <= row + r0)
                att = jnp.where(keep, _dot(qi, ki, NT), 0.0)
                parts.append(_dot(att, v))
            o_ref[0, :, sl] = o_inter + jnp.concatenate(parts, axis=0)
            b_last = b[last:last + 1, :]
            st_scr[h] = st * jnp.exp(b_last) + _dot(v.T, k * jnp.exp(b_last - b))

    @pl.when(d == 0)
    def _():
        body(False)

    @pl.when(d == 1)
    def _():
        body(True)


def hgrn_scan(q, v, logf, bsz, t):
    n, dm = q.shape
    heads = dm // LANES
    C = HG_CHUNK
    nc, nctx = t // C, CTX_LEN // C

    def rows(d, b, p):
        return (b * nc + _scan_chunk_index(d, p, nctx, nc), 0)

    dspec = pl.BlockSpec((1, C, dm), lambda d, b, p: (d,) + rows(d, b, p))
    return pl.pallas_call(
        functools.partial(_hg_scan_kernel, heads=heads),
        grid=(2, bsz, nc),
        in_specs=[pl.BlockSpec((C, dm), rows), pl.BlockSpec((C, dm), rows), dspec],
        out_specs=dspec,
        out_shape=jax.ShapeDtypeStruct((2, n, dm), F32),
        scratch_shapes=[pltpu.VMEM((heads, LANES, LANES), F32)],
        compiler_params=pltpu.CompilerParams(
            dimension_semantics=("arbitrary", "arbitrary", "arbitrary"), vmem_limit_bytes=VMEM_LIMIT),
        name="hgrn_scan",
    )(q, v, logf)


def _ffn_kernel(be_ref, x_ref, w1_ref, w3_ref, w2_ref, o_ref):
    del be_ref
    x = x_ref[...].astype(BF16)
    a = jnp.dot(x, w1_ref[0], preferred_element_type=F32)
    b = jnp.dot(x, w3_ref[0], preferred_element_type=F32)
    hid = (a * jax.nn.sigmoid(a) * b).astype(BF16)
    o_ref[...] = jnp.dot(hid, w2_ref[0], preferred_element_type=F32)


def expert_ffn(xb, block_expert, w1, w3, w2):
    nrows, dm = xb.shape
    f = w1.shape[2]
    nb = nrows // MOE_BLOCK
    return pl.pallas_call(
        _ffn_kernel,
        grid_spec=pltpu.PrefetchScalarGridSpec(
            num_scalar_prefetch=1,
            grid=(nb,),
            in_specs=[pl.BlockSpec((MOE_BLOCK, dm), lambda i, be: (i, 0)),
                      pl.BlockSpec((1, dm, f), lambda i, be: (be[i], 0, 0)),
                      pl.BlockSpec((1, dm, f), lambda i, be: (be[i], 0, 0)),
                      pl.BlockSpec((1, f, dm), lambda i, be: (be[i], 0, 0))],
            out_specs=pl.BlockSpec((MOE_BLOCK, dm), lambda i, be: (i, 0))),
        out_shape=jax.ShapeDtypeStruct((nrows, dm), F32),
        compiler_params=pltpu.CompilerParams(
            dimension_semantics=("arbitrary",), vmem_limit_bytes=VMEM_LIMIT),
        name="expert_ffn",
    )(block_expert, xb, w1.astype(BF16), w3.astype(BF16), w2.astype(BF16))


def _rmsnorm(x, g):
    return x * lax.rsqrt(jnp.mean(x * x, axis=-1, keepdims=True) + NORM_EPS) * g


def _split_ctx(a, axis):
    return (lax.slice_in_dim(a, 0, CTX_LEN, axis=axis),
            lax.slice_in_dim(a, CTX_LEN, a.shape[axis], axis=axis))


def _modulate(h, shift_c, scale_c, shift_x, scale_x):
    h_c, h_x = _split_ctx(h, 1)
    return jnp.concatenate([h_c * (1 + scale_c) + shift_c, h_x * (1 + scale_x) + shift_x], axis=1)


def _gated_residual(s, y, gate_c, gate_x):
    y_c, y_x = _split_ctx(y, 1)
    return s + jnp.concatenate([y_c * gate_c, y_x * gate_x], axis=1)


def _mm3(u, w):
    b, t, d = u.shape
    return mm(u.reshape(b * t, d), w).reshape(b, t, w.shape[1])


def _head_rmsnorm(o, g, heads):
    b, t, d = o.shape
    oh = o.reshape(b, t, heads, d // heads)
    oh = oh * lax.rsqrt(jnp.mean(oh * oh, axis=-1, keepdims=True) + NORM_EPS)
    return oh.reshape(b, t, d) * g


def _conv1d_centred(x, w):
    xp = jnp.pad(x, ((0, 0), (1, 1), (0, 0)))
    return xp[:, :-2] * w[0] + xp[:, 1:-1] * w[1] + xp[:, 2:] * w[2]


def _conv_grid(x, w, rows):
    b, t, ch = x.shape
    y = lax.conv_general_dilated(x.reshape(b, rows, GRID_W, ch), w[:, :, None, :].astype(x.dtype),
                                 (1, 1), 'SAME', dimension_numbers=('NHWC', 'HWIO', 'NHWC'),
                                 feature_group_count=ch)
    return y.reshape(b, t, ch)


def _mlstm_mixer(u, rows, w_in, w_gate, b_gate, conv, head_g, w_out):
    bsz, t, dm = u.shape
    heads = ML_HEADS
    z = _mm3(u, w_in)
    qk = z[..., :2 * dm]
    v = z[..., 2 * dm:3 * dm]
    o_gate = jax.nn.sigmoid(z[..., 3 * dm:])
    qk_c, qk_x = _split_ctx(qk, 1)
    qk = jax.nn.silu(jnp.concatenate([_conv1d_centred(qk_c, conv[1]), _conv_grid(qk_x, conv, rows)], axis=1))
    q = qk[..., :dm]
    k = qk[..., dm:] * (dm // heads) ** -0.5
    wg = jnp.concatenate([w_gate[0], w_gate[1]], axis=1)
    gates = _mm3(u, wg) + jnp.concatenate([b_gate[0], b_gate[1]])
    gates = gates.reshape(bsz, t, 2, 2 * heads)
    gates = jnp.concatenate([gates[..., :heads], jax.nn.log_sigmoid(gates[..., heads:])], axis=-1)
    gc = jnp.moveaxis(gates, 2, 0).reshape(2, bsz * t, 2 * heads)
    gr = jnp.transpose(gates, (2, 0, 3, 1))
    n = bsz * t
    h = mlstm_scan(q.reshape(n, dm), k.reshape(n, dm), v.reshape(n, dm), gc, gr, bsz, t)
    h = (h[0] + h[1]).reshape(bsz, t, dm)
    y = _head_rmsnorm(h, head_g, heads) * o_gate
    return _mm3(y, w_out)


def _centred_shift(a):
    ap = jnp.pad(a, ((0, 0), (1, 1), (0, 0)))
    return 0.5 * (ap[:, :-2] + ap[:, 2:]) - a


def _rwkv7_mixer(u, mu, w_rkv, w0, w1, w2, a0, a1, a2, g1, g2, k_k, k_a, r_k, ln_w, ln_b, w_out):
    bsz, t, dm = u.shape
    heads = dm // RW_HEAD_DIM
    n = bsz * t
    u_c, u_x = _split_ctx(u, 1)
    du = jnp.concatenate([_centred_shift(u_c), _centred_shift(u_x)], axis=1)

    def mix(idx):
        return (u + du * mu[idx]).reshape(n, dm)

    r = mm(mix(0), w_rkv[0])
    k = mm(mix(1), w_rkv[1])
    v = mm(mix(2), w_rkv[2])
    m3, m4 = mix(3), mix(4)
    w_pre = jnp.stack([w0[d] + mm(jnp.tanh(mm(m3, w1[d])), w2[d]) for d in range(2)])
    lw = -jnp.exp(-jax.nn.softplus(-w_pre) - 0.5)
    a = jax.nn.sigmoid(jnp.stack([a0[d] + mm(mm(m4, a1[d]), a2[d]) for d in range(2)]))
    g = mm(jax.nn.sigmoid(mm(mix(5), g1)), g2)
    kk = (k * k_k).reshape(n, heads, RW_HEAD_DIM)
    kk = (kk * lax.rsqrt(jnp.maximum(jnp.sum(kk * kk, axis=-1, keepdims=True), 1e-24))).reshape(n, dm)
    k_dir = k[None] * (1.0 + (a - 1.0) * k_a)
    o = rwkv_scan(lw, k_dir, a, r, v, kk, bsz, t)
    o = (o[0] + o[1]).reshape(n, heads, RW_HEAD_DIM)
    mean = jnp.mean(o, axis=-1, keepdims=True)
    var = jnp.mean(jnp.square(o - mean), axis=-1, keepdims=True)
    o = ((o - mean) * lax.rsqrt(var + RW_GN_EPS)).reshape(n, dm) * ln_w + ln_b
    bonus = jnp.sum((r * (k_dir[0] + k_dir[1]) * r_k).reshape(n, heads, RW_HEAD_DIM), axis=-1, keepdims=True)
    bonus = (bonus * v.reshape(n, heads, RW_HEAD_DIM)).reshape(n, dm)
    y = (o + bonus) * g
    return mm(y, w_out).reshape(bsz, t, dm)


def _hgrn2_mixer(u, layer_idx, w_in, w_f, b_f, lb_logits, head_g, w_out):
    bsz, t, dm = u.shape
    heads = dm // HG_EXPAND
    n = bsz * t
    uf = u.reshape(n, dm)
    z = mm(uf, w_in)
    q = jax.nn.silu(z[:, :dm])
    v = z[:, dm:2 * dm]
    out_gate = jax.nn.silu(z[:, 2 * dm:])
    p = jax.nn.softmax(lb_logits, axis=0)
    lb = jnp.cumsum(p, axis=0)[layer_idx] - p[0]
    f_pre = jnp.stack([mm(uf, w_f[d]) + b_f[d] for d in range(2)])
    log_f = jnp.logaddexp(jnp.log(lb), jnp.log1p(-lb) + jax.nn.log_sigmoid(f_pre))
    o = hgrn_scan(q, v, log_f, bsz, t)
    o = (o[0] + o[1]).reshape(bsz, t, dm)
    y = _head_rmsnorm(o, head_g, heads) * out_gate.reshape(bsz, t, dm)
    return _mm3(y, w_out)


def _expert_dispatch(h, expert_idx, gate, w1, w3, w2):
    n_tok, d = h.shape
    n_experts = w1.shape[0]
    n_assign = n_tok * TOP_K
    flat_e = expert_idx.reshape(n_assign)
    order = jnp.argsort(flat_e)
    sorted_e = flat_e[order]
    counts = jnp.zeros((n_experts,), jnp.int32).at[flat_e].add(1)
    padded = (counts + MOE_BLOCK - 1) // MOE_BLOCK * MOE_BLOCK
    start_sorted = jnp.cumsum(counts) - counts
    end_pad = jnp.cumsum(padded)
    start_pad = end_pad - padded
    dest = start_pad[sorted_e] + jnp.arange(n_assign, dtype=jnp.int32) - start_sorted[sorted_e]
    n_blocks = -(-n_assign // MOE_BLOCK) + n_experts
    token_of_slot = jnp.full((n_blocks * MOE_BLOCK,), n_tok, jnp.int32).at[dest].set(
        (order // TOP_K).astype(jnp.int32))
    block_expert = jnp.minimum(jnp.searchsorted(
        end_pad, jnp.arange(n_blocks, dtype=jnp.int32) * MOE_BLOCK, side='right'), n_experts - 1)
    h_pad = jnp.concatenate([h, jnp.zeros((1, d), h.dtype)], axis=0)
    xb = h_pad[token_of_slot]
    yb = expert_ffn(xb, block_expert.astype(jnp.int32), w1, w3, w2)
    contrib = yb[dest] * gate.reshape(n_assign)[order][:, None]
    return jnp.zeros((n_tok, d), F32).at[order // TOP_K].add(contrib)


def _grouped_moe(h, router_w, router_b, w1, w3, w2):
    n_tok = h.shape[0]
    n_experts = router_w.shape[1]
    per_group = n_experts // N_GROUPS
    affinity = jax.nn.sigmoid(mm(h, router_w, precise=True))
    sel = (affinity + router_b).reshape(n_tok, N_GROUPS, per_group)
    group_score = jnp.sum(lax.top_k(sel, 2)[0], axis=-1)
    best_group = jnp.argmax(group_score, axis=-1).astype(jnp.int32)
    in_group = jnp.take_along_axis(sel, best_group[:, None, None], axis=1)[:, 0]
    local = lax.top_k(in_group, TOP_K)[1]
    expert_idx = best_group[:, None] * per_group + local
    chosen = jnp.take_along_axis(affinity, expert_idx, axis=1)
    gate = chosen / jnp.sum(chosen, axis=-1, keepdims=True)
    return _expert_dispatch(h, expert_idx, gate, w1, w3, w2)


def kernel(x, c, ctx, c_ctx, ada_w, ada_b, norm_mix, norm_ffn, norm_out, ml_w_in, ml_w_gate, ml_b_gate, ml_conv, ml_head_g, ml_w_out, rw_mu, rw_w_rkv, rw_w0, rw_w1, rw_w2, rw_a0, rw_a1, rw_a2, rw_g1, rw_g2, rw_k_k, rw_k_a, rw_r_k, rw_ln_w, rw_ln_b, rw_w_out, hg_w_in, hg_w_f, hg_b_f, hg_lb_logits, hg_head_g, hg_w_out, router_w, router_b, ex_w1, ex_w3, ex_w2):
    depth = ada_w.shape[0]
    rows = x.shape[1] // GRID_W
    cond_x = jax.nn.silu(c)
    cond_c = jax.nn.silu(c_ctx)
    s = jnp.concatenate([ctx, x], axis=1)
    for i in range(depth):
        mod_x = jnp.split((cond_x @ ada_w[i] + ada_b[i])[:, None, :], 6, axis=-1)
        mod_c = jnp.split(cond_c @ ada_w[i] + ada_b[i], 6, axis=-1)
        h = _modulate(_rmsnorm(s, norm_mix[i]), mod_c[0], mod_c[1], mod_x[0], mod_x[1])
        kind, j = i % N_MIXERS, i // N_MIXERS
        if kind == 0:
            y = _mlstm_mixer(h, rows, ml_w_in[j], ml_w_gate[j], ml_b_gate[j], ml_conv[j],
                             ml_head_g[j], ml_w_out[j])
        elif kind == 1:
            y = _rwkv7_mixer(h, rw_mu[j], rw_w_rkv[j], rw_w0[j], rw_w1[j], rw_w2[j], rw_a0[j],
                             rw_a1[j], rw_a2[j], rw_g1[j], rw_g2[j], rw_k_k[j], rw_k_a[j],
                             rw_r_k[j], rw_ln_w[j], rw_ln_b[j], rw_w_out[j])
        else:
            y = _hgrn2_mixer(h, i, hg_w_in[j], hg_w_f[j], hg_b_f[j], hg_lb_logits,
                             hg_head_g[j], hg_w_out[j])
        s = _gated_residual(s, y, mod_c[2], mod_x[2])
        h = _modulate(_rmsnorm(s, norm_ffn[i]), mod_c[3], mod_c[4], mod_x[3], mod_x[4])
        y = _grouped_moe(h.reshape(-1, h.shape[-1]), router_w, router_b,
                         ex_w1[i], ex_w3[i], ex_w2[i]).reshape(h.shape)
        s = _gated_residual(s, y, mod_c[5], mod_x[5])
    return _rmsnorm(s[:, CTX_LEN:], norm_out)
```

```python
import functools

import jax
import jax.numpy as jnp
from jax import lax
from jax.experimental import pallas as pl
from jax.experimental.pallas import tpu as pltpu

F32 = jnp.float32
BF16 = jnp.bfloat16

GRID_W = 64
CTX_LEN = 256
N_MIXERS = 3
NORM_EPS = 1e-6
ML_HEADS = 8
RW_HEAD_DIM = 64
RW_GN_EPS = 64e-5
HG_EXPAND = 128
N_GROUPS = 4
TOP_K = 2
MOE_BLOCK = 256

LANES = 128
ML_CHUNK = 128
RW_CHUNK = 64
RW_PRE_CHUNKS = 4
HG_CHUNK = 64
HG_SUB = 16
HG_EXP_CLAMP = 80.0
VMEM_LIMIT = 48 * 1024 * 1024

NT = (((1,), (1,)), ((), ()))
NN = (((1,), (0,)), ((), ()))


def _dot(a, b, dims=NN, passes=1):
    a_hi = a.astype(BF16)
    b_hi = b.astype(BF16)
    out = lax.dot_general(a_hi, b_hi, dims, preferred_element_type=F32)
    if passes == 3:
        a_lo = (a - a_hi.astype(F32)).astype(BF16)
        b_lo = (b - b_hi.astype(F32)).astype(BF16)
        out = out + lax.dot_general(a_hi, b_lo, dims, preferred_element_type=F32)
        out = out + lax.dot_general(a_lo, b_hi, dims, preferred_element_type=F32)
    return out


def _cumsum_rows(x, reverse):
    n = x.shape[0]
    row = lax.broadcasted_iota(jnp.int32, x.shape, 0)
    s = 1
    while s < n:
        if reverse:
            x = x + jnp.where(row < n - s, pltpu.roll(x, n - s, axis=0), 0.0)
        else:
            x = x + jnp.where(row >= s, pltpu.roll(x, s, axis=0), 0.0)
        s *= 2
    return x


def _pick_tile(n, candidates):
    for c in candidates:
        if n % c == 0:
            return c
    raise ValueError(f"no tile for {n}")


def _scan_chunk_index(d, p, nctx, nc):
    rev = jnp.where(p < nctx, nctx - 1 - p, nc - 1 - (p - nctx))
    return jnp.where(d == 0, p, rev)


_ACTS = {
    None: lambda y: y,
    "sigmoid": jax.nn.sigmoid,
    "silu": lambda y: y * jax.nn.sigmoid(y),
    "tanh": jnp.tanh,
}


def _mm_kernel(x_ref, w_ref, b_ref, o_ref, *, act, precise):
    if precise:
        y = _dot(x_ref[...], w_ref[...], passes=3)
    else:
        y = jnp.dot(x_ref[...].astype(BF16), w_ref[...], preferred_element_type=F32)
    o_ref[...] = _ACTS[act](y + b_ref[...]).astype(o_ref.dtype)


def mm(x, w, bias=None, act=None, out_dtype=F32, precise=False):
    n, k = x.shape
    m = w.shape[1]
    tm = _pick_tile(n, (512, 256, 128, 64, 32, 16, 8))
    tn = m if m <= 1024 else _pick_tile(m, (1024, 512, 256, 128))
    if not precise:
        w = w.astype(BF16)
    if bias is None:
        bias = jnp.zeros((m,), F32)
    return pl.pallas_call(
        functools.partial(_mm_kernel, act=act, precise=precise),
        grid=(n // tm, m // tn),
        in_specs=[pl.BlockSpec((tm, k), lambda i, j: (i, 0)),
                  pl.BlockSpec((k, tn), lambda i, j: (0, j)),
                  pl.BlockSpec((1, tn), lambda i, j: (0, j))],
        out_specs=pl.BlockSpec((tm, tn), lambda i, j: (i, j)),
        out_shape=jax.ShapeDtypeStruct((n, m), out_dtype),
        compiler_params=pltpu.CompilerParams(vmem_limit_bytes=VMEM_LIMIT),
        name="mm",
    )(x, w, bias.reshape(1, m).astype(F32))


def _ml_scan_kernel(q_ref, k_ref, v_ref, gc_ref, gr_ref, o_ref, ct_scr, n_scr, m_scr, *, heads):
    L = q_ref.shape[0]
    d = pl.program_id(0)

    @pl.when(pl.program_id(2) == 0)
    def _():
        ct_scr[...] = jnp.zeros_like(ct_scr)
        n_scr[...] = jnp.zeros_like(n_scr)
        m_scr[...] = jnp.zeros_like(m_scr)

    row = lax.broadcasted_iota(jnp.int32, (L, L), 0)
    col = lax.broadcasted_iota(jnp.int32, (L, L), 1)

    def body(reverse):
        incl = (col >= row) if reverse else (col <= row)
        incl_t = (row >= col) if reverse else (row <= col)
        last = 0 if reverse else L - 1
        hs = range(heads)
        sls = [slice(h * LANES, (h + 1) * LANES) for h in hs]
        qk = [_dot(q_ref[:, sls[h]], k_ref[:, sls[h]], NT) for h in hs]
        qc = [_dot(q_ref[:, sls[h]], ct_scr[h]) for h in hs]
        stats = []
        for h in hs:
            ig_col = gc_ref[0, :, h:h + 1]
            lf_col = gc_ref[0, :, heads + h:heads + h + 1]
            ig_row = gr_ref[0, 0, h:h + 1, :]
            lf_row = gr_ref[0, 0, heads + h:heads + h + 1, :]
            b_col = jnp.sum(jnp.where(incl, lf_row, 0.0), axis=1, keepdims=True)
            b_row = jnp.sum(jnp.where(incl_t, lf_col, 0.0), axis=0, keepdims=True)
            m_prev = m_scr[h:h + 1, 0:1]
            dmat = jnp.where(incl, b_col - b_row + ig_row, -jnp.inf)
            inter = b_col + m_prev
            m_t = jnp.maximum(inter, jnp.max(dmat, axis=1, keepdims=True))
            b_last = b_col[last:last + 1, :]
            m_new = jnp.maximum(b_last + m_prev, jnp.max(b_last - b_row + ig_row, axis=1, keepdims=True))
            w_k = jnp.exp(b_last - b_col + ig_col - m_new)
            w_prev = jnp.exp(b_last + m_prev - m_new)
            stats.append((jnp.exp(dmat - m_t), jnp.exp(inter - m_t), jnp.exp(-m_t), w_k, w_prev, m_new))
        kv = [_dot(k_ref[:, sls[h]].T, stats[h][3] * v_ref[:, sls[h]]) for h in hs]
        s = [qk[h] * stats[h][0] for h in hs]
        sv = [_dot(s[h], v_ref[:, sls[h]]) for h in hs]
        for h in hs:
            _, w_inter, floor, w_k, w_prev, m_new = stats[h]
            n_row = n_scr[h:h + 1, :]
            num = sv[h] + w_inter * qc[h]
            den = (jnp.sum(s[h], axis=1, keepdims=True)
                   + w_inter * jnp.sum(q_ref[:, sls[h]] * n_row, axis=1, keepdims=True))
            o_ref[0, :, sls[h]] = num / jnp.maximum(jnp.abs(den), floor)
            ct_scr[h] = w_prev * ct_scr[h] + kv[h]
            n_scr[h:h + 1, :] = w_prev * n_row + jnp.sum(w_k * k_ref[:, sls[h]], axis=0, keepdims=True)
            m_scr[h:h + 1, :] = jnp.broadcast_to(m_new, (1, LANES))

    @pl.when(d == 0)
    def _():
        body(False)

    @pl.when(d == 1)
    def _():
        body(True)


def mlstm_scan(q, k, v, gc, gr, bsz, t):
    n, dm = q.shape
    heads = dm // LANES
    L = ML_CHUNK
    nc, nctx = t // L, CTX_LEN // L

    def rows(d, b, p):
        return (b * nc + _scan_chunk_index(d, p, nctx, nc), 0)

    return pl.pallas_call(
        functools.partial(_ml_scan_kernel, heads=heads),
        grid=(2, bsz, nc),
        in_specs=[pl.BlockSpec((L, dm), rows), pl.BlockSpec((L, dm), rows), pl.BlockSpec((L, dm), rows),
                  pl.BlockSpec((1, L, 2 * heads), lambda d, b, p: (d,) + rows(d, b, p)),
                  pl.BlockSpec((1, 1, 2 * heads, L),
                               lambda d, b, p: (d, b, 0, _scan_chunk_index(d, p, nctx, nc)))],
        out_specs=pl.BlockSpec((1, L, dm), lambda d, b, p: (d,) + rows(d, b, p)),
        out_shape=jax.ShapeDtypeStruct((2, n, dm), F32),
        scratch_shapes=[pltpu.VMEM((heads, LANES, LANES), F32), pltpu.VMEM((heads, LANES), F32),
                        pltpu.VMEM((heads, LANES), F32)],
        compiler_params=pltpu.CompilerParams(
            dimension_semantics=("arbitrary", "arbitrary", "arbitrary"), vmem_limit_bytes=VMEM_LIMIT),
        name="mlstm_scan",
    )(q, k, v, gc, gr)


RW_STATE_PASSES = 3


def _rw_pre_kernel(lw_ref, kd_ref, a_ref, r_ref, v_ref, kk_ref, rdp_ref, o0_ref, m_ref, ha_ref, *, nchunk):
    L = RW_CHUNK
    half = LANES // 2
    row = lax.broadcasted_iota(jnp.int32, (L, LANES), 0)
    col = lax.broadcasted_iota(jnp.int32, (L, LANES), 1) % half
    eye2 = (row == col).astype(F32)
    lane = lax.broadcasted_iota(jnp.int32, (1, LANES), 1)
    m0 = (lane < half).astype(F32)
    m1 = (lane >= half).astype(F32)
    r2 = lax.broadcasted_iota(jnp.int32, (LANES, LANES), 0)
    c2 = lax.broadcasted_iota(jnp.int32, (LANES, LANES), 1)
    same_head = (r2 // half) == (c2 // half)

    def stack(x):
        return jnp.concatenate([x * m0, x * m1], axis=0)

    chains = [(d, c) for c in range(nchunk) for d in range(2)]
    st = {}
    for d, c in chains:
        reverse = d == 1
        sl = pl.ds(c * L, L)
        lw = lw_ref[d, sl, :]
        k = kd_ref[d, sl, :]
        akk = kk_ref[sl, :] * a_ref[d, sl, :]
        g = _cumsum_rows(lw, reverse)
        ieg = jnp.exp(-g)
        g_last = g[0:1] if reverse else g[L - 1:L]
        dl = jnp.exp(g_last - g)
        st[d, c] = dict(kd=kk_ref[sl, :] * jnp.exp(g - lw), rd=r_ref[sl, :] * jnp.exp(g), ai=akk * ieg,
                        ki=k * ieg, ad=akk * dl, kdd=k * dl, eg_last=jnp.exp(g_last), v=v_ref[sl, :])
    for d, c in chains:
        s = st[d, c]
        reverse = d == 1
        incl = (col >= row) if reverse else (col <= row)
        strict = (col > row) if reverse else (col < row)
        x = jnp.concatenate([s["kd"], s["rd"]], axis=0)
        rhs = jnp.concatenate([stack(s["ai"]), stack(s["ki"])], axis=0)
        sc = _dot(x, rhs, NT)
        s["a_ab"] = jnp.where(strict, sc[:L, :LANES], 0.0)
        s["a_ak"] = jnp.where(strict, sc[:L, LANES:], 0.0)
        s["b_ra"] = jnp.where(incl, sc[L:, :LANES], 0.0)
        s["b_rk"] = jnp.where(incl, sc[L:, LANES:], 0.0)
        s["tinv"] = eye2 - s["a_ab"]
        s["pw"] = s["a_ab"]
    span = 2
    while span < L:
        for d, c in chains:
            s = st[d, c]
            s["pw"] = _dot(s["pw"], stack(s["pw"]))
        for d, c in chains:
            s = st[d, c]
            s["tinv"] = _dot(s["tinv"], stack(eye2 + s["pw"]))
        span *= 2
    for d, c in chains:
        s = st[d, c]
        s["w"] = -_dot(s["tinv"], stack(s["a_ak"]))
        s["kdp"] = _dot(s["tinv"], stack(s["kd"]))
    for d, c in chains:
        s = st[d, c]
        s["vst"] = stack(s["v"])
        s["u0"] = _dot(s["w"], s["vst"])
    for d, c in chains:
        s = st[d, c]
        sl = pl.ds(c * L, L)
        lhs = jnp.concatenate([s["b_ra"], s["b_rk"]], axis=1)
        rhs = jnp.concatenate([stack(s["u0"]), s["vst"]], axis=0)
        o0_ref[d, sl, :] = _dot(lhs, rhs)
        rdp_ref[d, sl, :] = s["rd"] - _dot(s["b_ra"], stack(s["kdp"]))
        diag = jnp.where(r2 == c2, s["eg_last"], 0.0)
        m_ref[d, c, 0] = jnp.where(same_head, diag - _dot(s["ad"].T, s["kdp"]), 0.0)
        at = jnp.concatenate([s["ad"], s["kdd"]], axis=0).T
        ha_ref[d, c, 0] = jnp.where(same_head, _dot(at, jnp.concatenate([s["u0"], s["v"]], axis=0)), 0.0)


def _rw_scan_kernel(rdp_ref, o0_ref, m_ref, ha_ref, o_ref, h_scr, *, pairs):
    @pl.when(pl.program_id(2) == 0)
    def _():
        h_scr[...] = jnp.zeros_like(h_scr)

    sls = [slice(p * LANES, (p + 1) * LANES) for p in range(pairs)]
    outs = [_dot(rdp_ref[0, :, sls[p]], h_scr[p], passes=RW_STATE_PASSES) for p in range(pairs)]
    nxt = [_dot(m_ref[0, 0, p], h_scr[p], passes=RW_STATE_PASSES) for p in range(pairs)]
    for p in range(pairs):
        o_ref[0, :, sls[p]] = outs[p] + o0_ref[0, :, sls[p]]
        h_scr[p] = nxt[p] + ha_ref[0, 0, p]


def rwkv_scan(lw, kdir, a, r, v, kk, bsz, t):
    n, dm = r.shape
    pairs = dm // LANES
    L = RW_CHUNK
    nc, nctx = t // L, CTX_LEN // L
    nct = n // L
    nchunk = RW_PRE_CHUNKS
    tb = nchunk * L
    dspec = pl.BlockSpec((2, tb, LANES), lambda i, p: (0, i, p))
    sspec = pl.BlockSpec((tb, LANES), lambda i, p: (i, p))
    mspec = pl.BlockSpec((2, nchunk, 1, LANES, LANES), lambda i, p: (0, i, p, 0, 0))
    rdp, o0, mm_, ha = pl.pallas_call(
        functools.partial(_rw_pre_kernel, nchunk=nchunk),
        grid=(n // tb, pairs),
        in_specs=[dspec, dspec, dspec, sspec, sspec, sspec],
        out_specs=[dspec, dspec, mspec, mspec],
        out_shape=[jax.ShapeDtypeStruct((2, n, dm), F32), jax.ShapeDtypeStruct((2, n, dm), F32),
                   jax.ShapeDtypeStruct((2, nct, pairs, LANES, LANES), F32),
                   jax.ShapeDtypeStruct((2, nct, pairs, LANES, LANES), F32)],
        compiler_params=pltpu.CompilerParams(
            dimension_semantics=("arbitrary", "arbitrary"), vmem_limit_bytes=VMEM_LIMIT),
        name="rwkv_pre",
    )(lw, kdir, a, r, v, kk)

    def chunk(d, b, p):
        return b * nc + _scan_chunk_index(d, p, nctx, nc)

    rspec = pl.BlockSpec((1, L, dm), lambda d, b, p: (d, chunk(d, b, p), 0))
    cspec = pl.BlockSpec((1, 1, pairs, LANES, LANES), lambda d, b, p: (d, chunk(d, b, p), 0, 0, 0))
    return pl.pallas_call(
        functools.partial(_rw_scan_kernel, pairs=pairs),
        grid=(2, bsz, nc),
        in_specs=[rspec, rspec, cspec, cspec],
        out_specs=rspec,
        out_shape=jax.ShapeDtypeStruct((2, n, dm), F32),
        scratch_shapes=[pltpu.VMEM((pairs, LANES, LANES), F32)],
        compiler_params=pltpu.CompilerParams(
            dimension_semantics=("arbitrary", "arbitrary", "arbitrary"), vmem_limit_bytes=VMEM_LIMIT),
        name="rwkv_scan",
    )(rdp, o0, mm_, ha)


def _hg_scan_kernel(q_ref, v_ref, lf_ref, o_ref, st_scr, *, heads):
    C = q_ref.shape[0]
    d = pl.program_id(0)
    nsub = C // HG_SUB

    @pl.when(pl.program_id(2) == 0)
    def _():
        st_scr[...] = jnp.zeros_like(st_scr)

    def body(reverse):
        last = 0 if reverse else C - 1
        hs = range(heads)
        sls = [slice(h * LANES, (h + 1) * LANES) for h in hs]
        g = [lf_ref[0, :, sls[h]] for h in hs]
        b = [_cumsum_rows(g[h], reverse) for h in hs]
        k = [-jnp.tanh(0.5 * g[h]) * (jnp.exp(g[h]) + 1.0) for h in hs]
        o_inter = [_dot(q_ref[:, sls[h]] * jnp.exp(b[h]), st_scr[h], NT) for h in hs]
        parts = [[None] * nsub for _ in hs]
        for i in range(nsub):
            r0 = i * HG_SUB
            lo, hi = (r0, C) if reverse else (0, r0 + HG_SUB)
            first = r0 + HG_SUB - 1 if reverse else r0
            row = lax.broadcasted_iota(jnp.int32, (HG_SUB, hi - lo), 0) + r0
            col = lax.broadcasted_iota(jnp.int32, (HG_SUB, hi - lo), 1) + lo
            keep = (col >= row) if reverse else (col <= row)
            att = []
            for h in hs:
                rho = b[h][first:first + 1, :] - g[h][first:first + 1, :]
                qi = q_ref[r0:r0 + HG_SUB, sls[h]] * jnp.exp(b[h][r0:r0 + HG_SUB] - rho)
                ki = k[h][lo:hi] * jnp.exp(jnp.minimum(rho - b[h][lo:hi], HG_EXP_CLAMP))
                att.append(jnp.where(keep, _dot(qi, ki, NT), 0.0))
            for h in hs:
                parts[h][i] = _dot(att[h], v_ref[lo:hi, sls[h]])
        for h in hs:
            o_ref[0, :, sls[h]] = o_inter[h] + jnp.concatenate(parts[h], axis=0)
        upd = []
        for h in hs:
            b_last = b[h][last:last + 1, :]
            upd.append((jnp.exp(b_last), _dot(v_ref[:, sls[h]].T, k[h] * jnp.exp(b_last - b[h]))))
        for h in hs:
            st_scr[h] = st_scr[h] * upd[h][0] + upd[h][1]

    @pl.when(d == 0)
    def _():
        body(False)

    @pl.when(d == 1)
    def _():
        body(True)


def hgrn_scan(q, v, logf, bsz, t):
    n, dm = q.shape
    heads = dm // LANES
    C = HG_CHUNK
    nc, nctx = t // C, CTX_LEN // C

    def rows(d, b, p):
        return (b * nc + _scan_chunk_index(d, p, nctx, nc), 0)

    dspec = pl.BlockSpec((1, C, dm), lambda d, b, p: (d,) + rows(d, b, p))
    return pl.pallas_call(
        functools.partial(_hg_scan_kernel, heads=heads),
        grid=(2, bsz, nc),
        in_specs=[pl.BlockSpec((C, dm), rows), pl.BlockSpec((C, dm), rows), dspec],
        out_specs=dspec,
        out_shape=jax.ShapeDtypeStruct((2, n, dm), F32),
        scratch_shapes=[pltpu.VMEM((heads, LANES, LANES), F32)],
        compiler_params=pltpu.CompilerParams(
            dimension_semantics=("arbitrary", "arbitrary", "arbitrary"), vmem_limit_bytes=VMEM_LIMIT),
        name="hgrn_scan",
    )(q, v, logf)


def _rank_kernel(e_ref, rank_ref, cnt_ref, carry_scr, *, n_experts):
    @pl.when(pl.program_id(0) == 0)
    def _():
        carry_scr[...] = jnp.zeros_like(carry_scr)

    bl = e_ref.shape[2]
    e_row = e_ref[0]
    sub = lax.broadcasted_iota(jnp.int32, (n_experts, bl), 0)
    onehot = (sub == e_row).astype(F32)
    ri = lax.broadcasted_iota(jnp.int32, (bl, bl), 0)
    ci = lax.broadcasted_iota(jnp.int32, (bl, bl), 1)
    earlier = (ri < ci).astype(BF16)
    cum = jnp.dot(onehot.astype(BF16), earlier, preferred_element_type=F32)
    carry = carry_scr[...]
    rank_ref[0] = jnp.sum(onehot * (cum + carry[:, :1]), axis=0, keepdims=True).astype(jnp.int32)
    carry = carry + jnp.sum(onehot, axis=1, keepdims=True)
    carry_scr[...] = carry
    cnt_ref[...] = carry.astype(jnp.int32)


def assignment_ranks(flat_e, n_experts):
    n_assign = flat_e.shape[0]
    bl = _pick_tile(n_assign, (512, 256, 128))
    nblk = n_assign // bl
    rank, cnt = pl.pallas_call(
        functools.partial(_rank_kernel, n_experts=n_experts),
        grid=(nblk,),
        in_specs=[pl.BlockSpec((1, 1, bl), lambda i: (i, 0, 0))],
        out_specs=[pl.BlockSpec((1, 1, bl), lambda i: (i, 0, 0)),
                   pl.BlockSpec((n_experts, LANES), lambda i: (0, 0))],
        out_shape=[jax.ShapeDtypeStruct((nblk, 1, bl), jnp.int32),
                   jax.ShapeDtypeStruct((n_experts, LANES), jnp.int32)],
        scratch_shapes=[pltpu.VMEM((n_experts, LANES), F32)],
        compiler_params=pltpu.CompilerParams(dimension_semantics=("arbitrary",)),
        name="assignment_ranks",
    )(flat_e.reshape(nblk, 1, bl))
    return rank.reshape(n_assign), cnt[:, 0]


def _ffn_kernel(be_ref, x_ref, w1_ref, w3_ref, w2_ref, o_ref):
    del be_ref
    x = x_ref[...].astype(BF16)
    a = jnp.dot(x, w1_ref[0], preferred_element_type=F32)
    b = jnp.dot(x, w3_ref[0], preferred_element_type=F32)
    hid = (a * jax.nn.sigmoid(a) * b).astype(BF16)
    o_ref[...] = jnp.dot(hid, w2_ref[0], preferred_element_type=F32)


def expert_ffn(xb, block_expert, w1, w3, w2):
    nrows, dm = xb.shape
    f = w1.shape[2]
    nb = nrows // MOE_BLOCK
    return pl.pallas_call(
        _ffn_kernel,
        grid_spec=pltpu.PrefetchScalarGridSpec(
            num_scalar_prefetch=1,
            grid=(nb,),
            in_specs=[pl.BlockSpec((MOE_BLOCK, dm), lambda i, be: (i, 0)),
                      pl.BlockSpec((1, dm, f), lambda i, be: (be[i], 0, 0)),
                      pl.BlockSpec((1, dm, f), lambda i, be: (be[i], 0, 0)),
                      pl.BlockSpec((1, f, dm), lambda i, be: (be[i], 0, 0))],
            out_specs=pl.BlockSpec((MOE_BLOCK, dm), lambda i, be: (i, 0))),
        out_shape=jax.ShapeDtypeStruct((nrows, dm), F32),
        compiler_params=pltpu.CompilerParams(
            dimension_semantics=("arbitrary",), vmem_limit_bytes=VMEM_LIMIT),
        name="expert_ffn",
    )(block_expert, xb, w1.astype(BF16), w3.astype(BF16), w2.astype(BF16))


def _rmsnorm(x, g):
    return x * lax.rsqrt(jnp.mean(x * x, axis=-1, keepdims=True) + NORM_EPS) * g


def _split_ctx(a, axis):
    return (lax.slice_in_dim(a, 0, CTX_LEN, axis=axis),
            lax.slice_in_dim(a, CTX_LEN, a.shape[axis], axis=axis))


def _modulate(h, shift_c, scale_c, shift_x, scale_x):
    h_c, h_x = _split_ctx(h, 1)
    return jnp.concatenate([h_c * (1 + scale_c) + shift_c, h_x * (1 + scale_x) + shift_x], axis=1)


def _gated_residual(s, y, gate_c, gate_x):
    y_c, y_x = _split_ctx(y, 1)
    return s + jnp.concatenate([y_c * gate_c, y_x * gate_x], axis=1)


def _head_rmsnorm(o, g, heads):
    n, d = o.shape
    oh = o.reshape(n, heads, d // heads)
    oh = oh * lax.rsqrt(jnp.mean(oh * oh, axis=-1, keepdims=True) + NORM_EPS)
    return oh.reshape(n, d) * g


def _conv1d_centred(x, w):
    xp = jnp.pad(x, ((0, 0), (1, 1), (0, 0)))
    return xp[:, :-2] * w[0] + xp[:, 1:-1] * w[1] + xp[:, 2:] * w[2]


def _conv_grid(x, w, rows):
    b, t, ch = x.shape
    y = lax.conv_general_dilated(x.reshape(b, rows, GRID_W, ch), w[:, :, None, :].astype(x.dtype),
                                 (1, 1), 'SAME', dimension_numbers=('NHWC', 'HWIO', 'NHWC'),
                                 feature_group_count=ch)
    return y.reshape(b, t, ch)


def _mlstm_mixer(u, rows, w_in, w_gate, b_gate, conv, head_g, w_out):
    bsz, t, dm = u.shape
    heads = ML_HEADS
    n = bsz * t
    uf = u.reshape(n, dm)
    qk = mm(uf, w_in[:, :2 * dm]).reshape(bsz, t, 2 * dm)
    v = mm(uf, w_in[:, 2 * dm:3 * dm])
    o_gate = mm(uf, w_in[:, 3 * dm:], act="sigmoid")
    qk_c, qk_x = _split_ctx(qk, 1)
    qk = jax.nn.silu(jnp.concatenate([_conv1d_centred(qk_c, conv[1]), _conv_grid(qk_x, conv, rows)], axis=1))
    q = qk[..., :dm].reshape(n, dm)
    k = (qk[..., dm:] * (dm // heads) ** -0.5).reshape(n, dm)
    wg = jnp.concatenate([w_gate[0], w_gate[1]], axis=1)
    gates = mm(uf, wg, bias=jnp.concatenate([b_gate[0], b_gate[1]]))
    gates = gates.reshape(bsz, t, 2, 2 * heads)
    gates = jnp.concatenate([gates[..., :heads], jax.nn.log_sigmoid(gates[..., heads:])], axis=-1)
    gc = jnp.moveaxis(gates, 2, 0).reshape(2, n, 2 * heads)
    gr = jnp.transpose(gates, (2, 0, 3, 1))
    h = mlstm_scan(q, k, v, gc, gr, bsz, t)
    y = _head_rmsnorm(h[0] + h[1], head_g, heads) * o_gate
    return mm(y, w_out).reshape(bsz, t, dm)


def _centred_shift(a):
    ap = jnp.pad(a, ((0, 0), (1, 1), (0, 0)))
    return 0.5 * (ap[:, :-2] + ap[:, 2:]) - a


def _rwkv7_mixer(u, mu, w_rkv, w0, w1, w2, a0, a1, a2, g1, g2, k_k, k_a, r_k, ln_w, ln_b, w_out):
    bsz, t, dm = u.shape
    heads = dm // RW_HEAD_DIM
    n = bsz * t
    u_c, u_x = _split_ctx(u, 1)
    du = jnp.concatenate([_centred_shift(u_c), _centred_shift(u_x)], axis=1)

    def mix(idx):
        return (u + du * mu[idx]).reshape(n, dm)

    r = mm(mix(0), w_rkv[0])
    k = mm(mix(1), w_rkv[1])
    v = mm(mix(2), w_rkv[2])
    m3, m4 = mix(3), mix(4)
    w_pre = jnp.stack([mm(mm(m3, w1[d], act="tanh"), w2[d], bias=w0[d]) for d in range(2)])
    lw = -jnp.exp(-jax.nn.softplus(-w_pre) - 0.5)
    a = jnp.stack([mm(mm(m4, a1[d]), a2[d], bias=a0[d], act="sigmoid") for d in range(2)])
    g = mm(mm(mix(5), g1, act="sigmoid"), g2)
    kk = (k * k_k).reshape(n, heads, RW_HEAD_DIM)
    kk = (kk * lax.rsqrt(jnp.maximum(jnp.sum(kk * kk, axis=-1, keepdims=True), 1e-24))).reshape(n, dm)
    k_dir = k[None] * (1.0 + (a - 1.0) * k_a)
    o = rwkv_scan(lw, k_dir, a, r, v, kk, bsz, t)
    o = (o[0] + o[1]).reshape(n, heads, RW_HEAD_DIM)
    mean = jnp.mean(o, axis=-1, keepdims=True)
    var = jnp.mean(jnp.square(o - mean), axis=-1, keepdims=True)
    o = ((o - mean) * lax.rsqrt(var + RW_GN_EPS)).reshape(n, dm) * ln_w + ln_b
    bonus = jnp.sum((r * (k_dir[0] + k_dir[1]) * r_k).reshape(n, heads, RW_HEAD_DIM), axis=-1, keepdims=True)
    bonus = (bonus * v.reshape(n, heads, RW_HEAD_DIM)).reshape(n, dm)
    y = (o + bonus) * g
    return mm(y, w_out).reshape(bsz, t, dm)


def _hgrn2_mixer(u, layer_idx, w_in, w_f, b_f, lb_logits, head_g, w_out):
    bsz, t, dm = u.shape
    heads = dm // HG_EXPAND
    n = bsz * t
    uf = u.reshape(n, dm)
    q = mm(uf, w_in[:, :dm], act="silu")
    v = mm(uf, w_in[:, dm:2 * dm])
    out_gate = mm(uf, w_in[:, 2 * dm:], act="silu")
    p = jax.nn.softmax(lb_logits, axis=0)
    lb = jnp.cumsum(p, axis=0)[layer_idx] - p[0]
    f_pre = jnp.stack([mm(uf, w_f[d], bias=b_f[d]) for d in range(2)])
    log_f = jnp.logaddexp(jnp.log(lb), jnp.log1p(-lb) + jax.nn.log_sigmoid(f_pre))
    o = hgrn_scan(q, v, log_f, bsz, t)
    y = _head_rmsnorm(o[0] + o[1], head_g, heads) * out_gate
    return mm(y, w_out).reshape(bsz, t, dm)


def _expert_dispatch(h, expert_idx, gate, w1, w3, w2):
    n_tok, d = h.shape
    n_experts = w1.shape[0]
    n_assign = n_tok * TOP_K
    flat_e = expert_idx.reshape(n_assign)
    rank, counts = assignment_ranks(flat_e, n_experts)
    padded = (counts + MOE_BLOCK - 1) // MOE_BLOCK * MOE_BLOCK
    end_pad = jnp.cumsum(padded)
    start_pad = end_pad - padded
    onehot = flat_e[:, None] == jnp.arange(n_experts, dtype=jnp.int32)[None, :]
    dest = jnp.sum(jnp.where(onehot, start_pad[None, :], 0), axis=1) + rank
    n_blocks = -(-n_assign // MOE_BLOCK) + n_experts
    token_of_slot = jnp.zeros((n_blocks * MOE_BLOCK,), jnp.int32).at[dest].set(
        jnp.arange(n_assign, dtype=jnp.int32) // TOP_K)
    block_expert = jnp.minimum(jnp.searchsorted(
        end_pad, jnp.arange(n_blocks, dtype=jnp.int32) * MOE_BLOCK, side='right'), n_experts - 1)
    xb = h.astype(BF16)[token_of_slot]
    yb = expert_ffn(xb, block_expert.astype(jnp.int32), w1, w3, w2)
    y2 = yb[dest].reshape(n_tok, TOP_K, d)
    return jnp.sum(y2 * gate[:, :, None], axis=1)


def _grouped_moe(h, router_w, router_b, w1, w3, w2):
    n_tok = h.shape[0]
    n_experts = router_w.shape[1]
    per_group = n_experts // N_GROUPS
    affinity = mm(h, router_w, act="sigmoid", precise=True)
    sel = (affinity + router_b).reshape(n_tok, N_GROUPS, per_group)
    group_score = jnp.sum(lax.top_k(sel, 2)[0], axis=-1)
    best_group = jnp.argmax(group_score, axis=-1).astype(jnp.int32)
    in_group = jnp.take_along_axis(sel, best_group[:, None, None], axis=1)[:, 0]
    local = lax.top_k(in_group, TOP_K)[1]
    expert_idx = best_group[:, None] * per_group + local
    chosen = jnp.take_along_axis(affinity, expert_idx, axis=1)
    gate = chosen / jnp.sum(chosen, axis=-1, keepdims=True)
    return _expert_dispatch(h, expert_idx, gate, w1, w3, w2)


def kernel(x, c, ctx, c_ctx, ada_w, ada_b, norm_mix, norm_ffn, norm_out, ml_w_in, ml_w_gate, ml_b_gate, ml_conv, ml_head_g, ml_w_out, rw_mu, rw_w_rkv, rw_w0, rw_w1, rw_w2, rw_a0, rw_a1, rw_a2, rw_g1, rw_g2, rw_k_k, rw_k_a, rw_r_k, rw_ln_w, rw_ln_b, rw_w_out, hg_w_in, hg_w_f, hg_b_f, hg_lb_logits, hg_head_g, hg_w_out, router_w, router_b, ex_w1, ex_w3, ex_w2):
    depth = ada_w.shape[0]
    rows = x.shape[1] // GRID_W
    bsz = x.shape[0]
    cond = jax.nn.silu(jnp.concatenate([c, c_ctx[None]], axis=0))
    cond = jnp.pad(cond, ((0, -(bsz + 1) % 8), (0, 0)))
    s = jnp.concatenate([ctx, x], axis=1)
    for i in range(depth):
        mod = mm(cond, ada_w[i], bias=ada_b[i])
        mod_x = jnp.split(mod[:bsz, None, :], 6, axis=-1)
        mod_c = jnp.split(mod[bsz], 6, axis=-1)
        h = _modulate(_rmsnorm(s, norm_mix[i]), mod_c[0], mod_c[1], mod_x[0], mod_x[1])
        kind, j = i % N_MIXERS, i // N_MIXERS
        if kind == 0:
            y = _mlstm_mixer(h, rows, ml_w_in[j], ml_w_gate[j], ml_b_gate[j], ml_conv[j],
                             ml_head_g[j], ml_w_out[j])
        elif kind == 1:
            y = _rwkv7_mixer(h, rw_mu[j], rw_w_rkv[j], rw_w0[j], rw_w1[j], rw_w2[j], rw_a0[j],
                             rw_a1[j], rw_a2[j], rw_g1[j], rw_g2[j], rw_k_k[j], rw_k_a[j],
                             rw_r_k[j], rw_ln_w[j], rw_ln_b[j], rw_w_out[j])
        else:
            y = _hgrn2_mixer(h, i, hg_w_in[j], hg_w_f[j], hg_b_f[j], hg_lb_logits,
                             hg_head_g[j], hg_w_out[j])
        s = _gated_residual(s, y, mod_c[2], mod_x[2])
        h = _modulate(_rmsnorm(s, norm_ffn[i]), mod_c[3], mod_c[4], mod_x[3], mod_x[4])
        y = _grouped_moe(h.reshape(-1, h.shape[-1]), router_w, router_b,
                         ex_w1[i], ex_w3[i], ex_w2[i]).reshape(h.shape)
        s = _gated_residual(s, y, mod_c[5], mod_x[5])
    return _rmsnorm(s[:, CTX_LEN:], norm_out)
```

```python
import functools

import jax
import jax.numpy as jnp
from jax import lax
from jax.experimental import pallas as pl
from jax.experimental.pallas import tpu as pltpu

F32 = jnp.float32
BF16 = jnp.bfloat16

GRID_W = 64
CTX_LEN = 256
N_MIXERS = 3
NORM_EPS = 1e-6
ML_HEADS = 8
RW_HEAD_DIM = 64
RW_GN_EPS = 64e-5
HG_EXPAND = 128
N_GROUPS = 4
TOP_K = 2
MOE_BLOCK = 256

LANES = 128
ML_CHUNK = 128
RW_CHUNK = 64
RW_PRE_CHUNKS = 4
HG_CHUNK = 64
HG_SUB = 16
HG_EXP_CLAMP = 80.0
VMEM_LIMIT = 48 * 1024 * 1024

NT = (((1,), (1,)), ((), ()))
NN = (((1,), (0,)), ((), ()))


def _dot(a, b, dims=NN, passes=1):
    a_hi = a.astype(BF16)
    b_hi = b.astype(BF16)
    out = lax.dot_general(a_hi, b_hi, dims, preferred_element_type=F32)
    if passes == 3:
        a_lo = (a - a_hi.astype(F32)).astype(BF16)
        b_lo = (b - b_hi.astype(F32)).astype(BF16)
        out = out + lax.dot_general(a_hi, b_lo, dims, preferred_element_type=F32)
        out = out + lax.dot_general(a_lo, b_hi, dims, preferred_element_type=F32)
    return out


def _cumsum_rows(x, reverse):
    n = x.shape[0]
    row = lax.broadcasted_iota(jnp.int32, x.shape, 0)
    s = 1
    while s < n:
        if reverse:
            x = x + jnp.where(row < n - s, pltpu.roll(x, n - s, axis=0), 0.0)
        else:
            x = x + jnp.where(row >= s, pltpu.roll(x, s, axis=0), 0.0)
        s *= 2
    return x


def _pick_tile(n, candidates):
    for c in candidates:
        if n % c == 0:
            return c
    raise ValueError(f"no tile for {n}")


def _scan_chunk_index(d, p, nctx, nc):
    rev = jnp.where(p < nctx, nctx - 1 - p, nc - 1 - (p - nctx))
    return jnp.where(d == 0, p, rev)


_ACTS = {
    None: lambda y: y,
    "sigmoid": jax.nn.sigmoid,
    "silu": lambda y: y * jax.nn.sigmoid(y),
    "tanh": jnp.tanh,
}


def _mm_kernel(x_ref, w_ref, b_ref, o_ref, *, act, precise):
    if precise:
        y = _dot(x_ref[...], w_ref[...], passes=3)
    else:
        y = jnp.dot(x_ref[...].astype(BF16), w_ref[...], preferred_element_type=F32)
    o_ref[...] = _ACTS[act](y + b_ref[...]).astype(o_ref.dtype)


def mm(x, w, bias=None, act=None, out_dtype=F32, precise=False):
    n, k = x.shape
    m = w.shape[1]
    tm = _pick_tile(n, (512, 256, 128, 64, 32, 16, 8))
    tn = m if m <= 1024 else _pick_tile(m, (1024, 512, 256, 128))
    if not precise:
        w = w.astype(BF16)
    if bias is None:
        bias = jnp.zeros((m,), F32)
    return pl.pallas_call(
        functools.partial(_mm_kernel, act=act, precise=precise),
        grid=(n // tm, m // tn),
        in_specs=[pl.BlockSpec((tm, k), lambda i, j: (i, 0)),
                  pl.BlockSpec((k, tn), lambda i, j: (0, j)),
                  pl.BlockSpec((1, tn), lambda i, j: (0, j))],
        out_specs=pl.BlockSpec((tm, tn), lambda i, j: (i, j)),
        out_shape=jax.ShapeDtypeStruct((n, m), out_dtype),
        compiler_params=pltpu.CompilerParams(vmem_limit_bytes=VMEM_LIMIT),
        name="mm",
    )(x, w, bias.reshape(1, m).astype(F32))


def _ml_scan_kernel(q_ref, k_ref, v_ref, gc_ref, gr_ref, o_ref, ct_scr, n_scr, m_scr, *, heads):
    L = q_ref.shape[0]
    d = pl.program_id(0)

    @pl.when(pl.program_id(2) == 0)
    def _():
        ct_scr[...] = jnp.zeros_like(ct_scr)
        n_scr[...] = jnp.zeros_like(n_scr)
        m_scr[...] = jnp.zeros_like(m_scr)

    row = lax.broadcasted_iota(jnp.int32, (L, L), 0)
    col = lax.broadcasted_iota(jnp.int32, (L, L), 1)

    def body(reverse):
        incl = (col >= row) if reverse else (col <= row)
        incl_t = (row >= col) if reverse else (row <= col)
        last = 0 if reverse else L - 1
        hs = range(heads)
        sls = [slice(h * LANES, (h + 1) * LANES) for h in hs]
        qk = [_dot(q_ref[:, sls[h]], k_ref[:, sls[h]], NT) for h in hs]
        qc = [_dot(q_ref[:, sls[h]], ct_scr[h]) for h in hs]
        stats = []
        for h in hs:
            ig_col = gc_ref[0, :, h:h + 1]
            lf_col = gc_ref[0, :, heads + h:heads + h + 1]
            ig_row = gr_ref[0, 0, h:h + 1, :]
            lf_row = gr_ref[0, 0, heads + h:heads + h + 1, :]
            b_col = jnp.sum(jnp.where(incl, lf_row, 0.0), axis=1, keepdims=True)
            b_row = jnp.sum(jnp.where(incl_t, lf_col, 0.0), axis=0, keepdims=True)
            m_prev = m_scr[h:h + 1, 0:1]
            dmat = jnp.where(incl, b_col - b_row + ig_row, -jnp.inf)
            inter = b_col + m_prev
            m_t = jnp.maximum(inter, jnp.max(dmat, axis=1, keepdims=True))
            b_last = b_col[last:last + 1, :]
            m_new = jnp.maximum(b_last + m_prev, jnp.max(b_last - b_row + ig_row, axis=1, keepdims=True))
            w_k = jnp.exp(b_last - b_col + ig_col - m_new)
            w_prev = jnp.exp(b_last + m_prev - m_new)
            stats.append((jnp.exp(dmat - m_t), jnp.exp(inter - m_t), jnp.exp(-m_t), w_k, w_prev, m_new))
        kv = [_dot(k_ref[:, sls[h]].T, stats[h][3] * v_ref[:, sls[h]]) for h in hs]
        s = [qk[h] * stats[h][0] for h in hs]
        sv = [_dot(s[h], v_ref[:, sls[h]]) for h in hs]
        for h in hs:
            _, w_inter, floor, w_k, w_prev, m_new = stats[h]
            n_row = n_scr[h:h + 1, :]
            num = sv[h] + w_inter * qc[h]
            den = (jnp.sum(s[h], axis=1, keepdims=True)
                   + w_inter * jnp.sum(q_ref[:, sls[h]] * n_row, axis=1, keepdims=True))
            o_ref[0, :, sls[h]] = num / jnp.maximum(jnp.abs(den), floor)
            ct_scr[h] = w_prev * ct_scr[h] + kv[h]
            n_scr[h:h + 1, :] = w_prev * n_row + jnp.sum(w_k * k_ref[:, sls[h]], axis=0, keepdims=True)
            m_scr[h:h + 1, :] = jnp.broadcast_to(m_new, (1, LANES))

    @pl.when(d == 0)
    def _():
        body(False)

    @pl.when(d == 1)
    def _():
        body(True)


def mlstm_scan(q, k, v, gc, gr, bsz, t):
    n, dm = q.shape
    heads = dm // LANES
    L = ML_CHUNK
    nc, nctx = t // L, CTX_LEN // L

    def rows(d, b, p):
        return (b * nc + _scan_chunk_index(d, p, nctx, nc), 0)

    return pl.pallas_call(
        functools.partial(_ml_scan_kernel, heads=heads),
        grid=(2, bsz, nc),
        in_specs=[pl.BlockSpec((L, dm), rows), pl.BlockSpec((L, dm), rows), pl.BlockSpec((L, dm), rows),
                  pl.BlockSpec((1, L, 2 * heads), lambda d, b, p: (d,) + rows(d, b, p)),
                  pl.BlockSpec((1, 1, 2 * heads, L),
                               lambda d, b, p: (d, b, 0, _scan_chunk_index(d, p, nctx, nc)))],
        out_specs=pl.BlockSpec((1, L, dm), lambda d, b, p: (d,) + rows(d, b, p)),
        out_shape=jax.ShapeDtypeStruct((2, n, dm), F32),
        scratch_shapes=[pltpu.VMEM((heads, LANES, LANES), F32), pltpu.VMEM((heads, LANES), F32),
                        pltpu.VMEM((heads, LANES), F32)],
        compiler_params=pltpu.CompilerParams(
            dimension_semantics=("arbitrary", "arbitrary", "arbitrary"), vmem_limit_bytes=VMEM_LIMIT),
        name="mlstm_scan",
    )(q, k, v, gc, gr)


RW_STATE_PASSES = 3


def _rw_pre_kernel(lw_ref, kd_ref, a_ref, r_ref, v_ref, kk_ref, rdp_ref, o0_ref, m_ref, ha_ref, *, nchunk):
    L = RW_CHUNK
    half = LANES // 2
    row = lax.broadcasted_iota(jnp.int32, (L, LANES), 0)
    col = lax.broadcasted_iota(jnp.int32, (L, LANES), 1) % half
    eye2 = (row == col).astype(F32)
    lane = lax.broadcasted_iota(jnp.int32, (1, LANES), 1)
    m0 = (lane < half).astype(F32)
    m1 = (lane >= half).astype(F32)
    r2 = lax.broadcasted_iota(jnp.int32, (LANES, LANES), 0)
    c2 = lax.broadcasted_iota(jnp.int32, (LANES, LANES), 1)
    same_head = (r2 // half) == (c2 // half)

    def stack(x):
        return jnp.concatenate([x * m0, x * m1], axis=0)

    chains = [(d, c) for c in range(nchunk) for d in range(2)]
    st = {}
    for d, c in chains:
        reverse = d == 1
        sl = pl.ds(c * L, L)
        lw = lw_ref[d, sl, :]
        k = kd_ref[d, sl, :]
        akk = kk_ref[sl, :] * a_ref[d, sl, :]
        g = _cumsum_rows(lw, reverse)
        ieg = jnp.exp(-g)
        g_last = g[0:1] if reverse else g[L - 1:L]
        dl = jnp.exp(g_last - g)
        st[d, c] = dict(kd=kk_ref[sl, :] * jnp.exp(g - lw), rd=r_ref[sl, :] * jnp.exp(g), ai=akk * ieg,
                        ki=k * ieg, ad=akk * dl, kdd=k * dl, eg_last=jnp.exp(g_last), v=v_ref[sl, :])
    for d, c in chains:
        s = st[d, c]
        reverse = d == 1
        incl = (col >= row) if reverse else (col <= row)
        strict = (col > row) if reverse else (col < row)
        x = jnp.concatenate([s["kd"], s["rd"]], axis=0)
        rhs = jnp.concatenate([stack(s["ai"]), stack(s["ki"])], axis=0)
        sc = _dot(x, rhs, NT)
        s["a_ab"] = jnp.where(strict, sc[:L, :LANES], 0.0)
        s["a_ak"] = jnp.where(strict, sc[:L, LANES:], 0.0)
        s["b_ra"] = jnp.where(incl, sc[L:, :LANES], 0.0)
        s["b_rk"] = jnp.where(incl, sc[L:, LANES:], 0.0)
        s["tinv"] = eye2 - s["a_ab"]
        s["pw"] = s["a_ab"]
    span = 2
    while span < L:
        for d, c in chains:
            s = st[d, c]
            s["pw"] = _dot(s["pw"], stack(s["pw"]))
        for d, c in chains:
            s = st[d, c]
            s["tinv"] = _dot(s["tinv"], stack(eye2 + s["pw"]))
        span *= 2
    for d, c in chains:
        s = st[d, c]
        s["w"] = -_dot(s["tinv"], stack(s["a_ak"]))
        s["kdp"] = _dot(s["tinv"], stack(s["kd"]))
    for d, c in chains:
        s = st[d, c]
        s["vst"] = stack(s["v"])
        s["u0"] = _dot(s["w"], s["vst"])
    for d, c in chains:
        s = st[d, c]
        sl = pl.ds(c * L, L)
        lhs = jnp.concatenate([s["b_ra"], s["b_rk"]], axis=1)
        rhs = jnp.concatenate([stack(s["u0"]), s["vst"]], axis=0)
        o0_ref[d, sl, :] = _dot(lhs, rhs)
        rdp_ref[d, sl, :] = s["rd"] - _dot(s["b_ra"], stack(s["kdp"]))
        diag = jnp.where(r2 == c2, s["eg_last"], 0.0)
        m_ref[d, c, 0] = jnp.where(same_head, diag - _dot(s["ad"].T, s["kdp"]), 0.0)
        at = jnp.concatenate([s["ad"], s["kdd"]], axis=0).T
        ha_ref[d, c, 0] = jnp.where(same_head, _dot(at, jnp.concatenate([s["u0"], s["v"]], axis=0)), 0.0)


def _rw_scan_kernel(rdp_ref, o0_ref, m_ref, ha_ref, o_ref, h_scr, *, pairs):
    @pl.when(pl.program_id(2) == 0)
    def _():
        h_scr[...] = jnp.zeros_like(h_scr)

    sls = [slice(p * LANES, (p + 1) * LANES) for p in range(pairs)]
    outs = [_dot(rdp_ref[0, :, sls[p]], h_scr[p], passes=RW_STATE_PASSES) for p in range(pairs)]
    nxt = [_dot(m_ref[0, 0, p], h_scr[p], passes=RW_STATE_PASSES) for p in range(pairs)]
    for p in range(pairs):
        o_ref[0, :, sls[p]] = outs[p] + o0_ref[0, :, sls[p]]
        h_scr[p] = nxt[p] + ha_ref[0, 0, p]


def rwkv_scan(lw, kdir, a, r, v, kk, bsz, t):
    n, dm = r.shape
    pairs = dm // LANES
    L = RW_CHUNK
    nc, nctx = t // L, CTX_LEN // L
    nct = n // L
    nchunk = RW_PRE_CHUNKS
    tb = nchunk * L
    dspec = pl.BlockSpec((2, tb, LANES), lambda i, p: (0, i, p))
    sspec = pl.BlockSpec((tb, LANES), lambda i, p: (i, p))
    mspec = pl.BlockSpec((2, nchunk, 1, LANES, LANES), lambda i, p: (0, i, p, 0, 0))
    rdp, o0, mm_, ha = pl.pallas_call(
        functools.partial(_rw_pre_kernel, nchunk=nchunk),
        grid=(n // tb, pairs),
        in_specs=[dspec, dspec, dspec, sspec, sspec, sspec],
        out_specs=[dspec, dspec, mspec, mspec],
        out_shape=[jax.ShapeDtypeStruct((2, n, dm), F32), jax.ShapeDtypeStruct((2, n, dm), F32),
                   jax.ShapeDtypeStruct((2, nct, pairs, LANES, LANES), F32),
                   jax.ShapeDtypeStruct((2, nct, pairs, LANES, LANES), F32)],
        compiler_params=pltpu.CompilerParams(
            dimension_semantics=("arbitrary", "arbitrary"), vmem_limit_bytes=VMEM_LIMIT),
        name="rwkv_pre",
    )(lw, kdir, a, r, v, kk)

    def chunk(d, b, p):
        return b * nc + _scan_chunk_index(d, p, nctx, nc)

    rspec = pl.BlockSpec((1, L, dm), lambda d, b, p: (d, chunk(d, b, p), 0))
    cspec = pl.BlockSpec((1, 1, pairs, LANES, LANES), lambda d, b, p: (d, chunk(d, b, p), 0, 0, 0))
    return pl.pallas_call(
        functools.partial(_rw_scan_kernel, pairs=pairs),
        grid=(2, bsz, nc),
        in_specs=[rspec, rspec, cspec, cspec],
        out_specs=rspec,
        out_shape=jax.ShapeDtypeStruct((2, n, dm), F32),
        scratch_shapes=[pltpu.VMEM((pairs, LANES, LANES), F32)],
        compiler_params=pltpu.CompilerParams(
            dimension_semantics=("arbitrary", "arbitrary", "arbitrary"), vmem_limit_bytes=VMEM_LIMIT),
        name="rwkv_scan",
    )(rdp, o0, mm_, ha)


def _hg_scan_kernel(q_ref, v_ref, lf_ref, o_ref, st_scr, *, heads):
    C = q_ref.shape[0]
    d = pl.program_id(0)
    nsub = C // HG_SUB

    @pl.when(pl.program_id(2) == 0)
    def _():
        st_scr[...] = jnp.zeros_like(st_scr)

    def body(reverse):
        last = 0 if reverse else C - 1
        hs = range(heads)
        sls = [slice(h * LANES, (h + 1) * LANES) for h in hs]
        g = [lf_ref[0, :, sls[h]] for h in hs]
        b = [_cumsum_rows(g[h], reverse) for h in hs]
        k = [-jnp.tanh(0.5 * g[h]) * (jnp.exp(g[h]) + 1.0) for h in hs]
        o_inter = [_dot(q_ref[:, sls[h]] * jnp.exp(b[h]), st_scr[h], NT) for h in hs]
        parts = [[None] * nsub for _ in hs]
        for i in range(nsub):
            r0 = i * HG_SUB
            lo, hi = (r0, C) if reverse else (0, r0 + HG_SUB)
            first = r0 + HG_SUB - 1 if reverse else r0
            row = lax.broadcasted_iota(jnp.int32, (HG_SUB, hi - lo), 0) + r0
            col = lax.broadcasted_iota(jnp.int32, (HG_SUB, hi - lo), 1) + lo
            keep = (col >= row) if reverse else (col <= row)
            att = []
            for h in hs:
                rho = b[h][first:first + 1, :] - g[h][first:first + 1, :]
                qi = q_ref[r0:r0 + HG_SUB, sls[h]] * jnp.exp(b[h][r0:r0 + HG_SUB] - rho)
                ki = k[h][lo:hi] * jnp.exp(jnp.minimum(rho - b[h][lo:hi], HG_EXP_CLAMP))
                att.append(jnp.where(keep, _dot(qi, ki, NT), 0.0))
            for h in hs:
                parts[h][i] = _dot(att[h], v_ref[lo:hi, sls[h]])
        for h in hs:
            o_ref[0, :, sls[h]] = o_inter[h] + jnp.concatenate(parts[h], axis=0)
        upd = []
        for h in hs:
            b_last = b[h][last:last + 1, :]
            upd.append((jnp.exp(b_last), _dot(v_ref[:, sls[h]].T, k[h] * jnp.exp(b_last - b[h]))))
        for h in hs:
            st_scr[h] = st_scr[h] * upd[h][0] + upd[h][1]

    @pl.when(d == 0)
    def _():
        body(False)

    @pl.when(d == 1)
    def _():
        body(True)


def hgrn_scan(q, v, logf, bsz, t):
    n, dm = q.shape
    heads = dm // LANES
    C = HG_CHUNK
    nc, nctx = t // C, CTX_LEN // C

    def rows(d, b, p):
        return (b * nc + _scan_chunk_index(d, p, nctx, nc), 0)

    dspec = pl.BlockSpec((1, C, dm), lambda d, b, p: (d,) + rows(d, b, p))
    return pl.pallas_call(
        functools.partial(_hg_scan_kernel, heads=heads),
        grid=(2, bsz, nc),
        in_specs=[pl.BlockSpec((C, dm), rows), pl.BlockSpec((C, dm), rows), dspec],
        out_specs=dspec,
        out_shape=jax.ShapeDtypeStruct((2, n, dm), F32),
        scratch_shapes=[pltpu.VMEM((heads, LANES, LANES), F32)],
        compiler_params=pltpu.CompilerParams(
            dimension_semantics=("arbitrary", "arbitrary", "arbitrary"), vmem_limit_bytes=VMEM_LIMIT),
        name="hgrn_scan",
    )(q, v, logf)


def _first_argmax(vals):
    best, idx = vals[0], jnp.zeros(vals[0].shape, jnp.int32)
    for i in range(1, len(vals)):
        better = vals[i] > best
        best = jnp.where(better, vals[i], best)
        idx = jnp.where(better, i, idx)
    return best, idx


def _router_kernel(h_ref, wt_ref, b_ref, e_ref, g_ref, *, n_groups, top_k):
    n_experts = wt_ref.shape[0]
    per = n_experts // n_groups
    aff = jax.nn.sigmoid(_dot(wt_ref[...], h_ref[...], NT, passes=3))
    sel = aff + b_ref[...]
    a = [aff[e:e + 1, :] for e in range(n_experts)]
    s = [sel[e:e + 1, :] for e in range(n_experts)]
    neg = jnp.full_like(s[0], -jnp.inf)
    scores = []
    for g in range(n_groups):
        grp = s[g * per:(g + 1) * per]
        m1, i1 = _first_argmax(grp)
        m2, _ = _first_argmax([jnp.where(i1 == j, neg, grp[j]) for j in range(per)])
        scores.append(m1 + m2)
    _, best = _first_argmax(scores)

    def in_best(rows):
        out = []
        for j in range(per):
            x = rows[j]
            for g in range(1, n_groups):
                x = jnp.where(best == g, rows[g * per + j], x)
            out.append(x)
        return out

    sb, ab = in_best(s), in_best(a)
    picked, chosen = [], []
    cand = sb
    for _ in range(top_k):
        _, i = _first_argmax(cand)
        c = ab[0]
        for j in range(1, per):
            c = jnp.where(i == j, ab[j], c)
        picked.append(i)
        chosen.append(c)
        cand = [jnp.where(i == j, neg, cand[j]) for j in range(per)]
    total = functools.reduce(jnp.add, chosen)
    for kk_ in range(top_k):
        e_ref[kk_:kk_ + 1, :] = best * per + picked[kk_]
        g_ref[kk_:kk_ + 1, :] = chosen[kk_] / total


def route(h, router_w, router_b):
    n, k = h.shape
    n_experts = router_w.shape[1]
    tm = _pick_tile(n, (512, 256, 128))
    return pl.pallas_call(
        functools.partial(_router_kernel, n_groups=N_GROUPS, top_k=TOP_K),
        grid=(n // tm,),
        in_specs=[pl.BlockSpec((tm, k), lambda i: (i, 0)),
                  pl.BlockSpec((n_experts, k), lambda i: (0, 0)),
                  pl.BlockSpec((n_experts, 1), lambda i: (0, 0))],
        out_specs=[pl.BlockSpec((TOP_K, tm), lambda i: (0, i)), pl.BlockSpec((TOP_K, tm), lambda i: (0, i))],
        out_shape=[jax.ShapeDtypeStruct((TOP_K, n), jnp.int32), jax.ShapeDtypeStruct((TOP_K, n), F32)],
        compiler_params=pltpu.CompilerParams(vmem_limit_bytes=VMEM_LIMIT),
        name="router",
    )(h, router_w.T, router_b.reshape(n_experts, 1).astype(F32))


def _rank_kernel(e_ref, rank_ref, cnt_ref, carry_scr, *, n_experts):
    @pl.when(pl.program_id(0) == 0)
    def _():
        carry_scr[...] = jnp.zeros_like(carry_scr)

    bl = e_ref.shape[2]
    e_row = e_ref[0]
    sub = lax.broadcasted_iota(jnp.int32, (n_experts, bl), 0)
    onehot = (sub == e_row).astype(F32)
    ri = lax.broadcasted_iota(jnp.int32, (bl, bl), 0)
    ci = lax.broadcasted_iota(jnp.int32, (bl, bl), 1)
    earlier = (ri < ci).astype(BF16)
    cum = jnp.dot(onehot.astype(BF16), earlier, preferred_element_type=F32)
    carry = carry_scr[...]
    rank_ref[0] = jnp.sum(onehot * (cum + carry[:, :1]), axis=0, keepdims=True).astype(jnp.int32)
    carry = carry + jnp.sum(onehot, axis=1, keepdims=True)
    carry_scr[...] = carry
    cnt_ref[...] = carry.astype(jnp.int32)


def assignment_ranks(flat_e, n_experts):
    n_assign = flat_e.shape[0]
    bl = _pick_tile(n_assign, (512, 256, 128))
    nblk = n_assign // bl
    rank, cnt = pl.pallas_call(
        functools.partial(_rank_kernel, n_experts=n_experts),
        grid=(nblk,),
        in_specs=[pl.BlockSpec((1, 1, bl), lambda i: (i, 0, 0))],
        out_specs=[pl.BlockSpec((1, 1, bl), lambda i: (i, 0, 0)),
                   pl.BlockSpec((n_experts, LANES), lambda i: (0, 0))],
        out_shape=[jax.ShapeDtypeStruct((nblk, 1, bl), jnp.int32),
                   jax.ShapeDtypeStruct((n_experts, LANES), jnp.int32)],
        scratch_shapes=[pltpu.VMEM((n_experts, LANES), F32)],
        compiler_params=pltpu.CompilerParams(dimension_semantics=("arbitrary",)),
        name="assignment_ranks",
    )(flat_e.reshape(nblk, 1, bl))
    return rank.reshape(n_assign), cnt[:, 0]


def _ffn_kernel(be_ref, x_ref, w1_ref, w3_ref, w2_ref, o_ref):
    del be_ref
    x = x_ref[...].astype(BF16)
    a = jnp.dot(x, w1_ref[0], preferred_element_type=F32)
    b = jnp.dot(x, w3_ref[0], preferred_element_type=F32)
    hid = (a * jax.nn.sigmoid(a) * b).astype(BF16)
    o_ref[...] = jnp.dot(hid, w2_ref[0], preferred_element_type=F32)


def expert_ffn(xb, block_expert, w1, w3, w2):
    nrows, dm = xb.shape
    f = w1.shape[2]
    nb = nrows // MOE_BLOCK
    return pl.pallas_call(
        _ffn_kernel,
        grid_spec=pltpu.PrefetchScalarGridSpec(
            num_scalar_prefetch=1,
            grid=(nb,),
            in_specs=[pl.BlockSpec((MOE_BLOCK, dm), lambda i, be: (i, 0)),
                      pl.BlockSpec((1, dm, f), lambda i, be: (be[i], 0, 0)),
                      pl.BlockSpec((1, dm, f), lambda i, be: (be[i], 0, 0)),
                      pl.BlockSpec((1, f, dm), lambda i, be: (be[i], 0, 0))],
            out_specs=pl.BlockSpec((MOE_BLOCK, dm), lambda i, be: (i, 0))),
        out_shape=jax.ShapeDtypeStruct((nrows, dm), F32),
        compiler_params=pltpu.CompilerParams(
            dimension_semantics=("arbitrary",), vmem_limit_bytes=VMEM_LIMIT),
        name="expert_ffn",
    )(block_expert, xb, w1.astype(BF16), w3.astype(BF16), w2.astype(BF16))


def _rmsnorm(x, g):
    return x * lax.rsqrt(jnp.mean(x * x, axis=-1, keepdims=True) + NORM_EPS) * g


def _split_ctx(a, axis):
    return (lax.slice_in_dim(a, 0, CTX_LEN, axis=axis),
            lax.slice_in_dim(a, CTX_LEN, a.shape[axis], axis=axis))


def _modulate(h, shift_c, scale_c, shift_x, scale_x):
    h_c, h_x = _split_ctx(h, 1)
    return jnp.concatenate([h_c * (1 + scale_c) + shift_c, h_x * (1 + scale_x) + shift_x], axis=1)


def _gated_residual(s, y, gate_c, gate_x):
    y_c, y_x = _split_ctx(y, 1)
    return s + jnp.concatenate([y_c * gate_c, y_x * gate_x], axis=1)


def _head_rmsnorm(o, g, heads):
    n, d = o.shape
    oh = o.reshape(n, heads, d // heads)
    oh = oh * lax.rsqrt(jnp.mean(oh * oh, axis=-1, keepdims=True) + NORM_EPS)
    return oh.reshape(n, d) * g


def _conv1d_centred(x, w):
    xp = jnp.pad(x, ((0, 0), (1, 1), (0, 0)))
    return xp[:, :-2] * w[0] + xp[:, 1:-1] * w[1] + xp[:, 2:] * w[2]


def _conv_grid(x, w, rows):
    b, t, ch = x.shape
    y = lax.conv_general_dilated(x.reshape(b, rows, GRID_W, ch), w[:, :, None, :].astype(x.dtype),
                                 (1, 1), 'SAME', dimension_numbers=('NHWC', 'HWIO', 'NHWC'),
                                 feature_group_count=ch)
    return y.reshape(b, t, ch)


def _mlstm_mixer(u, rows, w_in, w_gate, b_gate, conv, head_g, w_out):
    bsz, t, dm = u.shape
    heads = ML_HEADS
    n = bsz * t
    uf = u.reshape(n, dm)
    qk = mm(uf, w_in[:, :2 * dm]).reshape(bsz, t, 2 * dm)
    v = mm(uf, w_in[:, 2 * dm:3 * dm])
    o_gate = mm(uf, w_in[:, 3 * dm:], act="sigmoid")
    qk_c, qk_x = _split_ctx(qk, 1)
    qk = jax.nn.silu(jnp.concatenate([_conv1d_centred(qk_c, conv[1]), _conv_grid(qk_x, conv, rows)], axis=1))
    q = qk[..., :dm].reshape(n, dm)
    k = (qk[..., dm:] * (dm // heads) ** -0.5).reshape(n, dm)
    wg = jnp.concatenate([w_gate[0], w_gate[1]], axis=1)
    gates = mm(uf, wg, bias=jnp.concatenate([b_gate[0], b_gate[1]]))
    gates = gates.reshape(bsz, t, 2, 2 * heads)
    gates = jnp.concatenate([gates[..., :heads], jax.nn.log_sigmoid(gates[..., heads:])], axis=-1)
    gc = jnp.moveaxis(gates, 2, 0).reshape(2, n, 2 * heads)
    gr = jnp.transpose(gates, (2, 0, 3, 1))
    h = mlstm_scan(q, k, v, gc, gr, bsz, t)
    y = _head_rmsnorm(h[0] + h[1], head_g, heads) * o_gate
    return mm(y, w_out).reshape(bsz, t, dm)


def _centred_shift(a):
    ap = jnp.pad(a, ((0, 0), (1, 1), (0, 0)))
    return 0.5 * (ap[:, :-2] + ap[:, 2:]) - a


def _rwkv7_mixer(u, mu, w_rkv, w0, w1, w2, a0, a1, a2, g1, g2, k_k, k_a, r_k, ln_w, ln_b, w_out):
    bsz, t, dm = u.shape
    heads = dm // RW_HEAD_DIM
    n = bsz * t
    u_c, u_x = _split_ctx(u, 1)
    du = jnp.concatenate([_centred_shift(u_c), _centred_shift(u_x)], axis=1)

    def mix(idx):
        return (u + du * mu[idx]).reshape(n, dm)

    r = mm(mix(0), w_rkv[0])
    k = mm(mix(1), w_rkv[1])
    v = mm(mix(2), w_rkv[2])
    m3, m4 = mix(3), mix(4)
    w_pre = jnp.stack([mm(mm(m3, w1[d], act="tanh"), w2[d], bias=w0[d]) for d in range(2)])
    lw = -jnp.exp(-jax.nn.softplus(-w_pre) - 0.5)
    a = jnp.stack([mm(mm(m4, a1[d]), a2[d], bias=a0[d], act="sigmoid") for d in range(2)])
    g = mm(mm(mix(5), g1, act="sigmoid"), g2)
    kk = (k * k_k).reshape(n, heads, RW_HEAD_DIM)
    kk = (kk * lax.rsqrt(jnp.maximum(jnp.sum(kk * kk, axis=-1, keepdims=True), 1e-24))).reshape(n, dm)
    k_dir = k[None] * (1.0 + (a - 1.0) * k_a)
    o = rwkv_scan(lw, k_dir, a, r, v, kk, bsz, t)
    o = (o[0] + o[1]).reshape(n, heads, RW_HEAD_DIM)
    mean = jnp.mean(o, axis=-1, keepdims=True)
    var = jnp.mean(jnp.square(o - mean), axis=-1, keepdims=True)
    o = ((o - mean) * lax.rsqrt(var + RW_GN_EPS)).reshape(n, dm) * ln_w + ln_b
    bonus = jnp.sum((r * (k_dir[0] + k_dir[1]) * r_k).reshape(n, heads, RW_HEAD_DIM), axis=-1, keepdims=True)
    bonus = (bonus * v.reshape(n, heads, RW_HEAD_DIM)).reshape(n, dm)
    y = (o + bonus) * g
    return mm(y, w_out).reshape(bsz, t, dm)


def _hgrn2_mixer(u, layer_idx, w_in, w_f, b_f, lb_logits, head_g, w_out):
    bsz, t, dm = u.shape
    heads = dm // HG_EXPAND
    n = bsz * t
    uf = u.reshape(n, dm)
    q = mm(uf, w_in[:, :dm], act="silu")
    v = mm(uf, w_in[:, dm:2 * dm])
    out_gate = mm(uf, w_in[:, 2 * dm:], act="silu")
    p = jax.nn.softmax(lb_logits, axis=0)
    lb = jnp.cumsum(p, axis=0)[layer_idx] - p[0]
    f_pre = jnp.stack([mm(uf, w_f[d], bias=b_f[d]) for d in range(2)])
    log_f = jnp.logaddexp(jnp.log(lb), jnp.log1p(-lb) + jax.nn.log_sigmoid(f_pre))
    o = hgrn_scan(q, v, log_f, bsz, t)
    y = _head_rmsnorm(o[0] + o[1], head_g, heads) * out_gate
    return mm(y, w_out).reshape(bsz, t, dm)


def _expert_dispatch(h, expert_idx, gate, w1, w3, w2):
    n_tok, d = h.shape
    n_experts = w1.shape[0]
    n_assign = n_tok * TOP_K
    flat_e = expert_idx.reshape(n_assign)
    rank, counts = assignment_ranks(flat_e, n_experts)
    padded = (counts + MOE_BLOCK - 1) // MOE_BLOCK * MOE_BLOCK
    end_pad = jnp.cumsum(padded)
    start_pad = end_pad - padded
    onehot = flat_e[:, None] == jnp.arange(n_experts, dtype=jnp.int32)[None, :]
    dest = jnp.sum(jnp.where(onehot, start_pad[None, :], 0), axis=1) + rank
    n_blocks = -(-n_assign // MOE_BLOCK) + n_experts
    token_of_slot = jnp.zeros((n_blocks * MOE_BLOCK,), jnp.int32).at[dest].set(
        jnp.arange(n_assign, dtype=jnp.int32) // TOP_K)
    block_expert = jnp.minimum(jnp.searchsorted(
        end_pad, jnp.arange(n_blocks, dtype=jnp.int32) * MOE_BLOCK, side='right'), n_experts - 1)
    xb = h.astype(BF16)[token_of_slot]
    yb = expert_ffn(xb, block_expert.astype(jnp.int32), w1, w3, w2)
    y2 = yb[dest].reshape(n_tok, TOP_K, d)
    return jnp.sum(y2 * gate[:, :, None], axis=1)


def _grouped_moe(h, router_w, router_b, w1, w3, w2):
    e, g = route(h, router_w, router_b)
    return _expert_dispatch(h, e.T, g.T, w1, w3, w2)


def kernel(x, c, ctx, c_ctx, ada_w, ada_b, norm_mix, norm_ffn, norm_out, ml_w_in, ml_w_gate, ml_b_gate, ml_conv, ml_head_g, ml_w_out, rw_mu, rw_w_rkv, rw_w0, rw_w1, rw_w2, rw_a0, rw_a1, rw_a2, rw_g1, rw_g2, rw_k_k, rw_k_a, rw_r_k, rw_ln_w, rw_ln_b, rw_w_out, hg_w_in, hg_w_f, hg_b_f, hg_lb_logits, hg_head_g, hg_w_out, router_w, router_b, ex_w1, ex_w3, ex_w2):
    depth = ada_w.shape[0]
    rows = x.shape[1] // GRID_W
    bsz = x.shape[0]
    cond = jax.nn.silu(jnp.concatenate([c, c_ctx[None]], axis=0))
    cond = jnp.pad(cond, ((0, -(bsz + 1) % 8), (0, 0)))
    s = jnp.concatenate([ctx, x], axis=1)
    for i in range(depth):
        mod = mm(cond, ada_w[i], bias=ada_b[i])
        mod_x = jnp.split(mod[:bsz, None, :], 6, axis=-1)
        mod_c = jnp.split(mod[bsz], 6, axis=-1)
        h = _modulate(_rmsnorm(s, norm_mix[i]), mod_c[0], mod_c[1], mod_x[0], mod_x[1])
        kind, j = i % N_MIXERS, i // N_MIXERS
        if kind == 0:
            y = _mlstm_mixer(h, rows, ml_w_in[j], ml_w_gate[j], ml_b_gate[j], ml_conv[j],
                             ml_head_g[j], ml_w_out[j])
        elif kind == 1:
            y = _rwkv7_mixer(h, rw_mu[j], rw_w_rkv[j], rw_w0[j], rw_w1[j], rw_w2[j], rw_a0[j],
                             rw_a1[j], rw_a2[j], rw_g1[j], rw_g2[j], rw_k_k[j], rw_k_a[j],
                             rw_r_k[j], rw_ln_w[j], rw_ln_b[j], rw_w_out[j])
        else:
            y = _hgrn2_mixer(h, i, hg_w_in[j], hg_w_f[j], hg_b_f[j], hg_lb_logits,
                             hg_head_g[j], hg_w_out[j])
        s = _gated_residual(s, y, mod_c[2], mod_x[2])
        h = _modulate(_rmsnorm(s, norm_ffn[i]), mod_c[3], mod_c[4], mod_x[3], mod_x[4])
        y = _grouped_moe(h.reshape(-1, h.shape[-1]), router_w, router_b,
                         ex_w1[i], ex_w3[i], ex_w2[i]).reshape(h.shape)
        s = _gated_residual(s, y, mod_c[5], mod_x[5])
    return _rmsnorm(s[:, CTX_LEN:], norm_out)
```

```python
import functools

import jax
import jax.numpy as jnp
from jax import lax
from jax.experimental import pallas as pl
from jax.experimental.pallas import tpu as pltpu

F32 = jnp.float32
BF16 = jnp.bfloat16

GRID_W = 64
CTX_LEN = 256
N_MIXERS = 3
NORM_EPS = 1e-6
ML_HEADS = 8
RW_HEAD_DIM = 64
RW_GN_EPS = 64e-5
HG_EXPAND = 128
N_GROUPS = 4
TOP_K = 2
MOE_BLOCK = 256

LANES = 128
ML_CHUNK = 128
RW_CHUNK = 64
RW_PRE_CHUNKS = 4
HG_CHUNK = 64
HG_SUB = 16
HG_EXP_CLAMP = 80.0
VMEM_LIMIT = 48 * 1024 * 1024
VMEM_LIMIT_BIG = 56 * 1024 * 1024

NT = (((1,), (1,)), ((), ()))
NN = (((1,), (0,)), ((), ()))


def _dot(a, b, dims=NN, passes=1):
    a_hi = a.astype(BF16)
    b_hi = b.astype(BF16)
    out = lax.dot_general(a_hi, b_hi, dims, preferred_element_type=F32)
    if passes == 3:
        a_lo = (a - a_hi.astype(F32)).astype(BF16)
        b_lo = (b - b_hi.astype(F32)).astype(BF16)
        out = out + lax.dot_general(a_hi, b_lo, dims, preferred_element_type=F32)
        out = out + lax.dot_general(a_lo, b_hi, dims, preferred_element_type=F32)
    return out


def _cumsum_rows(x, reverse):
    n = x.shape[0]
    row = lax.broadcasted_iota(jnp.int32, x.shape, 0)
    s = 1
    while s < n:
        if reverse:
            x = x + jnp.where(row < n - s, pltpu.roll(x, n - s, axis=0), 0.0)
        else:
            x = x + jnp.where(row >= s, pltpu.roll(x, s, axis=0), 0.0)
        s *= 2
    return x


def _pick_tile(n, candidates):
    for c in candidates:
        if n % c == 0:
            return c
    raise ValueError(f"no tile for {n}")


def _scan_chunk_index(d, p, nctx, nc):
    rev = jnp.where(p < nctx, nctx - 1 - p, nc - 1 - (p - nctx))
    return jnp.where(d == 0, p, rev)


_ACTS = {
    None: lambda y: y,
    "sigmoid": jax.nn.sigmoid,
    "silu": lambda y: y * jax.nn.sigmoid(y),
    "tanh": jnp.tanh,
}


def _mm_kernel(x_ref, w_ref, b_ref, o_ref, *, act, precise):
    if precise:
        y = _dot(x_ref[...], w_ref[...], passes=3)
    else:
        y = jnp.dot(x_ref[...].astype(BF16), w_ref[...], preferred_element_type=F32)
    o_ref[...] = _ACTS[act](y + b_ref[...]).astype(o_ref.dtype)


def mm(x, w, bias=None, act=None, out_dtype=F32, precise=False):
    n, k = x.shape
    m = w.shape[1]
    tm = _pick_tile(n, (512, 256, 128, 64, 32, 16, 8))
    tn = m if m <= 1024 else _pick_tile(m, (1024, 512, 256, 128))
    if not precise:
        w = w.astype(BF16)
    if bias is None:
        bias = jnp.zeros((m,), F32)
    return pl.pallas_call(
        functools.partial(_mm_kernel, act=act, precise=precise),
        grid=(n // tm, m // tn),
        in_specs=[pl.BlockSpec((tm, k), lambda i, j: (i, 0)),
                  pl.BlockSpec((k, tn), lambda i, j: (0, j)),
                  pl.BlockSpec((1, tn), lambda i, j: (0, j))],
        out_specs=pl.BlockSpec((tm, tn), lambda i, j: (i, j)),
        out_shape=jax.ShapeDtypeStruct((n, m), out_dtype),
        compiler_params=pltpu.CompilerParams(vmem_limit_bytes=VMEM_LIMIT),
        name="mm",
    )(x, w, bias.reshape(1, m).astype(F32))


ROW_TILE = 256


def _log1p_exp_neg_abs(x):
    return jnp.log(1.0 + jnp.exp(-jnp.abs(x)))


def _log_sigmoid(y):
    return jnp.minimum(y, 0.0) - _log1p_exp_neg_abs(y)


def _softplus(x):
    return jnp.maximum(x, 0.0) + _log1p_exp_neg_abs(x)


def _logaddexp(a, b):
    return jnp.maximum(a, b) + _log1p_exp_neg_abs(a - b)


def _norm_mod(x, gain, shift):
    return x * lax.rsqrt(jnp.mean(x * x, axis=-1, keepdims=True) + NORM_EPS) * gain + shift


def _seg_map(tpb, nctx_t):
    def seg(i):
        return (i // tpb) * 2 + jnp.where(i % tpb < nctx_t, 0, 1)
    return seg


_EPILOGUES = {
    None: lambda y, aux: y,
    "sigmoid": lambda y, aux: jax.nn.sigmoid(y),
    "silu": lambda y, aux: y * jax.nn.sigmoid(y),
    "logf": lambda y, aux: _logaddexp(aux[0:1, :], aux[1:2, :] + _log_sigmoid(y)),
}


def _nmm_kernel(s_ref, gain_ref, shift_ref, w_ref, b_ref, aux_ref, o_ref, h_scr, *, acts):
    j = pl.program_id(1)

    @pl.when(j == 0)
    def _():
        h_scr[...] = _norm_mod(s_ref[...], gain_ref[0], shift_ref[0]).astype(BF16)

    y = jnp.dot(h_scr[...], w_ref[...], preferred_element_type=F32) + b_ref[...]
    for act in sorted(set(acts), key=str):
        cols = [jj for jj, a in enumerate(acts) if a == act]
        if len(cols) == len(acts):
            o_ref[...] = _EPILOGUES[act](y, aux_ref[...]).astype(o_ref.dtype)
        else:
            @pl.when(functools.reduce(jnp.logical_or, [j == jj for jj in cols]))
            def _(act=act):
                o_ref[...] = _EPILOGUES[act](y, aux_ref[...]).astype(o_ref.dtype)


def norm_mod_mm(s, gain, shift, w, bias, acts, geom, aux=None, out_dtype=None):
    out_dtype = out_dtype or BF16
    n, k = s.shape
    m = w.shape[1]
    tn = m // len(acts)
    tpb, nctx_t = geom
    seg = _seg_map(tpb, nctx_t)
    if bias is None:
        bias = jnp.zeros((m,), F32)
    if aux is None:
        aux = jnp.zeros((2, m), F32)
    return pl.pallas_call(
        functools.partial(_nmm_kernel, acts=tuple(acts)),
        grid=(n // ROW_TILE, m // tn),
        in_specs=[pl.BlockSpec((ROW_TILE, k), lambda i, j: (i, 0)),
                  pl.BlockSpec((1, 1, k), lambda i, j: (seg(i), 0, 0)),
                  pl.BlockSpec((1, 1, k), lambda i, j: (seg(i), 0, 0)),
                  pl.BlockSpec((k, tn), lambda i, j: (0, j)),
                  pl.BlockSpec((1, tn), lambda i, j: (0, j)),
                  pl.BlockSpec((2, tn), lambda i, j: (0, j))],
        out_specs=pl.BlockSpec((ROW_TILE, tn), lambda i, j: (i, j)),
        out_shape=jax.ShapeDtypeStruct((n, m), out_dtype),
        scratch_shapes=[pltpu.VMEM((ROW_TILE, k), BF16)],
        compiler_params=pltpu.CompilerParams(
            dimension_semantics=("arbitrary", "arbitrary"), vmem_limit_bytes=VMEM_LIMIT),
        name="norm_mod_mm",
    )(s, gain, shift, w.astype(BF16), bias.reshape(1, m).astype(F32), aux.astype(F32))


def _post_kernel(h_ref, g_ref, s_ref, hg_ref, gm_ref, w_ref, o_ref, *, heads):
    x = h_ref[0].astype(F32) + h_ref[1].astype(F32)
    hd = x.shape[1] // heads
    parts = []
    for h in range(heads):
        xh = x[:, h * hd:(h + 1) * hd]
        parts.append(xh * lax.rsqrt(jnp.mean(xh * xh, axis=-1, keepdims=True) + NORM_EPS))
    y = (jnp.concatenate(parts, axis=1) * hg_ref[...] * g_ref[...].astype(F32)).astype(BF16)
    o_ref[...] = s_ref[...] + gm_ref[0] * jnp.dot(y, w_ref[...], preferred_element_type=F32)


def post_mm_residual(h2, gate_arr, gate_block, s, head_g, gm, w_out, heads, geom):
    n, dm = s.shape
    seg = _seg_map(*geom)
    return pl.pallas_call(
        functools.partial(_post_kernel, heads=heads),
        grid=(n // ROW_TILE,),
        in_specs=[pl.BlockSpec((2, ROW_TILE, dm), lambda i: (0, i, 0)),
                  pl.BlockSpec((ROW_TILE, dm), lambda i: (i, gate_block)),
                  pl.BlockSpec((ROW_TILE, dm), lambda i: (i, 0)),
                  pl.BlockSpec((1, dm), lambda i: (0, 0)),
                  pl.BlockSpec((1, 1, dm), lambda i: (seg(i), 0, 0)),
                  pl.BlockSpec((dm, dm), lambda i: (0, 0))],
        out_specs=pl.BlockSpec((ROW_TILE, dm), lambda i: (i, 0)),
        out_shape=jax.ShapeDtypeStruct((n, dm), F32),
        compiler_params=pltpu.CompilerParams(dimension_semantics=("arbitrary",), vmem_limit_bytes=VMEM_LIMIT),
        name="post_mm_residual",
    )(h2, gate_arr, s, head_g.reshape(1, dm), gm, w_out.astype(BF16))


CONV_COLS = 512


def _conv_kernel(cur_ref, up_ref, dn_ref, w_ref, sc_ref, o_ref, *, tpb, nctx_t):
    ti = pl.program_id(0) % tpb
    is_ctx = ti < nctx_t
    no_up = jnp.logical_or(is_ctx, ti == nctx_t)
    no_dn = jnp.logical_or(is_ctx, ti == tpb - 1)
    x = cur_ref[...].astype(F32)
    up = jnp.where(no_up, 0.0, up_ref[...].astype(F32))
    dn = jnp.where(no_dn, 0.0, dn_ref[...].astype(F32))
    ext = jnp.concatenate([up, x, dn], axis=0)
    nr = ext.shape[0]
    ext_m = pltpu.roll(ext, 1, axis=0)
    ext_p = pltpu.roll(ext, nr - 1, axis=0)
    tpos = lax.broadcasted_iota(jnp.int32, (ROW_TILE, 1), 0)
    col = tpos % GRID_W
    left_ok = jnp.where(is_ctx, (tpos > 0).astype(F32), (col > 0).astype(F32))
    right_ok = jnp.where(is_ctx, (tpos < ROW_TILE - 1).astype(F32), (col < GRID_W - 1).astype(F32))
    vert = jnp.where(is_ctx, 0.0, 1.0)
    w = w_ref[...]
    acc = None
    for dr in (-1, 0, 1):
        base = GRID_W * (1 + dr)
        r3 = 3 * (dr + 1)
        term = (ext[base:base + ROW_TILE] * w[r3 + 1:r3 + 2]
                + ext_m[base:base + ROW_TILE] * w[r3:r3 + 1] * left_ok
                + ext_p[base:base + ROW_TILE] * w[r3 + 2:r3 + 3] * right_ok)
        if dr != 0:
            term = term * vert
        acc = term if acc is None else acc + term
    o_ref[...] = (acc * jax.nn.sigmoid(acc) * sc_ref[...]).astype(o_ref.dtype)


def conv_silu(z, conv_w, scale, width, geom):
    n = z.shape[0]
    tpb, nctx_t = geom
    assert nctx_t == 1 and ROW_TILE % GRID_W == 0
    hb = ROW_TILE // GRID_W
    last = n // GRID_W - 1
    return pl.pallas_call(
        functools.partial(_conv_kernel, tpb=tpb, nctx_t=nctx_t),
        grid=(n // ROW_TILE, width // CONV_COLS),
        in_specs=[pl.BlockSpec((ROW_TILE, CONV_COLS), lambda i, c: (i, c)),
                  pl.BlockSpec((GRID_W, CONV_COLS), lambda i, c: (jnp.maximum(i * hb - 1, 0), c)),
                  pl.BlockSpec((GRID_W, CONV_COLS), lambda i, c: (jnp.minimum((i + 1) * hb, last), c)),
                  pl.BlockSpec((9, CONV_COLS), lambda i, c: (0, c)),
                  pl.BlockSpec((1, CONV_COLS), lambda i, c: (0, c))],
        out_specs=pl.BlockSpec((ROW_TILE, CONV_COLS), lambda i, c: (i, c)),
        out_shape=jax.ShapeDtypeStruct((n, width), BF16),
        compiler_params=pltpu.CompilerParams(
            dimension_semantics=("arbitrary", "arbitrary"), vmem_limit_bytes=VMEM_LIMIT),
        name="conv_silu",
    )(z, z, z, conv_w.reshape(9, width).astype(F32), scale.reshape(1, width).astype(F32))


def _ml_scan_kernel(q_ref, k_ref, v_ref, gc_ref, gr_ref, o_ref, ct_scr, n_scr, m_scr, *, heads):
    L = q_ref.shape[0]
    d = pl.program_id(0)

    @pl.when(pl.program_id(2) == 0)
    def _():
        ct_scr[...] = jnp.zeros_like(ct_scr)
        n_scr[...] = jnp.zeros_like(n_scr)
        m_scr[...] = jnp.zeros_like(m_scr)

    row = lax.broadcasted_iota(jnp.int32, (L, L), 0)
    col = lax.broadcasted_iota(jnp.int32, (L, L), 1)

    def body(reverse):
        incl = (col >= row) if reverse else (col <= row)
        incl_t = (row >= col) if reverse else (row <= col)
        last = 0 if reverse else L - 1
        hs = range(heads)
        sls = [slice(h * LANES, (h + 1) * LANES) for h in hs]
        qk = [_dot(q_ref[:, sls[h]], k_ref[:, sls[h]], NT) for h in hs]
        qc = [_dot(q_ref[:, sls[h]], ct_scr[h]) for h in hs]
        stats = []
        for h in hs:
            ig_col = gc_ref[0, :, h:h + 1]
            lf_col = gc_ref[0, :, heads + h:heads + h + 1]
            ig_row = gr_ref[0, 0, h:h + 1, :]
            lf_row = gr_ref[0, 0, heads + h:heads + h + 1, :]
            b_col = jnp.sum(jnp.where(incl, lf_row, 0.0), axis=1, keepdims=True)
            b_row = jnp.sum(jnp.where(incl_t, lf_col, 0.0), axis=0, keepdims=True)
            m_prev = m_scr[h:h + 1, 0:1]
            dmat = jnp.where(incl, b_col - b_row + ig_row, -jnp.inf)
            inter = b_col + m_prev
            m_t = jnp.maximum(inter, jnp.max(dmat, axis=1, keepdims=True))
            b_last = b_col[last:last + 1, :]
            m_new = jnp.maximum(b_last + m_prev, jnp.max(b_last - b_row + ig_row, axis=1, keepdims=True))
            w_k = jnp.exp(b_last - b_col + ig_col - m_new)
            w_prev = jnp.exp(b_last + m_prev - m_new)
            stats.append((jnp.exp(dmat - m_t), jnp.exp(inter - m_t), jnp.exp(-m_t), w_k, w_prev, m_new))
        kv = [_dot(k_ref[:, sls[h]].astype(F32).T, stats[h][3] * v_ref[:, sls[h]].astype(F32)) for h in hs]
        s = [qk[h] * stats[h][0] for h in hs]
        sv = [_dot(s[h], v_ref[:, sls[h]]) for h in hs]
        for h in hs:
            _, w_inter, floor, w_k, w_prev, m_new = stats[h]
            n_row = n_scr[h:h + 1, :]
            num = sv[h] + w_inter * qc[h]
            den = (jnp.sum(s[h], axis=1, keepdims=True)
                   + w_inter * jnp.sum(q_ref[:, sls[h]].astype(F32) * n_row, axis=1, keepdims=True))
            o_ref[0, :, sls[h]] = (num / jnp.maximum(jnp.abs(den), floor)).astype(o_ref.dtype)
            ct_scr[h] = w_prev * ct_scr[h] + kv[h]
            n_scr[h:h + 1, :] = w_prev * n_row + jnp.sum(w_k * k_ref[:, sls[h]].astype(F32), axis=0, keepdims=True)
            m_scr[h:h + 1, :] = jnp.broadcast_to(m_new, (1, LANES))

    @pl.when(d == 0)
    def _():
        body(False)

    @pl.when(d == 1)
    def _():
        body(True)


def mlstm_scan(qk, z, gc, gr, dm, bsz, t):
    n = qk.shape[0]
    heads = dm // LANES
    L = ML_CHUNK
    nc, nctx = t // L, CTX_LEN // L

    def row(d, b, p):
        return b * nc + _scan_chunk_index(d, p, nctx, nc)

    return pl.pallas_call(
        functools.partial(_ml_scan_kernel, heads=heads),
        grid=(2, bsz, nc),
        in_specs=[pl.BlockSpec((L, dm), lambda d, b, p: (row(d, b, p), 0)),
                  pl.BlockSpec((L, dm), lambda d, b, p: (row(d, b, p), 1)),
                  pl.BlockSpec((L, dm), lambda d, b, p: (row(d, b, p), 2)),
                  pl.BlockSpec((1, L, 2 * heads), lambda d, b, p: (d, row(d, b, p), 0)),
                  pl.BlockSpec((1, 1, 2 * heads, L),
                               lambda d, b, p: (d, b, 0, _scan_chunk_index(d, p, nctx, nc)))],
        out_specs=pl.BlockSpec((1, L, dm), lambda d, b, p: (d, row(d, b, p), 0)),
        out_shape=jax.ShapeDtypeStruct((2, n, dm), BF16),
        scratch_shapes=[pltpu.VMEM((heads, LANES, LANES), F32), pltpu.VMEM((heads, LANES), F32),
                        pltpu.VMEM((heads, LANES), F32)],
        compiler_params=pltpu.CompilerParams(
            dimension_semantics=("arbitrary", "arbitrary", "arbitrary"), vmem_limit_bytes=VMEM_LIMIT),
        name="mlstm_scan",
    )(qk, qk, z, gc, gr)


RW_STATE_PASSES = 3


def _rw_pre_kernel(lw0_ref, lw1_ref, kd0_ref, kd1_ref, a0_ref, a1_ref, r_ref, v_ref, kk_ref,
                   rdp_ref, o0_ref, m_ref, ha_ref, *, nchunk):
    L = RW_CHUNK
    half = LANES // 2
    row = lax.broadcasted_iota(jnp.int32, (L, LANES), 0)
    col = lax.broadcasted_iota(jnp.int32, (L, LANES), 1) % half
    eye2 = (row == col).astype(F32)
    lane = lax.broadcasted_iota(jnp.int32, (1, LANES), 1)
    m0 = (lane < half).astype(F32)
    m1 = (lane >= half).astype(F32)
    r2 = lax.broadcasted_iota(jnp.int32, (LANES, LANES), 0)
    c2 = lax.broadcasted_iota(jnp.int32, (LANES, LANES), 1)
    same_head = (r2 // half) == (c2 // half)

    def stack(x):
        return jnp.concatenate([x * m0, x * m1], axis=0)

    chains = [(d, c) for c in range(nchunk) for d in range(2)]
    st = {}
    for d, c in chains:
        reverse = d == 1
        sl = pl.ds(c * L, L)
        lw = (lw0_ref, lw1_ref)[d][sl, :]
        k = (kd0_ref, kd1_ref)[d][sl, :].astype(F32)
        kk = kk_ref[sl, :].astype(F32)
        akk = kk * (a0_ref, a1_ref)[d][sl, :].astype(F32)
        g = _cumsum_rows(lw, reverse)
        ieg = jnp.exp(-g)
        g_last = g[0:1] if reverse else g[L - 1:L]
        dl = jnp.exp(g_last - g)
        st[d, c] = dict(kd=kk * jnp.exp(g - lw), rd=r_ref[sl, :].astype(F32) * jnp.exp(g), ai=akk * ieg,
                        ki=k * ieg, ad=akk * dl, kdd=k * dl, eg_last=jnp.exp(g_last),
                        v=v_ref[sl, :].astype(F32))
    for d, c in chains:
        s = st[d, c]
        reverse = d == 1
        incl = (col >= row) if reverse else (col <= row)
        strict = (col > row) if reverse else (col < row)
        x = jnp.concatenate([s["kd"], s["rd"]], axis=0)
        rhs = jnp.concatenate([stack(s["ai"]), stack(s["ki"])], axis=0)
        sc = _dot(x, rhs, NT)
        s["a_ab"] = jnp.where(strict, sc[:L, :LANES], 0.0)
        s["a_ak"] = jnp.where(strict, sc[:L, LANES:], 0.0)
        s["b_ra"] = jnp.where(incl, sc[L:, :LANES], 0.0)
        s["b_rk"] = jnp.where(incl, sc[L:, LANES:], 0.0)
        s["tinv"] = eye2 - s["a_ab"]
        s["pw"] = s["a_ab"]
    span = 2
    while span < L:
        for d, c in chains:
            s = st[d, c]
            s["pw"] = _dot(s["pw"], stack(s["pw"]))
        for d, c in chains:
            s = st[d, c]
            s["tinv"] = _dot(s["tinv"], stack(eye2 + s["pw"]))
        span *= 2
    for d, c in chains:
        s = st[d, c]
        s["w"] = -_dot(s["tinv"], stack(s["a_ak"]))
        s["kdp"] = _dot(s["tinv"], stack(s["kd"]))
    for d, c in chains:
        s = st[d, c]
        s["vst"] = stack(s["v"])
        s["u0"] = _dot(s["w"], s["vst"])
    for d, c in chains:
        s = st[d, c]
        sl = pl.ds(c * L, L)
        lhs = jnp.concatenate([s["b_ra"], s["b_rk"]], axis=1)
        rhs = jnp.concatenate([stack(s["u0"]), s["vst"]], axis=0)
        o0_ref[d, sl, :] = _dot(lhs, rhs).astype(o0_ref.dtype)
        rdp_ref[d, sl, :] = (s["rd"] - _dot(s["b_ra"], stack(s["kdp"]))).astype(rdp_ref.dtype)
        diag = jnp.where(r2 == c2, s["eg_last"], 0.0)
        m_ref[d, c, 0] = jnp.where(same_head, diag - _dot(s["ad"].T, s["kdp"]), 0.0)
        at = jnp.concatenate([s["ad"], s["kdd"]], axis=0).T
        ha_ref[d, c, 0] = jnp.where(same_head, _dot(at, jnp.concatenate([s["u0"], s["v"]], axis=0)), 0.0)


def _rw_scan_kernel(rdp_ref, o0_ref, m_ref, ha_ref, o_ref, h_scr, *, pairs):
    @pl.when(pl.program_id(2) == 0)
    def _():
        h_scr[...] = jnp.zeros_like(h_scr)

    sls = [slice(p * LANES, (p + 1) * LANES) for p in range(pairs)]
    outs = [_dot(rdp_ref[0, :, sls[p]].astype(F32), h_scr[p], passes=RW_STATE_PASSES) for p in range(pairs)]
    nxt = [_dot(m_ref[0, 0, p], h_scr[p], passes=RW_STATE_PASSES) for p in range(pairs)]
    for p in range(pairs):
        o_ref[0, :, sls[p]] = (outs[p] + o0_ref[0, :, sls[p]].astype(F32)).astype(o_ref.dtype)
        h_scr[p] = nxt[p] + ha_ref[0, 0, p]


def rwkv_scan(lw, kda, rvkg, dm, bsz, t):
    n = lw.shape[0]
    pairs = dm // LANES
    L = RW_CHUNK
    nc, nctx = t // L, CTX_LEN // L
    nct = n // L
    nchunk = RW_PRE_CHUNKS
    tb = nchunk * L

    def cspec(block):
        return pl.BlockSpec((tb, LANES), lambda i, p: (i, block * pairs + p))

    dspec = pl.BlockSpec((2, tb, LANES), lambda i, p: (0, i, p))
    mspec = pl.BlockSpec((2, nchunk, 1, LANES, LANES), lambda i, p: (0, i, p, 0, 0))
    rdp, o0, mm_, ha = pl.pallas_call(
        functools.partial(_rw_pre_kernel, nchunk=nchunk),
        grid=(n // tb, pairs),
        in_specs=[cspec(0), cspec(1), cspec(0), cspec(1), cspec(2), cspec(3), cspec(0), cspec(1), cspec(2)],
        out_specs=[dspec, dspec, mspec, mspec],
        out_shape=[jax.ShapeDtypeStruct((2, n, dm), BF16), jax.ShapeDtypeStruct((2, n, dm), BF16),
                   jax.ShapeDtypeStruct((2, nct, pairs, LANES, LANES), F32),
                   jax.ShapeDtypeStruct((2, nct, pairs, LANES, LANES), F32)],
        compiler_params=pltpu.CompilerParams(
            dimension_semantics=("arbitrary", "arbitrary"), vmem_limit_bytes=VMEM_LIMIT),
        name="rwkv_pre",
    )(lw, lw, kda, kda, kda, kda, rvkg, rvkg, rvkg)

    def chunk(d, b, p):
        return b * nc + _scan_chunk_index(d, p, nctx, nc)

    rspec = pl.BlockSpec((1, L, dm), lambda d, b, p: (d, chunk(d, b, p), 0))
    sspec = pl.BlockSpec((1, 1, pairs, LANES, LANES), lambda d, b, p: (d, chunk(d, b, p), 0, 0, 0))
    return pl.pallas_call(
        functools.partial(_rw_scan_kernel, pairs=pairs),
        grid=(2, bsz, nc),
        in_specs=[rspec, rspec, sspec, sspec],
        out_specs=rspec,
        out_shape=jax.ShapeDtypeStruct((2, n, dm), BF16),
        scratch_shapes=[pltpu.VMEM((pairs, LANES, LANES), F32)],
        compiler_params=pltpu.CompilerParams(
            dimension_semantics=("arbitrary", "arbitrary", "arbitrary"), vmem_limit_bytes=VMEM_LIMIT),
        name="rwkv_scan",
    )(rdp, o0, mm_, ha)


HALO_ROWS = 8


def _group_sum(x, width):
    r = lax.broadcasted_iota(jnp.int32, (LANES, LANES), 0) // width
    c = lax.broadcasted_iota(jnp.int32, (LANES, LANES), 1) // width
    ones = (r == c).astype(BF16)
    hi = x.astype(BF16)
    lo = (x - hi.astype(F32)).astype(BF16)
    parts = []
    for j in range(x.shape[1] // LANES):
        sl = slice(j * LANES, (j + 1) * LANES)
        parts.append(jnp.dot(hi[:, sl], ones, preferred_element_type=F32)
                     + jnp.dot(lo[:, sl], ones, preferred_element_type=F32))
    return jnp.concatenate(parts, axis=1)


def _rw_proj_kernel(s_ref, up_ref, dn_ref, gain_ref, shift_ref, mu_ref, wrkv_ref, w1_ref, w2_ref, w0_ref,
                    a1_ref, a2_ref, a0_ref, g1_ref, g2_ref, kk_ref, ka_ref,
                    lw_ref, kda_ref, rvkg_ref, *, tpb, nctx_t):
    tm, dm = s_ref.shape
    ti = pl.program_id(0) % tpb
    is_ctx = ti < nctx_t
    has_up = jnp.logical_not(jnp.logical_or(is_ctx, ti == nctx_t))
    has_dn = jnp.logical_not(jnp.logical_or(is_ctx, ti == tpb - 1))
    gain = gain_ref[0]
    shift = shift_ref[0]
    u = _norm_mod(s_ref[...], gain, shift)
    u_up = jnp.where(has_up, _norm_mod(up_ref[HALO_ROWS - 1:HALO_ROWS, :], gain, shift), 0.0)
    u_dn = jnp.where(has_dn, _norm_mod(dn_ref[0:1, :], gain, shift), 0.0)
    row = lax.broadcasted_iota(jnp.int32, (tm, 1), 0)
    u_m = jnp.where(row == 0, u_up, pltpu.roll(u, 1, axis=0))
    u_p = jnp.where(row == tm - 1, u_dn, pltpu.roll(u, tm - 1, axis=0))
    du = 0.5 * (u_m + u_p) - u
    mu = mu_ref[...]

    def mix(i):
        return (u + du * mu[i:i + 1]).astype(BF16)

    def dot(a, b):
        return jnp.dot(a.astype(BF16), b, preferred_element_type=F32)

    r = dot(mix(0), wrkv_ref[0])
    k = dot(mix(1), wrkv_ref[1])
    v = dot(mix(2), wrkv_ref[2])
    w_pre = dot(jnp.tanh(dot(mix(3), w1_ref[...])), w2_ref[...]) + w0_ref[...]
    lw_ref[...] = -jnp.exp(-_softplus(-w_pre) - 0.5)
    a = jax.nn.sigmoid(dot(dot(mix(4), a1_ref[...]), a2_ref[...]) + a0_ref[...])
    g = dot(jax.nn.sigmoid(dot(mix(5), g1_ref[...])), g2_ref[...])
    kk = k * kk_ref[...]
    kk = kk * lax.rsqrt(jnp.maximum(_group_sum(kk * kk, RW_HEAD_DIM), 1e-24))
    ka = ka_ref[...]
    for d in range(2):
        kda_ref[:, d * dm:(d + 1) * dm] = (k * (1.0 + (a[:, d * dm:(d + 1) * dm] - 1.0) * ka)).astype(kda_ref.dtype)
    kda_ref[:, 2 * dm:] = a.astype(kda_ref.dtype)
    for j, val in enumerate((r, v, kk, g)):
        rvkg_ref[:, j * dm:(j + 1) * dm] = val.astype(rvkg_ref.dtype)


def rwkv_proj(s, gain, shift, geom, mu, w_rkv, w0, w1, w2, a0, a1, a2, g1, g2, k_k, k_a):
    n, dm = s.shape
    tpb, nctx_t = geom
    assert nctx_t == 1
    seg = _seg_map(tpb, nctx_t)
    hb = ROW_TILE // HALO_ROWS
    last = n // HALO_ROWS - 1
    lora = w1.shape[2]

    def blockdiag(w):
        z = jnp.zeros_like(w[0])
        return jnp.concatenate([jnp.concatenate([w[0], z], axis=1), jnp.concatenate([z, w[1]], axis=1)], axis=0)

    consts = [jnp.pad(mu, ((0, HALO_ROWS - mu.shape[0]), (0, 0))), w_rkv.astype(BF16),
              jnp.concatenate([w1[0], w1[1]], axis=1).astype(BF16), blockdiag(w2).astype(BF16),
              jnp.concatenate([w0[0], w0[1]])[None],
              jnp.concatenate([a1[0], a1[1]], axis=1).astype(BF16), blockdiag(a2).astype(BF16),
              jnp.concatenate([a0[0], a0[1]])[None],
              g1.astype(BF16), g2.astype(BF16), k_k[None], k_a[None]]

    def const_spec(x):
        nd = x.ndim
        return pl.BlockSpec(x.shape, lambda i: (0,) * nd)

    return pl.pallas_call(
        functools.partial(_rw_proj_kernel, tpb=tpb, nctx_t=nctx_t),
        grid=(n // ROW_TILE,),
        in_specs=[pl.BlockSpec((ROW_TILE, dm), lambda i: (i, 0)),
                  pl.BlockSpec((HALO_ROWS, dm), lambda i: (jnp.maximum(i * hb - 1, 0), 0)),
                  pl.BlockSpec((HALO_ROWS, dm), lambda i: (jnp.minimum((i + 1) * hb, last), 0)),
                  pl.BlockSpec((1, 1, dm), lambda i: (seg(i), 0, 0)),
                  pl.BlockSpec((1, 1, dm), lambda i: (seg(i), 0, 0))] + [const_spec(x) for x in consts],
        out_specs=[pl.BlockSpec((ROW_TILE, 2 * dm), lambda i: (i, 0)),
                   pl.BlockSpec((ROW_TILE, 4 * dm), lambda i: (i, 0)),
                   pl.BlockSpec((ROW_TILE, 4 * dm), lambda i: (i, 0))],
        out_shape=[jax.ShapeDtypeStruct((n, 2 * dm), F32), jax.ShapeDtypeStruct((n, 4 * dm), BF16),
                   jax.ShapeDtypeStruct((n, 4 * dm), BF16)],
        compiler_params=pltpu.CompilerParams(dimension_semantics=("arbitrary",), vmem_limit_bytes=VMEM_LIMIT_BIG),
        name="rwkv_proj",
    )(s, s, s, gain, shift, *consts)


def _rw_post_kernel(o_ref, r_ref, v_ref, g_ref, k0_ref, k1_ref, s_ref, lnw_ref, lnb_ref, rk_ref, gm_ref, w_ref,
                    out_ref):
    o = o_ref[0].astype(F32) + o_ref[1].astype(F32)
    inv = 1.0 / RW_HEAD_DIM
    mean = _group_sum(o, RW_HEAD_DIM) * inv
    oc = o - mean
    var = _group_sum(oc * oc, RW_HEAD_DIM) * inv
    xn = oc * lax.rsqrt(var + RW_GN_EPS) * lnw_ref[...] + lnb_ref[...]
    r = r_ref[...].astype(F32)
    ksum = k0_ref[...].astype(F32) + k1_ref[...].astype(F32)
    bonus = _group_sum(r * ksum * rk_ref[...], RW_HEAD_DIM) * v_ref[...].astype(F32)
    y = ((xn + bonus) * g_ref[...].astype(F32)).astype(BF16)
    out_ref[...] = s_ref[...] + gm_ref[0] * jnp.dot(y, w_ref[...], preferred_element_type=F32)


def rwkv_post(o2, rvkg, kda, s, ln_w, ln_b, r_k, gm, w_out, geom):
    n, dm = s.shape
    seg = _seg_map(*geom)

    def col(block):
        return pl.BlockSpec((ROW_TILE, dm), lambda i: (i, block))

    vec = pl.BlockSpec((1, dm), lambda i: (0, 0))
    return pl.pallas_call(
        _rw_post_kernel,
        grid=(n // ROW_TILE,),
        in_specs=[pl.BlockSpec((2, ROW_TILE, dm), lambda i: (0, i, 0)), col(0), col(1), col(3), col(0), col(1),
                  col(0), vec, vec, vec, pl.BlockSpec((1, 1, dm), lambda i: (seg(i), 0, 0)),
                  pl.BlockSpec((dm, dm), lambda i: (0, 0))],
        out_specs=pl.BlockSpec((ROW_TILE, dm), lambda i: (i, 0)),
        out_shape=jax.ShapeDtypeStruct((n, dm), F32),
        compiler_params=pltpu.CompilerParams(dimension_semantics=("arbitrary",), vmem_limit_bytes=VMEM_LIMIT),
        name="rwkv_post",
    )(o2, rvkg, rvkg, rvkg, kda, kda, s, ln_w[None], ln_b[None], r_k[None], gm, w_out.astype(BF16))


def _hg_scan_kernel(q_ref, v_ref, lf_ref, o_ref, st_scr, *, heads):
    C = q_ref.shape[0]
    d = pl.program_id(0)
    nsub = C // HG_SUB

    @pl.when(pl.program_id(2) == 0)
    def _():
        st_scr[...] = jnp.zeros_like(st_scr)

    def body(reverse):
        last = 0 if reverse else C - 1
        hs = range(heads)
        sls = [slice(h * LANES, (h + 1) * LANES) for h in hs]
        g = [lf_ref[:, sls[h]] for h in hs]
        b = [_cumsum_rows(g[h], reverse) for h in hs]
        k = [-jnp.tanh(0.5 * g[h]) * (jnp.exp(g[h]) + 1.0) for h in hs]
        o_inter = [_dot(q_ref[:, sls[h]].astype(F32) * jnp.exp(b[h]), st_scr[h], NT) for h in hs]
        parts = [[None] * nsub for _ in hs]
        for i in range(nsub):
            r0 = i * HG_SUB
            lo, hi = (r0, C) if reverse else (0, r0 + HG_SUB)
            first = r0 + HG_SUB - 1 if reverse else r0
            row = lax.broadcasted_iota(jnp.int32, (HG_SUB, hi - lo), 0) + r0
            col = lax.broadcasted_iota(jnp.int32, (HG_SUB, hi - lo), 1) + lo
            keep = (col >= row) if reverse else (col <= row)
            att = []
            for h in hs:
                rho = b[h][first:first + 1, :] - g[h][first:first + 1, :]
                qi = q_ref[r0:r0 + HG_SUB, sls[h]].astype(F32) * jnp.exp(b[h][r0:r0 + HG_SUB] - rho)
                ki = k[h][lo:hi] * jnp.exp(jnp.minimum(rho - b[h][lo:hi], HG_EXP_CLAMP))
                att.append(jnp.where(keep, _dot(qi, ki, NT), 0.0))
            for h in hs:
                parts[h][i] = _dot(att[h], v_ref[lo:hi, sls[h]])
        for h in hs:
            o_ref[0, :, sls[h]] = (o_inter[h] + jnp.concatenate(parts[h], axis=0)).astype(o_ref.dtype)
        upd = []
        for h in hs:
            b_last = b[h][last:last + 1, :]
            upd.append((jnp.exp(b_last),
                        _dot(v_ref[:, sls[h]].astype(F32).T, k[h] * jnp.exp(b_last - b[h]))))
        for h in hs:
            st_scr[h] = st_scr[h] * upd[h][0] + upd[h][1]

    @pl.when(d == 0)
    def _():
        body(False)

    @pl.when(d == 1)
    def _():
        body(True)


def hgrn_scan(z, logf, dm, bsz, t):
    n = z.shape[0]
    heads = dm // LANES
    C = HG_CHUNK
    nc, nctx = t // C, CTX_LEN // C

    def row(d, b, p):
        return b * nc + _scan_chunk_index(d, p, nctx, nc)

    return pl.pallas_call(
        functools.partial(_hg_scan_kernel, heads=heads),
        grid=(2, bsz, nc),
        in_specs=[pl.BlockSpec((C, dm), lambda d, b, p: (row(d, b, p), 0)),
                  pl.BlockSpec((C, dm), lambda d, b, p: (row(d, b, p), 1)),
                  pl.BlockSpec((C, dm), lambda d, b, p: (row(d, b, p), d))],
        out_specs=pl.BlockSpec((1, C, dm), lambda d, b, p: (d, row(d, b, p), 0)),
        out_shape=jax.ShapeDtypeStruct((2, n, dm), BF16),
        scratch_shapes=[pltpu.VMEM((heads, LANES, LANES), F32)],
        compiler_params=pltpu.CompilerParams(
            dimension_semantics=("arbitrary", "arbitrary", "arbitrary"), vmem_limit_bytes=VMEM_LIMIT),
        name="hgrn_scan",
    )(z, z, logf)


def _first_argmax(vals):
    best, idx = vals[0], jnp.zeros(vals[0].shape, jnp.int32)
    for i in range(1, len(vals)):
        better = vals[i] > best
        best = jnp.where(better, vals[i], best)
        idx = jnp.where(better, i, idx)
    return best, idx


def _router_kernel(s_ref, gain_ref, shift_ref, wt_ref, b_ref, h_ref, e_ref, g_ref, *, n_groups, top_k):
    n_experts = wt_ref.shape[0]
    per = n_experts // n_groups
    h = _norm_mod(s_ref[...], gain_ref[0], shift_ref[0])
    h_ref[...] = h.astype(h_ref.dtype)
    aff = jax.nn.sigmoid(_dot(wt_ref[...], h, NT, passes=3))
    sel = aff + b_ref[...]
    a = [aff[e:e + 1, :] for e in range(n_experts)]
    s = [sel[e:e + 1, :] for e in range(n_experts)]
    neg = jnp.full_like(s[0], -jnp.inf)
    scores = []
    for g in range(n_groups):
        grp = s[g * per:(g + 1) * per]
        m1, i1 = _first_argmax(grp)
        m2, _ = _first_argmax([jnp.where(i1 == j, neg, grp[j]) for j in range(per)])
        scores.append(m1 + m2)
    _, best = _first_argmax(scores)

    def in_best(rows):
        out = []
        for j in range(per):
            x = rows[j]
            for g in range(1, n_groups):
                x = jnp.where(best == g, rows[g * per + j], x)
            out.append(x)
        return out

    sb, ab = in_best(s), in_best(a)
    picked, chosen = [], []
    cand = sb
    for _ in range(top_k):
        _, i = _first_argmax(cand)
        c = ab[0]
        for j in range(1, per):
            c = jnp.where(i == j, ab[j], c)
        picked.append(i)
        chosen.append(c)
        cand = [jnp.where(i == j, neg, cand[j]) for j in range(per)]
    total = functools.reduce(jnp.add, chosen)
    for kk_ in range(top_k):
        e_ref[kk_:kk_ + 1, :] = best * per + picked[kk_]
        g_ref[kk_:kk_ + 1, :] = chosen[kk_] / total


def norm_route(s, gain, shift, geom, router_w, router_b):
    n, k = s.shape
    n_experts = router_w.shape[1]
    tm = ROW_TILE
    seg = _seg_map(*geom)
    return pl.pallas_call(
        functools.partial(_router_kernel, n_groups=N_GROUPS, top_k=TOP_K),
        grid=(n // tm,),
        in_specs=[pl.BlockSpec((tm, k), lambda i: (i, 0)),
                  pl.BlockSpec((1, 1, k), lambda i: (seg(i), 0, 0)),
                  pl.BlockSpec((1, 1, k), lambda i: (seg(i), 0, 0)),
                  pl.BlockSpec((n_experts, k), lambda i: (0, 0)),
                  pl.BlockSpec((n_experts, 1), lambda i: (0, 0))],
        out_specs=[pl.BlockSpec((tm, k), lambda i: (i, 0)),
                   pl.BlockSpec((TOP_K, tm), lambda i: (0, i)), pl.BlockSpec((TOP_K, tm), lambda i: (0, i))],
        out_shape=[jax.ShapeDtypeStruct((n, k), BF16),
                   jax.ShapeDtypeStruct((TOP_K, n), jnp.int32), jax.ShapeDtypeStruct((TOP_K, n), F32)],
        compiler_params=pltpu.CompilerParams(dimension_semantics=("arbitrary",), vmem_limit_bytes=VMEM_LIMIT),
        name="norm_route",
    )(s, gain, shift, router_w.T, router_b.reshape(n_experts, 1).astype(F32))


def _combine_kernel(y_ref, g_ref, s_ref, gm_ref, o_ref):
    dm = s_ref.shape[1]
    g = g_ref[...]
    y = sum(y_ref[:, k * dm:(k + 1) * dm].astype(F32) * g[:, k:k + 1] for k in range(g.shape[1]))
    o_ref[...] = s_ref[...] + gm_ref[0] * y


def moe_combine(y2, gate, s, gm, geom):
    n, dm = s.shape
    seg = _seg_map(*geom)
    return pl.pallas_call(
        _combine_kernel,
        grid=(n // ROW_TILE,),
        in_specs=[pl.BlockSpec((ROW_TILE, y2.shape[1]), lambda i: (i, 0)),
                  pl.BlockSpec((ROW_TILE, gate.shape[1]), lambda i: (i, 0)),
                  pl.BlockSpec((ROW_TILE, dm), lambda i: (i, 0)),
                  pl.BlockSpec((1, 1, dm), lambda i: (seg(i), 0, 0))],
        out_specs=pl.BlockSpec((ROW_TILE, dm), lambda i: (i, 0)),
        out_shape=jax.ShapeDtypeStruct((n, dm), F32),
        compiler_params=pltpu.CompilerParams(dimension_semantics=("arbitrary",), vmem_limit_bytes=VMEM_LIMIT),
        name="moe_combine",
    )(y2, gate, s, gm)


def _final_norm_kernel(s_ref, g_ref, o_ref):
    x = s_ref[...]
    o_ref[0] = x * lax.rsqrt(jnp.mean(x * x, axis=-1, keepdims=True) + NORM_EPS) * g_ref[...]


def final_norm(s, g, bsz, t, geom):
    dm = s.shape[1]
    tpb, nctx_t = geom
    return pl.pallas_call(
        _final_norm_kernel,
        grid=(bsz, tpb - nctx_t),
        in_specs=[pl.BlockSpec((ROW_TILE, dm), lambda b, i: (b * tpb + nctx_t + i, 0)),
                  pl.BlockSpec((1, dm), lambda b, i: (0, 0))],
        out_specs=pl.BlockSpec((1, ROW_TILE, dm), lambda b, i: (b, i, 0)),
        out_shape=jax.ShapeDtypeStruct((bsz, t - CTX_LEN, dm), F32),
        name="final_norm",
    )(s, g[None])


def _rank_kernel(e_ref, rank_ref, cnt_ref, carry_scr, *, n_experts):
    @pl.when(pl.program_id(0) == 0)
    def _():
        carry_scr[...] = jnp.zeros_like(carry_scr)

    bl = e_ref.shape[2]
    e_row = e_ref[0]
    sub = lax.broadcasted_iota(jnp.int32, (n_experts, bl), 0)
    onehot = (sub == e_row).astype(F32)
    ri = lax.broadcasted_iota(jnp.int32, (bl, bl), 0)
    ci = lax.broadcasted_iota(jnp.int32, (bl, bl), 1)
    earlier = (ri < ci).astype(BF16)
    cum = jnp.dot(onehot.astype(BF16), earlier, preferred_element_type=F32)
    carry = carry_scr[...]
    rank_ref[0] = jnp.sum(onehot * (cum + carry[:, :1]), axis=0, keepdims=True).astype(jnp.int32)
    carry = carry + jnp.sum(onehot, axis=1, keepdims=True)
    carry_scr[...] = carry
    cnt_ref[...] = carry.astype(jnp.int32)


def assignment_ranks(flat_e, n_experts):
    n_assign = flat_e.shape[0]
    bl = _pick_tile(n_assign, (512, 256, 128))
    nblk = n_assign // bl
    rank, cnt = pl.pallas_call(
        functools.partial(_rank_kernel, n_experts=n_experts),
        grid=(nblk,),
        in_specs=[pl.BlockSpec((1, 1, bl), lambda i: (i, 0, 0))],
        out_specs=[pl.BlockSpec((1, 1, bl), lambda i: (i, 0, 0)),
                   pl.BlockSpec((n_experts, LANES), lambda i: (0, 0))],
        out_shape=[jax.ShapeDtypeStruct((nblk, 1, bl), jnp.int32),
                   jax.ShapeDtypeStruct((n_experts, LANES), jnp.int32)],
        scratch_shapes=[pltpu.VMEM((n_experts, LANES), F32)],
        compiler_params=pltpu.CompilerParams(dimension_semantics=("arbitrary",)),
        name="assignment_ranks",
    )(flat_e.reshape(nblk, 1, bl))
    return rank.reshape(n_assign), cnt[:, 0]


def _ffn_kernel(be_ref, x_ref, w1_ref, w3_ref, w2_ref, o_ref):
    del be_ref
    x = x_ref[...].astype(BF16)
    a = jnp.dot(x, w1_ref[0], preferred_element_type=F32)
    b = jnp.dot(x, w3_ref[0], preferred_element_type=F32)
    hid = (a * jax.nn.sigmoid(a) * b).astype(BF16)
    o_ref[...] = jnp.dot(hid, w2_ref[0], preferred_element_type=F32).astype(o_ref.dtype)


def expert_ffn(xb, block_expert, w1, w3, w2):
    nrows, dm = xb.shape
    f = w1.shape[2]
    nb = nrows // MOE_BLOCK
    return pl.pallas_call(
        _ffn_kernel,
        grid_spec=pltpu.PrefetchScalarGridSpec(
            num_scalar_prefetch=1,
            grid=(nb,),
            in_specs=[pl.BlockSpec((MOE_BLOCK, dm), lambda i, be: (i, 0)),
                      pl.BlockSpec((1, dm, f), lambda i, be: (be[i], 0, 0)),
                      pl.BlockSpec((1, dm, f), lambda i, be: (be[i], 0, 0)),
                      pl.BlockSpec((1, f, dm), lambda i, be: (be[i], 0, 0))],
            out_specs=pl.BlockSpec((MOE_BLOCK, dm), lambda i, be: (i, 0))),
        out_shape=jax.ShapeDtypeStruct((nrows, dm), BF16),
        compiler_params=pltpu.CompilerParams(
            dimension_semantics=("arbitrary",), vmem_limit_bytes=VMEM_LIMIT),
        name="expert_ffn",
    )(block_expert, xb, w1.astype(BF16), w3.astype(BF16), w2.astype(BF16))


def _mlstm_layer(s, gain, shift, gate_mod, geom, bsz, t, w_in, w_gate, b_gate, conv, head_g, w_out):
    n, dm = s.shape
    heads = ML_HEADS
    z = norm_mod_mm(s, gain, shift, w_in, None, (None, None, None, "sigmoid"), geom)
    scale = jnp.concatenate([jnp.ones((dm,), F32), jnp.full((dm,), (dm // heads) ** -0.5, F32)])
    qk = conv_silu(z, conv, scale, 2 * dm, geom)
    ng = 4 * heads
    wg = jnp.pad(jnp.concatenate([w_gate[0], w_gate[1]], axis=1), ((0, 0), (0, LANES - ng)))
    bg = jnp.pad(jnp.concatenate([b_gate[0], b_gate[1]]), (0, LANES - ng))
    gates = norm_mod_mm(s, gain, shift, wg, bg, (None,), geom, out_dtype=F32)[:, :ng]
    gates = gates.reshape(bsz, t, 2, 2 * heads)
    gates = jnp.concatenate([gates[..., :heads], jax.nn.log_sigmoid(gates[..., heads:])], axis=-1)
    gc = jnp.moveaxis(gates, 2, 0).reshape(2, n, 2 * heads)
    gr = jnp.transpose(gates, (2, 0, 3, 1))
    h = mlstm_scan(qk, z, gc, gr, dm, bsz, t)
    return post_mm_residual(h, z, 3, s, head_g, gate_mod, w_out, heads, geom)


def _rwkv7_layer(s, gain, shift, gate_mod, geom, bsz, t, mu, w_rkv, w0, w1, w2, a0, a1, a2, g1, g2,
                 k_k, k_a, r_k, ln_w, ln_b, w_out):
    dm = s.shape[1]
    lw, kda, rvkg = rwkv_proj(s, gain, shift, geom, mu, w_rkv, w0, w1, w2, a0, a1, a2, g1, g2, k_k, k_a)
    o = rwkv_scan(lw, kda, rvkg, dm, bsz, t)
    return rwkv_post(o, rvkg, kda, s, ln_w, ln_b, r_k, gate_mod, w_out, geom)


def _hgrn2_layer(s, gain, shift, gate_mod, geom, bsz, t, layer_idx, w_in, w_f, b_f, lb_logits, head_g, w_out):
    dm = s.shape[1]
    z = norm_mod_mm(s, gain, shift, w_in, None, ("silu", None, "silu"), geom)
    p = jax.nn.softmax(lb_logits, axis=0)
    lb = jnp.cumsum(p, axis=0)[layer_idx] - p[0]
    aux = jnp.tile(jnp.stack([jnp.log(lb), jnp.log1p(-lb)]), (1, 2))
    log_f = norm_mod_mm(s, gain, shift, jnp.concatenate([w_f[0], w_f[1]], axis=1),
                        jnp.concatenate([b_f[0], b_f[1]]), ("logf", "logf"), geom, aux=aux, out_dtype=F32)
    o = hgrn_scan(z, log_f, dm, bsz, t)
    return post_mm_residual(o, z, 2, s, head_g, gate_mod, w_out, dm // HG_EXPAND, geom)


def _moe_layer(s, gain, shift, gate_mod, geom, router_w, router_b, w1, w3, w2):
    n_tok, d = s.shape
    n_experts = w1.shape[0]
    n_assign = n_tok * TOP_K
    h, e, g = norm_route(s, gain, shift, geom, router_w, router_b)
    flat_e = e.T.reshape(n_assign)
    rank, counts = assignment_ranks(flat_e, n_experts)
    padded = (counts + MOE_BLOCK - 1) // MOE_BLOCK * MOE_BLOCK
    end_pad = jnp.cumsum(padded)
    start_pad = end_pad - padded
    onehot = flat_e[:, None] == jnp.arange(n_experts, dtype=jnp.int32)[None, :]
    dest = jnp.sum(jnp.where(onehot, start_pad[None, :], 0), axis=1) + rank
    n_blocks = -(-n_assign // MOE_BLOCK) + n_experts
    token_of_slot = jnp.zeros((n_blocks * MOE_BLOCK,), jnp.int32).at[dest].set(
        jnp.arange(n_assign, dtype=jnp.int32) // TOP_K)
    block_expert = jnp.minimum(jnp.searchsorted(
        end_pad, jnp.arange(n_blocks, dtype=jnp.int32) * MOE_BLOCK, side='right'), n_experts - 1)
    xb = h[token_of_slot]
    yb = expert_ffn(xb, block_expert.astype(jnp.int32), w1, w3, w2)
    y2 = yb[dest].reshape(n_tok, TOP_K * d)
    return moe_combine(y2, g.T, s, gate_mod, geom)


def kernel(x, c, ctx, c_ctx, ada_w, ada_b, norm_mix, norm_ffn, norm_out, ml_w_in, ml_w_gate, ml_b_gate, ml_conv, ml_head_g, ml_w_out, rw_mu, rw_w_rkv, rw_w0, rw_w1, rw_w2, rw_a0, rw_a1, rw_a2, rw_g1, rw_g2, rw_k_k, rw_k_a, rw_r_k, rw_ln_w, rw_ln_b, rw_w_out, hg_w_in, hg_w_f, hg_b_f, hg_lb_logits, hg_head_g, hg_w_out, router_w, router_b, ex_w1, ex_w3, ex_w2):
    depth = ada_w.shape[0]
    bsz = x.shape[0]
    cond = jax.nn.silu(jnp.concatenate([c, c_ctx[None]], axis=0))
    cond = jnp.pad(cond, ((0, -(bsz + 1) % 8), (0, 0)))
    dm = x.shape[2]
    t = CTX_LEN + x.shape[1]
    n = bsz * t
    geom = (t // ROW_TILE, CTX_LEN // ROW_TILE)
    s = jnp.concatenate([ctx, x], axis=1).reshape(n, dm)
    for i in range(depth):
        mod = mm(cond, ada_w[i], bias=ada_b[i])
        mod_x = jnp.split(mod[:bsz, None, :], 6, axis=-1)
        mod_c = jnp.split(mod[bsz], 6, axis=-1)

        def table(idx):
            return jnp.stack([jnp.broadcast_to(mod_c[idx], (bsz, dm)), mod_x[idx][:, 0]], axis=1).reshape(2 * bsz, 1, dm)

        kind, j = i % N_MIXERS, i // N_MIXERS
        if kind == 2:
            s = _hgrn2_layer(s, norm_mix[i] * (1 + table(1)), table(0), table(2), geom, bsz, t, i,
                             hg_w_in[j], hg_w_f[j], hg_b_f[j], hg_lb_logits, hg_head_g[j], hg_w_out[j])
        elif kind == 0:
            s = _mlstm_layer(s, norm_mix[i] * (1 + table(1)), table(0), table(2), geom, bsz, t,
                             ml_w_in[j], ml_w_gate[j], ml_b_gate[j], ml_conv[j], ml_head_g[j], ml_w_out[j])
        else:
            s = _rwkv7_layer(s, norm_mix[i] * (1 + table(1)), table(0), table(2), geom, bsz, t,
                             rw_mu[j], rw_w_rkv[j], rw_w0[j], rw_w1[j], rw_w2[j], rw_a0[j],
                             rw_a1[j], rw_a2[j], rw_g1[j], rw_g2[j], rw_k_k[j], rw_k_a[j],
                             rw_r_k[j], rw_ln_w[j], rw_ln_b[j], rw_w_out[j])
        s = _moe_layer(s, norm_ffn[i] * (1 + table(4)), table(3), table(5), geom, router_w, router_b,
                       ex_w1[i], ex_w3[i], ex_w2[i])
    return final_norm(s, norm_out, bsz, t, geom)
```

```python
import functools

import jax
import jax.numpy as jnp
from jax import lax
from jax.experimental import pallas as pl
from jax.experimental.pallas import tpu as pltpu

F32 = jnp.float32
BF16 = jnp.bfloat16

GRID_W = 64
CTX_LEN = 256
N_MIXERS = 3
NORM_EPS = 1e-6
ML_HEADS = 8
RW_HEAD_DIM = 64
RW_GN_EPS = 64e-5
HG_EXPAND = 128
N_GROUPS = 4
TOP_K = 2
MOE_BLOCK = 256

LANES = 128
ML_CHUNK = 128
RW_CHUNK = 64
RW_PRE_CHUNKS = 4
HG_CHUNK = 64
HG_SUB = 16
HG_EXP_CLAMP = 80.0
VMEM_LIMIT = 48 * 1024 * 1024
VMEM_LIMIT_BIG = 56 * 1024 * 1024

NT = (((1,), (1,)), ((), ()))
NN = (((1,), (0,)), ((), ()))


def _dot(a, b, dims=NN, passes=1):
    a_hi = a.astype(BF16)
    b_hi = b.astype(BF16)
    out = lax.dot_general(a_hi, b_hi, dims, preferred_element_type=F32)
    if passes == 3:
        a_lo = (a - a_hi.astype(F32)).astype(BF16)
        b_lo = (b - b_hi.astype(F32)).astype(BF16)
        out = out + lax.dot_general(a_hi, b_lo, dims, preferred_element_type=F32)
        out = out + lax.dot_general(a_lo, b_hi, dims, preferred_element_type=F32)
    return out


def _cumsum_rows(x, reverse):
    n = x.shape[0]
    row = lax.broadcasted_iota(jnp.int32, x.shape, 0)
    s = 1
    while s < n:
        if reverse:
            x = x + jnp.where(row < n - s, pltpu.roll(x, n - s, axis=0), 0.0)
        else:
            x = x + jnp.where(row >= s, pltpu.roll(x, s, axis=0), 0.0)
        s *= 2
    return x


def _pick_tile(n, candidates):
    for c in candidates:
        if n % c == 0:
            return c
    raise ValueError(f"no tile for {n}")


def _scan_chunk_index(d, p, nctx, nc):
    rev = jnp.where(p < nctx, nctx - 1 - p, nc - 1 - (p - nctx))
    return jnp.where(d == 0, p, rev)


_ACTS = {
    None: lambda y: y,
    "sigmoid": jax.nn.sigmoid,
    "silu": lambda y: y * jax.nn.sigmoid(y),
    "tanh": jnp.tanh,
}


def _mm_kernel(x_ref, w_ref, b_ref, o_ref, *, act, precise):
    if precise:
        y = _dot(x_ref[...], w_ref[...], passes=3)
    else:
        y = jnp.dot(x_ref[...].astype(BF16), w_ref[...], preferred_element_type=F32)
    o_ref[...] = _ACTS[act](y + b_ref[...]).astype(o_ref.dtype)


def mm(x, w, bias=None, act=None, out_dtype=F32, precise=False):
    n, k = x.shape
    m = w.shape[1]
    tm = _pick_tile(n, (512, 256, 128, 64, 32, 16, 8))
    tn = m if m <= 1024 else _pick_tile(m, (1024, 512, 256, 128))
    if not precise:
        w = w.astype(BF16)
    if bias is None:
        bias = jnp.zeros((m,), F32)
    return pl.pallas_call(
        functools.partial(_mm_kernel, act=act, precise=precise),
        grid=(n // tm, m // tn),
        in_specs=[pl.BlockSpec((tm, k), lambda i, j: (i, 0)),
                  pl.BlockSpec((k, tn), lambda i, j: (0, j)),
                  pl.BlockSpec((1, tn), lambda i, j: (0, j))],
        out_specs=pl.BlockSpec((tm, tn), lambda i, j: (i, j)),
        out_shape=jax.ShapeDtypeStruct((n, m), out_dtype),
        compiler_params=pltpu.CompilerParams(vmem_limit_bytes=VMEM_LIMIT),
        name="mm",
    )(x, w, bias.reshape(1, m).astype(F32))


ROW_TILE = 256


def _log1p_exp_neg_abs(x):
    return jnp.log(1.0 + jnp.exp(-jnp.abs(x)))


def _log_sigmoid(y):
    return jnp.minimum(y, 0.0) - _log1p_exp_neg_abs(y)


def _softplus(x):
    return jnp.maximum(x, 0.0) + _log1p_exp_neg_abs(x)


def _logaddexp(a, b):
    return jnp.maximum(a, b) + _log1p_exp_neg_abs(a - b)


def _norm_mod(x, gain, shift):
    return x * lax.rsqrt(jnp.mean(x * x, axis=-1, keepdims=True) + NORM_EPS) * gain + shift


def _seg_map(tpb, nctx_t):
    def seg(i):
        return (i // tpb) * 2 + jnp.where(i % tpb < nctx_t, 0, 1)
    return seg


_EPILOGUES = {
    None: lambda y, aux: y,
    "sigmoid": lambda y, aux: jax.nn.sigmoid(y),
    "silu": lambda y, aux: y * jax.nn.sigmoid(y),
    "logf": lambda y, aux: _logaddexp(aux[0:1, :], aux[1:2, :] + _log_sigmoid(y)),
}


def _sub_tiles(n, most=4):
    return _pick_tile(n // ROW_TILE, tuple(range(most, 0, -1)))


def _nmm_kernel(s_ref, gain_ref, shift_ref, w_ref, b_ref, aux_ref, o_ref, h_scr, *, acts, sub, seg):
    j = pl.program_id(1)

    @pl.when(j == 0)
    def _():
        for k in range(sub):
            rows = pl.ds(k * ROW_TILE, ROW_TILE)
            sk = seg(pl.program_id(0) * sub + k)
            h_scr[rows, :] = _norm_mod(s_ref[rows, :], gain_ref[sk], shift_ref[sk]).astype(BF16)

    y = jnp.dot(h_scr[...], w_ref[...], preferred_element_type=F32) + b_ref[...]
    for act in sorted(set(acts), key=str):
        cols = [jj for jj, a in enumerate(acts) if a == act]
        if len(cols) == len(acts):
            o_ref[...] = _EPILOGUES[act](y, aux_ref[...]).astype(o_ref.dtype)
        else:
            @pl.when(functools.reduce(jnp.logical_or, [j == jj for jj in cols]))
            def _(act=act):
                o_ref[...] = _EPILOGUES[act](y, aux_ref[...]).astype(o_ref.dtype)


def norm_mod_mm(s, gain, shift, w, bias, acts, geom, aux=None, out_dtype=None):
    out_dtype = out_dtype or BF16
    n, k = s.shape
    m = w.shape[1]
    tn = m // len(acts)
    tpb, nctx_t = geom
    seg = _seg_map(tpb, nctx_t)
    sub = _sub_tiles(n)
    tm = sub * ROW_TILE
    if bias is None:
        bias = jnp.zeros((m,), F32)
    if aux is None:
        aux = jnp.zeros((2, m), F32)
    return pl.pallas_call(
        functools.partial(_nmm_kernel, acts=tuple(acts), sub=sub, seg=seg),
        grid=(n // tm, m // tn),
        in_specs=[pl.BlockSpec((tm, k), lambda i, j: (i, 0)),
                  pl.BlockSpec(gain.shape, lambda i, j: (0, 0, 0)),
                  pl.BlockSpec(shift.shape, lambda i, j: (0, 0, 0)),
                  pl.BlockSpec((k, tn), lambda i, j: (0, j)),
                  pl.BlockSpec((1, tn), lambda i, j: (0, j)),
                  pl.BlockSpec((2, tn), lambda i, j: (0, j))],
        out_specs=pl.BlockSpec((tm, tn), lambda i, j: (i, j)),
        out_shape=jax.ShapeDtypeStruct((n, m), out_dtype),
        scratch_shapes=[pltpu.VMEM((tm, k), BF16)],
        compiler_params=pltpu.CompilerParams(
            dimension_semantics=("arbitrary", "arbitrary"), vmem_limit_bytes=VMEM_LIMIT),
        name="norm_mod_mm",
    )(s, gain, shift, w.astype(BF16), bias.reshape(1, m).astype(F32), aux.astype(F32))


def _gated_residual_store(o_ref, s_ref, gm_ref, y, sub, seg):
    for k in range(sub):
        rows = pl.ds(k * ROW_TILE, ROW_TILE)
        gm = gm_ref[seg(pl.program_id(0) * sub + k)]
        o_ref[rows, :] = s_ref[rows, :] + gm * y[k * ROW_TILE:(k + 1) * ROW_TILE]


def _post_kernel(h_ref, g_ref, s_ref, hg_ref, gm_ref, w_ref, o_ref, *, heads, sub, seg):
    x = h_ref[0].astype(F32) + h_ref[1].astype(F32)
    hd = x.shape[1] // heads
    parts = []
    for h in range(heads):
        xh = x[:, h * hd:(h + 1) * hd]
        parts.append(xh * lax.rsqrt(jnp.mean(xh * xh, axis=-1, keepdims=True) + NORM_EPS))
    y = (jnp.concatenate(parts, axis=1) * hg_ref[...] * g_ref[...].astype(F32)).astype(BF16)
    _gated_residual_store(o_ref, s_ref, gm_ref, jnp.dot(y, w_ref[...], preferred_element_type=F32), sub, seg)


def post_mm_residual(h2, gate_arr, gate_block, s, head_g, gm, w_out, heads, geom):
    n, dm = s.shape
    seg = _seg_map(*geom)
    sub = _sub_tiles(n, most=2)
    tm = sub * ROW_TILE
    return pl.pallas_call(
        functools.partial(_post_kernel, heads=heads, sub=sub, seg=seg),
        grid=(n // tm,),
        in_specs=[pl.BlockSpec((2, tm, dm), lambda i: (0, i, 0)),
                  pl.BlockSpec((tm, dm), lambda i: (i, gate_block)),
                  pl.BlockSpec((tm, dm), lambda i: (i, 0)),
                  pl.BlockSpec((1, dm), lambda i: (0, 0)),
                  pl.BlockSpec(gm.shape, lambda i: (0, 0, 0)),
                  pl.BlockSpec((dm, dm), lambda i: (0, 0))],
        out_specs=pl.BlockSpec((tm, dm), lambda i: (i, 0)),
        out_shape=jax.ShapeDtypeStruct((n, dm), F32),
        compiler_params=pltpu.CompilerParams(dimension_semantics=("arbitrary",), vmem_limit_bytes=VMEM_LIMIT),
        name="post_mm_residual",
    )(h2, gate_arr, s, head_g.reshape(1, dm), gm, w_out.astype(BF16))


CONV_COLS = 512


def _conv_kernel(cur_ref, up_ref, dn_ref, w_ref, sc_ref, o_ref, *, tpb, nctx_t):
    ti = pl.program_id(0) % tpb
    is_ctx = ti < nctx_t
    no_up = jnp.logical_or(is_ctx, ti == nctx_t)
    no_dn = jnp.logical_or(is_ctx, ti == tpb - 1)
    x = cur_ref[...].astype(F32)
    up = jnp.where(no_up, 0.0, up_ref[...].astype(F32))
    dn = jnp.where(no_dn, 0.0, dn_ref[...].astype(F32))
    ext = jnp.concatenate([up, x, dn], axis=0)
    nr = ext.shape[0]
    ext_m = pltpu.roll(ext, 1, axis=0)
    ext_p = pltpu.roll(ext, nr - 1, axis=0)
    tpos = lax.broadcasted_iota(jnp.int32, (ROW_TILE, 1), 0)
    col = tpos % GRID_W
    left_ok = jnp.where(is_ctx, (tpos > 0).astype(F32), (col > 0).astype(F32))
    right_ok = jnp.where(is_ctx, (tpos < ROW_TILE - 1).astype(F32), (col < GRID_W - 1).astype(F32))
    vert = jnp.where(is_ctx, 0.0, 1.0)
    w = w_ref[...]
    acc = None
    for dr in (-1, 0, 1):
        base = GRID_W * (1 + dr)
        r3 = 3 * (dr + 1)
        term = (ext[base:base + ROW_TILE] * w[r3 + 1:r3 + 2]
                + ext_m[base:base + ROW_TILE] * w[r3:r3 + 1] * left_ok
                + ext_p[base:base + ROW_TILE] * w[r3 + 2:r3 + 3] * right_ok)
        if dr != 0:
            term = term * vert
        acc = term if acc is None else acc + term
    o_ref[...] = (acc * jax.nn.sigmoid(acc) * sc_ref[...]).astype(o_ref.dtype)


def conv_silu(z, conv_w, scale, width, geom):
    n = z.shape[0]
    tpb, nctx_t = geom
    assert nctx_t == 1 and ROW_TILE % GRID_W == 0
    hb = ROW_TILE // GRID_W
    last = n // GRID_W - 1
    return pl.pallas_call(
        functools.partial(_conv_kernel, tpb=tpb, nctx_t=nctx_t),
        grid=(n // ROW_TILE, width // CONV_COLS),
        in_specs=[pl.BlockSpec((ROW_TILE, CONV_COLS), lambda i, c: (i, c)),
                  pl.BlockSpec((GRID_W, CONV_COLS), lambda i, c: (jnp.maximum(i * hb - 1, 0), c)),
                  pl.BlockSpec((GRID_W, CONV_COLS), lambda i, c: (jnp.minimum((i + 1) * hb, last), c)),
                  pl.BlockSpec((9, CONV_COLS), lambda i, c: (0, c)),
                  pl.BlockSpec((1, CONV_COLS), lambda i, c: (0, c))],
        out_specs=pl.BlockSpec((ROW_TILE, CONV_COLS), lambda i, c: (i, c)),
        out_shape=jax.ShapeDtypeStruct((n, width), BF16),
        compiler_params=pltpu.CompilerParams(
            dimension_semantics=("arbitrary", "arbitrary"), vmem_limit_bytes=VMEM_LIMIT),
        name="conv_silu",
    )(z, z, z, conv_w.reshape(9, width).astype(F32), scale.reshape(1, width).astype(F32))


def _ml_scan_kernel(q_ref, k_ref, v_ref, gc_ref, gr_ref, o_ref, ct_scr, n_scr, m_scr, *, heads):
    L = q_ref.shape[0]
    d = pl.program_id(0)

    @pl.when(pl.program_id(2) == 0)
    def _():
        ct_scr[...] = jnp.zeros_like(ct_scr)
        n_scr[...] = jnp.zeros_like(n_scr)
        m_scr[...] = jnp.zeros_like(m_scr)

    row = lax.broadcasted_iota(jnp.int32, (L, L), 0)
    col = lax.broadcasted_iota(jnp.int32, (L, L), 1)

    def body(reverse):
        incl = (col >= row) if reverse else (col <= row)
        incl_t = (row >= col) if reverse else (row <= col)
        last = 0 if reverse else L - 1
        hs = range(heads)
        sls = [slice(h * LANES, (h + 1) * LANES) for h in hs]
        qk = [_dot(q_ref[:, sls[h]], k_ref[:, sls[h]], NT) for h in hs]
        qc = [_dot(q_ref[:, sls[h]], ct_scr[h]) for h in hs]
        stats = []
        for h in hs:
            ig_col = gc_ref[0, :, h:h + 1]
            lf_col = gc_ref[0, :, heads + h:heads + h + 1]
            ig_row = gr_ref[0, 0, h:h + 1, :]
            lf_row = gr_ref[0, 0, heads + h:heads + h + 1, :]
            b_col = jnp.sum(jnp.where(incl, lf_row, 0.0), axis=1, keepdims=True)
            b_row = jnp.sum(jnp.where(incl_t, lf_col, 0.0), axis=0, keepdims=True)
            m_prev = m_scr[h:h + 1, 0:1]
            dmat = jnp.where(incl, b_col - b_row + ig_row, -jnp.inf)
            inter = b_col + m_prev
            m_t = jnp.maximum(inter, jnp.max(dmat, axis=1, keepdims=True))
            b_last = b_col[last:last + 1, :]
            m_new = jnp.maximum(b_last + m_prev, jnp.max(b_last - b_row + ig_row, axis=1, keepdims=True))
            w_k = jnp.exp(b_last - b_col + ig_col - m_new)
            w_prev = jnp.exp(b_last + m_prev - m_new)
            stats.append((jnp.exp(dmat - m_t), jnp.exp(inter - m_t), jnp.exp(-m_t), w_k, w_prev, m_new))
        kv = [_dot(k_ref[:, sls[h]].astype(F32).T, stats[h][3] * v_ref[:, sls[h]].astype(F32)) for h in hs]
        s = [qk[h] * stats[h][0] for h in hs]
        sv = [_dot(s[h], v_ref[:, sls[h]]) for h in hs]
        for h in hs:
            _, w_inter, floor, w_k, w_prev, m_new = stats[h]
            n_row = n_scr[h:h + 1, :]
            num = sv[h] + w_inter * qc[h]
            den = (jnp.sum(s[h], axis=1, keepdims=True)
                   + w_inter * jnp.sum(q_ref[:, sls[h]].astype(F32) * n_row, axis=1, keepdims=True))
            o_ref[0, :, sls[h]] = (num / jnp.maximum(jnp.abs(den), floor)).astype(o_ref.dtype)
            ct_scr[h] = w_prev * ct_scr[h] + kv[h]
            n_scr[h:h + 1, :] = w_prev * n_row + jnp.sum(w_k * k_ref[:, sls[h]].astype(F32), axis=0, keepdims=True)
            m_scr[h:h + 1, :] = jnp.broadcast_to(m_new, (1, LANES))

    @pl.when(d == 0)
    def _():
        body(False)

    @pl.when(d == 1)
    def _():
        body(True)


def mlstm_scan(qk, z, gc, gr, dm, bsz, t):
    n = qk.shape[0]
    heads = dm // LANES
    L = ML_CHUNK
    nc, nctx = t // L, CTX_LEN // L

    def row(d, b, p):
        return b * nc + _scan_chunk_index(d, p, nctx, nc)

    return pl.pallas_call(
        functools.partial(_ml_scan_kernel, heads=heads),
        grid=(2, bsz, nc),
        in_specs=[pl.BlockSpec((L, dm), lambda d, b, p: (row(d, b, p), 0)),
                  pl.BlockSpec((L, dm), lambda d, b, p: (row(d, b, p), 1)),
                  pl.BlockSpec((L, dm), lambda d, b, p: (row(d, b, p), 2)),
                  pl.BlockSpec((1, L, 2 * heads), lambda d, b, p: (d, row(d, b, p), 0)),
                  pl.BlockSpec((1, 1, 2 * heads, L),
                               lambda d, b, p: (d, b, 0, _scan_chunk_index(d, p, nctx, nc)))],
        out_specs=pl.BlockSpec((1, L, dm), lambda d, b, p: (d, row(d, b, p), 0)),
        out_shape=jax.ShapeDtypeStruct((2, n, dm), BF16),
        scratch_shapes=[pltpu.VMEM((heads, LANES, LANES), F32), pltpu.VMEM((heads, LANES), F32),
                        pltpu.VMEM((heads, LANES), F32)],
        compiler_params=pltpu.CompilerParams(
            dimension_semantics=("arbitrary", "arbitrary", "arbitrary"), vmem_limit_bytes=VMEM_LIMIT),
        name="mlstm_scan",
    )(qk, qk, z, gc, gr)


RW_STATE_PASSES = 3


def _rw_pre_kernel(lw0_ref, lw1_ref, kd0_ref, kd1_ref, a0_ref, a1_ref, r_ref, v_ref, kk_ref,
                   rdp_ref, o0_ref, m_ref, ha_ref, *, nchunk):
    L = RW_CHUNK
    half = LANES // 2
    row = lax.broadcasted_iota(jnp.int32, (L, LANES), 0)
    col = lax.broadcasted_iota(jnp.int32, (L, LANES), 1) % half
    eye2 = (row == col).astype(F32)
    lane = lax.broadcasted_iota(jnp.int32, (1, LANES), 1)
    m0 = (lane < half).astype(F32)
    m1 = (lane >= half).astype(F32)
    r2 = lax.broadcasted_iota(jnp.int32, (LANES, LANES), 0)
    c2 = lax.broadcasted_iota(jnp.int32, (LANES, LANES), 1)
    same_head = (r2 // half) == (c2 // half)

    def stack(x):
        return jnp.concatenate([x * m0, x * m1], axis=0)

    chains = [(d, c) for c in range(nchunk) for d in range(2)]
    st = {}
    for d, c in chains:
        reverse = d == 1
        sl = pl.ds(c * L, L)
        lw = (lw0_ref, lw1_ref)[d][sl, :]
        k = (kd0_ref, kd1_ref)[d][sl, :].astype(F32)
        kk = kk_ref[sl, :].astype(F32)
        akk = kk * (a0_ref, a1_ref)[d][sl, :].astype(F32)
        g = _cumsum_rows(lw, reverse)
        ieg = jnp.exp(-g)
        g_last = g[0:1] if reverse else g[L - 1:L]
        dl = jnp.exp(g_last - g)
        st[d, c] = dict(kd=kk * jnp.exp(g - lw), rd=r_ref[sl, :].astype(F32) * jnp.exp(g), ai=akk * ieg,
                        ki=k * ieg, ad=akk * dl, kdd=k * dl, eg_last=jnp.exp(g_last),
                        v=v_ref[sl, :].astype(F32))
    for d, c in chains:
        s = st[d, c]
        reverse = d == 1
        incl = (col >= row) if reverse else (col <= row)
        strict = (col > row) if reverse else (col < row)
        x = jnp.concatenate([s["kd"], s["rd"]], axis=0)
        rhs = jnp.concatenate([stack(s["ai"]), stack(s["ki"])], axis=0)
        sc = _dot(x, rhs, NT)
        s["a_ab"] = jnp.where(strict, sc[:L, :LANES], 0.0)
        s["a_ak"] = jnp.where(strict, sc[:L, LANES:], 0.0)
        s["b_ra"] = jnp.where(incl, sc[L:, :LANES], 0.0)
        s["b_rk"] = jnp.where(incl, sc[L:, LANES:], 0.0)
        s["tinv"] = eye2 - s["a_ab"]
        s["pw"] = s["a_ab"]
    span = 2
    while span < L:
        for d, c in chains:
            s = st[d, c]
            s["pw"] = _dot(s["pw"], stack(s["pw"]))
        for d, c in chains:
            s = st[d, c]
            s["tinv"] = _dot(s["tinv"], stack(eye2 + s["pw"]))
        span *= 2
    for d, c in chains:
        s = st[d, c]
        s["w"] = -_dot(s["tinv"], stack(s["a_ak"]))
        s["kdp"] = _dot(s["tinv"], stack(s["kd"]))
    for d, c in chains:
        s = st[d, c]
        s["vst"] = stack(s["v"])
        s["u0"] = _dot(s["w"], s["vst"])
    for d, c in chains:
        s = st[d, c]
        sl = pl.ds(c * L, L)
        lhs = jnp.concatenate([s["b_ra"], s["b_rk"]], axis=1)
        rhs = jnp.concatenate([stack(s["u0"]), s["vst"]], axis=0)
        o0_ref[d, sl, :] = _dot(lhs, rhs).astype(o0_ref.dtype)
        rdp_ref[d, sl, :] = (s["rd"] - _dot(s["b_ra"], stack(s["kdp"]))).astype(rdp_ref.dtype)
        diag = jnp.where(r2 == c2, s["eg_last"], 0.0)
        m_ref[d, c, 0] = jnp.where(same_head, diag - _dot(s["ad"].T, s["kdp"]), 0.0)
        at = jnp.concatenate([s["ad"], s["kdd"]], axis=0).T
        ha_ref[d, c, 0] = jnp.where(same_head, _dot(at, jnp.concatenate([s["u0"], s["v"]], axis=0)), 0.0)


def _rw_scan_kernel(rdp_ref, o0_ref, m_ref, ha_ref, o_ref, h_scr, *, pairs):
    @pl.when(pl.program_id(2) == 0)
    def _():
        h_scr[...] = jnp.zeros_like(h_scr)

    sls = [slice(p * LANES, (p + 1) * LANES) for p in range(pairs)]
    outs = [_dot(rdp_ref[0, :, sls[p]].astype(F32), h_scr[p], passes=RW_STATE_PASSES) for p in range(pairs)]
    nxt = [_dot(m_ref[0, 0, p], h_scr[p], passes=RW_STATE_PASSES) for p in range(pairs)]
    for p in range(pairs):
        o_ref[0, :, sls[p]] = (outs[p] + o0_ref[0, :, sls[p]].astype(F32)).astype(o_ref.dtype)
        h_scr[p] = nxt[p] + ha_ref[0, 0, p]


def rwkv_scan(lw, kda, rvkg, dm, bsz, t):
    n = lw.shape[0]
    pairs = dm // LANES
    L = RW_CHUNK
    nc, nctx = t // L, CTX_LEN // L
    nct = n // L
    nchunk = RW_PRE_CHUNKS
    tb = nchunk * L

    def cspec(block):
        return pl.BlockSpec((tb, LANES), lambda i, p: (i, block * pairs + p))

    dspec = pl.BlockSpec((2, tb, LANES), lambda i, p: (0, i, p))
    mspec = pl.BlockSpec((2, nchunk, 1, LANES, LANES), lambda i, p: (0, i, p, 0, 0))
    rdp, o0, mm_, ha = pl.pallas_call(
        functools.partial(_rw_pre_kernel, nchunk=nchunk),
        grid=(n // tb, pairs),
        in_specs=[cspec(0), cspec(1), cspec(0), cspec(1), cspec(2), cspec(3), cspec(0), cspec(1), cspec(2)],
        out_specs=[dspec, dspec, mspec, mspec],
        out_shape=[jax.ShapeDtypeStruct((2, n, dm), BF16), jax.ShapeDtypeStruct((2, n, dm), BF16),
                   jax.ShapeDtypeStruct((2, nct, pairs, LANES, LANES), F32),
                   jax.ShapeDtypeStruct((2, nct, pairs, LANES, LANES), F32)],
        compiler_params=pltpu.CompilerParams(
            dimension_semantics=("arbitrary", "arbitrary"), vmem_limit_bytes=VMEM_LIMIT),
        name="rwkv_pre",
    )(lw, lw, kda, kda, kda, kda, rvkg, rvkg, rvkg)

    def chunk(d, b, p):
        return b * nc + _scan_chunk_index(d, p, nctx, nc)

    rspec = pl.BlockSpec((1, L, dm), lambda d, b, p: (d, chunk(d, b, p), 0))
    sspec = pl.BlockSpec((1, 1, pairs, LANES, LANES), lambda d, b, p: (d, chunk(d, b, p), 0, 0, 0))
    return pl.pallas_call(
        functools.partial(_rw_scan_kernel, pairs=pairs),
        grid=(2, bsz, nc),
        in_specs=[rspec, rspec, sspec, sspec],
        out_specs=rspec,
        out_shape=jax.ShapeDtypeStruct((2, n, dm), BF16),
        scratch_shapes=[pltpu.VMEM((pairs, LANES, LANES), F32)],
        compiler_params=pltpu.CompilerParams(
            dimension_semantics=("arbitrary", "arbitrary", "arbitrary"), vmem_limit_bytes=VMEM_LIMIT),
        name="rwkv_scan",
    )(rdp, o0, mm_, ha)


HALO_ROWS = 8


def _group_sum(x, width):
    r = lax.broadcasted_iota(jnp.int32, (LANES, LANES), 0) // width
    c = lax.broadcasted_iota(jnp.int32, (LANES, LANES), 1) // width
    ones = (r == c).astype(BF16)
    hi = x.astype(BF16)
    lo = (x - hi.astype(F32)).astype(BF16)
    parts = []
    for j in range(x.shape[1] // LANES):
        sl = slice(j * LANES, (j + 1) * LANES)
        parts.append(jnp.dot(hi[:, sl], ones, preferred_element_type=F32)
                     + jnp.dot(lo[:, sl], ones, preferred_element_type=F32))
    return jnp.concatenate(parts, axis=1)


def _rw_proj_kernel(s_ref, up_ref, dn_ref, gain_ref, shift_ref, mu_ref, wrkv_ref, w1_ref, w2_ref, w0_ref,
                    a1_ref, a2_ref, a0_ref, g1_ref, g2_ref, kk_ref, ka_ref,
                    lw_ref, kda_ref, rvkg_ref, *, tpb, nctx_t):
    tm, dm = s_ref.shape
    ti = pl.program_id(0) % tpb
    is_ctx = ti < nctx_t
    has_up = jnp.logical_not(jnp.logical_or(is_ctx, ti == nctx_t))
    has_dn = jnp.logical_not(jnp.logical_or(is_ctx, ti == tpb - 1))
    gain = gain_ref[0]
    shift = shift_ref[0]
    u = _norm_mod(s_ref[...], gain, shift)
    u_up = jnp.where(has_up, _norm_mod(up_ref[HALO_ROWS - 1:HALO_ROWS, :], gain, shift), 0.0)
    u_dn = jnp.where(has_dn, _norm_mod(dn_ref[0:1, :], gain, shift), 0.0)
    row = lax.broadcasted_iota(jnp.int32, (tm, 1), 0)
    u_m = jnp.where(row == 0, u_up, pltpu.roll(u, 1, axis=0))
    u_p = jnp.where(row == tm - 1, u_dn, pltpu.roll(u, tm - 1, axis=0))
    du = 0.5 * (u_m + u_p) - u
    mu = mu_ref[...]

    def mix(i):
        return (u + du * mu[i:i + 1]).astype(BF16)

    def dot(a, b):
        return jnp.dot(a.astype(BF16), b, preferred_element_type=F32)

    r = dot(mix(0), wrkv_ref[0])
    k = dot(mix(1), wrkv_ref[1])
    v = dot(mix(2), wrkv_ref[2])
    w_pre = dot(jnp.tanh(dot(mix(3), w1_ref[...])), w2_ref[...]) + w0_ref[...]
    lw_ref[...] = -jnp.exp(-_softplus(-w_pre) - 0.5)
    a = jax.nn.sigmoid(dot(dot(mix(4), a1_ref[...]), a2_ref[...]) + a0_ref[...])
    g = dot(jax.nn.sigmoid(dot(mix(5), g1_ref[...])), g2_ref[...])
    kk = k * kk_ref[...]
    kk = kk * lax.rsqrt(jnp.maximum(_group_sum(kk * kk, RW_HEAD_DIM), 1e-24))
    ka = ka_ref[...]
    for d in range(2):
        kda_ref[:, d * dm:(d + 1) * dm] = (k * (1.0 + (a[:, d * dm:(d + 1) * dm] - 1.0) * ka)).astype(kda_ref.dtype)
    kda_ref[:, 2 * dm:] = a.astype(kda_ref.dtype)
    for j, val in enumerate((r, v, kk, g)):
        rvkg_ref[:, j * dm:(j + 1) * dm] = val.astype(rvkg_ref.dtype)


def rwkv_proj(s, gain, shift, geom, mu, w_rkv, w0, w1, w2, a0, a1, a2, g1, g2, k_k, k_a):
    n, dm = s.shape
    tpb, nctx_t = geom
    assert nctx_t == 1
    seg = _seg_map(tpb, nctx_t)
    hb = ROW_TILE // HALO_ROWS
    last = n // HALO_ROWS - 1
    lora = w1.shape[2]

    def blockdiag(w):
        z = jnp.zeros_like(w[0])
        return jnp.concatenate([jnp.concatenate([w[0], z], axis=1), jnp.concatenate([z, w[1]], axis=1)], axis=0)

    consts = [jnp.pad(mu, ((0, HALO_ROWS - mu.shape[0]), (0, 0))), w_rkv.astype(BF16),
              jnp.concatenate([w1[0], w1[1]], axis=1).astype(BF16), blockdiag(w2).astype(BF16),
              jnp.concatenate([w0[0], w0[1]])[None],
              jnp.concatenate([a1[0], a1[1]], axis=1).astype(BF16), blockdiag(a2).astype(BF16),
              jnp.concatenate([a0[0], a0[1]])[None],
              g1.astype(BF16), g2.astype(BF16), k_k[None], k_a[None]]

    def const_spec(x):
        nd = x.ndim
        return pl.BlockSpec(x.shape, lambda i: (0,) * nd)

    return pl.pallas_call(
        functools.partial(_rw_proj_kernel, tpb=tpb, nctx_t=nctx_t),
        grid=(n // ROW_TILE,),
        in_specs=[pl.BlockSpec((ROW_TILE, dm), lambda i: (i, 0)),
                  pl.BlockSpec((HALO_ROWS, dm), lambda i: (jnp.maximum(i * hb - 1, 0), 0)),
                  pl.BlockSpec((HALO_ROWS, dm), lambda i: (jnp.minimum((i + 1) * hb, last), 0)),
                  pl.BlockSpec((1, 1, dm), lambda i: (seg(i), 0, 0)),
                  pl.BlockSpec((1, 1, dm), lambda i: (seg(i), 0, 0))] + [const_spec(x) for x in consts],
        out_specs=[pl.BlockSpec((ROW_TILE, 2 * dm), lambda i: (i, 0)),
                   pl.BlockSpec((ROW_TILE, 4 * dm), lambda i: (i, 0)),
                   pl.BlockSpec((ROW_TILE, 4 * dm), lambda i: (i, 0))],
        out_shape=[jax.ShapeDtypeStruct((n, 2 * dm), F32), jax.ShapeDtypeStruct((n, 4 * dm), BF16),
                   jax.ShapeDtypeStruct((n, 4 * dm), BF16)],
        compiler_params=pltpu.CompilerParams(dimension_semantics=("arbitrary",), vmem_limit_bytes=VMEM_LIMIT_BIG),
        name="rwkv_proj",
    )(s, s, s, gain, shift, *consts)


def _rw_post_kernel(o_ref, r_ref, v_ref, g_ref, k0_ref, k1_ref, s_ref, lnw_ref, lnb_ref, rk_ref, gm_ref, w_ref,
                    out_ref, *, sub, seg):
    o = o_ref[0].astype(F32) + o_ref[1].astype(F32)
    inv = 1.0 / RW_HEAD_DIM
    mean = _group_sum(o, RW_HEAD_DIM) * inv
    oc = o - mean
    var = _group_sum(oc * oc, RW_HEAD_DIM) * inv
    xn = oc * lax.rsqrt(var + RW_GN_EPS) * lnw_ref[...] + lnb_ref[...]
    r = r_ref[...].astype(F32)
    ksum = k0_ref[...].astype(F32) + k1_ref[...].astype(F32)
    bonus = _group_sum(r * ksum * rk_ref[...], RW_HEAD_DIM) * v_ref[...].astype(F32)
    y = ((xn + bonus) * g_ref[...].astype(F32)).astype(BF16)
    _gated_residual_store(out_ref, s_ref, gm_ref, jnp.dot(y, w_ref[...], preferred_element_type=F32), sub, seg)


def rwkv_post(o2, rvkg, kda, s, ln_w, ln_b, r_k, gm, w_out, geom):
    n, dm = s.shape
    seg = _seg_map(*geom)
    sub = _sub_tiles(n, most=2)
    tm = sub * ROW_TILE

    def col(block):
        return pl.BlockSpec((tm, dm), lambda i: (i, block))

    vec = pl.BlockSpec((1, dm), lambda i: (0, 0))
    return pl.pallas_call(
        functools.partial(_rw_post_kernel, sub=sub, seg=seg),
        grid=(n // tm,),
        in_specs=[pl.BlockSpec((2, tm, dm), lambda i: (0, i, 0)), col(0), col(1), col(3), col(0), col(1),
                  col(0), vec, vec, vec, pl.BlockSpec(gm.shape, lambda i: (0, 0, 0)),
                  pl.BlockSpec((dm, dm), lambda i: (0, 0))],
        out_specs=pl.BlockSpec((tm, dm), lambda i: (i, 0)),
        out_shape=jax.ShapeDtypeStruct((n, dm), F32),
        compiler_params=pltpu.CompilerParams(dimension_semantics=("arbitrary",), vmem_limit_bytes=VMEM_LIMIT),
        name="rwkv_post",
    )(o2, rvkg, rvkg, rvkg, kda, kda, s, ln_w[None], ln_b[None], r_k[None], gm, w_out.astype(BF16))


def _hg_scan_kernel(q_ref, v_ref, lf_ref, o_ref, st_scr, *, heads):
    C = q_ref.shape[0]
    d = pl.program_id(0)
    nsub = C // HG_SUB

    @pl.when(pl.program_id(2) == 0)
    def _():
        st_scr[...] = jnp.zeros_like(st_scr)

    def body(reverse):
        last = 0 if reverse else C - 1
        hs = range(heads)
        sls = [slice(h * LANES, (h + 1) * LANES) for h in hs]
        g = [lf_ref[:, sls[h]] for h in hs]
        b = [_cumsum_rows(g[h], reverse) for h in hs]
        k = [-jnp.tanh(0.5 * g[h]) * (jnp.exp(g[h]) + 1.0) for h in hs]
        o_inter = [_dot(q_ref[:, sls[h]].astype(F32) * jnp.exp(b[h]), st_scr[h], NT) for h in hs]
        parts = [[None] * nsub for _ in hs]
        for i in range(nsub):
            r0 = i * HG_SUB
            lo, hi = (r0, C) if reverse else (0, r0 + HG_SUB)
            first = r0 + HG_SUB - 1 if reverse else r0
            row = lax.broadcasted_iota(jnp.int32, (HG_SUB, hi - lo), 0) + r0
            col = lax.broadcasted_iota(jnp.int32, (HG_SUB, hi - lo), 1) + lo
            keep = (col >= row) if reverse else (col <= row)
            att = []
            for h in hs:
                rho = b[h][first:first + 1, :] - g[h][first:first + 1, :]
                qi = q_ref[r0:r0 + HG_SUB, sls[h]].astype(F32) * jnp.exp(b[h][r0:r0 + HG_SUB] - rho)
                ki = k[h][lo:hi] * jnp.exp(jnp.minimum(rho - b[h][lo:hi], HG_EXP_CLAMP))
                att.append(jnp.where(keep, _dot(qi, ki, NT), 0.0))
            for h in hs:
                parts[h][i] = _dot(att[h], v_ref[lo:hi, sls[h]])
        for h in hs:
            o_ref[0, :, sls[h]] = (o_inter[h] + jnp.concatenate(parts[h], axis=0)).astype(o_ref.dtype)
        upd = []
        for h in hs:
            b_last = b[h][last:last + 1, :]
            upd.append((jnp.exp(b_last),
                        _dot(v_ref[:, sls[h]].astype(F32).T, k[h] * jnp.exp(b_last - b[h]))))
        for h in hs:
            st_scr[h] = st_scr[h] * upd[h][0] + upd[h][1]

    @pl.when(d == 0)
    def _():
        body(False)

    @pl.when(d == 1)
    def _():
        body(True)


def hgrn_scan(z, logf, dm, bsz, t):
    n = z.shape[0]
    heads = dm // LANES
    C = HG_CHUNK
    nc, nctx = t // C, CTX_LEN // C

    def row(d, b, p):
        return b * nc + _scan_chunk_index(d, p, nctx, nc)

    return pl.pallas_call(
        functools.partial(_hg_scan_kernel, heads=heads),
        grid=(2, bsz, nc),
        in_specs=[pl.BlockSpec((C, dm), lambda d, b, p: (row(d, b, p), 0)),
                  pl.BlockSpec((C, dm), lambda d, b, p: (row(d, b, p), 1)),
                  pl.BlockSpec((C, dm), lambda d, b, p: (row(d, b, p), d))],
        out_specs=pl.BlockSpec((1, C, dm), lambda d, b, p: (d, row(d, b, p), 0)),
        out_shape=jax.ShapeDtypeStruct((2, n, dm), BF16),
        scratch_shapes=[pltpu.VMEM((heads, LANES, LANES), F32)],
        compiler_params=pltpu.CompilerParams(
            dimension_semantics=("arbitrary", "arbitrary", "arbitrary"), vmem_limit_bytes=VMEM_LIMIT),
        name="hgrn_scan",
    )(z, z, logf)


def _first_argmax(vals):
    best, idx = vals[0], jnp.zeros(vals[0].shape, jnp.int32)
    for i in range(1, len(vals)):
        better = vals[i] > best
        best = jnp.where(better, vals[i], best)
        idx = jnp.where(better, i, idx)
    return best, idx


def _router_kernel(s_ref, gain_ref, shift_ref, wt_ref, b_ref, h_ref, e_ref, g_ref, *, n_groups, top_k):
    n_experts = wt_ref.shape[0]
    per = n_experts // n_groups
    h = _norm_mod(s_ref[...], gain_ref[0], shift_ref[0])
    h_ref[...] = h.astype(h_ref.dtype)
    aff = jax.nn.sigmoid(_dot(wt_ref[...], h, NT, passes=3))
    sel = aff + b_ref[...]
    a = [aff[e:e + 1, :] for e in range(n_experts)]
    s = [sel[e:e + 1, :] for e in range(n_experts)]
    neg = jnp.full_like(s[0], -jnp.inf)
    scores = []
    for g in range(n_groups):
        grp = s[g * per:(g + 1) * per]
        m1, i1 = _first_argmax(grp)
        m2, _ = _first_argmax([jnp.where(i1 == j, neg, grp[j]) for j in range(per)])
        scores.append(m1 + m2)
    _, best = _first_argmax(scores)

    def in_best(rows):
        out = []
        for j in range(per):
            x = rows[j]
            for g in range(1, n_groups):
                x = jnp.where(best == g, rows[g * per + j], x)
            out.append(x)
        return out

    sb, ab = in_best(s), in_best(a)
    picked, chosen = [], []
    cand = sb
    for _ in range(top_k):
        _, i = _first_argmax(cand)
        c = ab[0]
        for j in range(1, per):
            c = jnp.where(i == j, ab[j], c)
        picked.append(i)
        chosen.append(c)
        cand = [jnp.where(i == j, neg, cand[j]) for j in range(per)]
    total = functools.reduce(jnp.add, chosen)
    for kk_ in range(top_k):
        e_ref[kk_:kk_ + 1, :] = best * per + picked[kk_]
        g_ref[kk_:kk_ + 1, :] = chosen[kk_] / total


def norm_route(s, gain, shift, geom, router_w, router_b):
    n, k = s.shape
    n_experts = router_w.shape[1]
    tm = ROW_TILE
    seg = _seg_map(*geom)
    return pl.pallas_call(
        functools.partial(_router_kernel, n_groups=N_GROUPS, top_k=TOP_K),
        grid=(n // tm,),
        in_specs=[pl.BlockSpec((tm, k), lambda i: (i, 0)),
                  pl.BlockSpec((1, 1, k), lambda i: (seg(i), 0, 0)),
                  pl.BlockSpec((1, 1, k), lambda i: (seg(i), 0, 0)),
                  pl.BlockSpec((n_experts, k), lambda i: (0, 0)),
                  pl.BlockSpec((n_experts, 1), lambda i: (0, 0))],
        out_specs=[pl.BlockSpec((tm, k), lambda i: (i, 0)),
                   pl.BlockSpec((TOP_K, tm), lambda i: (0, i)), pl.BlockSpec((TOP_K, tm), lambda i: (0, i))],
        out_shape=[jax.ShapeDtypeStruct((n, k), BF16),
                   jax.ShapeDtypeStruct((TOP_K, n), jnp.int32), jax.ShapeDtypeStruct((TOP_K, n), F32)],
        compiler_params=pltpu.CompilerParams(dimension_semantics=("arbitrary",), vmem_limit_bytes=VMEM_LIMIT),
        name="norm_route",
    )(s, gain, shift, router_w.T, router_b.reshape(n_experts, 1).astype(F32))


def _combine_kernel(*refs):
    *y_refs, g_ref, s_ref, gm_ref, o_ref = refs
    g = g_ref[...]
    y = sum(y_ref[...].astype(F32) * g[:, k:k + 1] for k, y_ref in enumerate(y_refs))
    o_ref[...] = s_ref[...] + gm_ref[0] * y


def moe_combine(ys, gate, s, gm, geom):
    n, dm = s.shape
    seg = _seg_map(*geom)
    row = pl.BlockSpec((ROW_TILE, dm), lambda i: (i, 0))
    return pl.pallas_call(
        _combine_kernel,
        grid=(n // ROW_TILE,),
        in_specs=[row] * len(ys) + [pl.BlockSpec((ROW_TILE, gate.shape[1]), lambda i: (i, 0)), row,
                                    pl.BlockSpec((1, 1, dm), lambda i: (seg(i), 0, 0))],
        out_specs=row,
        out_shape=jax.ShapeDtypeStruct((n, dm), F32),
        compiler_params=pltpu.CompilerParams(dimension_semantics=("arbitrary",), vmem_limit_bytes=VMEM_LIMIT),
        name="moe_combine",
    )(*ys, gate, s, gm)


def _final_norm_kernel(s_ref, g_ref, o_ref):
    x = s_ref[...]
    o_ref[0] = x * lax.rsqrt(jnp.mean(x * x, axis=-1, keepdims=True) + NORM_EPS) * g_ref[...]


def final_norm(s, g, bsz, t, geom):
    dm = s.shape[1]
    tpb, nctx_t = geom
    return pl.pallas_call(
        _final_norm_kernel,
        grid=(bsz, tpb - nctx_t),
        in_specs=[pl.BlockSpec((ROW_TILE, dm), lambda b, i: (b * tpb + nctx_t + i, 0)),
                  pl.BlockSpec((1, dm), lambda b, i: (0, 0))],
        out_specs=pl.BlockSpec((1, ROW_TILE, dm), lambda b, i: (b, i, 0)),
        out_shape=jax.ShapeDtypeStruct((bsz, t - CTX_LEN, dm), F32),
        name="final_norm",
    )(s, g[None])


def _rank_kernel(e_ref, rank_ref, cnt_ref, carry_scr, *, n_experts):
    @pl.when(pl.program_id(0) == 0)
    def _():
        carry_scr[...] = jnp.zeros_like(carry_scr)

    bl = e_ref.shape[2]
    e_row = e_ref[0]
    sub = lax.broadcasted_iota(jnp.int32, (n_experts, bl), 0)
    onehot = (sub == e_row).astype(F32)
    ri = lax.broadcasted_iota(jnp.int32, (bl, bl), 0)
    ci = lax.broadcasted_iota(jnp.int32, (bl, bl), 1)
    earlier = (ri < ci).astype(BF16)
    cum = jnp.dot(onehot.astype(BF16), earlier, preferred_element_type=F32)
    carry = carry_scr[...]
    rank_ref[0] = jnp.sum(onehot * (cum + carry[:, :1]), axis=0, keepdims=True).astype(jnp.int32)
    carry = carry + jnp.sum(onehot, axis=1, keepdims=True)
    carry_scr[...] = carry
    cnt_ref[...] = carry.astype(jnp.int32)


def assignment_ranks(flat_e, n_experts):
    n_assign = flat_e.shape[0]
    bl = _pick_tile(n_assign, (512, 256, 128))
    nblk = n_assign // bl
    rank, cnt = pl.pallas_call(
        functools.partial(_rank_kernel, n_experts=n_experts),
        grid=(nblk,),
        in_specs=[pl.BlockSpec((1, 1, bl), lambda i: (i, 0, 0))],
        out_specs=[pl.BlockSpec((1, 1, bl), lambda i: (i, 0, 0)),
                   pl.BlockSpec((n_experts, LANES), lambda i: (0, 0))],
        out_shape=[jax.ShapeDtypeStruct((nblk, 1, bl), jnp.int32),
                   jax.ShapeDtypeStruct((n_experts, LANES), jnp.int32)],
        scratch_shapes=[pltpu.VMEM((n_experts, LANES), F32)],
        compiler_params=pltpu.CompilerParams(dimension_semantics=("arbitrary",)),
        name="assignment_ranks",
    )(flat_e.reshape(nblk, 1, bl))
    return rank.reshape(n_assign), cnt[:, 0]


def _ffn_kernel(be_ref, x_ref, w1_ref, w3_ref, w2_ref, o_ref):
    del be_ref
    x = x_ref[...].astype(BF16)
    a = jnp.dot(x, w1_ref[0], preferred_element_type=F32)
    b = jnp.dot(x, w3_ref[0], preferred_element_type=F32)
    hid = (a * jax.nn.sigmoid(a) * b).astype(BF16)
    o_ref[...] = jnp.dot(hid, w2_ref[0], preferred_element_type=F32).astype(o_ref.dtype)


def expert_ffn(xb, block_expert, w1, w3, w2):
    nrows, dm = xb.shape
    f = w1.shape[2]
    nb = nrows // MOE_BLOCK
    return pl.pallas_call(
        _ffn_kernel,
        grid_spec=pltpu.PrefetchScalarGridSpec(
            num_scalar_prefetch=1,
            grid=(nb,),
            in_specs=[pl.BlockSpec((MOE_BLOCK, dm), lambda i, be: (i, 0)),
                      pl.BlockSpec((1, dm, f), lambda i, be: (be[i], 0, 0)),
                      pl.BlockSpec((1, dm, f), lambda i, be: (be[i], 0, 0)),
                      pl.BlockSpec((1, f, dm), lambda i, be: (be[i], 0, 0))],
            out_specs=pl.BlockSpec((MOE_BLOCK, dm), lambda i, be: (i, 0))),
        out_shape=jax.ShapeDtypeStruct((nrows, dm), BF16),
        compiler_params=pltpu.CompilerParams(
            dimension_semantics=("arbitrary",), vmem_limit_bytes=VMEM_LIMIT),
        name="expert_ffn",
    )(block_expert, xb, w1.astype(BF16), w3.astype(BF16), w2.astype(BF16))


def _mlstm_layer(s, gain, shift, gate_mod, geom, bsz, t, w_in, w_gate, b_gate, conv, head_g, w_out):
    n, dm = s.shape
    heads = ML_HEADS
    z = norm_mod_mm(s, gain, shift, w_in, None, (None, None, None, "sigmoid"), geom)
    scale = jnp.concatenate([jnp.ones((dm,), F32), jnp.full((dm,), (dm // heads) ** -0.5, F32)])
    qk = conv_silu(z, conv, scale, 2 * dm, geom)
    ng = 4 * heads
    wg = jnp.pad(jnp.concatenate([w_gate[0], w_gate[1]], axis=1), ((0, 0), (0, LANES - ng)))
    bg = jnp.pad(jnp.concatenate([b_gate[0], b_gate[1]]), (0, LANES - ng))
    gates = norm_mod_mm(s, gain, shift, wg, bg, (None,), geom, out_dtype=F32)[:, :ng]
    gates = gates.reshape(bsz, t, 2, 2 * heads)
    gates = jnp.concatenate([gates[..., :heads], jax.nn.log_sigmoid(gates[..., heads:])], axis=-1)
    gc = jnp.moveaxis(gates, 2, 0).reshape(2, n, 2 * heads)
    gr = jnp.transpose(gates, (2, 0, 3, 1))
    h = mlstm_scan(qk, z, gc, gr, dm, bsz, t)
    return post_mm_residual(h, z, 3, s, head_g, gate_mod, w_out, heads, geom)


def _rwkv7_layer(s, gain, shift, gate_mod, geom, bsz, t, mu, w_rkv, w0, w1, w2, a0, a1, a2, g1, g2,
                 k_k, k_a, r_k, ln_w, ln_b, w_out):
    dm = s.shape[1]
    lw, kda, rvkg = rwkv_proj(s, gain, shift, geom, mu, w_rkv, w0, w1, w2, a0, a1, a2, g1, g2, k_k, k_a)
    o = rwkv_scan(lw, kda, rvkg, dm, bsz, t)
    return rwkv_post(o, rvkg, kda, s, ln_w, ln_b, r_k, gate_mod, w_out, geom)


def _hgrn2_layer(s, gain, shift, gate_mod, geom, bsz, t, layer_idx, w_in, w_f, b_f, lb_logits, head_g, w_out):
    dm = s.shape[1]
    z = norm_mod_mm(s, gain, shift, w_in, None, ("silu", None, "silu"), geom)
    p = jax.nn.softmax(lb_logits, axis=0)
    lb = jnp.cumsum(p, axis=0)[layer_idx] - p[0]
    aux = jnp.tile(jnp.stack([jnp.log(lb), jnp.log1p(-lb)]), (1, 2))
    log_f = norm_mod_mm(s, gain, shift, jnp.concatenate([w_f[0], w_f[1]], axis=1),
                        jnp.concatenate([b_f[0], b_f[1]]), ("logf", "logf"), geom, aux=aux, out_dtype=F32)
    o = hgrn_scan(z, log_f, dm, bsz, t)
    return post_mm_residual(o, z, 2, s, head_g, gate_mod, w_out, dm // HG_EXPAND, geom)


def _moe_layer(s, gain, shift, gate_mod, geom, router_w, router_b, w1, w3, w2):
    n_tok, d = s.shape
    n_experts = w1.shape[0]
    n_assign = n_tok * TOP_K
    h, e, g = norm_route(s, gain, shift, geom, router_w, router_b)
    flat_e = e.T.reshape(n_assign)
    rank, counts = assignment_ranks(flat_e, n_experts)
    padded = (counts + MOE_BLOCK - 1) // MOE_BLOCK * MOE_BLOCK
    end_pad = jnp.cumsum(padded)
    start_pad = end_pad - padded
    onehot = flat_e[:, None] == jnp.arange(n_experts, dtype=jnp.int32)[None, :]
    dest = jnp.sum(jnp.where(onehot, start_pad[None, :], 0), axis=1) + rank
    n_blocks = -(-n_assign // MOE_BLOCK) + n_experts
    token_of_slot = jnp.zeros((n_blocks * MOE_BLOCK,), jnp.int32).at[dest].set(
        jnp.arange(n_assign, dtype=jnp.int32) // TOP_K)
    block_expert = jnp.minimum(jnp.searchsorted(
        end_pad, jnp.arange(n_blocks, dtype=jnp.int32) * MOE_BLOCK, side='right'), n_experts - 1)
    xb = h[token_of_slot]
    yb = expert_ffn(xb, block_expert.astype(jnp.int32), w1, w3, w2)
    dest_k = dest.reshape(n_tok, TOP_K)
    ys = [yb[dest_k[:, k]] for k in range(TOP_K)]
    return moe_combine(ys, g.T, s, gate_mod, geom)


def kernel(x, c, ctx, c_ctx, ada_w, ada_b, norm_mix, norm_ffn, norm_out, ml_w_in, ml_w_gate, ml_b_gate, ml_conv, ml_head_g, ml_w_out, rw_mu, rw_w_rkv, rw_w0, rw_w1, rw_w2, rw_a0, rw_a1, rw_a2, rw_g1, rw_g2, rw_k_k, rw_k_a, rw_r_k, rw_ln_w, rw_ln_b, rw_w_out, hg_w_in, hg_w_f, hg_b_f, hg_lb_logits, hg_head_g, hg_w_out, router_w, router_b, ex_w1, ex_w3, ex_w2):
    depth = ada_w.shape[0]
    bsz = x.shape[0]
    cond = jax.nn.silu(jnp.concatenate([c, c_ctx[None]], axis=0))
    cond = jnp.pad(cond, ((0, -(bsz + 1) % 8), (0, 0)))
    dm = x.shape[2]
    t = CTX_LEN + x.shape[1]
    n = bsz * t
    geom = (t // ROW_TILE, CTX_LEN // ROW_TILE)
    s = jnp.concatenate([ctx, x], axis=1).reshape(n, dm)
    for i in range(depth):
        mod = mm(cond, ada_w[i], bias=ada_b[i])
        mod_x = jnp.split(mod[:bsz, None, :], 6, axis=-1)
        mod_c = jnp.split(mod[bsz], 6, axis=-1)

        def table(idx):
            return jnp.stack([jnp.broadcast_to(mod_c[idx], (bsz, dm)), mod_x[idx][:, 0]], axis=1).reshape(2 * bsz, 1, dm)

        kind, j = i % N_MIXERS, i // N_MIXERS
        if kind == 2:
            s = _hgrn2_layer(s, norm_mix[i] * (1 + table(1)), table(0), table(2), geom, bsz, t, i,
                             hg_w_in[j], hg_w_f[j], hg_b_f[j], hg_lb_logits, hg_head_g[j], hg_w_out[j])
        elif kind == 0:
            s = _mlstm_layer(s, norm_mix[i] * (1 + table(1)), table(0), table(2), geom, bsz, t,
                             ml_w_in[j], ml_w_gate[j], ml_b_gate[j], ml_conv[j], ml_head_g[j], ml_w_out[j])
        else:
            s = _rwkv7_layer(s, norm_mix[i] * (1 + table(1)), table(0), table(2), geom, bsz, t,
                             rw_mu[j], rw_w_rkv[j], rw_w0[j], rw_w1[j], rw_w2[j], rw_a0[j],
                             rw_a1[j], rw_a2[j], rw_g1[j], rw_g2[j], rw_k_k[j], rw_k_a[j],
                             rw_r_k[j], rw_ln_w[j], rw_ln_b[j], rw_w_out[j])
        s = _moe_layer(s, norm_ffn[i] * (1 + table(4)), table(3), table(5), geom, router_w, router_b,
                       ex_w1[i], ex_w3[i], ex_w2[i])
    return final_norm(s, norm_out, bsz, t, geom)
```

```python
import functools

import jax
import jax.numpy as jnp
from jax import lax
from jax.experimental import pallas as pl
from jax.experimental.pallas import tpu as pltpu

F32 = jnp.float32
BF16 = jnp.bfloat16

GRID_W = 64
CTX_LEN = 256
N_MIXERS = 3
NORM_EPS = 1e-6
ML_HEADS = 8
RW_HEAD_DIM = 64
RW_GN_EPS = 64e-5
HG_EXPAND = 128
N_GROUPS = 4
TOP_K = 2
MOE_BLOCK = 256

LANES = 128
ML_CHUNK = 128
RW_CHUNK = 64
RW_PRE_CHUNKS = 4
HG_CHUNK = 64
HG_SUB = 16
HG_EXP_CLAMP = 80.0
VMEM_LIMIT = 48 * 1024 * 1024
VMEM_LIMIT_BIG = 56 * 1024 * 1024

NT = (((1,), (1,)), ((), ()))
NN = (((1,), (0,)), ((), ()))


def _dot(a, b, dims=NN, passes=1):
    a_hi = a.astype(BF16)
    b_hi = b.astype(BF16)
    out = lax.dot_general(a_hi, b_hi, dims, preferred_element_type=F32)
    if passes == 3:
        a_lo = (a - a_hi.astype(F32)).astype(BF16)
        b_lo = (b - b_hi.astype(F32)).astype(BF16)
        out = out + lax.dot_general(a_hi, b_lo, dims, preferred_element_type=F32)
        out = out + lax.dot_general(a_lo, b_hi, dims, preferred_element_type=F32)
    return out


def _cumsum_rows(x, reverse):
    n = x.shape[0]
    row = lax.broadcasted_iota(jnp.int32, x.shape, 0)
    s = 1
    while s < n:
        if reverse:
            x = x + jnp.where(row < n - s, pltpu.roll(x, n - s, axis=0), 0.0)
        else:
            x = x + jnp.where(row >= s, pltpu.roll(x, s, axis=0), 0.0)
        s *= 2
    return x


def _pick_tile(n, candidates):
    for c in candidates:
        if n % c == 0:
            return c
    raise ValueError(f"no tile for {n}")


def _scan_chunk_index(d, p, nctx, nc):
    rev = jnp.where(p < nctx, nctx - 1 - p, nc - 1 - (p - nctx))
    return jnp.where(d == 0, p, rev)


_ACTS = {
    None: lambda y: y,
    "sigmoid": jax.nn.sigmoid,
    "silu": lambda y: y * jax.nn.sigmoid(y),
    "tanh": jnp.tanh,
}


def _mm_kernel(x_ref, w_ref, b_ref, o_ref, *, act, precise):
    if precise:
        y = _dot(x_ref[...], w_ref[...], passes=3)
    else:
        y = jnp.dot(x_ref[...].astype(BF16), w_ref[...], preferred_element_type=F32)
    o_ref[...] = _ACTS[act](y + b_ref[...]).astype(o_ref.dtype)


def mm(x, w, bias=None, act=None, out_dtype=F32, precise=False):
    n, k = x.shape
    m = w.shape[1]
    tm = _pick_tile(n, (512, 256, 128, 64, 32, 16, 8))
    tn = m if m <= 1024 else _pick_tile(m, (1024, 512, 256, 128))
    if not precise:
        w = w.astype(BF16)
    if bias is None:
        bias = jnp.zeros((m,), F32)
    return pl.pallas_call(
        functools.partial(_mm_kernel, act=act, precise=precise),
        grid=(n // tm, m // tn),
        in_specs=[pl.BlockSpec((tm, k), lambda i, j: (i, 0)),
                  pl.BlockSpec((k, tn), lambda i, j: (0, j)),
                  pl.BlockSpec((1, tn), lambda i, j: (0, j))],
        out_specs=pl.BlockSpec((tm, tn), lambda i, j: (i, j)),
        out_shape=jax.ShapeDtypeStruct((n, m), out_dtype),
        compiler_params=pltpu.CompilerParams(vmem_limit_bytes=VMEM_LIMIT),
        name="mm",
    )(x, w, bias.reshape(1, m).astype(F32))


ROW_TILE = 256


def _log1p_exp_neg_abs(x):
    return jnp.log(1.0 + jnp.exp(-jnp.abs(x)))


def _log_sigmoid(y):
    return jnp.minimum(y, 0.0) - _log1p_exp_neg_abs(y)


def _softplus(x):
    return jnp.maximum(x, 0.0) + _log1p_exp_neg_abs(x)


def _logaddexp(a, b):
    return jnp.maximum(a, b) + _log1p_exp_neg_abs(a - b)


def _norm_mod(x, gain, shift):
    return x * lax.rsqrt(jnp.mean(x * x, axis=-1, keepdims=True) + NORM_EPS) * gain + shift


def _seg_map(tpb, nctx_t):
    def seg(i):
        return (i // tpb) * 2 + jnp.where(i % tpb < nctx_t, 0, 1)
    return seg


_EPILOGUES = {
    None: lambda y, aux: y,
    "sigmoid": lambda y, aux: jax.nn.sigmoid(y),
    "silu": lambda y, aux: y * jax.nn.sigmoid(y),
    "logf": lambda y, aux: _logaddexp(aux[0:1, :], aux[1:2, :] + _log_sigmoid(y)),
}


def _sub_tiles(n, most=4):
    return _pick_tile(n // ROW_TILE, tuple(range(most, 0, -1)))


def _nmm_kernel(s_ref, gain_ref, shift_ref, w_ref, b_ref, aux_ref, o_ref, h_scr, *, acts, sub, seg):
    j = pl.program_id(1)

    @pl.when(j == 0)
    def _():
        for k in range(sub):
            rows = pl.ds(k * ROW_TILE, ROW_TILE)
            sk = seg(pl.program_id(0) * sub + k)
            h_scr[rows, :] = _norm_mod(s_ref[rows, :], gain_ref[sk], shift_ref[sk]).astype(BF16)

    y = jnp.dot(h_scr[...], w_ref[...], preferred_element_type=F32) + b_ref[...]
    for act in sorted(set(acts), key=str):
        cols = [jj for jj, a in enumerate(acts) if a == act]
        if len(cols) == len(acts):
            o_ref[...] = _EPILOGUES[act](y, aux_ref[...]).astype(o_ref.dtype)
        else:
            @pl.when(functools.reduce(jnp.logical_or, [j == jj for jj in cols]))
            def _(act=act):
                o_ref[...] = _EPILOGUES[act](y, aux_ref[...]).astype(o_ref.dtype)


def norm_mod_mm(s, gain, shift, w, bias, acts, geom, aux=None, out_dtype=None):
    out_dtype = out_dtype or BF16
    n, k = s.shape
    m = w.shape[1]
    tn = m // len(acts)
    tpb, nctx_t = geom
    seg = _seg_map(tpb, nctx_t)
    sub = _sub_tiles(n)
    tm = sub * ROW_TILE
    if bias is None:
        bias = jnp.zeros((m,), F32)
    if aux is None:
        aux = jnp.zeros((2, m), F32)
    return pl.pallas_call(
        functools.partial(_nmm_kernel, acts=tuple(acts), sub=sub, seg=seg),
        grid=(n // tm, m // tn),
        in_specs=[pl.BlockSpec((tm, k), lambda i, j: (i, 0)),
                  pl.BlockSpec(gain.shape, lambda i, j: (0, 0, 0)),
                  pl.BlockSpec(shift.shape, lambda i, j: (0, 0, 0)),
                  pl.BlockSpec((k, tn), lambda i, j: (0, j)),
                  pl.BlockSpec((1, tn), lambda i, j: (0, j)),
                  pl.BlockSpec((2, tn), lambda i, j: (0, j))],
        out_specs=pl.BlockSpec((tm, tn), lambda i, j: (i, j)),
        out_shape=jax.ShapeDtypeStruct((n, m), out_dtype),
        scratch_shapes=[pltpu.VMEM((tm, k), BF16)],
        compiler_params=pltpu.CompilerParams(
            dimension_semantics=("arbitrary", "arbitrary"), vmem_limit_bytes=VMEM_LIMIT),
        name="norm_mod_mm",
    )(s, gain, shift, w.astype(BF16), bias.reshape(1, m).astype(F32), aux.astype(F32))


def _gated_residual_store(o_ref, s_ref, gm_ref, y, sub, seg):
    for k in range(sub):
        rows = pl.ds(k * ROW_TILE, ROW_TILE)
        gm = gm_ref[seg(pl.program_id(0) * sub + k)]
        o_ref[rows, :] = s_ref[rows, :] + gm * y[k * ROW_TILE:(k + 1) * ROW_TILE]


def _post_kernel(h_ref, g_ref, s_ref, hg_ref, gm_ref, w_ref, o_ref, *, heads, sub, seg):
    x = h_ref[0].astype(F32) + h_ref[1].astype(F32)
    hd = x.shape[1] // heads
    parts = []
    for h in range(heads):
        xh = x[:, h * hd:(h + 1) * hd]
        parts.append(xh * lax.rsqrt(jnp.mean(xh * xh, axis=-1, keepdims=True) + NORM_EPS))
    y = (jnp.concatenate(parts, axis=1) * hg_ref[...] * g_ref[...].astype(F32)).astype(BF16)
    _gated_residual_store(o_ref, s_ref, gm_ref, jnp.dot(y, w_ref[...], preferred_element_type=F32), sub, seg)


def post_mm_residual(h2, gate_arr, gate_block, s, head_g, gm, w_out, heads, geom):
    n, dm = s.shape
    seg = _seg_map(*geom)
    sub = _sub_tiles(n, most=2)
    tm = sub * ROW_TILE
    return pl.pallas_call(
        functools.partial(_post_kernel, heads=heads, sub=sub, seg=seg),
        grid=(n // tm,),
        in_specs=[pl.BlockSpec((2, tm, dm), lambda i: (0, i, 0)),
                  pl.BlockSpec((tm, dm), lambda i: (i, gate_block)),
                  pl.BlockSpec((tm, dm), lambda i: (i, 0)),
                  pl.BlockSpec((1, dm), lambda i: (0, 0)),
                  pl.BlockSpec(gm.shape, lambda i: (0, 0, 0)),
                  pl.BlockSpec((dm, dm), lambda i: (0, 0))],
        out_specs=pl.BlockSpec((tm, dm), lambda i: (i, 0)),
        out_shape=jax.ShapeDtypeStruct((n, dm), F32),
        compiler_params=pltpu.CompilerParams(dimension_semantics=("arbitrary",), vmem_limit_bytes=VMEM_LIMIT),
        name="post_mm_residual",
    )(h2, gate_arr, s, head_g.reshape(1, dm), gm, w_out.astype(BF16))


CONV_COLS = 512


def _conv_kernel(cur_ref, up_ref, dn_ref, w_ref, sc_ref, o_ref, *, tpb, nctx_t):
    ti = pl.program_id(0) % tpb
    is_ctx = ti < nctx_t
    no_up = jnp.logical_or(is_ctx, ti == nctx_t)
    no_dn = jnp.logical_or(is_ctx, ti == tpb - 1)
    x = cur_ref[...].astype(F32)
    up = jnp.where(no_up, 0.0, up_ref[...].astype(F32))
    dn = jnp.where(no_dn, 0.0, dn_ref[...].astype(F32))
    ext = jnp.concatenate([up, x, dn], axis=0)
    nr = ext.shape[0]
    ext_m = pltpu.roll(ext, 1, axis=0)
    ext_p = pltpu.roll(ext, nr - 1, axis=0)
    tpos = lax.broadcasted_iota(jnp.int32, (ROW_TILE, 1), 0)
    col = tpos % GRID_W
    left_ok = jnp.where(is_ctx, (tpos > 0).astype(F32), (col > 0).astype(F32))
    right_ok = jnp.where(is_ctx, (tpos < ROW_TILE - 1).astype(F32), (col < GRID_W - 1).astype(F32))
    vert = jnp.where(is_ctx, 0.0, 1.0)
    w = w_ref[...]
    acc = None
    for dr in (-1, 0, 1):
        base = GRID_W * (1 + dr)
        r3 = 3 * (dr + 1)
        term = (ext[base:base + ROW_TILE] * w[r3 + 1:r3 + 2]
                + ext_m[base:base + ROW_TILE] * w[r3:r3 + 1] * left_ok
                + ext_p[base:base + ROW_TILE] * w[r3 + 2:r3 + 3] * right_ok)
        if dr != 0:
            term = term * vert
        acc = term if acc is None else acc + term
    o_ref[...] = (acc * jax.nn.sigmoid(acc) * sc_ref[...]).astype(o_ref.dtype)


def conv_silu(z, conv_w, scale, width, geom):
    n = z.shape[0]
    tpb, nctx_t = geom
    assert nctx_t == 1 and ROW_TILE % GRID_W == 0
    hb = ROW_TILE // GRID_W
    last = n // GRID_W - 1
    return pl.pallas_call(
        functools.partial(_conv_kernel, tpb=tpb, nctx_t=nctx_t),
        grid=(n // ROW_TILE, width // CONV_COLS),
        in_specs=[pl.BlockSpec((ROW_TILE, CONV_COLS), lambda i, c: (i, c)),
                  pl.BlockSpec((GRID_W, CONV_COLS), lambda i, c: (jnp.maximum(i * hb - 1, 0), c)),
                  pl.BlockSpec((GRID_W, CONV_COLS), lambda i, c: (jnp.minimum((i + 1) * hb, last), c)),
                  pl.BlockSpec((9, CONV_COLS), lambda i, c: (0, c)),
                  pl.BlockSpec((1, CONV_COLS), lambda i, c: (0, c))],
        out_specs=pl.BlockSpec((ROW_TILE, CONV_COLS), lambda i, c: (i, c)),
        out_shape=jax.ShapeDtypeStruct((n, width), BF16),
        compiler_params=pltpu.CompilerParams(
            dimension_semantics=("arbitrary", "arbitrary"), vmem_limit_bytes=VMEM_LIMIT),
        name="conv_silu",
    )(z, z, z, conv_w.reshape(9, width).astype(F32), scale.reshape(1, width).astype(F32))


def _ml_scan_kernel(q_ref, k_ref, v_ref, gc_ref, gr_ref, o_ref, ct_scr, n_scr, m_scr, *, heads):
    L = q_ref.shape[0]
    d = pl.program_id(0)

    @pl.when(pl.program_id(2) == 0)
    def _():
        ct_scr[...] = jnp.zeros_like(ct_scr)
        n_scr[...] = jnp.zeros_like(n_scr)
        m_scr[...] = jnp.zeros_like(m_scr)

    row = lax.broadcasted_iota(jnp.int32, (L, L), 0)
    col = lax.broadcasted_iota(jnp.int32, (L, L), 1)

    def body(reverse):
        incl = (col >= row) if reverse else (col <= row)
        incl_t = (row >= col) if reverse else (row <= col)
        last = 0 if reverse else L - 1
        hs = range(heads)
        sls = [slice(h * LANES, (h + 1) * LANES) for h in hs]
        qk = [_dot(q_ref[:, sls[h]], k_ref[:, sls[h]], NT) for h in hs]
        qc = [_dot(q_ref[:, sls[h]], ct_scr[h]) for h in hs]
        stats = []
        for h in hs:
            ig_col = gc_ref[0, :, h:h + 1]
            lf_col = gc_ref[0, :, heads + h:heads + h + 1]
            ig_row = gr_ref[0, 0, h:h + 1, :]
            lf_row = gr_ref[0, 0, heads + h:heads + h + 1, :]
            b_col = jnp.sum(jnp.where(incl, lf_row, 0.0), axis=1, keepdims=True)
            b_row = jnp.sum(jnp.where(incl_t, lf_col, 0.0), axis=0, keepdims=True)
            m_prev = m_scr[h:h + 1, 0:1]
            dmat = jnp.where(incl, b_col - b_row + ig_row, -jnp.inf)
            inter = b_col + m_prev
            m_t = jnp.maximum(inter, jnp.max(dmat, axis=1, keepdims=True))
            b_last = b_col[last:last + 1, :]
            m_new = jnp.maximum(b_last + m_prev, jnp.max(b_last - b_row + ig_row, axis=1, keepdims=True))
            w_k = jnp.exp(b_last - b_col + ig_col - m_new)
            w_prev = jnp.exp(b_last + m_prev - m_new)
            stats.append((jnp.exp(dmat - m_t), jnp.exp(inter - m_t), jnp.exp(-m_t), w_k, w_prev, m_new))
        kv = [_dot(k_ref[:, sls[h]].astype(F32).T, stats[h][3] * v_ref[:, sls[h]].astype(F32)) for h in hs]
        s = [qk[h] * stats[h][0] for h in hs]
        sv = [_dot(s[h], v_ref[:, sls[h]]) for h in hs]
        for h in hs:
            _, w_inter, floor, w_k, w_prev, m_new = stats[h]
            n_row = n_scr[h:h + 1, :]
            num = sv[h] + w_inter * qc[h]
            den = (jnp.sum(s[h], axis=1, keepdims=True)
                   + w_inter * jnp.sum(q_ref[:, sls[h]].astype(F32) * n_row, axis=1, keepdims=True))
            o_ref[0, :, sls[h]] = (num / jnp.maximum(jnp.abs(den), floor)).astype(o_ref.dtype)
            ct_scr[h] = w_prev * ct_scr[h] + kv[h]
            n_scr[h:h + 1, :] = w_prev * n_row + jnp.sum(w_k * k_ref[:, sls[h]].astype(F32), axis=0, keepdims=True)
            m_scr[h:h + 1, :] = jnp.broadcast_to(m_new, (1, LANES))

    @pl.when(d == 0)
    def _():
        body(False)

    @pl.when(d == 1)
    def _():
        body(True)


def mlstm_scan(qk, z, gc, gr, dm, bsz, t):
    n = qk.shape[0]
    heads = dm // LANES
    L = ML_CHUNK
    nc, nctx = t // L, CTX_LEN // L

    def row(d, b, p):
        return b * nc + _scan_chunk_index(d, p, nctx, nc)

    return pl.pallas_call(
        functools.partial(_ml_scan_kernel, heads=heads),
        grid=(2, bsz, nc),
        in_specs=[pl.BlockSpec((L, dm), lambda d, b, p: (row(d, b, p), 0)),
                  pl.BlockSpec((L, dm), lambda d, b, p: (row(d, b, p), 1)),
                  pl.BlockSpec((L, dm), lambda d, b, p: (row(d, b, p), 2)),
                  pl.BlockSpec((1, L, 2 * heads), lambda d, b, p: (d, row(d, b, p), 0)),
                  pl.BlockSpec((1, 1, 2 * heads, L),
                               lambda d, b, p: (d, b, 0, _scan_chunk_index(d, p, nctx, nc)))],
        out_specs=pl.BlockSpec((1, L, dm), lambda d, b, p: (d, row(d, b, p), 0)),
        out_shape=jax.ShapeDtypeStruct((2, n, dm), BF16),
        scratch_shapes=[pltpu.VMEM((heads, LANES, LANES), F32), pltpu.VMEM((heads, LANES), F32),
                        pltpu.VMEM((heads, LANES), F32)],
        compiler_params=pltpu.CompilerParams(
            dimension_semantics=("arbitrary", "arbitrary", "arbitrary"), vmem_limit_bytes=VMEM_LIMIT),
        name="mlstm_scan",
    )(qk, qk, z, gc, gr)


RW_STATE_PASSES = 3


def _rw_scan_kernel(lw0_ref, lw1_ref, kd0_ref, kd1_ref, a0_ref, a1_ref, r0_ref, r1_ref, v0_ref, v1_ref,
                    kk0_ref, kk1_ref, of_ref, ob_ref, h_scr, rdp_scr, o0_scr, m_scr, ha_scr, *, nchunk):
    L = RW_CHUNK
    j = pl.program_id(2)

    @pl.when(j == 0)
    def _():
        for ref in (h_scr, rdp_scr, o0_scr, m_scr, ha_scr):
            ref[...] = jnp.zeros_like(ref)

    hs = [h_scr[0], h_scr[1]]

    def recurrence_step(k):
        for d, o_ref in ((0, of_ref), (1, ob_ref)):
            c = k if d == 0 else nchunk - 1 - k
            o_ref[pl.ds(c * L, L), :] = (_dot(rdp_scr[d, c], hs[d], passes=RW_STATE_PASSES)
                                         + o0_scr[d, c]).astype(o_ref.dtype)
            hs[d] = _dot(m_scr[d, c], hs[d], passes=RW_STATE_PASSES) + ha_scr[d, c]

    pending = list(range(nchunk))

    half = LANES // 2
    row = lax.broadcasted_iota(jnp.int32, (L, LANES), 0)
    col = lax.broadcasted_iota(jnp.int32, (L, LANES), 1) % half
    eye2 = (row == col).astype(F32)
    lane = lax.broadcasted_iota(jnp.int32, (1, LANES), 1)
    m0 = (lane < half).astype(F32)
    m1 = (lane >= half).astype(F32)
    r2 = lax.broadcasted_iota(jnp.int32, (LANES, LANES), 0)
    c2 = lax.broadcasted_iota(jnp.int32, (LANES, LANES), 1)
    same_head = (r2 // half) == (c2 // half)

    def stack(x):
        return jnp.concatenate([x * m0, x * m1], axis=0)

    chains = [(d, c) for c in range(nchunk) for d in range(2)]
    st = {}
    for d, c in chains:
        reverse = d == 1
        sl = pl.ds(c * L, L)
        lw = (lw0_ref, lw1_ref)[d][sl, :]
        k = (kd0_ref, kd1_ref)[d][sl, :].astype(F32)
        kk = (kk0_ref, kk1_ref)[d][sl, :].astype(F32)
        akk = kk * (a0_ref, a1_ref)[d][sl, :].astype(F32)
        g = _cumsum_rows(lw, reverse)
        ieg = jnp.exp(-g)
        g_last = g[0:1] if reverse else g[L - 1:L]
        dl = jnp.exp(g_last - g)
        st[d, c] = dict(kd=kk * jnp.exp(g - lw), rd=(r0_ref, r1_ref)[d][sl, :].astype(F32) * jnp.exp(g),
                        ai=akk * ieg, ki=k * ieg, ad=akk * dl, kdd=k * dl, eg_last=jnp.exp(g_last),
                        v=(v0_ref, v1_ref)[d][sl, :].astype(F32))
    recurrence_step(pending.pop(0))
    for d, c in chains:
        s = st[d, c]
        reverse = d == 1
        incl = (col >= row) if reverse else (col <= row)
        strict = (col > row) if reverse else (col < row)
        x = jnp.concatenate([s["kd"], s["rd"]], axis=0)
        rhs = jnp.concatenate([stack(s["ai"]), stack(s["ki"])], axis=0)
        sc = _dot(x, rhs, NT)
        s["a_ab"] = jnp.where(strict, sc[:L, :LANES], 0.0)
        s["a_ak"] = jnp.where(strict, sc[:L, LANES:], 0.0)
        s["b_ra"] = jnp.where(incl, sc[L:, :LANES], 0.0)
        s["b_rk"] = jnp.where(incl, sc[L:, LANES:], 0.0)
        s["tinv"] = eye2 - s["a_ab"]
        s["pw"] = s["a_ab"]
    span = 2
    while span < L:
        for d, c in chains:
            s = st[d, c]
            s["pw"] = _dot(s["pw"], stack(s["pw"]))
        for d, c in chains:
            s = st[d, c]
            s["tinv"] = _dot(s["tinv"], stack(eye2 + s["pw"]))
        if pending:
            recurrence_step(pending.pop(0))
        span *= 2
    while pending:
        recurrence_step(pending.pop(0))
    h_scr[0] = hs[0]
    h_scr[1] = hs[1]
    for d, c in chains:
        s = st[d, c]
        s["w"] = -_dot(s["tinv"], stack(s["a_ak"]))
        s["kdp"] = _dot(s["tinv"], stack(s["kd"]))
    for d, c in chains:
        s = st[d, c]
        s["vst"] = stack(s["v"])
        s["u0"] = _dot(s["w"], s["vst"])
    for d, c in chains:
        s = st[d, c]
        lhs = jnp.concatenate([s["b_ra"], s["b_rk"]], axis=1)
        rhs = jnp.concatenate([stack(s["u0"]), s["vst"]], axis=0)
        o0_scr[d, c] = _dot(lhs, rhs)
        rdp_scr[d, c] = s["rd"] - _dot(s["b_ra"], stack(s["kdp"]))
        diag = jnp.where(r2 == c2, s["eg_last"], 0.0)
        m_scr[d, c] = jnp.where(same_head, diag - _dot(s["ad"].T, s["kdp"]), 0.0)
        at = jnp.concatenate([s["ad"], s["kdd"]], axis=0).T
        ha_scr[d, c] = jnp.where(same_head, _dot(at, jnp.concatenate([s["u0"], s["v"]], axis=0)), 0.0)


def rwkv_scan(lw, kda, rvkg, dm, bsz, t):
    n = lw.shape[0]
    pairs = dm // LANES
    L = RW_CHUNK
    nchunk = RW_PRE_CHUNKS
    tb = nchunk * L
    nblk, nctx = t // tb, CTX_LEN // tb

    def block(d, b, j):
        return b * nblk + _scan_chunk_index(d, jnp.minimum(j, nblk - 1), nctx, nblk)

    def ispec(d, col):
        return pl.BlockSpec((tb, LANES), lambda b, p, j: (block(d, b, j), col * pairs + p))

    def ospec(d):
        return pl.BlockSpec((tb, LANES), lambda b, p, j: (block(d, b, jnp.maximum(j - 1, 0)), p))

    return pl.pallas_call(
        functools.partial(_rw_scan_kernel, nchunk=nchunk),
        grid=(bsz, pairs, nblk + 1),
        in_specs=[ispec(0, 0), ispec(1, 1), ispec(0, 0), ispec(1, 1), ispec(0, 2), ispec(1, 3),
                  ispec(0, 0), ispec(1, 0), ispec(0, 1), ispec(1, 1), ispec(0, 2), ispec(1, 2)],
        out_specs=[ospec(0), ospec(1)],
        out_shape=[jax.ShapeDtypeStruct((n, dm), BF16), jax.ShapeDtypeStruct((n, dm), BF16)],
        scratch_shapes=[pltpu.VMEM((2, LANES, LANES), F32), pltpu.VMEM((2, nchunk, L, LANES), F32),
                        pltpu.VMEM((2, nchunk, L, LANES), F32), pltpu.VMEM((2, nchunk, LANES, LANES), F32),
                        pltpu.VMEM((2, nchunk, LANES, LANES), F32)],
        compiler_params=pltpu.CompilerParams(
            dimension_semantics=("arbitrary", "arbitrary", "arbitrary"), vmem_limit_bytes=VMEM_LIMIT),
        name="rwkv_scan",
    )(lw, lw, kda, kda, kda, kda, rvkg, rvkg, rvkg, rvkg, rvkg, rvkg)


HALO_ROWS = 8


def _group_sum(x, width):
    r = lax.broadcasted_iota(jnp.int32, (LANES, LANES), 0) // width
    c = lax.broadcasted_iota(jnp.int32, (LANES, LANES), 1) // width
    ones = (r == c).astype(BF16)
    hi = x.astype(BF16)
    lo = (x - hi.astype(F32)).astype(BF16)
    parts = []
    for j in range(x.shape[1] // LANES):
        sl = slice(j * LANES, (j + 1) * LANES)
        parts.append(jnp.dot(hi[:, sl], ones, preferred_element_type=F32)
                     + jnp.dot(lo[:, sl], ones, preferred_element_type=F32))
    return jnp.concatenate(parts, axis=1)


def _rw_proj_kernel(s_ref, up_ref, dn_ref, gain_ref, shift_ref, mu_ref, wrkv_ref, w1_ref, w2_ref, w0_ref,
                    a1_ref, a2_ref, a0_ref, g1_ref, g2_ref, kk_ref, ka_ref,
                    lw_ref, kda_ref, rvkg_ref, *, tpb, nctx_t):
    tm, dm = s_ref.shape
    ti = pl.program_id(0) % tpb
    is_ctx = ti < nctx_t
    has_up = jnp.logical_not(jnp.logical_or(is_ctx, ti == nctx_t))
    has_dn = jnp.logical_not(jnp.logical_or(is_ctx, ti == tpb - 1))
    gain = gain_ref[0]
    shift = shift_ref[0]
    u = _norm_mod(s_ref[...], gain, shift)
    u_up = jnp.where(has_up, _norm_mod(up_ref[HALO_ROWS - 1:HALO_ROWS, :], gain, shift), 0.0)
    u_dn = jnp.where(has_dn, _norm_mod(dn_ref[0:1, :], gain, shift), 0.0)
    row = lax.broadcasted_iota(jnp.int32, (tm, 1), 0)
    u_m = jnp.where(row == 0, u_up, pltpu.roll(u, 1, axis=0))
    u_p = jnp.where(row == tm - 1, u_dn, pltpu.roll(u, tm - 1, axis=0))
    du = 0.5 * (u_m + u_p) - u
    mu = mu_ref[...]

    def mix(i):
        return (u + du * mu[i:i + 1]).astype(BF16)

    def dot(a, b):
        return jnp.dot(a.astype(BF16), b, preferred_element_type=F32)

    r = dot(mix(0), wrkv_ref[0])
    k = dot(mix(1), wrkv_ref[1])
    v = dot(mix(2), wrkv_ref[2])
    w_pre = dot(jnp.tanh(dot(mix(3), w1_ref[...])), w2_ref[...]) + w0_ref[...]
    lw_ref[...] = -jnp.exp(-_softplus(-w_pre) - 0.5)
    a = jax.nn.sigmoid(dot(dot(mix(4), a1_ref[...]), a2_ref[...]) + a0_ref[...])
    g = dot(jax.nn.sigmoid(dot(mix(5), g1_ref[...])), g2_ref[...])
    kk = k * kk_ref[...]
    kk = kk * lax.rsqrt(jnp.maximum(_group_sum(kk * kk, RW_HEAD_DIM), 1e-24))
    ka = ka_ref[...]
    for d in range(2):
        kda_ref[:, d * dm:(d + 1) * dm] = (k * (1.0 + (a[:, d * dm:(d + 1) * dm] - 1.0) * ka)).astype(kda_ref.dtype)
    kda_ref[:, 2 * dm:] = a.astype(kda_ref.dtype)
    for j, val in enumerate((r, v, kk, g)):
        rvkg_ref[:, j * dm:(j + 1) * dm] = val.astype(rvkg_ref.dtype)


def rwkv_proj(s, gain, shift, geom, mu, w_rkv, w0, w1, w2, a0, a1, a2, g1, g2, k_k, k_a):
    n, dm = s.shape
    tpb, nctx_t = geom
    assert nctx_t == 1
    seg = _seg_map(tpb, nctx_t)
    hb = ROW_TILE // HALO_ROWS
    last = n // HALO_ROWS - 1
    lora = w1.shape[2]

    def blockdiag(w):
        z = jnp.zeros_like(w[0])
        return jnp.concatenate([jnp.concatenate([w[0], z], axis=1), jnp.concatenate([z, w[1]], axis=1)], axis=0)

    consts = [jnp.pad(mu, ((0, HALO_ROWS - mu.shape[0]), (0, 0))), w_rkv.astype(BF16),
              jnp.concatenate([w1[0], w1[1]], axis=1).astype(BF16), blockdiag(w2).astype(BF16),
              jnp.concatenate([w0[0], w0[1]])[None],
              jnp.concatenate([a1[0], a1[1]], axis=1).astype(BF16), blockdiag(a2).astype(BF16),
              jnp.concatenate([a0[0], a0[1]])[None],
              g1.astype(BF16), g2.astype(BF16), k_k[None], k_a[None]]

    def const_spec(x):
        nd = x.ndim
        return pl.BlockSpec(x.shape, lambda i: (0,) * nd)

    return pl.pallas_call(
        functools.partial(_rw_proj_kernel, tpb=tpb, nctx_t=nctx_t),
        grid=(n // ROW_TILE,),
        in_specs=[pl.BlockSpec((ROW_TILE, dm), lambda i: (i, 0)),
                  pl.BlockSpec((HALO_ROWS, dm), lambda i: (jnp.maximum(i * hb - 1, 0), 0)),
                  pl.BlockSpec((HALO_ROWS, dm), lambda i: (jnp.minimum((i + 1) * hb, last), 0)),
                  pl.BlockSpec((1, 1, dm), lambda i: (seg(i), 0, 0)),
                  pl.BlockSpec((1, 1, dm), lambda i: (seg(i), 0, 0))] + [const_spec(x) for x in consts],
        out_specs=[pl.BlockSpec((ROW_TILE, 2 * dm), lambda i: (i, 0)),
                   pl.BlockSpec((ROW_TILE, 4 * dm), lambda i: (i, 0)),
                   pl.BlockSpec((ROW_TILE, 4 * dm), lambda i: (i, 0))],
        out_shape=[jax.ShapeDtypeStruct((n, 2 * dm), F32), jax.ShapeDtypeStruct((n, 4 * dm), BF16),
                   jax.ShapeDtypeStruct((n, 4 * dm), BF16)],
        compiler_params=pltpu.CompilerParams(dimension_semantics=("arbitrary",), vmem_limit_bytes=VMEM_LIMIT_BIG),
        name="rwkv_proj",
    )(s, s, s, gain, shift, *consts)


def _rw_post_kernel(of_ref, ob_ref, r_ref, v_ref, g_ref, k0_ref, k1_ref, s_ref, lnw_ref, lnb_ref, rk_ref, gm_ref,
                    w_ref, out_ref, *, sub, seg):
    o = of_ref[...].astype(F32) + ob_ref[...].astype(F32)
    inv = 1.0 / RW_HEAD_DIM
    mean = _group_sum(o, RW_HEAD_DIM) * inv
    oc = o - mean
    var = _group_sum(oc * oc, RW_HEAD_DIM) * inv
    xn = oc * lax.rsqrt(var + RW_GN_EPS) * lnw_ref[...] + lnb_ref[...]
    r = r_ref[...].astype(F32)
    ksum = k0_ref[...].astype(F32) + k1_ref[...].astype(F32)
    bonus = _group_sum(r * ksum * rk_ref[...], RW_HEAD_DIM) * v_ref[...].astype(F32)
    y = ((xn + bonus) * g_ref[...].astype(F32)).astype(BF16)
    _gated_residual_store(out_ref, s_ref, gm_ref, jnp.dot(y, w_ref[...], preferred_element_type=F32), sub, seg)


def rwkv_post(o_f, o_b, rvkg, kda, s, ln_w, ln_b, r_k, gm, w_out, geom):
    n, dm = s.shape
    seg = _seg_map(*geom)
    sub = _sub_tiles(n, most=2)
    tm = sub * ROW_TILE

    def col(block):
        return pl.BlockSpec((tm, dm), lambda i: (i, block))

    vec = pl.BlockSpec((1, dm), lambda i: (0, 0))
    return pl.pallas_call(
        functools.partial(_rw_post_kernel, sub=sub, seg=seg),
        grid=(n // tm,),
        in_specs=[col(0), col(0), col(0), col(1), col(3), col(0), col(1),
                  col(0), vec, vec, vec, pl.BlockSpec(gm.shape, lambda i: (0, 0, 0)),
                  pl.BlockSpec((dm, dm), lambda i: (0, 0))],
        out_specs=pl.BlockSpec((tm, dm), lambda i: (i, 0)),
        out_shape=jax.ShapeDtypeStruct((n, dm), F32),
        compiler_params=pltpu.CompilerParams(dimension_semantics=("arbitrary",), vmem_limit_bytes=VMEM_LIMIT),
        name="rwkv_post",
    )(o_f, o_b, rvkg, rvkg, rvkg, kda, kda, s, ln_w[None], ln_b[None], r_k[None], gm, w_out.astype(BF16))


def _hg_scan_kernel(q_ref, v_ref, lf_ref, o_ref, st_scr, *, heads):
    C = q_ref.shape[0]
    d = pl.program_id(0)
    nsub = C // HG_SUB

    @pl.when(pl.program_id(2) == 0)
    def _():
        st_scr[...] = jnp.zeros_like(st_scr)

    def body(reverse):
        last = 0 if reverse else C - 1
        hs = range(heads)
        sls = [slice(h * LANES, (h + 1) * LANES) for h in hs]
        g = [lf_ref[:, sls[h]] for h in hs]
        b = [_cumsum_rows(g[h], reverse) for h in hs]
        k = [-jnp.tanh(0.5 * g[h]) * (jnp.exp(g[h]) + 1.0) for h in hs]
        o_inter = [_dot(q_ref[:, sls[h]].astype(F32) * jnp.exp(b[h]), st_scr[h], NT) for h in hs]
        parts = [[None] * nsub for _ in hs]
        for i in range(nsub):
            r0 = i * HG_SUB
            lo, hi = (r0, C) if reverse else (0, r0 + HG_SUB)
            first = r0 + HG_SUB - 1 if reverse else r0
            row = lax.broadcasted_iota(jnp.int32, (HG_SUB, hi - lo), 0) + r0
            col = lax.broadcasted_iota(jnp.int32, (HG_SUB, hi - lo), 1) + lo
            keep = (col >= row) if reverse else (col <= row)
            att = []
            for h in hs:
                rho = b[h][first:first + 1, :] - g[h][first:first + 1, :]
                qi = q_ref[r0:r0 + HG_SUB, sls[h]].astype(F32) * jnp.exp(b[h][r0:r0 + HG_SUB] - rho)
                ki = k[h][lo:hi] * jnp.exp(jnp.minimum(rho - b[h][lo:hi], HG_EXP_CLAMP))
                att.append(jnp.where(keep, _dot(qi, ki, NT), 0.0))
            for h in hs:
                parts[h][i] = _dot(att[h], v_ref[lo:hi, sls[h]])
        for h in hs:
            o_ref[0, :, sls[h]] = (o_inter[h] + jnp.concatenate(parts[h], axis=0)).astype(o_ref.dtype)
        upd = []
        for h in hs:
            b_last = b[h][last:last + 1, :]
            upd.append((jnp.exp(b_last),
                        _dot(v_ref[:, sls[h]].astype(F32).T, k[h] * jnp.exp(b_last - b[h]))))
        for h in hs:
            st_scr[h] = st_scr[h] * upd[h][0] + upd[h][1]

    @pl.when(d == 0)
    def _():
        body(False)

    @pl.when(d == 1)
    def _():
        body(True)


def hgrn_scan(z, logf, dm, bsz, t):
    n = z.shape[0]
    heads = dm // LANES
    C = HG_CHUNK
    nc, nctx = t // C, CTX_LEN // C

    def row(d, b, p):
        return b * nc + _scan_chunk_index(d, p, nctx, nc)

    return pl.pallas_call(
        functools.partial(_hg_scan_kernel, heads=heads),
        grid=(2, bsz, nc),
        in_specs=[pl.BlockSpec((C, dm), lambda d, b, p: (row(d, b, p), 0)),
                  pl.BlockSpec((C, dm), lambda d, b, p: (row(d, b, p), 1)),
                  pl.BlockSpec((C, dm), lambda d, b, p: (row(d, b, p), d))],
        out_specs=pl.BlockSpec((1, C, dm), lambda d, b, p: (d, row(d, b, p), 0)),
        out_shape=jax.ShapeDtypeStruct((2, n, dm), BF16),
        scratch_shapes=[pltpu.VMEM((heads, LANES, LANES), F32)],
        compiler_params=pltpu.CompilerParams(
            dimension_semantics=("arbitrary", "arbitrary", "arbitrary"), vmem_limit_bytes=VMEM_LIMIT),
        name="hgrn_scan",
    )(z, z, logf)


def _first_argmax(vals):
    best, idx = vals[0], jnp.zeros(vals[0].shape, jnp.int32)
    for i in range(1, len(vals)):
        better = vals[i] > best
        best = jnp.where(better, vals[i], best)
        idx = jnp.where(better, i, idx)
    return best, idx


def _router_kernel(s_ref, gain_ref, shift_ref, wt_ref, b_ref, h_ref, e_ref, g_ref, *, n_groups, top_k):
    n_experts = wt_ref.shape[0]
    per = n_experts // n_groups
    h = _norm_mod(s_ref[...], gain_ref[0], shift_ref[0])
    h_ref[...] = h.astype(h_ref.dtype)
    aff = jax.nn.sigmoid(_dot(wt_ref[...], h, NT, passes=3))
    sel = aff + b_ref[...]
    a = [aff[e:e + 1, :] for e in range(n_experts)]
    s = [sel[e:e + 1, :] for e in range(n_experts)]
    neg = jnp.full_like(s[0], -jnp.inf)
    scores = []
    for g in range(n_groups):
        grp = s[g * per:(g + 1) * per]
        m1, i1 = _first_argmax(grp)
        m2, _ = _first_argmax([jnp.where(i1 == j, neg, grp[j]) for j in range(per)])
        scores.append(m1 + m2)
    _, best = _first_argmax(scores)

    def in_best(rows):
        out = []
        for j in range(per):
            x = rows[j]
            for g in range(1, n_groups):
                x = jnp.where(best == g, rows[g * per + j], x)
            out.append(x)
        return out

    sb, ab = in_best(s), in_best(a)
    picked, chosen = [], []
    cand = sb
    for _ in range(top_k):
        _, i = _first_argmax(cand)
        c = ab[0]
        for j in range(1, per):
            c = jnp.where(i == j, ab[j], c)
        picked.append(i)
        chosen.append(c)
        cand = [jnp.where(i == j, neg, cand[j]) for j in range(per)]
    total = functools.reduce(jnp.add, chosen)
    for kk_ in range(top_k):
        e_ref[kk_:kk_ + 1, :] = best * per + picked[kk_]
        g_ref[kk_:kk_ + 1, :] = chosen[kk_] / total


def norm_route(s, gain, shift, geom, router_w, router_b):
    n, k = s.shape
    n_experts = router_w.shape[1]
    tm = ROW_TILE
    seg = _seg_map(*geom)
    return pl.pallas_call(
        functools.partial(_router_kernel, n_groups=N_GROUPS, top_k=TOP_K),
        grid=(n // tm,),
        in_specs=[pl.BlockSpec((tm, k), lambda i: (i, 0)),
                  pl.BlockSpec((1, 1, k), lambda i: (seg(i), 0, 0)),
                  pl.BlockSpec((1, 1, k), lambda i: (seg(i), 0, 0)),
                  pl.BlockSpec((n_experts, k), lambda i: (0, 0)),
                  pl.BlockSpec((n_experts, 1), lambda i: (0, 0))],
        out_specs=[pl.BlockSpec((tm, k), lambda i: (i, 0)),
                   pl.BlockSpec((TOP_K, tm), lambda i: (0, i)), pl.BlockSpec((TOP_K, tm), lambda i: (0, i))],
        out_shape=[jax.ShapeDtypeStruct((n, k), BF16),
                   jax.ShapeDtypeStruct((TOP_K, n), jnp.int32), jax.ShapeDtypeStruct((TOP_K, n), F32)],
        compiler_params=pltpu.CompilerParams(dimension_semantics=("arbitrary",), vmem_limit_bytes=VMEM_LIMIT),
        name="norm_route",
    )(s, gain, shift, router_w.T, router_b.reshape(n_experts, 1).astype(F32))


def _combine_kernel(*refs):
    *y_refs, g_ref, s_ref, gm_ref, o_ref = refs
    g = g_ref[...]
    y = sum(y_ref[...].astype(F32) * g[:, k:k + 1] for k, y_ref in enumerate(y_refs))
    o_ref[...] = s_ref[...] + gm_ref[0] * y


def moe_combine(ys, gate, s, gm, geom):
    n, dm = s.shape
    seg = _seg_map(*geom)
    row = pl.BlockSpec((ROW_TILE, dm), lambda i: (i, 0))
    return pl.pallas_call(
        _combine_kernel,
        grid=(n // ROW_TILE,),
        in_specs=[row] * len(ys) + [pl.BlockSpec((ROW_TILE, gate.shape[1]), lambda i: (i, 0)), row,
                                    pl.BlockSpec((1, 1, dm), lambda i: (seg(i), 0, 0))],
        out_specs=row,
        out_shape=jax.ShapeDtypeStruct((n, dm), F32),
        compiler_params=pltpu.CompilerParams(dimension_semantics=("arbitrary",), vmem_limit_bytes=VMEM_LIMIT),
        name="moe_combine",
    )(*ys, gate, s, gm)


def _final_norm_kernel(s_ref, g_ref, o_ref):
    x = s_ref[...]
    o_ref[0] = x * lax.rsqrt(jnp.mean(x * x, axis=-1, keepdims=True) + NORM_EPS) * g_ref[...]


def final_norm(s, g, bsz, t, geom):
    dm = s.shape[1]
    tpb, nctx_t = geom
    return pl.pallas_call(
        _final_norm_kernel,
        grid=(bsz, tpb - nctx_t),
        in_specs=[pl.BlockSpec((ROW_TILE, dm), lambda b, i: (b * tpb + nctx_t + i, 0)),
                  pl.BlockSpec((1, dm), lambda b, i: (0, 0))],
        out_specs=pl.BlockSpec((1, ROW_TILE, dm), lambda b, i: (b, i, 0)),
        out_shape=jax.ShapeDtypeStruct((bsz, t - CTX_LEN, dm), F32),
        name="final_norm",
    )(s, g[None])


def _rank_kernel(e_ref, rank_ref, cnt_ref, carry_scr, *, n_experts):
    @pl.when(pl.program_id(0) == 0)
    def _():
        carry_scr[...] = jnp.zeros_like(carry_scr)

    bl = e_ref.shape[2]
    e_row = e_ref[0]
    sub = lax.broadcasted_iota(jnp.int32, (n_experts, bl), 0)
    onehot = (sub == e_row).astype(F32)
    ri = lax.broadcasted_iota(jnp.int32, (bl, bl), 0)
    ci = lax.broadcasted_iota(jnp.int32, (bl, bl), 1)
    earlier = (ri < ci).astype(BF16)
    cum = jnp.dot(onehot.astype(BF16), earlier, preferred_element_type=F32)
    carry = carry_scr[...]
    rank_ref[0] = jnp.sum(onehot * (cum + carry[:, :1]), axis=0, keepdims=True).astype(jnp.int32)
    carry = carry + jnp.sum(onehot, axis=1, keepdims=True)
    carry_scr[...] = carry
    cnt_ref[...] = carry.astype(jnp.int32)


def assignment_ranks(flat_e, n_experts):
    n_assign = flat_e.shape[0]
    bl = _pick_tile(n_assign, (512, 256, 128))
    nblk = n_assign // bl
    rank, cnt = pl.pallas_call(
        functools.partial(_rank_kernel, n_experts=n_experts),
        grid=(nblk,),
        in_specs=[pl.BlockSpec((1, 1, bl), lambda i: (i, 0, 0))],
        out_specs=[pl.BlockSpec((1, 1, bl), lambda i: (i, 0, 0)),
                   pl.BlockSpec((n_experts, LANES), lambda i: (0, 0))],
        out_shape=[jax.ShapeDtypeStruct((nblk, 1, bl), jnp.int32),
                   jax.ShapeDtypeStruct((n_experts, LANES), jnp.int32)],
        scratch_shapes=[pltpu.VMEM((n_experts, LANES), F32)],
        compiler_params=pltpu.CompilerParams(dimension_semantics=("arbitrary",)),
        name="assignment_ranks",
    )(flat_e.reshape(nblk, 1, bl))
    return rank.reshape(n_assign), cnt[:, 0]


def _ffn_kernel(be_ref, x_ref, w1_ref, w3_ref, w2_ref, o_ref):
    del be_ref
    x = x_ref[...].astype(BF16)
    a = jnp.dot(x, w1_ref[0], preferred_element_type=F32)
    b = jnp.dot(x, w3_ref[0], preferred_element_type=F32)
    hid = (a * jax.nn.sigmoid(a) * b).astype(BF16)
    o_ref[...] = jnp.dot(hid, w2_ref[0], preferred_element_type=F32).astype(o_ref.dtype)


def expert_ffn(xb, block_expert, w1, w3, w2):
    nrows, dm = xb.shape
    f = w1.shape[2]
    nb = nrows // MOE_BLOCK
    return pl.pallas_call(
        _ffn_kernel,
        grid_spec=pltpu.PrefetchScalarGridSpec(
            num_scalar_prefetch=1,
            grid=(nb,),
            in_specs=[pl.BlockSpec((MOE_BLOCK, dm), lambda i, be: (i, 0)),
                      pl.BlockSpec((1, dm, f), lambda i, be: (be[i], 0, 0)),
                      pl.BlockSpec((1, dm, f), lambda i, be: (be[i], 0, 0)),
                      pl.BlockSpec((1, f, dm), lambda i, be: (be[i], 0, 0))],
            out_specs=pl.BlockSpec((MOE_BLOCK, dm), lambda i, be: (i, 0))),
        out_shape=jax.ShapeDtypeStruct((nrows, dm), BF16),
        compiler_params=pltpu.CompilerParams(
            dimension_semantics=("arbitrary",), vmem_limit_bytes=VMEM_LIMIT),
        name="expert_ffn",
    )(block_expert, xb, w1.astype(BF16), w3.astype(BF16), w2.astype(BF16))


def _mlstm_layer(s, gain, shift, gate_mod, geom, bsz, t, w_in, w_gate, b_gate, conv, head_g, w_out):
    n, dm = s.shape
    heads = ML_HEADS
    z = norm_mod_mm(s, gain, shift, w_in, None, (None, None, None, "sigmoid"), geom)
    scale = jnp.concatenate([jnp.ones((dm,), F32), jnp.full((dm,), (dm // heads) ** -0.5, F32)])
    qk = conv_silu(z, conv, scale, 2 * dm, geom)
    ng = 4 * heads
    wg = jnp.pad(jnp.concatenate([w_gate[0], w_gate[1]], axis=1), ((0, 0), (0, LANES - ng)))
    bg = jnp.pad(jnp.concatenate([b_gate[0], b_gate[1]]), (0, LANES - ng))
    gates = norm_mod_mm(s, gain, shift, wg, bg, (None,), geom, out_dtype=F32)[:, :ng]
    gates = gates.reshape(bsz, t, 2, 2 * heads)
    gates = jnp.concatenate([gates[..., :heads], jax.nn.log_sigmoid(gates[..., heads:])], axis=-1)
    gc = jnp.moveaxis(gates, 2, 0).reshape(2, n, 2 * heads)
    gr = jnp.transpose(gates, (2, 0, 3, 1))
    h = mlstm_scan(qk, z, gc, gr, dm, bsz, t)
    return post_mm_residual(h, z, 3, s, head_g, gate_mod, w_out, heads, geom)


def _rwkv7_layer(s, gain, shift, gate_mod, geom, bsz, t, mu, w_rkv, w0, w1, w2, a0, a1, a2, g1, g2,
                 k_k, k_a, r_k, ln_w, ln_b, w_out):
    dm = s.shape[1]
    lw, kda, rvkg = rwkv_proj(s, gain, shift, geom, mu, w_rkv, w0, w1, w2, a0, a1, a2, g1, g2, k_k, k_a)
    o_f, o_b = rwkv_scan(lw, kda, rvkg, dm, bsz, t)
    return rwkv_post(o_f, o_b, rvkg, kda, s, ln_w, ln_b, r_k, gate_mod, w_out, geom)


def _hgrn2_layer(s, gain, shift, gate_mod, geom, bsz, t, layer_idx, w_in, w_f, b_f, lb_logits, head_g, w_out):
    dm = s.shape[1]
    z = norm_mod_mm(s, gain, shift, w_in, None, ("silu", None, "silu"), geom)
    p = jax.nn.softmax(lb_logits, axis=0)
    lb = jnp.cumsum(p, axis=0)[layer_idx] - p[0]
    aux = jnp.tile(jnp.stack([jnp.log(lb), jnp.log1p(-lb)]), (1, 2))
    log_f = norm_mod_mm(s, gain, shift, jnp.concatenate([w_f[0], w_f[1]], axis=1),
                        jnp.concatenate([b_f[0], b_f[1]]), ("logf", "logf"), geom, aux=aux, out_dtype=F32)
    o = hgrn_scan(z, log_f, dm, bsz, t)
    return post_mm_residual(o, z, 2, s, head_g, gate_mod, w_out, dm // HG_EXPAND, geom)


def _moe_layer(s, gain, shift, gate_mod, geom, router_w, router_b, w1, w3, w2):
    n_tok, d = s.shape
    n_experts = w1.shape[0]
    n_assign = n_tok * TOP_K
    h, e, g = norm_route(s, gain, shift, geom, router_w, router_b)
    flat_e = e.T.reshape(n_assign)
    rank, counts = assignment_ranks(flat_e, n_experts)
    padded = (counts + MOE_BLOCK - 1) // MOE_BLOCK * MOE_BLOCK
    end_pad = jnp.cumsum(padded)
    start_pad = end_pad - padded
    onehot = flat_e[:, None] == jnp.arange(n_experts, dtype=jnp.int32)[None, :]
    dest = jnp.sum(jnp.where(onehot, start_pad[None, :], 0), axis=1) + rank
    n_blocks = -(-n_assign // MOE_BLOCK) + n_experts
    token_of_slot = jnp.zeros((n_blocks * MOE_BLOCK,), jnp.int32).at[dest].set(
        jnp.arange(n_assign, dtype=jnp.int32) // TOP_K)
    block_start = jnp.arange(n_blocks, dtype=jnp.int32) * MOE_BLOCK
    block_expert = jnp.minimum(jnp.sum(end_pad[None, :] <= block_start[:, None], axis=1), n_experts - 1)
    xb = h[token_of_slot]
    yb = expert_ffn(xb, block_expert.astype(jnp.int32), w1, w3, w2)
    dest_k = dest.reshape(n_tok, TOP_K)
    ys = [yb[dest_k[:, k]] for k in range(TOP_K)]
    return moe_combine(ys, g.T, s, gate_mod, geom)


def kernel(x, c, ctx, c_ctx, ada_w, ada_b, norm_mix, norm_ffn, norm_out, ml_w_in, ml_w_gate, ml_b_gate, ml_conv, ml_head_g, ml_w_out, rw_mu, rw_w_rkv, rw_w0, rw_w1, rw_w2, rw_a0, rw_a1, rw_a2, rw_g1, rw_g2, rw_k_k, rw_k_a, rw_r_k, rw_ln_w, rw_ln_b, rw_w_out, hg_w_in, hg_w_f, hg_b_f, hg_lb_logits, hg_head_g, hg_w_out, router_w, router_b, ex_w1, ex_w3, ex_w2):
    depth = ada_w.shape[0]
    bsz = x.shape[0]
    cond = jax.nn.silu(jnp.concatenate([c, c_ctx[None]], axis=0))
    cond = jnp.pad(cond, ((0, -(bsz + 1) % 8), (0, 0)))
    dm = x.shape[2]
    t = CTX_LEN + x.shape[1]
    n = bsz * t
    geom = (t // ROW_TILE, CTX_LEN // ROW_TILE)
    s = jnp.concatenate([ctx, x], axis=1).reshape(n, dm)
    for i in range(depth):
        mod = mm(cond, ada_w[i], bias=ada_b[i])
        mod_x = jnp.split(mod[:bsz, None, :], 6, axis=-1)
        mod_c = jnp.split(mod[bsz], 6, axis=-1)

        def table(idx):
            return jnp.stack([jnp.broadcast_to(mod_c[idx], (bsz, dm)), mod_x[idx][:, 0]], axis=1).reshape(2 * bsz, 1, dm)

        kind, j = i % N_MIXERS, i // N_MIXERS
        if kind == 2:
            s = _hgrn2_layer(s, norm_mix[i] * (1 + table(1)), table(0), table(2), geom, bsz, t, i,
                             hg_w_in[j], hg_w_f[j], hg_b_f[j], hg_lb_logits, hg_head_g[j], hg_w_out[j])
        elif kind == 0:
            s = _mlstm_layer(s, norm_mix[i] * (1 + table(1)), table(0), table(2), geom, bsz, t,
                             ml_w_in[j], ml_w_gate[j], ml_b_gate[j], ml_conv[j], ml_head_g[j], ml_w_out[j])
        else:
            s = _rwkv7_layer(s, norm_mix[i] * (1 + table(1)), table(0), table(2), geom, bsz, t,
                             rw_mu[j], rw_w_rkv[j], rw_w0[j], rw_w1[j], rw_w2[j], rw_a0[j],
                             rw_a1[j], rw_a2[j], rw_g1[j], rw_g2[j], rw_k_k[j], rw_k_a[j],
                             rw_r_k[j], rw_ln_w[j], rw_ln_b[j], rw_w_out[j])
        s = _moe_layer(s, norm_ffn[i] * (1 + table(4)), table(3), table(5), geom, router_w, router_b,
                       ex_w1[i], ex_w3[i], ex_w2[i])
    return final_norm(s, norm_out, bsz, t, geom)
```

```python
import functools

import jax
import jax.numpy as jnp
from jax import lax
from jax.experimental import pallas as pl
from jax.experimental.pallas import tpu as pltpu

F32 = jnp.float32
BF16 = jnp.bfloat16

GRID_W = 64
CTX_LEN = 256
N_MIXERS = 3
NORM_EPS = 1e-6
ML_HEADS = 8
RW_HEAD_DIM = 64
RW_GN_EPS = 64e-5
HG_EXPAND = 128
N_GROUPS = 4
TOP_K = 2
MOE_BLOCK = 256

LANES = 128
ML_CHUNK = 128
RW_CHUNK = 64
RW_PRE_CHUNKS = 4
HG_CHUNK = 64
HG_SUB = 16
HG_EXP_CLAMP = 80.0
VMEM_LIMIT = 48 * 1024 * 1024
VMEM_LIMIT_BIG = 56 * 1024 * 1024

NT = (((1,), (1,)), ((), ()))
NN = (((1,), (0,)), ((), ()))


def _dot(a, b, dims=NN, passes=1):
    a_hi = a.astype(BF16)
    b_hi = b.astype(BF16)
    out = lax.dot_general(a_hi, b_hi, dims, preferred_element_type=F32)
    if passes == 3:
        a_lo = (a - a_hi.astype(F32)).astype(BF16)
        b_lo = (b - b_hi.astype(F32)).astype(BF16)
        out = out + lax.dot_general(a_hi, b_lo, dims, preferred_element_type=F32)
        out = out + lax.dot_general(a_lo, b_hi, dims, preferred_element_type=F32)
    return out


def _cumsum_rows(x, reverse):
    n = x.shape[0]
    row = lax.broadcasted_iota(jnp.int32, x.shape, 0)
    s = 1
    while s < n:
        if reverse:
            x = x + jnp.where(row < n - s, pltpu.roll(x, n - s, axis=0), 0.0)
        else:
            x = x + jnp.where(row >= s, pltpu.roll(x, s, axis=0), 0.0)
        s *= 2
    return x


def _pick_tile(n, candidates):
    for c in candidates:
        if n % c == 0:
            return c
    raise ValueError(f"no tile for {n}")


def _scan_chunk_index(d, p, nctx, nc):
    rev = jnp.where(p < nctx, nctx - 1 - p, nc - 1 - (p - nctx))
    return jnp.where(d == 0, p, rev)


_ACTS = {
    None: lambda y: y,
    "sigmoid": jax.nn.sigmoid,
    "silu": lambda y: y * jax.nn.sigmoid(y),
    "tanh": jnp.tanh,
}


def _mm_kernel(x_ref, w_ref, b_ref, o_ref, *, act, precise):
    if precise:
        y = _dot(x_ref[...], w_ref[...], passes=3)
    else:
        y = jnp.dot(x_ref[...].astype(BF16), w_ref[...], preferred_element_type=F32)
    o_ref[...] = _ACTS[act](y + b_ref[...]).astype(o_ref.dtype)


def mm(x, w, bias=None, act=None, out_dtype=F32, precise=False):
    n, k = x.shape
    m = w.shape[1]
    tm = _pick_tile(n, (512, 256, 128, 64, 32, 16, 8))
    tn = m if m <= 1024 else _pick_tile(m, (1024, 512, 256, 128))
    if not precise:
        w = w.astype(BF16)
    if bias is None:
        bias = jnp.zeros((m,), F32)
    return pl.pallas_call(
        functools.partial(_mm_kernel, act=act, precise=precise),
        grid=(n // tm, m // tn),
        in_specs=[pl.BlockSpec((tm, k), lambda i, j: (i, 0)),
                  pl.BlockSpec((k, tn), lambda i, j: (0, j)),
                  pl.BlockSpec((1, tn), lambda i, j: (0, j))],
        out_specs=pl.BlockSpec((tm, tn), lambda i, j: (i, j)),
        out_shape=jax.ShapeDtypeStruct((n, m), out_dtype),
        compiler_params=pltpu.CompilerParams(vmem_limit_bytes=VMEM_LIMIT),
        name="mm",
    )(x, w, bias.reshape(1, m).astype(F32))


ROW_TILE = 256


def _log1p_exp_neg_abs(x):
    return jnp.log(1.0 + jnp.exp(-jnp.abs(x)))


def _log_sigmoid(y):
    return jnp.minimum(y, 0.0) - _log1p_exp_neg_abs(y)


def _softplus(x):
    return jnp.maximum(x, 0.0) + _log1p_exp_neg_abs(x)


def _logaddexp(a, b):
    return jnp.maximum(a, b) + _log1p_exp_neg_abs(a - b)


def _norm_mod(x, gain, shift):
    return x * lax.rsqrt(jnp.mean(x * x, axis=-1, keepdims=True) + NORM_EPS) * gain + shift


def _seg_map(tpb, nctx_t):
    def seg(i):
        return (i // tpb) * 2 + jnp.where(i % tpb < nctx_t, 0, 1)
    return seg


_EPILOGUES = {
    None: lambda y, aux: y,
    "sigmoid": lambda y, aux: jax.nn.sigmoid(y),
    "silu": lambda y, aux: y * jax.nn.sigmoid(y),
    "logf": lambda y, aux: _logaddexp(aux[0:1, :], aux[1:2, :] + _log_sigmoid(y)),
}


def _sub_tiles(n, most=4):
    return _pick_tile(n // ROW_TILE, tuple(range(most, 0, -1)))


def _nmm_kernel(s_ref, gain_ref, shift_ref, w_ref, b_ref, aux_ref, o_ref, h_scr, *, acts, sub, seg):
    j = pl.program_id(1)

    @pl.when(j == 0)
    def _():
        for k in range(sub):
            rows = pl.ds(k * ROW_TILE, ROW_TILE)
            sk = seg(pl.program_id(0) * sub + k)
            h_scr[rows, :] = _norm_mod(s_ref[rows, :], gain_ref[sk], shift_ref[sk]).astype(BF16)

    y = jnp.dot(h_scr[...], w_ref[...], preferred_element_type=F32) + b_ref[...]
    for act in sorted(set(acts), key=str):
        cols = [jj for jj, a in enumerate(acts) if a == act]
        if len(cols) == len(acts):
            o_ref[...] = _EPILOGUES[act](y, aux_ref[...]).astype(o_ref.dtype)
        else:
            @pl.when(functools.reduce(jnp.logical_or, [j == jj for jj in cols]))
            def _(act=act):
                o_ref[...] = _EPILOGUES[act](y, aux_ref[...]).astype(o_ref.dtype)


def norm_mod_mm(s, gain, shift, w, bias, acts, geom, aux=None, out_dtype=None):
    out_dtype = out_dtype or BF16
    n, k = s.shape
    m = w.shape[1]
    tn = m // len(acts)
    tpb, nctx_t = geom
    seg = _seg_map(tpb, nctx_t)
    sub = _sub_tiles(n)
    tm = sub * ROW_TILE
    if bias is None:
        bias = jnp.zeros((m,), F32)
    if aux is None:
        aux = jnp.zeros((2, m), F32)
    return pl.pallas_call(
        functools.partial(_nmm_kernel, acts=tuple(acts), sub=sub, seg=seg),
        grid=(n // tm, m // tn),
        in_specs=[pl.BlockSpec((tm, k), lambda i, j: (i, 0)),
                  pl.BlockSpec(gain.shape, lambda i, j: (0, 0, 0)),
                  pl.BlockSpec(shift.shape, lambda i, j: (0, 0, 0)),
                  pl.BlockSpec((k, tn), lambda i, j: (0, j)),
                  pl.BlockSpec((1, tn), lambda i, j: (0, j)),
                  pl.BlockSpec((2, tn), lambda i, j: (0, j))],
        out_specs=pl.BlockSpec((tm, tn), lambda i, j: (i, j)),
        out_shape=jax.ShapeDtypeStruct((n, m), out_dtype),
        scratch_shapes=[pltpu.VMEM((tm, k), BF16)],
        compiler_params=pltpu.CompilerParams(
            dimension_semantics=("arbitrary", "arbitrary"), vmem_limit_bytes=VMEM_LIMIT),
        name="norm_mod_mm",
    )(s, gain, shift, w.astype(BF16), bias.reshape(1, m).astype(F32), aux.astype(F32))


def _gated_residual_store(o_ref, s_ref, gm_ref, y, sub, seg):
    for k in range(sub):
        rows = pl.ds(k * ROW_TILE, ROW_TILE)
        gm = gm_ref[seg(pl.program_id(0) * sub + k)]
        o_ref[rows, :] = s_ref[rows, :] + gm * y[k * ROW_TILE:(k + 1) * ROW_TILE]


def _post_kernel(h_ref, g_ref, s_ref, hg_ref, gm_ref, w_ref, o_ref, *, heads, sub, seg):
    x = h_ref[0].astype(F32) + h_ref[1].astype(F32)
    hd = x.shape[1] // heads
    parts = []
    for h in range(heads):
        xh = x[:, h * hd:(h + 1) * hd]
        parts.append(xh * lax.rsqrt(jnp.mean(xh * xh, axis=-1, keepdims=True) + NORM_EPS))
    y = (jnp.concatenate(parts, axis=1) * hg_ref[...] * g_ref[...].astype(F32)).astype(BF16)
    _gated_residual_store(o_ref, s_ref, gm_ref, jnp.dot(y, w_ref[...], preferred_element_type=F32), sub, seg)


def post_mm_residual(h2, gate_arr, gate_block, s, head_g, gm, w_out, heads, geom):
    n, dm = s.shape
    seg = _seg_map(*geom)
    sub = _sub_tiles(n, most=2)
    tm = sub * ROW_TILE
    return pl.pallas_call(
        functools.partial(_post_kernel, heads=heads, sub=sub, seg=seg),
        grid=(n // tm,),
        in_specs=[pl.BlockSpec((2, tm, dm), lambda i: (0, i, 0)),
                  pl.BlockSpec((tm, dm), lambda i: (i, gate_block)),
                  pl.BlockSpec((tm, dm), lambda i: (i, 0)),
                  pl.BlockSpec((1, dm), lambda i: (0, 0)),
                  pl.BlockSpec(gm.shape, lambda i: (0, 0, 0)),
                  pl.BlockSpec((dm, dm), lambda i: (0, 0))],
        out_specs=pl.BlockSpec((tm, dm), lambda i: (i, 0)),
        out_shape=jax.ShapeDtypeStruct((n, dm), F32),
        compiler_params=pltpu.CompilerParams(dimension_semantics=("arbitrary",), vmem_limit_bytes=VMEM_LIMIT),
        name="post_mm_residual",
    )(h2, gate_arr, s, head_g.reshape(1, dm), gm, w_out.astype(BF16))


CONV_COLS = 512


def _conv_kernel(cur_ref, up_ref, dn_ref, w_ref, sc_ref, o_ref, *, tpb, nctx_t):
    ti = pl.program_id(0) % tpb
    is_ctx = ti < nctx_t
    no_up = jnp.logical_or(is_ctx, ti == nctx_t)
    no_dn = jnp.logical_or(is_ctx, ti == tpb - 1)
    x = cur_ref[...].astype(F32)
    up = jnp.where(no_up, 0.0, up_ref[...].astype(F32))
    dn = jnp.where(no_dn, 0.0, dn_ref[...].astype(F32))
    ext = jnp.concatenate([up, x, dn], axis=0)
    nr = ext.shape[0]
    ext_m = pltpu.roll(ext, 1, axis=0)
    ext_p = pltpu.roll(ext, nr - 1, axis=0)
    tpos = lax.broadcasted_iota(jnp.int32, (ROW_TILE, 1), 0)
    col = tpos % GRID_W
    left_ok = jnp.where(is_ctx, (tpos > 0).astype(F32), (col > 0).astype(F32))
    right_ok = jnp.where(is_ctx, (tpos < ROW_TILE - 1).astype(F32), (col < GRID_W - 1).astype(F32))
    vert = jnp.where(is_ctx, 0.0, 1.0)
    w = w_ref[...]
    acc = None
    for dr in (-1, 0, 1):
        base = GRID_W * (1 + dr)
        r3 = 3 * (dr + 1)
        term = (ext[base:base + ROW_TILE] * w[r3 + 1:r3 + 2]
                + ext_m[base:base + ROW_TILE] * w[r3:r3 + 1] * left_ok
                + ext_p[base:base + ROW_TILE] * w[r3 + 2:r3 + 3] * right_ok)
        if dr != 0:
            term = term * vert
        acc = term if acc is None else acc + term
    o_ref[...] = (acc * jax.nn.sigmoid(acc) * sc_ref[...]).astype(o_ref.dtype)


def conv_silu(z, conv_w, scale, width, geom):
    n = z.shape[0]
    tpb, nctx_t = geom
    assert nctx_t == 1 and ROW_TILE % GRID_W == 0
    hb = ROW_TILE // GRID_W
    last = n // GRID_W - 1
    return pl.pallas_call(
        functools.partial(_conv_kernel, tpb=tpb, nctx_t=nctx_t),
        grid=(n // ROW_TILE, width // CONV_COLS),
        in_specs=[pl.BlockSpec((ROW_TILE, CONV_COLS), lambda i, c: (i, c)),
                  pl.BlockSpec((GRID_W, CONV_COLS), lambda i, c: (jnp.maximum(i * hb - 1, 0), c)),
                  pl.BlockSpec((GRID_W, CONV_COLS), lambda i, c: (jnp.minimum((i + 1) * hb, last), c)),
                  pl.BlockSpec((9, CONV_COLS), lambda i, c: (0, c)),
                  pl.BlockSpec((1, CONV_COLS), lambda i, c: (0, c))],
        out_specs=pl.BlockSpec((ROW_TILE, CONV_COLS), lambda i, c: (i, c)),
        out_shape=jax.ShapeDtypeStruct((n, width), BF16),
        compiler_params=pltpu.CompilerParams(
            dimension_semantics=("arbitrary", "arbitrary"), vmem_limit_bytes=VMEM_LIMIT),
        name="conv_silu",
    )(z, z, z, conv_w.reshape(9, width).astype(F32), scale.reshape(1, width).astype(F32))


def _ml_scan_kernel(q_ref, k_ref, v_ref, gc_ref, gr_ref, o_ref, ct_scr, n_scr, m_scr, *, heads):
    L = q_ref.shape[0]
    d = pl.program_id(0)

    @pl.when(pl.program_id(2) == 0)
    def _():
        ct_scr[...] = jnp.zeros_like(ct_scr)
        n_scr[...] = jnp.zeros_like(n_scr)
        m_scr[...] = jnp.zeros_like(m_scr)

    row = lax.broadcasted_iota(jnp.int32, (L, L), 0)
    col = lax.broadcasted_iota(jnp.int32, (L, L), 1)

    def body(reverse):
        incl = (col >= row) if reverse else (col <= row)
        incl_t = (row >= col) if reverse else (row <= col)
        last = 0 if reverse else L - 1
        hs = range(heads)
        sls = [slice(h * LANES, (h + 1) * LANES) for h in hs]
        qk = [_dot(q_ref[:, sls[h]], k_ref[:, sls[h]], NT) for h in hs]
        qc = [_dot(q_ref[:, sls[h]], ct_scr[h]) for h in hs]
        stats = []
        for h in hs:
            ig_col = gc_ref[0, :, h:h + 1]
            lf_col = gc_ref[0, :, heads + h:heads + h + 1]
            ig_row = gr_ref[0, 0, h:h + 1, :]
            lf_row = gr_ref[0, 0, heads + h:heads + h + 1, :]
            b_col = jnp.sum(jnp.where(incl, lf_row, 0.0), axis=1, keepdims=True)
            b_row = jnp.sum(jnp.where(incl_t, lf_col, 0.0), axis=0, keepdims=True)
            m_prev = m_scr[h:h + 1, 0:1]
            dmat = jnp.where(incl, b_col - b_row + ig_row, -jnp.inf)
            inter = b_col + m_prev
            m_t = jnp.maximum(inter, jnp.max(dmat, axis=1, keepdims=True))
            b_last = b_col[last:last + 1, :]
            m_new = jnp.maximum(b_last + m_prev, jnp.max(b_last - b_row + ig_row, axis=1, keepdims=True))
            w_k = jnp.exp(b_last - b_col + ig_col - m_new)
            w_prev = jnp.exp(b_last + m_prev - m_new)
            stats.append((jnp.exp(dmat - m_t), jnp.exp(inter - m_t), jnp.exp(-m_t), w_k, w_prev, m_new))
        kv = [_dot(k_ref[:, sls[h]].astype(F32).T, stats[h][3] * v_ref[:, sls[h]].astype(F32)) for h in hs]
        s = [qk[h] * stats[h][0] for h in hs]
        sv = [_dot(s[h], v_ref[:, sls[h]]) for h in hs]
        for h in hs:
            _, w_inter, floor, w_k, w_prev, m_new = stats[h]
            n_row = n_scr[h:h + 1, :]
            num = sv[h] + w_inter * qc[h]
            den = (jnp.sum(s[h], axis=1, keepdims=True)
                   + w_inter * jnp.sum(q_ref[:, sls[h]].astype(F32) * n_row, axis=1, keepdims=True))
            o_ref[0, :, sls[h]] = (num / jnp.maximum(jnp.abs(den), floor)).astype(o_ref.dtype)
            ct_scr[h] = w_prev * ct_scr[h] + kv[h]
            n_scr[h:h + 1, :] = w_prev * n_row + jnp.sum(w_k * k_ref[:, sls[h]].astype(F32), axis=0, keepdims=True)
            m_scr[h:h + 1, :] = jnp.broadcast_to(m_new, (1, LANES))

    @pl.when(d == 0)
    def _():
        body(False)

    @pl.when(d == 1)
    def _():
        body(True)


def mlstm_scan(qk, z, gc, gr, dm, bsz, t):
    n = qk.shape[0]
    heads = dm // LANES
    L = ML_CHUNK
    nc, nctx = t // L, CTX_LEN // L

    def row(d, b, p):
        return b * nc + _scan_chunk_index(d, p, nctx, nc)

    return pl.pallas_call(
        functools.partial(_ml_scan_kernel, heads=heads),
        grid=(2, bsz, nc),
        in_specs=[pl.BlockSpec((L, dm), lambda d, b, p: (row(d, b, p), 0)),
                  pl.BlockSpec((L, dm), lambda d, b, p: (row(d, b, p), 1)),
                  pl.BlockSpec((L, dm), lambda d, b, p: (row(d, b, p), 2)),
                  pl.BlockSpec((1, L, 2 * heads), lambda d, b, p: (d, row(d, b, p), 0)),
                  pl.BlockSpec((1, 1, 2 * heads, L),
                               lambda d, b, p: (d, b, 0, _scan_chunk_index(d, p, nctx, nc)))],
        out_specs=pl.BlockSpec((1, L, dm), lambda d, b, p: (d, row(d, b, p), 0)),
        out_shape=jax.ShapeDtypeStruct((2, n, dm), BF16),
        scratch_shapes=[pltpu.VMEM((heads, LANES, LANES), F32), pltpu.VMEM((heads, LANES), F32),
                        pltpu.VMEM((heads, LANES), F32)],
        compiler_params=pltpu.CompilerParams(
            dimension_semantics=("arbitrary", "arbitrary", "arbitrary"), vmem_limit_bytes=VMEM_LIMIT),
        name="mlstm_scan",
    )(qk, qk, z, gc, gr)


RW_STATE_PASSES = 3


def _rw_scan_kernel(lw0_ref, lw1_ref, kd0_ref, kd1_ref, a0_ref, a1_ref, r0_ref, r1_ref, v0_ref, v1_ref,
                    kk0_ref, kk1_ref, of_ref, ob_ref, h_scr, rdp_scr, o0_scr, m_scr, ha_scr, *, nchunk):
    L = RW_CHUNK
    j = pl.program_id(2)

    @pl.when(j == 0)
    def _():
        for ref in (h_scr, rdp_scr, o0_scr, m_scr, ha_scr):
            ref[...] = jnp.zeros_like(ref)

    hs = [h_scr[0], h_scr[1]]

    def recurrence_step(k):
        for d, o_ref in ((0, of_ref), (1, ob_ref)):
            c = k if d == 0 else nchunk - 1 - k
            o_ref[pl.ds(c * L, L), :] = (_dot(rdp_scr[d, c], hs[d], passes=RW_STATE_PASSES)
                                         + o0_scr[d, c]).astype(o_ref.dtype)
            hs[d] = _dot(m_scr[d, c], hs[d], passes=RW_STATE_PASSES) + ha_scr[d, c]

    pending = list(range(nchunk))

    half = LANES // 2
    row = lax.broadcasted_iota(jnp.int32, (L, LANES), 0)
    col = lax.broadcasted_iota(jnp.int32, (L, LANES), 1) % half
    eye2 = (row == col).astype(F32)
    lane = lax.broadcasted_iota(jnp.int32, (1, LANES), 1)
    m0 = (lane < half).astype(F32)
    m1 = (lane >= half).astype(F32)
    r2 = lax.broadcasted_iota(jnp.int32, (LANES, LANES), 0)
    c2 = lax.broadcasted_iota(jnp.int32, (LANES, LANES), 1)
    same_head = (r2 // half) == (c2 // half)

    def stack(x):
        return jnp.concatenate([x * m0, x * m1], axis=0)

    chains = [(d, c) for c in range(nchunk) for d in range(2)]
    st = {}
    for d, c in chains:
        reverse = d == 1
        sl = pl.ds(c * L, L)
        lw = (lw0_ref, lw1_ref)[d][sl, :]
        k = (kd0_ref, kd1_ref)[d][sl, :].astype(F32)
        kk = (kk0_ref, kk1_ref)[d][sl, :].astype(F32)
        akk = kk * (a0_ref, a1_ref)[d][sl, :].astype(F32)
        g = _cumsum_rows(lw, reverse)
        ieg = jnp.exp(-g)
        g_last = g[0:1] if reverse else g[L - 1:L]
        dl = jnp.exp(g_last - g)
        st[d, c] = dict(kd=kk * jnp.exp(g - lw), rd=(r0_ref, r1_ref)[d][sl, :].astype(F32) * jnp.exp(g),
                        ai=akk * ieg, ki=k * ieg, ad=akk * dl, kdd=k * dl, eg_last=jnp.exp(g_last),
                        v=(v0_ref, v1_ref)[d][sl, :].astype(F32))
    recurrence_step(pending.pop(0))
    for d, c in chains:
        s = st[d, c]
        reverse = d == 1
        incl = (col >= row) if reverse else (col <= row)
        strict = (col > row) if reverse else (col < row)
        x = jnp.concatenate([s["kd"], s["rd"]], axis=0)
        rhs = jnp.concatenate([stack(s["ai"]), stack(s["ki"])], axis=0)
        sc = _dot(x, rhs, NT)
        s["a_ab"] = jnp.where(strict, sc[:L, :LANES], 0.0)
        s["a_ak"] = jnp.where(strict, sc[:L, LANES:], 0.0)
        s["b_ra"] = jnp.where(incl, sc[L:, :LANES], 0.0)
        s["b_rk"] = jnp.where(incl, sc[L:, LANES:], 0.0)
        s["tinv"] = eye2 - s["a_ab"]
        s["pw"] = s["a_ab"]
    span = 2
    while span < L:
        for d, c in chains:
            s = st[d, c]
            s["pw"] = _dot(s["pw"], stack(s["pw"]))
        for d, c in chains:
            s = st[d, c]
            s["tinv"] = _dot(s["tinv"], stack(eye2 + s["pw"]))
        if pending:
            recurrence_step(pending.pop(0))
        span *= 2
    while pending:
        recurrence_step(pending.pop(0))
    h_scr[0] = hs[0]
    h_scr[1] = hs[1]
    for d, c in chains:
        s = st[d, c]
        s["w"] = -_dot(s["tinv"], stack(s["a_ak"]))
        s["kdp"] = _dot(s["tinv"], stack(s["kd"]))
    for d, c in chains:
        s = st[d, c]
        s["vst"] = stack(s["v"])
        s["u0"] = _dot(s["w"], s["vst"])
    for d, c in chains:
        s = st[d, c]
        lhs = jnp.concatenate([s["b_ra"], s["b_rk"]], axis=1)
        rhs = jnp.concatenate([stack(s["u0"]), s["vst"]], axis=0)
        o0_scr[d, c] = _dot(lhs, rhs)
        rdp_scr[d, c] = s["rd"] - _dot(s["b_ra"], stack(s["kdp"]))
        diag = jnp.where(r2 == c2, s["eg_last"], 0.0)
        m_scr[d, c] = jnp.where(same_head, diag - _dot(s["ad"].T, s["kdp"]), 0.0)
        at = jnp.concatenate([s["ad"], s["kdd"]], axis=0).T
        ha_scr[d, c] = jnp.where(same_head, _dot(at, jnp.concatenate([s["u0"], s["v"]], axis=0)), 0.0)


def rwkv_scan(lw, kda, rvkg, dm, bsz, t):
    n = lw.shape[0]
    pairs = dm // LANES
    L = RW_CHUNK
    nchunk = RW_PRE_CHUNKS
    tb = nchunk * L
    nblk, nctx = t // tb, CTX_LEN // tb

    def block(d, b, j):
        return b * nblk + _scan_chunk_index(d, jnp.minimum(j, nblk - 1), nctx, nblk)

    def ispec(d, col):
        return pl.BlockSpec((tb, LANES), lambda b, p, j: (block(d, b, j), col * pairs + p))

    def ospec(d):
        return pl.BlockSpec((tb, LANES), lambda b, p, j: (block(d, b, jnp.maximum(j - 1, 0)), p))

    return pl.pallas_call(
        functools.partial(_rw_scan_kernel, nchunk=nchunk),
        grid=(bsz, pairs, nblk + 1),
        in_specs=[ispec(0, 0), ispec(1, 1), ispec(0, 0), ispec(1, 1), ispec(0, 2), ispec(1, 3),
                  ispec(0, 0), ispec(1, 0), ispec(0, 1), ispec(1, 1), ispec(0, 2), ispec(1, 2)],
        out_specs=[ospec(0), ospec(1)],
        out_shape=[jax.ShapeDtypeStruct((n, dm), BF16), jax.ShapeDtypeStruct((n, dm), BF16)],
        scratch_shapes=[pltpu.VMEM((2, LANES, LANES), F32), pltpu.VMEM((2, nchunk, L, LANES), F32),
                        pltpu.VMEM((2, nchunk, L, LANES), F32), pltpu.VMEM((2, nchunk, LANES, LANES), F32),
                        pltpu.VMEM((2, nchunk, LANES, LANES), F32)],
        compiler_params=pltpu.CompilerParams(
            dimension_semantics=("arbitrary", "arbitrary", "arbitrary"), vmem_limit_bytes=VMEM_LIMIT),
        name="rwkv_scan",
    )(lw, lw, kda, kda, kda, kda, rvkg, rvkg, rvkg, rvkg, rvkg, rvkg)


HALO_ROWS = 8


def _group_sum(x, width):
    r = lax.broadcasted_iota(jnp.int32, (LANES, LANES), 0) // width
    c = lax.broadcasted_iota(jnp.int32, (LANES, LANES), 1) // width
    ones = (r == c).astype(BF16)
    hi = x.astype(BF16)
    lo = (x - hi.astype(F32)).astype(BF16)
    parts = []
    for j in range(x.shape[1] // LANES):
        sl = slice(j * LANES, (j + 1) * LANES)
        parts.append(jnp.dot(hi[:, sl], ones, preferred_element_type=F32)
                     + jnp.dot(lo[:, sl], ones, preferred_element_type=F32))
    return jnp.concatenate(parts, axis=1)


def _rw_proj_kernel(s_ref, up_ref, dn_ref, gain_ref, shift_ref, mu_ref, wrkv_ref, w1_ref, w2_ref, w0_ref,
                    a1_ref, a2_ref, a0_ref, g1_ref, g2_ref, kk_ref, ka_ref,
                    lw_ref, kda_ref, rvkg_ref, *, tpb, nctx_t):
    tm, dm = s_ref.shape
    ti = pl.program_id(0) % tpb
    is_ctx = ti < nctx_t
    has_up = jnp.logical_not(jnp.logical_or(is_ctx, ti == nctx_t))
    has_dn = jnp.logical_not(jnp.logical_or(is_ctx, ti == tpb - 1))
    gain = gain_ref[0]
    shift = shift_ref[0]
    u = _norm_mod(s_ref[...], gain, shift)
    u_up = jnp.where(has_up, _norm_mod(up_ref[HALO_ROWS - 1:HALO_ROWS, :], gain, shift), 0.0)
    u_dn = jnp.where(has_dn, _norm_mod(dn_ref[0:1, :], gain, shift), 0.0)
    row = lax.broadcasted_iota(jnp.int32, (tm, 1), 0)
    u_m = jnp.where(row == 0, u_up, pltpu.roll(u, 1, axis=0))
    u_p = jnp.where(row == tm - 1, u_dn, pltpu.roll(u, tm - 1, axis=0))
    du = 0.5 * (u_m + u_p) - u
    mu = mu_ref[...]

    def mix(i):
        return (u + du * mu[i:i + 1]).astype(BF16)

    def dot(a, b):
        return jnp.dot(a.astype(BF16), b, preferred_element_type=F32)

    r = dot(mix(0), wrkv_ref[0])
    k = dot(mix(1), wrkv_ref[1])
    v = dot(mix(2), wrkv_ref[2])
    w_pre = dot(jnp.tanh(dot(mix(3), w1_ref[...])), w2_ref[...]) + w0_ref[...]
    lw_ref[...] = -jnp.exp(-_softplus(-w_pre) - 0.5)
    a = jax.nn.sigmoid(dot(dot(mix(4), a1_ref[...]), a2_ref[...]) + a0_ref[...])
    g = dot(jax.nn.sigmoid(dot(mix(5), g1_ref[...])), g2_ref[...])
    kk = k * kk_ref[...]
    kk = kk * lax.rsqrt(jnp.maximum(_group_sum(kk * kk, RW_HEAD_DIM), 1e-24))
    ka = ka_ref[...]
    for d in range(2):
        kda_ref[:, d * dm:(d + 1) * dm] = (k * (1.0 + (a[:, d * dm:(d + 1) * dm] - 1.0) * ka)).astype(kda_ref.dtype)
    kda_ref[:, 2 * dm:] = a.astype(kda_ref.dtype)
    for j, val in enumerate((r, v, kk, g)):
        rvkg_ref[:, j * dm:(j + 1) * dm] = val.astype(rvkg_ref.dtype)


def rwkv_proj(s, gain, shift, geom, mu, w_rkv, w0, w1, w2, a0, a1, a2, g1, g2, k_k, k_a):
    n, dm = s.shape
    tpb, nctx_t = geom
    assert nctx_t == 1
    seg = _seg_map(tpb, nctx_t)
    hb = ROW_TILE // HALO_ROWS
    last = n // HALO_ROWS - 1
    lora = w1.shape[2]

    def blockdiag(w):
        z = jnp.zeros_like(w[0])
        return jnp.concatenate([jnp.concatenate([w[0], z], axis=1), jnp.concatenate([z, w[1]], axis=1)], axis=0)

    consts = [jnp.pad(mu, ((0, HALO_ROWS - mu.shape[0]), (0, 0))), w_rkv.astype(BF16),
              jnp.concatenate([w1[0], w1[1]], axis=1).astype(BF16), blockdiag(w2).astype(BF16),
              jnp.concatenate([w0[0], w0[1]])[None],
              jnp.concatenate([a1[0], a1[1]], axis=1).astype(BF16), blockdiag(a2).astype(BF16),
              jnp.concatenate([a0[0], a0[1]])[None],
              g1.astype(BF16), g2.astype(BF16), k_k[None], k_a[None]]

    def const_spec(x):
        nd = x.ndim
        return pl.BlockSpec(x.shape, lambda i: (0,) * nd)

    return pl.pallas_call(
        functools.partial(_rw_proj_kernel, tpb=tpb, nctx_t=nctx_t),
        grid=(n // ROW_TILE,),
        in_specs=[pl.BlockSpec((ROW_TILE, dm), lambda i: (i, 0)),
                  pl.BlockSpec((HALO_ROWS, dm), lambda i: (jnp.maximum(i * hb - 1, 0), 0)),
                  pl.BlockSpec((HALO_ROWS, dm), lambda i: (jnp.minimum((i + 1) * hb, last), 0)),
                  pl.BlockSpec((1, 1, dm), lambda i: (seg(i), 0, 0)),
                  pl.BlockSpec((1, 1, dm), lambda i: (seg(i), 0, 0))] + [const_spec(x) for x in consts],
        out_specs=[pl.BlockSpec((ROW_TILE, 2 * dm), lambda i: (i, 0)),
                   pl.BlockSpec((ROW_TILE, 4 * dm), lambda i: (i, 0)),
                   pl.BlockSpec((ROW_TILE, 4 * dm), lambda i: (i, 0))],
        out_shape=[jax.ShapeDtypeStruct((n, 2 * dm), F32), jax.ShapeDtypeStruct((n, 4 * dm), BF16),
                   jax.ShapeDtypeStruct((n, 4 * dm), BF16)],
        compiler_params=pltpu.CompilerParams(dimension_semantics=("arbitrary",), vmem_limit_bytes=VMEM_LIMIT_BIG),
        name="rwkv_proj",
    )(s, s, s, gain, shift, *consts)


def _rw_post_kernel(of_ref, ob_ref, r_ref, v_ref, g_ref, k0_ref, k1_ref, s_ref, lnw_ref, lnb_ref, rk_ref, gm_ref,
                    w_ref, out_ref, *, sub, seg):
    o = of_ref[...].astype(F32) + ob_ref[...].astype(F32)
    inv = 1.0 / RW_HEAD_DIM
    mean = _group_sum(o, RW_HEAD_DIM) * inv
    oc = o - mean
    var = _group_sum(oc * oc, RW_HEAD_DIM) * inv
    xn = oc * lax.rsqrt(var + RW_GN_EPS) * lnw_ref[...] + lnb_ref[...]
    r = r_ref[...].astype(F32)
    ksum = k0_ref[...].astype(F32) + k1_ref[...].astype(F32)
    bonus = _group_sum(r * ksum * rk_ref[...], RW_HEAD_DIM) * v_ref[...].astype(F32)
    y = ((xn + bonus) * g_ref[...].astype(F32)).astype(BF16)
    _gated_residual_store(out_ref, s_ref, gm_ref, jnp.dot(y, w_ref[...], preferred_element_type=F32), sub, seg)


def rwkv_post(o_f, o_b, rvkg, kda, s, ln_w, ln_b, r_k, gm, w_out, geom):
    n, dm = s.shape
    seg = _seg_map(*geom)
    sub = _sub_tiles(n, most=2)
    tm = sub * ROW_TILE

    def col(block):
        return pl.BlockSpec((tm, dm), lambda i: (i, block))

    vec = pl.BlockSpec((1, dm), lambda i: (0, 0))
    return pl.pallas_call(
        functools.partial(_rw_post_kernel, sub=sub, seg=seg),
        grid=(n // tm,),
        in_specs=[col(0), col(0), col(0), col(1), col(3), col(0), col(1),
                  col(0), vec, vec, vec, pl.BlockSpec(gm.shape, lambda i: (0, 0, 0)),
                  pl.BlockSpec((dm, dm), lambda i: (0, 0))],
        out_specs=pl.BlockSpec((tm, dm), lambda i: (i, 0)),
        out_shape=jax.ShapeDtypeStruct((n, dm), F32),
        compiler_params=pltpu.CompilerParams(dimension_semantics=("arbitrary",), vmem_limit_bytes=VMEM_LIMIT),
        name="rwkv_post",
    )(o_f, o_b, rvkg, rvkg, rvkg, kda, kda, s, ln_w[None], ln_b[None], r_k[None], gm, w_out.astype(BF16))


def _hg_scan_kernel(q_ref, v_ref, lf_ref, o_ref, st_scr, *, heads):
    C = q_ref.shape[0]
    d = pl.program_id(0)
    nsub = C // HG_SUB

    @pl.when(pl.program_id(2) == 0)
    def _():
        st_scr[...] = jnp.zeros_like(st_scr)

    def body(reverse):
        last = 0 if reverse else C - 1
        hs = range(heads)
        sls = [slice(h * LANES, (h + 1) * LANES) for h in hs]
        g = [lf_ref[:, sls[h]] for h in hs]
        b = [_cumsum_rows(g[h], reverse) for h in hs]
        k = [-jnp.tanh(0.5 * g[h]) * (jnp.exp(g[h]) + 1.0) for h in hs]
        o_inter = [_dot(q_ref[:, sls[h]].astype(F32) * jnp.exp(b[h]), st_scr[h], NT) for h in hs]
        parts = [[None] * nsub for _ in hs]
        for i in range(nsub):
            r0 = i * HG_SUB
            lo, hi = (r0, C) if reverse else (0, r0 + HG_SUB)
            first = r0 + HG_SUB - 1 if reverse else r0
            row = lax.broadcasted_iota(jnp.int32, (HG_SUB, hi - lo), 0) + r0
            col = lax.broadcasted_iota(jnp.int32, (HG_SUB, hi - lo), 1) + lo
            keep = (col >= row) if reverse else (col <= row)
            att = []
            for h in hs:
                rho = b[h][first:first + 1, :] - g[h][first:first + 1, :]
                qi = q_ref[r0:r0 + HG_SUB, sls[h]].astype(F32) * jnp.exp(b[h][r0:r0 + HG_SUB] - rho)
                ki = k[h][lo:hi] * jnp.exp(jnp.minimum(rho - b[h][lo:hi], HG_EXP_CLAMP))
                att.append(jnp.where(keep, _dot(qi, ki, NT), 0.0))
            for h in hs:
                parts[h][i] = _dot(att[h], v_ref[lo:hi, sls[h]])
        for h in hs:
            o_ref[0, :, sls[h]] = (o_inter[h] + jnp.concatenate(parts[h], axis=0)).astype(o_ref.dtype)
        upd = []
        for h in hs:
            b_last = b[h][last:last + 1, :]
            upd.append((jnp.exp(b_last),
                        _dot(v_ref[:, sls[h]].astype(F32).T, k[h] * jnp.exp(b_last - b[h]))))
        for h in hs:
            st_scr[h] = st_scr[h] * upd[h][0] + upd[h][1]

    @pl.when(d == 0)
    def _():
        body(False)

    @pl.when(d == 1)
    def _():
        body(True)


def hgrn_scan(z, logf, dm, bsz, t):
    n = z.shape[0]
    heads = dm // LANES
    C = HG_CHUNK
    nc, nctx = t // C, CTX_LEN // C

    def row(d, b, p):
        return b * nc + _scan_chunk_index(d, p, nctx, nc)

    return pl.pallas_call(
        functools.partial(_hg_scan_kernel, heads=heads),
        grid=(2, bsz, nc),
        in_specs=[pl.BlockSpec((C, dm), lambda d, b, p: (row(d, b, p), 0)),
                  pl.BlockSpec((C, dm), lambda d, b, p: (row(d, b, p), 1)),
                  pl.BlockSpec((C, dm), lambda d, b, p: (row(d, b, p), d))],
        out_specs=pl.BlockSpec((1, C, dm), lambda d, b, p: (d, row(d, b, p), 0)),
        out_shape=jax.ShapeDtypeStruct((2, n, dm), BF16),
        scratch_shapes=[pltpu.VMEM((heads, LANES, LANES), F32)],
        compiler_params=pltpu.CompilerParams(
            dimension_semantics=("arbitrary", "arbitrary", "arbitrary"), vmem_limit_bytes=VMEM_LIMIT),
        name="hgrn_scan",
    )(z, z, logf)


def _first_argmax(vals):
    best, idx = vals[0], jnp.zeros(vals[0].shape, jnp.int32)
    for i in range(1, len(vals)):
        better = vals[i] > best
        best = jnp.where(better, vals[i], best)
        idx = jnp.where(better, i, idx)
    return best, idx


def _router_kernel(s_ref, gain_ref, shift_ref, wt_ref, b_ref, e_ref, g_ref, *, n_groups, top_k):
    n_experts = wt_ref.shape[0]
    per = n_experts // n_groups
    h = _norm_mod(s_ref[...], gain_ref[0], shift_ref[0])
    aff = jax.nn.sigmoid(_dot(wt_ref[...], h, NT, passes=3))
    sel = aff + b_ref[...]
    a = [aff[e:e + 1, :] for e in range(n_experts)]
    s = [sel[e:e + 1, :] for e in range(n_experts)]
    neg = jnp.full_like(s[0], -jnp.inf)
    scores = []
    for g in range(n_groups):
        grp = s[g * per:(g + 1) * per]
        m1, i1 = _first_argmax(grp)
        m2, _ = _first_argmax([jnp.where(i1 == j, neg, grp[j]) for j in range(per)])
        scores.append(m1 + m2)
    _, best = _first_argmax(scores)

    def in_best(rows):
        out = []
        for j in range(per):
            x = rows[j]
            for g in range(1, n_groups):
                x = jnp.where(best == g, rows[g * per + j], x)
            out.append(x)
        return out

    sb, ab = in_best(s), in_best(a)
    picked, chosen = [], []
    cand = sb
    for _ in range(top_k):
        _, i = _first_argmax(cand)
        c = ab[0]
        for j in range(1, per):
            c = jnp.where(i == j, ab[j], c)
        picked.append(i)
        chosen.append(c)
        cand = [jnp.where(i == j, neg, cand[j]) for j in range(per)]
    total = functools.reduce(jnp.add, chosen)
    for kk_ in range(top_k):
        e_ref[kk_:kk_ + 1, :] = best * per + picked[kk_]
        g_ref[kk_:kk_ + 1, :] = chosen[kk_] / total


def norm_route(s, gain, shift, geom, router_w, router_b):
    n, k = s.shape
    n_experts = router_w.shape[1]
    tm = ROW_TILE
    seg = _seg_map(*geom)
    return pl.pallas_call(
        functools.partial(_router_kernel, n_groups=N_GROUPS, top_k=TOP_K),
        grid=(n // tm,),
        in_specs=[pl.BlockSpec((tm, k), lambda i: (i, 0)),
                  pl.BlockSpec((1, 1, k), lambda i: (seg(i), 0, 0)),
                  pl.BlockSpec((1, 1, k), lambda i: (seg(i), 0, 0)),
                  pl.BlockSpec((n_experts, k), lambda i: (0, 0)),
                  pl.BlockSpec((n_experts, 1), lambda i: (0, 0))],
        out_specs=[pl.BlockSpec((TOP_K, tm), lambda i: (0, i)), pl.BlockSpec((TOP_K, tm), lambda i: (0, i))],
        out_shape=[jax.ShapeDtypeStruct((TOP_K, n), jnp.int32), jax.ShapeDtypeStruct((TOP_K, n), F32)],
        compiler_params=pltpu.CompilerParams(dimension_semantics=("arbitrary",), vmem_limit_bytes=VMEM_LIMIT),
        name="norm_route",
    )(s, gain, shift, router_w.T, router_b.reshape(n_experts, 1).astype(F32))


def _final_norm_kernel(s_ref, g_ref, o_ref):
    x = s_ref[...]
    o_ref[0] = x * lax.rsqrt(jnp.mean(x * x, axis=-1, keepdims=True) + NORM_EPS) * g_ref[...]


def final_norm(s, g, bsz, t, geom):
    dm = s.shape[1]
    tpb, nctx_t = geom
    return pl.pallas_call(
        _final_norm_kernel,
        grid=(bsz, tpb - nctx_t),
        in_specs=[pl.BlockSpec((ROW_TILE, dm), lambda b, i: (b * tpb + nctx_t + i, 0)),
                  pl.BlockSpec((1, dm), lambda b, i: (0, 0))],
        out_specs=pl.BlockSpec((1, ROW_TILE, dm), lambda b, i: (b, i, 0)),
        out_shape=jax.ShapeDtypeStruct((bsz, t - CTX_LEN, dm), F32),
        name="final_norm",
    )(s, g[None])


def _row_copy_waits(src_row, dst_row, sem, count):
    def body(_, carry):
        pltpu.make_async_copy(src_row, dst_row, sem).wait()
        return carry
    lax.fori_loop(0, count, body, 0)


def _scatter_kernel(dest_ref, meta_ref, s_ref, gain_ref, shift_ref, xb_ref, hbuf, zrow, sems, zsem, *, n_experts):
    i = pl.program_id(0)
    last = pl.num_programs(0) - 1
    slot = i % 2
    per_tile = TOP_K * ROW_TILE

    def wait_tile(sl):
        _row_copy_waits(hbuf.at[sl, pl.ds(0, 1), :], xb_ref.at[pl.ds(0, 1), :], sems.at[sl], per_tile)

    @pl.when(i >= 2)
    def _():
        wait_tile(slot)

    hbuf[slot] = _norm_mod(s_ref[...], gain_ref[0], shift_ref[0])

    def issue(r, carry):
        for k in range(TOP_K):
            d = dest_ref[0, 0, TOP_K * r + k]
            pltpu.make_async_copy(hbuf.at[slot, pl.ds(r, 1), :], xb_ref.at[pl.ds(d, 1), :], sems.at[slot]).start()
        return carry
    lax.fori_loop(0, ROW_TILE, issue, 0)

    @pl.when(i == last)
    def _():
        @pl.when(last >= 1)
        def _():
            wait_tile(1 - slot)
        wait_tile(slot)
        zrow[...] = jnp.zeros_like(zrow)
        for e in range(n_experts):
            lo = meta_ref[2, e] + meta_ref[0, e]
            hi = meta_ref[2, e] + meta_ref[1, e]

            def pad_start(q, carry):
                pltpu.make_async_copy(zrow.at[pl.ds(0, 1), :], xb_ref.at[pl.ds(q, 1), :], zsem.at[0]).start()
                return carry
            lax.fori_loop(lo, hi, pad_start, 0)
            _row_copy_waits(zrow.at[pl.ds(0, 1), :], xb_ref.at[pl.ds(0, 1), :], zsem.at[0], hi - lo)
        end = meta_ref[2, n_experts - 1] + meta_ref[1, n_experts - 1]

        def tail_start(q, carry):
            pltpu.make_async_copy(zrow.at[pl.ds(0, 1), :], xb_ref.at[pl.ds(q, 1), :], zsem.at[0]).start()
            return carry
        lax.fori_loop(end, xb_ref.shape[0], tail_start, 0)
        _row_copy_waits(zrow.at[pl.ds(0, 1), :], xb_ref.at[pl.ds(0, 1), :], zsem.at[0], xb_ref.shape[0] - end)


def moe_scatter(s, gain, shift, geom, dest, meta, n_slots):
    n, dm = s.shape
    seg = _seg_map(*geom)
    nt = n // ROW_TILE
    return pl.pallas_call(
        functools.partial(_scatter_kernel, n_experts=meta.shape[1]),
        grid=(nt,),
        in_specs=[pl.BlockSpec((1, 1, TOP_K * ROW_TILE), lambda i: (i, 0, 0), memory_space=pltpu.SMEM),
                  pl.BlockSpec(memory_space=pltpu.SMEM),
                  pl.BlockSpec((ROW_TILE, dm), lambda i: (i, 0)),
                  pl.BlockSpec((1, 1, dm), lambda i: (seg(i), 0, 0)),
                  pl.BlockSpec((1, 1, dm), lambda i: (seg(i), 0, 0))],
        out_specs=pl.BlockSpec(memory_space=pl.ANY),
        out_shape=jax.ShapeDtypeStruct((n_slots, dm), F32),
        scratch_shapes=[pltpu.VMEM((2, ROW_TILE, dm), F32), pltpu.VMEM((8, dm), F32),
                        pltpu.SemaphoreType.DMA((2,)), pltpu.SemaphoreType.DMA((1,))],
        compiler_params=pltpu.CompilerParams(dimension_semantics=("arbitrary",), vmem_limit_bytes=VMEM_LIMIT),
        name="moe_scatter",
    )(dest.reshape(nt, 1, TOP_K * ROW_TILE), meta, s, gain, shift)


def _gather_combine_kernel(dcur_ref, dnxt_ref, g_ref, s_ref, gm_ref, yb_ref, o_ref, ybuf, sems):
    i = pl.program_id(0)
    nsteps = pl.num_programs(0)
    slot = i % 2

    def start_tile(dref, sl):
        def issue(r, carry):
            for k in range(TOP_K):
                d = dref[0, 0, TOP_K * r + k]
                pltpu.make_async_copy(yb_ref.at[pl.ds(d, 1), :], ybuf.at[sl, k, pl.ds(r, 1), :], sems.at[sl]).start()
            return carry
        lax.fori_loop(0, ROW_TILE, issue, 0)

    @pl.when(i == 0)
    def _():
        start_tile(dcur_ref, 0)

    @pl.when(i + 1 < nsteps)
    def _():
        start_tile(dnxt_ref, 1 - slot)

    _row_copy_waits(yb_ref.at[pl.ds(0, 1), :], ybuf.at[slot, 0, pl.ds(0, 1), :], sems.at[slot], TOP_K * ROW_TILE)
    g = g_ref[...]
    y = sum(ybuf[slot, k] * g[:, k:k + 1] for k in range(TOP_K))
    o_ref[...] = s_ref[...] + gm_ref[0] * y


def moe_gather_combine(yb, dest, gate, s, gm, geom):
    n, dm = s.shape
    seg = _seg_map(*geom)
    nt = n // ROW_TILE
    d3 = dest.reshape(nt, 1, TOP_K * ROW_TILE)
    row = pl.BlockSpec((ROW_TILE, dm), lambda i: (i, 0))
    return pl.pallas_call(
        _gather_combine_kernel,
        grid=(nt,),
        in_specs=[pl.BlockSpec((1, 1, TOP_K * ROW_TILE), lambda i: (i, 0, 0), memory_space=pltpu.SMEM),
                  pl.BlockSpec((1, 1, TOP_K * ROW_TILE), lambda i: (jnp.minimum(i + 1, nt - 1), 0, 0),
                               memory_space=pltpu.SMEM),
                  pl.BlockSpec((ROW_TILE, TOP_K), lambda i: (i, 0)), row,
                  pl.BlockSpec((1, 1, dm), lambda i: (seg(i), 0, 0)),
                  pl.BlockSpec(memory_space=pl.ANY)],
        out_specs=row,
        out_shape=jax.ShapeDtypeStruct((n, dm), F32),
        scratch_shapes=[pltpu.VMEM((2, TOP_K, ROW_TILE, dm), F32), pltpu.SemaphoreType.DMA((2,))],
        compiler_params=pltpu.CompilerParams(dimension_semantics=("arbitrary",), vmem_limit_bytes=VMEM_LIMIT),
        name="moe_gather_combine",
    )(d3, d3, gate, s, gm, yb)


def _rank_kernel(e_ref, rank_ref, cnt_ref, carry_scr, *, n_experts):
    @pl.when(pl.program_id(0) == 0)
    def _():
        carry_scr[...] = jnp.zeros_like(carry_scr)

    bl = e_ref.shape[2]
    e_row = e_ref[0]
    sub = lax.broadcasted_iota(jnp.int32, (n_experts, bl), 0)
    onehot = (sub == e_row).astype(F32)
    ri = lax.broadcasted_iota(jnp.int32, (bl, bl), 0)
    ci = lax.broadcasted_iota(jnp.int32, (bl, bl), 1)
    earlier = (ri < ci).astype(BF16)
    cum = jnp.dot(onehot.astype(BF16), earlier, preferred_element_type=F32)
    carry = carry_scr[...]
    rank_ref[0] = jnp.sum(onehot * (cum + carry[:, :1]), axis=0, keepdims=True).astype(jnp.int32)
    carry = carry + jnp.sum(onehot, axis=1, keepdims=True)
    carry_scr[...] = carry
    cnt_ref[...] = carry.astype(jnp.int32)


def assignment_ranks(flat_e, n_experts):
    n_assign = flat_e.shape[0]
    bl = _pick_tile(n_assign, (512, 256, 128))
    nblk = n_assign // bl
    rank, cnt = pl.pallas_call(
        functools.partial(_rank_kernel, n_experts=n_experts),
        grid=(nblk,),
        in_specs=[pl.BlockSpec((1, 1, bl), lambda i: (i, 0, 0))],
        out_specs=[pl.BlockSpec((1, 1, bl), lambda i: (i, 0, 0)),
                   pl.BlockSpec((n_experts, LANES), lambda i: (0, 0))],
        out_shape=[jax.ShapeDtypeStruct((nblk, 1, bl), jnp.int32),
                   jax.ShapeDtypeStruct((n_experts, LANES), jnp.int32)],
        scratch_shapes=[pltpu.VMEM((n_experts, LANES), F32)],
        compiler_params=pltpu.CompilerParams(dimension_semantics=("arbitrary",)),
        name="assignment_ranks",
    )(flat_e.reshape(nblk, 1, bl))
    return rank.reshape(n_assign), cnt[:, 0]


def _ffn_kernel(be_ref, x_ref, w1_ref, w3_ref, w2_ref, o_ref):
    del be_ref
    x = x_ref[...].astype(BF16)
    a = jnp.dot(x, w1_ref[0], preferred_element_type=F32)
    b = jnp.dot(x, w3_ref[0], preferred_element_type=F32)
    hid = (a * jax.nn.sigmoid(a) * b).astype(BF16)
    o_ref[...] = jnp.dot(hid, w2_ref[0], preferred_element_type=F32).astype(o_ref.dtype)


def expert_ffn(xb, block_expert, w1, w3, w2):
    nrows, dm = xb.shape
    f = w1.shape[2]
    nb = nrows // MOE_BLOCK
    return pl.pallas_call(
        _ffn_kernel,
        grid_spec=pltpu.PrefetchScalarGridSpec(
            num_scalar_prefetch=1,
            grid=(nb,),
            in_specs=[pl.BlockSpec((MOE_BLOCK, dm), lambda i, be: (i, 0)),
                      pl.BlockSpec((1, dm, f), lambda i, be: (be[i], 0, 0)),
                      pl.BlockSpec((1, dm, f), lambda i, be: (be[i], 0, 0)),
                      pl.BlockSpec((1, f, dm), lambda i, be: (be[i], 0, 0))],
            out_specs=pl.BlockSpec((MOE_BLOCK, dm), lambda i, be: (i, 0))),
        out_shape=jax.ShapeDtypeStruct((nrows, dm), F32),
        compiler_params=pltpu.CompilerParams(
            dimension_semantics=("arbitrary",), vmem_limit_bytes=VMEM_LIMIT),
        name="expert_ffn",
    )(block_expert, xb, w1.astype(BF16), w3.astype(BF16), w2.astype(BF16))


def _mlstm_layer(s, gain, shift, gate_mod, geom, bsz, t, w_in, w_gate, b_gate, conv, head_g, w_out):
    n, dm = s.shape
    heads = ML_HEADS
    z = norm_mod_mm(s, gain, shift, w_in, None, (None, None, None, "sigmoid"), geom)
    scale = jnp.concatenate([jnp.ones((dm,), F32), jnp.full((dm,), (dm // heads) ** -0.5, F32)])
    qk = conv_silu(z, conv, scale, 2 * dm, geom)
    ng = 4 * heads
    wg = jnp.pad(jnp.concatenate([w_gate[0], w_gate[1]], axis=1), ((0, 0), (0, LANES - ng)))
    bg = jnp.pad(jnp.concatenate([b_gate[0], b_gate[1]]), (0, LANES - ng))
    gates = norm_mod_mm(s, gain, shift, wg, bg, (None,), geom, out_dtype=F32)[:, :ng]
    gates = gates.reshape(bsz, t, 2, 2 * heads)
    gates = jnp.concatenate([gates[..., :heads], jax.nn.log_sigmoid(gates[..., heads:])], axis=-1)
    gc = jnp.moveaxis(gates, 2, 0).reshape(2, n, 2 * heads)
    gr = jnp.transpose(gates, (2, 0, 3, 1))
    h = mlstm_scan(qk, z, gc, gr, dm, bsz, t)
    return post_mm_residual(h, z, 3, s, head_g, gate_mod, w_out, heads, geom)


def _rwkv7_layer(s, gain, shift, gate_mod, geom, bsz, t, mu, w_rkv, w0, w1, w2, a0, a1, a2, g1, g2,
                 k_k, k_a, r_k, ln_w, ln_b, w_out):
    dm = s.shape[1]
    lw, kda, rvkg = rwkv_proj(s, gain, shift, geom, mu, w_rkv, w0, w1, w2, a0, a1, a2, g1, g2, k_k, k_a)
    o_f, o_b = rwkv_scan(lw, kda, rvkg, dm, bsz, t)
    return rwkv_post(o_f, o_b, rvkg, kda, s, ln_w, ln_b, r_k, gate_mod, w_out, geom)


def _hgrn2_layer(s, gain, shift, gate_mod, geom, bsz, t, layer_idx, w_in, w_f, b_f, lb_logits, head_g, w_out):
    dm = s.shape[1]
    z = norm_mod_mm(s, gain, shift, w_in, None, ("silu", None, "silu"), geom)
    p = jax.nn.softmax(lb_logits, axis=0)
    lb = jnp.cumsum(p, axis=0)[layer_idx] - p[0]
    aux = jnp.tile(jnp.stack([jnp.log(lb), jnp.log1p(-lb)]), (1, 2))
    log_f = norm_mod_mm(s, gain, shift, jnp.concatenate([w_f[0], w_f[1]], axis=1),
                        jnp.concatenate([b_f[0], b_f[1]]), ("logf", "logf"), geom, aux=aux, out_dtype=F32)
    o = hgrn_scan(z, log_f, dm, bsz, t)
    return post_mm_residual(o, z, 2, s, head_g, gate_mod, w_out, dm // HG_EXPAND, geom)


def _moe_layer(s, gain, shift, gate_mod, geom, router_w, router_b, w1, w3, w2):
    n_tok, d = s.shape
    n_experts = w1.shape[0]
    n_assign = n_tok * TOP_K
    e, g = norm_route(s, gain, shift, geom, router_w, router_b)
    flat_e = e.T.reshape(n_assign)
    rank, counts = assignment_ranks(flat_e, n_experts)
    padded = (counts + MOE_BLOCK - 1) // MOE_BLOCK * MOE_BLOCK
    end_pad = jnp.cumsum(padded)
    start_pad = end_pad - padded
    onehot = flat_e[:, None] == jnp.arange(n_experts, dtype=jnp.int32)[None, :]
    dest = jnp.sum(jnp.where(onehot, start_pad[None, :], 0), axis=1) + rank
    n_blocks = -(-n_assign // MOE_BLOCK) + n_experts
    block_start = jnp.arange(n_blocks, dtype=jnp.int32) * MOE_BLOCK
    block_expert = jnp.minimum(jnp.sum(end_pad[None, :] <= block_start[:, None], axis=1), n_experts - 1)
    meta = jnp.stack([counts, padded, start_pad]).astype(jnp.int32)
    xb = moe_scatter(s, gain, shift, geom, dest.astype(jnp.int32), meta, n_blocks * MOE_BLOCK)
    yb = expert_ffn(xb, block_expert.astype(jnp.int32), w1, w3, w2)
    return moe_gather_combine(yb, dest.astype(jnp.int32), g.T, s, gate_mod, geom)


def kernel(x, c, ctx, c_ctx, ada_w, ada_b, norm_mix, norm_ffn, norm_out, ml_w_in, ml_w_gate, ml_b_gate, ml_conv, ml_head_g, ml_w_out, rw_mu, rw_w_rkv, rw_w0, rw_w1, rw_w2, rw_a0, rw_a1, rw_a2, rw_g1, rw_g2, rw_k_k, rw_k_a, rw_r_k, rw_ln_w, rw_ln_b, rw_w_out, hg_w_in, hg_w_f, hg_b_f, hg_lb_logits, hg_head_g, hg_w_out, router_w, router_b, ex_w1, ex_w3, ex_w2):
    depth = ada_w.shape[0]
    bsz = x.shape[0]
    cond = jax.nn.silu(jnp.concatenate([c, c_ctx[None]], axis=0))
    cond = jnp.pad(cond, ((0, -(bsz + 1) % 8), (0, 0)))
    dm = x.shape[2]
    t = CTX_LEN + x.shape[1]
    n = bsz * t
    geom = (t // ROW_TILE, CTX_LEN // ROW_TILE)
    s = jnp.concatenate([ctx, x], axis=1).reshape(n, dm)
    for i in range(depth):
        mod = mm(cond, ada_w[i], bias=ada_b[i])
        mod_x = jnp.split(mod[:bsz, None, :], 6, axis=-1)
        mod_c = jnp.split(mod[bsz], 6, axis=-1)

        def table(idx):
            return jnp.stack([jnp.broadcast_to(mod_c[idx], (bsz, dm)), mod_x[idx][:, 0]], axis=1).reshape(2 * bsz, 1, dm)

        kind, j = i % N_MIXERS, i // N_MIXERS
        if kind == 2:
            s = _hgrn2_layer(s, norm_mix[i] * (1 + table(1)), table(0), table(2), geom, bsz, t, i,
                             hg_w_in[j], hg_w_f[j], hg_b_f[j], hg_lb_logits, hg_head_g[j], hg_w_out[j])
        elif kind == 0:
            s = _mlstm_layer(s, norm_mix[i] * (1 + table(1)), table(0), table(2), geom, bsz, t,
                             ml_w_in[j], ml_w_gate[j], ml_b_gate[j], ml_conv[j], ml_head_g[j], ml_w_out[j])
        else:
            s = _rwkv7_layer(s, norm_mix[i] * (1 + table(1)), table(0), table(2), geom, bsz, t,
                             rw_mu[j], rw_w_rkv[j], rw_w0[j], rw_w1[j], rw_w2[j], rw_a0[j],
                             rw_a1[j], rw_a2[j], rw_g1[j], rw_g2[j], rw_k_k[j], rw_k_a[j],
                             rw_r_k[j], rw_ln_w[j], rw_ln_b[j], rw_w_out[j])
        s = _moe_layer(s, norm_ffn[i] * (1 + table(4)), table(3), table(5), geom, router_w, router_b,
                       ex_w1[i], ex_w3[i], ex_w2[i])
    return final_norm(s, norm_out, bsz, t, geom)
```

```python
import functools

import jax
import jax.numpy as jnp
from jax import lax
from jax.experimental import pallas as pl
from jax.experimental.pallas import tpu as pltpu

F32 = jnp.float32
BF16 = jnp.bfloat16

GRID_W = 64
CTX_LEN = 256
N_MIXERS = 3
NORM_EPS = 1e-6
ML_HEADS = 8
RW_HEAD_DIM = 64
RW_GN_EPS = 64e-5
HG_EXPAND = 128
N_GROUPS = 4
TOP_K = 2
MOE_BLOCK = 256

LANES = 128
ML_CHUNK = 128
RW_CHUNK = 64
RW_PRE_CHUNKS = 4
HG_CHUNK = 64
HG_SUB = 16
HG_EXP_CLAMP = 80.0
VMEM_LIMIT = 48 * 1024 * 1024
VMEM_LIMIT_BIG = 56 * 1024 * 1024

NT = (((1,), (1,)), ((), ()))
NN = (((1,), (0,)), ((), ()))


def _dot(a, b, dims=NN, passes=1):
    a_hi = a.astype(BF16)
    b_hi = b.astype(BF16)
    out = lax.dot_general(a_hi, b_hi, dims, preferred_element_type=F32)
    if passes == 3:
        a_lo = (a - a_hi.astype(F32)).astype(BF16)
        b_lo = (b - b_hi.astype(F32)).astype(BF16)
        out = out + lax.dot_general(a_hi, b_lo, dims, preferred_element_type=F32)
        out = out + lax.dot_general(a_lo, b_hi, dims, preferred_element_type=F32)
    return out


def _cumsum_rows(x, reverse):
    n = x.shape[0]
    row = lax.broadcasted_iota(jnp.int32, x.shape, 0)
    s = 1
    while s < n:
        if reverse:
            x = x + jnp.where(row < n - s, pltpu.roll(x, n - s, axis=0), 0.0)
        else:
            x = x + jnp.where(row >= s, pltpu.roll(x, s, axis=0), 0.0)
        s *= 2
    return x


def _pick_tile(n, candidates):
    for c in candidates:
        if n % c == 0:
            return c
    raise ValueError(f"no tile for {n}")


def _scan_chunk_index(d, p, nctx, nc):
    rev = jnp.where(p < nctx, nctx - 1 - p, nc - 1 - (p - nctx))
    return jnp.where(d == 0, p, rev)


_ACTS = {
    None: lambda y: y,
    "sigmoid": jax.nn.sigmoid,
    "silu": lambda y: y * jax.nn.sigmoid(y),
    "tanh": jnp.tanh,
}


def _mm_kernel(x_ref, w_ref, b_ref, o_ref, *, act, precise):
    if precise:
        y = _dot(x_ref[...], w_ref[...], passes=3)
    else:
        y = jnp.dot(x_ref[...].astype(BF16), w_ref[...], preferred_element_type=F32)
    o_ref[...] = _ACTS[act](y + b_ref[...]).astype(o_ref.dtype)


def mm(x, w, bias=None, act=None, out_dtype=F32, precise=False):
    n, k = x.shape
    m = w.shape[1]
    tm = _pick_tile(n, (512, 256, 128, 64, 32, 16, 8))
    tn = m if m <= 1024 else _pick_tile(m, (1024, 512, 256, 128))
    if not precise:
        w = w.astype(BF16)
    if bias is None:
        bias = jnp.zeros((m,), F32)
    return pl.pallas_call(
        functools.partial(_mm_kernel, act=act, precise=precise),
        grid=(n // tm, m // tn),
        in_specs=[pl.BlockSpec((tm, k), lambda i, j: (i, 0)),
                  pl.BlockSpec((k, tn), lambda i, j: (0, j)),
                  pl.BlockSpec((1, tn), lambda i, j: (0, j))],
        out_specs=pl.BlockSpec((tm, tn), lambda i, j: (i, j)),
        out_shape=jax.ShapeDtypeStruct((n, m), out_dtype),
        compiler_params=pltpu.CompilerParams(vmem_limit_bytes=VMEM_LIMIT),
        name="mm",
    )(x, w, bias.reshape(1, m).astype(F32))


ROW_TILE = 256


def _log1p_exp_neg_abs(x):
    return jnp.log(1.0 + jnp.exp(-jnp.abs(x)))


def _log_sigmoid(y):
    return jnp.minimum(y, 0.0) - _log1p_exp_neg_abs(y)


def _softplus(x):
    return jnp.maximum(x, 0.0) + _log1p_exp_neg_abs(x)


def _logaddexp(a, b):
    return jnp.maximum(a, b) + _log1p_exp_neg_abs(a - b)


def _norm_mod(x, gain, shift):
    return x * lax.rsqrt(jnp.mean(x * x, axis=-1, keepdims=True) + NORM_EPS) * gain + shift


def _seg_map(tpb, nctx_t):
    def seg(i):
        return (i // tpb) * 2 + jnp.where(i % tpb < nctx_t, 0, 1)
    return seg


_EPILOGUES = {
    None: lambda y, aux: y,
    "sigmoid": lambda y, aux: jax.nn.sigmoid(y),
    "silu": lambda y, aux: y * jax.nn.sigmoid(y),
    "logf": lambda y, aux: _logaddexp(aux[0:1, :], aux[1:2, :] + _log_sigmoid(y)),
}


def _sub_tiles(n, most=4):
    return _pick_tile(n // ROW_TILE, tuple(range(most, 0, -1)))


def _nmm_kernel(s_ref, gain_ref, shift_ref, w_ref, b_ref, aux_ref, o_ref, h_scr, *, acts, sub, seg):
    j = pl.program_id(1)

    @pl.when(j == 0)
    def _():
        for k in range(sub):
            rows = pl.ds(k * ROW_TILE, ROW_TILE)
            sk = seg(pl.program_id(0) * sub + k)
            h_scr[rows, :] = _norm_mod(s_ref[rows, :], gain_ref[sk], shift_ref[sk]).astype(BF16)

    y = jnp.dot(h_scr[...], w_ref[...], preferred_element_type=F32) + b_ref[...]
    for act in sorted(set(acts), key=str):
        cols = [jj for jj, a in enumerate(acts) if a == act]
        if len(cols) == len(acts):
            o_ref[...] = _EPILOGUES[act](y, aux_ref[...]).astype(o_ref.dtype)
        else:
            @pl.when(functools.reduce(jnp.logical_or, [j == jj for jj in cols]))
            def _(act=act):
                o_ref[...] = _EPILOGUES[act](y, aux_ref[...]).astype(o_ref.dtype)


def norm_mod_mm(s, gain, shift, w, bias, acts, geom, aux=None, out_dtype=None):
    out_dtype = out_dtype or BF16
    n, k = s.shape
    m = w.shape[1]
    tn = m // len(acts)
    tpb, nctx_t = geom
    seg = _seg_map(tpb, nctx_t)
    sub = _sub_tiles(n)
    tm = sub * ROW_TILE
    if bias is None:
        bias = jnp.zeros((m,), F32)
    if aux is None:
        aux = jnp.zeros((2, m), F32)
    return pl.pallas_call(
        functools.partial(_nmm_kernel, acts=tuple(acts), sub=sub, seg=seg),
        grid=(n // tm, m // tn),
        in_specs=[pl.BlockSpec((tm, k), lambda i, j: (i, 0)),
                  pl.BlockSpec(gain.shape, lambda i, j: (0, 0, 0)),
                  pl.BlockSpec(shift.shape, lambda i, j: (0, 0, 0)),
                  pl.BlockSpec((k, tn), lambda i, j: (0, j)),
                  pl.BlockSpec((1, tn), lambda i, j: (0, j)),
                  pl.BlockSpec((2, tn), lambda i, j: (0, j))],
        out_specs=pl.BlockSpec((tm, tn), lambda i, j: (i, j)),
        out_shape=jax.ShapeDtypeStruct((n, m), out_dtype),
        scratch_shapes=[pltpu.VMEM((tm, k), BF16)],
        compiler_params=pltpu.CompilerParams(
            dimension_semantics=("arbitrary", "arbitrary"), vmem_limit_bytes=VMEM_LIMIT),
        name="norm_mod_mm",
    )(s, gain, shift, w.astype(BF16), bias.reshape(1, m).astype(F32), aux.astype(F32))


def _gated_residual_store(o_ref, s_ref, gm_ref, y, sub, seg):
    for k in range(sub):
        rows = pl.ds(k * ROW_TILE, ROW_TILE)
        gm = gm_ref[seg(pl.program_id(0) * sub + k)]
        o_ref[rows, :] = s_ref[rows, :] + gm * y[k * ROW_TILE:(k + 1) * ROW_TILE]


def _post_kernel(h_ref, g_ref, s_ref, hg_ref, gm_ref, w_ref, o_ref, *, heads, sub, seg):
    x = h_ref[0].astype(F32) + h_ref[1].astype(F32)
    hd = x.shape[1] // heads
    parts = []
    for h in range(heads):
        xh = x[:, h * hd:(h + 1) * hd]
        parts.append(xh * lax.rsqrt(jnp.mean(xh * xh, axis=-1, keepdims=True) + NORM_EPS))
    y = (jnp.concatenate(parts, axis=1) * hg_ref[...] * g_ref[...].astype(F32)).astype(BF16)
    _gated_residual_store(o_ref, s_ref, gm_ref, jnp.dot(y, w_ref[...], preferred_element_type=F32), sub, seg)


def post_mm_residual(h2, gate_arr, gate_block, s, head_g, gm, w_out, heads, geom):
    n, dm = s.shape
    seg = _seg_map(*geom)
    sub = _sub_tiles(n, most=2)
    tm = sub * ROW_TILE
    return pl.pallas_call(
        functools.partial(_post_kernel, heads=heads, sub=sub, seg=seg),
        grid=(n // tm,),
        in_specs=[pl.BlockSpec((2, tm, dm), lambda i: (0, i, 0)),
                  pl.BlockSpec((tm, dm), lambda i: (i, gate_block)),
                  pl.BlockSpec((tm, dm), lambda i: (i, 0)),
                  pl.BlockSpec((1, dm), lambda i: (0, 0)),
                  pl.BlockSpec(gm.shape, lambda i: (0, 0, 0)),
                  pl.BlockSpec((dm, dm), lambda i: (0, 0))],
        out_specs=pl.BlockSpec((tm, dm), lambda i: (i, 0)),
        out_shape=jax.ShapeDtypeStruct((n, dm), F32),
        compiler_params=pltpu.CompilerParams(dimension_semantics=("arbitrary",), vmem_limit_bytes=VMEM_LIMIT),
        name="post_mm_residual",
    )(h2, gate_arr, s, head_g.reshape(1, dm), gm, w_out.astype(BF16))


CONV_COLS = 512


def _conv_kernel(cur_ref, up_ref, dn_ref, w_ref, sc_ref, o_ref, *, tpb, nctx_t):
    ti = pl.program_id(0) % tpb
    is_ctx = ti < nctx_t
    no_up = jnp.logical_or(is_ctx, ti == nctx_t)
    no_dn = jnp.logical_or(is_ctx, ti == tpb - 1)
    x = cur_ref[...].astype(F32)
    up = jnp.where(no_up, 0.0, up_ref[...].astype(F32))
    dn = jnp.where(no_dn, 0.0, dn_ref[...].astype(F32))
    ext = jnp.concatenate([up, x, dn], axis=0)
    nr = ext.shape[0]
    ext_m = pltpu.roll(ext, 1, axis=0)
    ext_p = pltpu.roll(ext, nr - 1, axis=0)
    tpos = lax.broadcasted_iota(jnp.int32, (ROW_TILE, 1), 0)
    col = tpos % GRID_W
    left_ok = jnp.where(is_ctx, (tpos > 0).astype(F32), (col > 0).astype(F32))
    right_ok = jnp.where(is_ctx, (tpos < ROW_TILE - 1).astype(F32), (col < GRID_W - 1).astype(F32))
    vert = jnp.where(is_ctx, 0.0, 1.0)
    w = w_ref[...]
    acc = None
    for dr in (-1, 0, 1):
        base = GRID_W * (1 + dr)
        r3 = 3 * (dr + 1)
        term = (ext[base:base + ROW_TILE] * w[r3 + 1:r3 + 2]
                + ext_m[base:base + ROW_TILE] * w[r3:r3 + 1] * left_ok
                + ext_p[base:base + ROW_TILE] * w[r3 + 2:r3 + 3] * right_ok)
        if dr != 0:
            term = term * vert
        acc = term if acc is None else acc + term
    o_ref[...] = (acc * jax.nn.sigmoid(acc) * sc_ref[...]).astype(o_ref.dtype)


def conv_silu(z, conv_w, scale, width, geom):
    n = z.shape[0]
    tpb, nctx_t = geom
    assert nctx_t == 1 and ROW_TILE % GRID_W == 0
    hb = ROW_TILE // GRID_W
    last = n // GRID_W - 1
    return pl.pallas_call(
        functools.partial(_conv_kernel, tpb=tpb, nctx_t=nctx_t),
        grid=(n // ROW_TILE, width // CONV_COLS),
        in_specs=[pl.BlockSpec((ROW_TILE, CONV_COLS), lambda i, c: (i, c)),
                  pl.BlockSpec((GRID_W, CONV_COLS), lambda i, c: (jnp.maximum(i * hb - 1, 0), c)),
                  pl.BlockSpec((GRID_W, CONV_COLS), lambda i, c: (jnp.minimum((i + 1) * hb, last), c)),
                  pl.BlockSpec((9, CONV_COLS), lambda i, c: (0, c)),
                  pl.BlockSpec((1, CONV_COLS), lambda i, c: (0, c))],
        out_specs=pl.BlockSpec((ROW_TILE, CONV_COLS), lambda i, c: (i, c)),
        out_shape=jax.ShapeDtypeStruct((n, width), BF16),
        compiler_params=pltpu.CompilerParams(
            dimension_semantics=("arbitrary", "arbitrary"), vmem_limit_bytes=VMEM_LIMIT),
        name="conv_silu",
    )(z, z, z, conv_w.reshape(9, width).astype(F32), scale.reshape(1, width).astype(F32))


def _ml_scan_kernel(q_ref, k_ref, v_ref, gc_ref, gr_ref, o_ref, ct_scr, n_scr, m_scr, *, heads):
    L = q_ref.shape[0]
    d = pl.program_id(0)

    @pl.when(pl.program_id(2) == 0)
    def _():
        ct_scr[...] = jnp.zeros_like(ct_scr)
        n_scr[...] = jnp.zeros_like(n_scr)
        m_scr[...] = jnp.zeros_like(m_scr)

    row = lax.broadcasted_iota(jnp.int32, (L, L), 0)
    col = lax.broadcasted_iota(jnp.int32, (L, L), 1)

    def body(reverse):
        incl = (col >= row) if reverse else (col <= row)
        incl_t = (row >= col) if reverse else (row <= col)
        last = 0 if reverse else L - 1
        hs = range(heads)
        sls = [slice(h * LANES, (h + 1) * LANES) for h in hs]
        qk = [_dot(q_ref[:, sls[h]], k_ref[:, sls[h]], NT) for h in hs]
        qc = [_dot(q_ref[:, sls[h]], ct_scr[h]) for h in hs]
        stats = []
        for h in hs:
            ig_col = gc_ref[0, :, h:h + 1]
            lf_col = gc_ref[0, :, heads + h:heads + h + 1]
            ig_row = gr_ref[0, 0, h:h + 1, :]
            lf_row = gr_ref[0, 0, heads + h:heads + h + 1, :]
            b_col = jnp.sum(jnp.where(incl, lf_row, 0.0), axis=1, keepdims=True)
            b_row = jnp.sum(jnp.where(incl_t, lf_col, 0.0), axis=0, keepdims=True)
            m_prev = m_scr[h:h + 1, 0:1]
            dmat = jnp.where(incl, b_col - b_row + ig_row, -jnp.inf)
            inter = b_col + m_prev
            m_t = jnp.maximum(inter, jnp.max(dmat, axis=1, keepdims=True))
            b_last = b_col[last:last + 1, :]
            m_new = jnp.maximum(b_last + m_prev, jnp.max(b_last - b_row + ig_row, axis=1, keepdims=True))
            w_k = jnp.exp(b_last - b_col + ig_col - m_new)
            w_prev = jnp.exp(b_last + m_prev - m_new)
            stats.append((jnp.exp(dmat - m_t), jnp.exp(inter - m_t), jnp.exp(-m_t), w_k, w_prev, m_new))
        kv = [_dot(k_ref[:, sls[h]].astype(F32).T, stats[h][3] * v_ref[:, sls[h]].astype(F32)) for h in hs]
        s = [qk[h] * stats[h][0] for h in hs]
        sv = [_dot(s[h], v_ref[:, sls[h]]) for h in hs]
        for h in hs:
            _, w_inter, floor, w_k, w_prev, m_new = stats[h]
            n_row = n_scr[h:h + 1, :]
            num = sv[h] + w_inter * qc[h]
            den = (jnp.sum(s[h], axis=1, keepdims=True)
                   + w_inter * jnp.sum(q_ref[:, sls[h]].astype(F32) * n_row, axis=1, keepdims=True))
            o_ref[0, :, sls[h]] = (num / jnp.maximum(jnp.abs(den), floor)).astype(o_ref.dtype)
            ct_scr[h] = w_prev * ct_scr[h] + kv[h]
            n_scr[h:h + 1, :] = w_prev * n_row + jnp.sum(w_k * k_ref[:, sls[h]].astype(F32), axis=0, keepdims=True)
            m_scr[h:h + 1, :] = jnp.broadcast_to(m_new, (1, LANES))

    @pl.when(d == 0)
    def _():
        body(False)

    @pl.when(d == 1)
    def _():
        body(True)


def mlstm_scan(qk, z, gc, gr, dm, bsz, t):
    n = qk.shape[0]
    heads = dm // LANES
    L = ML_CHUNK
    nc, nctx = t // L, CTX_LEN // L

    def row(d, b, p):
        return b * nc + _scan_chunk_index(d, p, nctx, nc)

    return pl.pallas_call(
        functools.partial(_ml_scan_kernel, heads=heads),
        grid=(2, bsz, nc),
        in_specs=[pl.BlockSpec((L, dm), lambda d, b, p: (row(d, b, p), 0)),
                  pl.BlockSpec((L, dm), lambda d, b, p: (row(d, b, p), 1)),
                  pl.BlockSpec((L, dm), lambda d, b, p: (row(d, b, p), 2)),
                  pl.BlockSpec((1, L, 2 * heads), lambda d, b, p: (d, row(d, b, p), 0)),
                  pl.BlockSpec((1, 1, 2 * heads, L),
                               lambda d, b, p: (d, b, 0, _scan_chunk_index(d, p, nctx, nc)))],
        out_specs=pl.BlockSpec((1, L, dm), lambda d, b, p: (d, row(d, b, p), 0)),
        out_shape=jax.ShapeDtypeStruct((2, n, dm), BF16),
        scratch_shapes=[pltpu.VMEM((heads, LANES, LANES), F32), pltpu.VMEM((heads, LANES), F32),
                        pltpu.VMEM((heads, LANES), F32)],
        compiler_params=pltpu.CompilerParams(
            dimension_semantics=("arbitrary", "arbitrary", "arbitrary"), vmem_limit_bytes=VMEM_LIMIT),
        name="mlstm_scan",
    )(qk, qk, z, gc, gr)


RW_STATE_PASSES = 3


def _rw_scan_kernel(lw0_ref, lw1_ref, kd0_ref, kd1_ref, a0_ref, a1_ref, r0_ref, r1_ref, v0_ref, v1_ref,
                    kk0_ref, kk1_ref, of_ref, ob_ref, h_scr, rdp_scr, o0_scr, m_scr, ha_scr, *, nchunk):
    L = RW_CHUNK
    j = pl.program_id(2)

    @pl.when(j == 0)
    def _():
        for ref in (h_scr, rdp_scr, o0_scr, m_scr, ha_scr):
            ref[...] = jnp.zeros_like(ref)

    hs = [h_scr[0], h_scr[1]]

    def recurrence_step(k):
        for d, o_ref in ((0, of_ref), (1, ob_ref)):
            c = k if d == 0 else nchunk - 1 - k
            o_ref[pl.ds(c * L, L), :] = (_dot(rdp_scr[d, c], hs[d], passes=RW_STATE_PASSES)
                                         + o0_scr[d, c]).astype(o_ref.dtype)
            hs[d] = _dot(m_scr[d, c], hs[d], passes=RW_STATE_PASSES) + ha_scr[d, c]

    pending = list(range(nchunk))

    half = LANES // 2
    row = lax.broadcasted_iota(jnp.int32, (L, LANES), 0)
    col = lax.broadcasted_iota(jnp.int32, (L, LANES), 1) % half
    eye2 = (row == col).astype(F32)
    lane = lax.broadcasted_iota(jnp.int32, (1, LANES), 1)
    m0 = (lane < half).astype(F32)
    m1 = (lane >= half).astype(F32)
    r2 = lax.broadcasted_iota(jnp.int32, (LANES, LANES), 0)
    c2 = lax.broadcasted_iota(jnp.int32, (LANES, LANES), 1)
    same_head = (r2 // half) == (c2 // half)

    def stack(x):
        return jnp.concatenate([x * m0, x * m1], axis=0)

    chains = [(d, c) for c in range(nchunk) for d in range(2)]
    st = {}
    for d, c in chains:
        reverse = d == 1
        sl = pl.ds(c * L, L)
        lw = (lw0_ref, lw1_ref)[d][sl, :]
        k = (kd0_ref, kd1_ref)[d][sl, :].astype(F32)
        kk = (kk0_ref, kk1_ref)[d][sl, :].astype(F32)
        akk = kk * (a0_ref, a1_ref)[d][sl, :].astype(F32)
        g = _cumsum_rows(lw, reverse)
        ieg = jnp.exp(-g)
        g_last = g[0:1] if reverse else g[L - 1:L]
        dl = jnp.exp(g_last - g)
        st[d, c] = dict(kd=kk * jnp.exp(g - lw), rd=(r0_ref, r1_ref)[d][sl, :].astype(F32) * jnp.exp(g),
                        ai=akk * ieg, ki=k * ieg, ad=akk * dl, kdd=k * dl, eg_last=jnp.exp(g_last),
                        v=(v0_ref, v1_ref)[d][sl, :].astype(F32))
    recurrence_step(pending.pop(0))
    for d, c in chains:
        s = st[d, c]
        reverse = d == 1
        incl = (col >= row) if reverse else (col <= row)
        strict = (col > row) if reverse else (col < row)
        x = jnp.concatenate([s["kd"], s["rd"]], axis=0)
        rhs = jnp.concatenate([stack(s["ai"]), stack(s["ki"])], axis=0)
        sc = _dot(x, rhs, NT)
        s["a_ab"] = jnp.where(strict, sc[:L, :LANES], 0.0)
        s["a_ak"] = jnp.where(strict, sc[:L, LANES:], 0.0)
        s["b_ra"] = jnp.where(incl, sc[L:, :LANES], 0.0)
        s["b_rk"] = jnp.where(incl, sc[L:, LANES:], 0.0)
        s["tinv"] = eye2 - s["a_ab"]
        s["pw"] = s["a_ab"]
    span = 2
    while span < L:
        for d, c in chains:
            s = st[d, c]
            s["pw"] = _dot(s["pw"], stack(s["pw"]))
        for d, c in chains:
            s = st[d, c]
            s["tinv"] = _dot(s["tinv"], stack(eye2 + s["pw"]))
        if pending:
            recurrence_step(pending.pop(0))
        span *= 2
    while pending:
        recurrence_step(pending.pop(0))
    h_scr[0] = hs[0]
    h_scr[1] = hs[1]
    for d, c in chains:
        s = st[d, c]
        s["w"] = -_dot(s["tinv"], stack(s["a_ak"]))
        s["kdp"] = _dot(s["tinv"], stack(s["kd"]))
    for d, c in chains:
        s = st[d, c]
        s["vst"] = stack(s["v"])
        s["u0"] = _dot(s["w"], s["vst"])
    for d, c in chains:
        s = st[d, c]
        lhs = jnp.concatenate([s["b_ra"], s["b_rk"]], axis=1)
        rhs = jnp.concatenate([stack(s["u0"]), s["vst"]], axis=0)
        o0_scr[d, c] = _dot(lhs, rhs)
        rdp_scr[d, c] = s["rd"] - _dot(s["b_ra"], stack(s["kdp"]))
        diag = jnp.where(r2 == c2, s["eg_last"], 0.0)
        m_scr[d, c] = jnp.where(same_head, diag - _dot(s["ad"].T, s["kdp"]), 0.0)
        at = jnp.concatenate([s["ad"], s["kdd"]], axis=0).T
        ha_scr[d, c] = jnp.where(same_head, _dot(at, jnp.concatenate([s["u0"], s["v"]], axis=0)), 0.0)


def rwkv_scan(lw, kda, rvkg, dm, bsz, t):
    n = lw.shape[0]
    pairs = dm // LANES
    L = RW_CHUNK
    nchunk = RW_PRE_CHUNKS
    tb = nchunk * L
    nblk, nctx = t // tb, CTX_LEN // tb

    def block(d, b, j):
        return b * nblk + _scan_chunk_index(d, jnp.minimum(j, nblk - 1), nctx, nblk)

    def ispec(d, col):
        return pl.BlockSpec((tb, LANES), lambda b, p, j: (block(d, b, j), col * pairs + p))

    def ospec(d):
        return pl.BlockSpec((tb, LANES), lambda b, p, j: (block(d, b, jnp.maximum(j - 1, 0)), p))

    return pl.pallas_call(
        functools.partial(_rw_scan_kernel, nchunk=nchunk),
        grid=(bsz, pairs, nblk + 1),
        in_specs=[ispec(0, 0), ispec(1, 1), ispec(0, 0), ispec(1, 1), ispec(0, 2), ispec(1, 3),
                  ispec(0, 0), ispec(1, 0), ispec(0, 1), ispec(1, 1), ispec(0, 2), ispec(1, 2)],
        out_specs=[ospec(0), ospec(1)],
        out_shape=[jax.ShapeDtypeStruct((n, dm), BF16), jax.ShapeDtypeStruct((n, dm), BF16)],
        scratch_shapes=[pltpu.VMEM((2, LANES, LANES), F32), pltpu.VMEM((2, nchunk, L, LANES), F32),
                        pltpu.VMEM((2, nchunk, L, LANES), F32), pltpu.VMEM((2, nchunk, LANES, LANES), F32),
                        pltpu.VMEM((2, nchunk, LANES, LANES), F32)],
        compiler_params=pltpu.CompilerParams(
            dimension_semantics=("arbitrary", "arbitrary", "arbitrary"), vmem_limit_bytes=VMEM_LIMIT),
        name="rwkv_scan",
    )(lw, lw, kda, kda, kda, kda, rvkg, rvkg, rvkg, rvkg, rvkg, rvkg)


HALO_ROWS = 8


def _group_sum(x, width):
    r = lax.broadcasted_iota(jnp.int32, (LANES, LANES), 0) // width
    c = lax.broadcasted_iota(jnp.int32, (LANES, LANES), 1) // width
    ones = (r == c).astype(BF16)
    hi = x.astype(BF16)
    lo = (x - hi.astype(F32)).astype(BF16)
    parts = []
    for j in range(x.shape[1] // LANES):
        sl = slice(j * LANES, (j + 1) * LANES)
        parts.append(jnp.dot(hi[:, sl], ones, preferred_element_type=F32)
                     + jnp.dot(lo[:, sl], ones, preferred_element_type=F32))
    return jnp.concatenate(parts, axis=1)


def _rw_proj_kernel(s_ref, up_ref, dn_ref, gain_ref, shift_ref, mu_ref, wrkv_ref, w1_ref, w2_ref, w0_ref,
                    a1_ref, a2_ref, a0_ref, g1_ref, g2_ref, kk_ref, ka_ref,
                    lw_ref, kda_ref, rvkg_ref, *, tpb, nctx_t):
    tm, dm = s_ref.shape
    ti = pl.program_id(0) % tpb
    is_ctx = ti < nctx_t
    has_up = jnp.logical_not(jnp.logical_or(is_ctx, ti == nctx_t))
    has_dn = jnp.logical_not(jnp.logical_or(is_ctx, ti == tpb - 1))
    gain = gain_ref[0]
    shift = shift_ref[0]
    u = _norm_mod(s_ref[...], gain, shift)
    u_up = jnp.where(has_up, _norm_mod(up_ref[HALO_ROWS - 1:HALO_ROWS, :], gain, shift), 0.0)
    u_dn = jnp.where(has_dn, _norm_mod(dn_ref[0:1, :], gain, shift), 0.0)
    row = lax.broadcasted_iota(jnp.int32, (tm, 1), 0)
    u_m = jnp.where(row == 0, u_up, pltpu.roll(u, 1, axis=0))
    u_p = jnp.where(row == tm - 1, u_dn, pltpu.roll(u, tm - 1, axis=0))
    du = 0.5 * (u_m + u_p) - u
    mu = mu_ref[...]

    def mix(i):
        return (u + du * mu[i:i + 1]).astype(BF16)

    def dot(a, b):
        return jnp.dot(a.astype(BF16), b, preferred_element_type=F32)

    r = dot(mix(0), wrkv_ref[0])
    k = dot(mix(1), wrkv_ref[1])
    v = dot(mix(2), wrkv_ref[2])
    w_pre = dot(jnp.tanh(dot(mix(3), w1_ref[...])), w2_ref[...]) + w0_ref[...]
    lw_ref[...] = -jnp.exp(-_softplus(-w_pre) - 0.5)
    a = jax.nn.sigmoid(dot(dot(mix(4), a1_ref[...]), a2_ref[...]) + a0_ref[...])
    g = dot(jax.nn.sigmoid(dot(mix(5), g1_ref[...])), g2_ref[...])
    kk = k * kk_ref[...]
    kk = kk * lax.rsqrt(jnp.maximum(_group_sum(kk * kk, RW_HEAD_DIM), 1e-24))
    ka = ka_ref[...]
    for d in range(2):
        kda_ref[:, d * dm:(d + 1) * dm] = (k * (1.0 + (a[:, d * dm:(d + 1) * dm] - 1.0) * ka)).astype(kda_ref.dtype)
    kda_ref[:, 2 * dm:] = a.astype(kda_ref.dtype)
    for j, val in enumerate((r, v, kk, g)):
        rvkg_ref[:, j * dm:(j + 1) * dm] = val.astype(rvkg_ref.dtype)


def rwkv_proj(s, gain, shift, geom, mu, w_rkv, w0, w1, w2, a0, a1, a2, g1, g2, k_k, k_a):
    n, dm = s.shape
    tpb, nctx_t = geom
    assert nctx_t == 1
    seg = _seg_map(tpb, nctx_t)
    hb = ROW_TILE // HALO_ROWS
    last = n // HALO_ROWS - 1
    lora = w1.shape[2]

    def blockdiag(w):
        z = jnp.zeros_like(w[0])
        return jnp.concatenate([jnp.concatenate([w[0], z], axis=1), jnp.concatenate([z, w[1]], axis=1)], axis=0)

    consts = [jnp.pad(mu, ((0, HALO_ROWS - mu.shape[0]), (0, 0))), w_rkv.astype(BF16),
              jnp.concatenate([w1[0], w1[1]], axis=1).astype(BF16), blockdiag(w2).astype(BF16),
              jnp.concatenate([w0[0], w0[1]])[None],
              jnp.concatenate([a1[0], a1[1]], axis=1).astype(BF16), blockdiag(a2).astype(BF16),
              jnp.concatenate([a0[0], a0[1]])[None],
              g1.astype(BF16), g2.astype(BF16), k_k[None], k_a[None]]

    def const_spec(x):
        nd = x.ndim
        return pl.BlockSpec(x.shape, lambda i: (0,) * nd)

    return pl.pallas_call(
        functools.partial(_rw_proj_kernel, tpb=tpb, nctx_t=nctx_t),
        grid=(n // ROW_TILE,),
        in_specs=[pl.BlockSpec((ROW_TILE, dm), lambda i: (i, 0)),
                  pl.BlockSpec((HALO_ROWS, dm), lambda i: (jnp.maximum(i * hb - 1, 0), 0)),
                  pl.BlockSpec((HALO_ROWS, dm), lambda i: (jnp.minimum((i + 1) * hb, last), 0)),
                  pl.BlockSpec((1, 1, dm), lambda i: (seg(i), 0, 0)),
                  pl.BlockSpec((1, 1, dm), lambda i: (seg(i), 0, 0))] + [const_spec(x) for x in consts],
        out_specs=[pl.BlockSpec((ROW_TILE, 2 * dm), lambda i: (i, 0)),
                   pl.BlockSpec((ROW_TILE, 4 * dm), lambda i: (i, 0)),
                   pl.BlockSpec((ROW_TILE, 4 * dm), lambda i: (i, 0))],
        out_shape=[jax.ShapeDtypeStruct((n, 2 * dm), F32), jax.ShapeDtypeStruct((n, 4 * dm), BF16),
                   jax.ShapeDtypeStruct((n, 4 * dm), BF16)],
        compiler_params=pltpu.CompilerParams(dimension_semantics=("arbitrary",), vmem_limit_bytes=VMEM_LIMIT_BIG),
        name="rwkv_proj",
    )(s, s, s, gain, shift, *consts)


def _rw_post_kernel(of_ref, ob_ref, r_ref, v_ref, g_ref, k0_ref, k1_ref, s_ref, lnw_ref, lnb_ref, rk_ref, gm_ref,
                    w_ref, out_ref, *, sub, seg):
    o = of_ref[...].astype(F32) + ob_ref[...].astype(F32)
    inv = 1.0 / RW_HEAD_DIM
    mean = _group_sum(o, RW_HEAD_DIM) * inv
    oc = o - mean
    var = _group_sum(oc * oc, RW_HEAD_DIM) * inv
    xn = oc * lax.rsqrt(var + RW_GN_EPS) * lnw_ref[...] + lnb_ref[...]
    r = r_ref[...].astype(F32)
    ksum = k0_ref[...].astype(F32) + k1_ref[...].astype(F32)
    bonus = _group_sum(r * ksum * rk_ref[...], RW_HEAD_DIM) * v_ref[...].astype(F32)
    y = ((xn + bonus) * g_ref[...].astype(F32)).astype(BF16)
    _gated_residual_store(out_ref, s_ref, gm_ref, jnp.dot(y, w_ref[...], preferred_element_type=F32), sub, seg)


def rwkv_post(o_f, o_b, rvkg, kda, s, ln_w, ln_b, r_k, gm, w_out, geom):
    n, dm = s.shape
    seg = _seg_map(*geom)
    sub = _sub_tiles(n, most=2)
    tm = sub * ROW_TILE

    def col(block):
        return pl.BlockSpec((tm, dm), lambda i: (i, block))

    vec = pl.BlockSpec((1, dm), lambda i: (0, 0))
    return pl.pallas_call(
        functools.partial(_rw_post_kernel, sub=sub, seg=seg),
        grid=(n // tm,),
        in_specs=[col(0), col(0), col(0), col(1), col(3), col(0), col(1),
                  col(0), vec, vec, vec, pl.BlockSpec(gm.shape, lambda i: (0, 0, 0)),
                  pl.BlockSpec((dm, dm), lambda i: (0, 0))],
        out_specs=pl.BlockSpec((tm, dm), lambda i: (i, 0)),
        out_shape=jax.ShapeDtypeStruct((n, dm), F32),
        compiler_params=pltpu.CompilerParams(dimension_semantics=("arbitrary",), vmem_limit_bytes=VMEM_LIMIT),
        name="rwkv_post",
    )(o_f, o_b, rvkg, rvkg, rvkg, kda, kda, s, ln_w[None], ln_b[None], r_k[None], gm, w_out.astype(BF16))


def _hg_scan_kernel(q_ref, v_ref, lf_ref, o_ref, st_scr, *, heads):
    C = q_ref.shape[0]
    d = pl.program_id(0)
    nsub = C // HG_SUB

    @pl.when(pl.program_id(2) == 0)
    def _():
        st_scr[...] = jnp.zeros_like(st_scr)

    def body(reverse):
        last = 0 if reverse else C - 1
        hs = range(heads)
        sls = [slice(h * LANES, (h + 1) * LANES) for h in hs]
        g = [lf_ref[:, sls[h]] for h in hs]
        b = [_cumsum_rows(g[h], reverse) for h in hs]
        k = [-jnp.tanh(0.5 * g[h]) * (jnp.exp(g[h]) + 1.0) for h in hs]
        o_inter = [_dot(q_ref[:, sls[h]].astype(F32) * jnp.exp(b[h]), st_scr[h], NT) for h in hs]
        parts = [[None] * nsub for _ in hs]
        for i in range(nsub):
            r0 = i * HG_SUB
            lo, hi = (r0, C) if reverse else (0, r0 + HG_SUB)
            first = r0 + HG_SUB - 1 if reverse else r0
            row = lax.broadcasted_iota(jnp.int32, (HG_SUB, hi - lo), 0) + r0
            col = lax.broadcasted_iota(jnp.int32, (HG_SUB, hi - lo), 1) + lo
            keep = (col >= row) if reverse else (col <= row)
            att = []
            for h in hs:
                rho = b[h][first:first + 1, :] - g[h][first:first + 1, :]
                qi = q_ref[r0:r0 + HG_SUB, sls[h]].astype(F32) * jnp.exp(b[h][r0:r0 + HG_SUB] - rho)
                ki = k[h][lo:hi] * jnp.exp(jnp.minimum(rho - b[h][lo:hi], HG_EXP_CLAMP))
                att.append(jnp.where(keep, _dot(qi, ki, NT), 0.0))
            for h in hs:
                parts[h][i] = _dot(att[h], v_ref[lo:hi, sls[h]])
        for h in hs:
            o_ref[0, :, sls[h]] = (o_inter[h] + jnp.concatenate(parts[h], axis=0)).astype(o_ref.dtype)
        upd = []
        for h in hs:
            b_last = b[h][last:last + 1, :]
            upd.append((jnp.exp(b_last),
                        _dot(v_ref[:, sls[h]].astype(F32).T, k[h] * jnp.exp(b_last - b[h]))))
        for h in hs:
            st_scr[h] = st_scr[h] * upd[h][0] + upd[h][1]

    @pl.when(d == 0)
    def _():
        body(False)

    @pl.when(d == 1)
    def _():
        body(True)


def hgrn_scan(z, logf, dm, bsz, t):
    n = z.shape[0]
    heads = dm // LANES
    C = HG_CHUNK
    nc, nctx = t // C, CTX_LEN // C

    def row(d, b, p):
        return b * nc + _scan_chunk_index(d, p, nctx, nc)

    return pl.pallas_call(
        functools.partial(_hg_scan_kernel, heads=heads),
        grid=(2, bsz, nc),
        in_specs=[pl.BlockSpec((C, dm), lambda d, b, p: (row(d, b, p), 0)),
                  pl.BlockSpec((C, dm), lambda d, b, p: (row(d, b, p), 1)),
                  pl.BlockSpec((C, dm), lambda d, b, p: (row(d, b, p), d))],
        out_specs=pl.BlockSpec((1, C, dm), lambda d, b, p: (d, row(d, b, p), 0)),
        out_shape=jax.ShapeDtypeStruct((2, n, dm), BF16),
        scratch_shapes=[pltpu.VMEM((heads, LANES, LANES), F32)],
        compiler_params=pltpu.CompilerParams(
            dimension_semantics=("arbitrary", "arbitrary", "arbitrary"), vmem_limit_bytes=VMEM_LIMIT),
        name="hgrn_scan",
    )(z, z, logf)


def _first_argmax(vals):
    best, idx = vals[0], jnp.zeros(vals[0].shape, jnp.int32)
    for i in range(1, len(vals)):
        better = vals[i] > best
        best = jnp.where(better, vals[i], best)
        idx = jnp.where(better, i, idx)
    return best, idx


def _router_kernel(s_ref, gain_ref, shift_ref, wt_ref, b_ref, e_ref, g_ref, *, n_groups, top_k):
    n_experts = wt_ref.shape[0]
    per = n_experts // n_groups
    h = _norm_mod(s_ref[...], gain_ref[0], shift_ref[0])
    aff = jax.nn.sigmoid(_dot(wt_ref[...], h, NT, passes=3))
    sel = aff + b_ref[...]
    a = [aff[e:e + 1, :] for e in range(n_experts)]
    s = [sel[e:e + 1, :] for e in range(n_experts)]
    neg = jnp.full_like(s[0], -jnp.inf)
    scores = []
    for g in range(n_groups):
        grp = s[g * per:(g + 1) * per]
        m1, i1 = _first_argmax(grp)
        m2, _ = _first_argmax([jnp.where(i1 == j, neg, grp[j]) for j in range(per)])
        scores.append(m1 + m2)
    _, best = _first_argmax(scores)

    def in_best(rows):
        out = []
        for j in range(per):
            x = rows[j]
            for g in range(1, n_groups):
                x = jnp.where(best == g, rows[g * per + j], x)
            out.append(x)
        return out

    sb, ab = in_best(s), in_best(a)
    picked, chosen = [], []
    cand = sb
    for _ in range(top_k):
        _, i = _first_argmax(cand)
        c = ab[0]
        for j in range(1, per):
            c = jnp.where(i == j, ab[j], c)
        picked.append(i)
        chosen.append(c)
        cand = [jnp.where(i == j, neg, cand[j]) for j in range(per)]
    total = functools.reduce(jnp.add, chosen)
    for kk_ in range(top_k):
        e_ref[kk_:kk_ + 1, :] = best * per + picked[kk_]
        g_ref[kk_:kk_ + 1, :] = chosen[kk_] / total


def norm_route(s, gain, shift, geom, router_w, router_b):
    n, k = s.shape
    n_experts = router_w.shape[1]
    tm = ROW_TILE
    seg = _seg_map(*geom)
    return pl.pallas_call(
        functools.partial(_router_kernel, n_groups=N_GROUPS, top_k=TOP_K),
        grid=(n // tm,),
        in_specs=[pl.BlockSpec((tm, k), lambda i: (i, 0)),
                  pl.BlockSpec((1, 1, k), lambda i: (seg(i), 0, 0)),
                  pl.BlockSpec((1, 1, k), lambda i: (seg(i), 0, 0)),
                  pl.BlockSpec((n_experts, k), lambda i: (0, 0)),
                  pl.BlockSpec((n_experts, 1), lambda i: (0, 0))],
        out_specs=[pl.BlockSpec((TOP_K, tm), lambda i: (0, i)), pl.BlockSpec((TOP_K, tm), lambda i: (0, i))],
        out_shape=[jax.ShapeDtypeStruct((TOP_K, n), jnp.int32), jax.ShapeDtypeStruct((TOP_K, n), F32)],
        compiler_params=pltpu.CompilerParams(dimension_semantics=("arbitrary",), vmem_limit_bytes=VMEM_LIMIT),
        name="norm_route",
    )(s, gain, shift, router_w.T, router_b.reshape(n_experts, 1).astype(F32))


def _final_norm_kernel(s_ref, g_ref, o_ref):
    x = s_ref[...]
    o_ref[0] = x * lax.rsqrt(jnp.mean(x * x, axis=-1, keepdims=True) + NORM_EPS) * g_ref[...]


def final_norm(s, g, bsz, t, geom):
    dm = s.shape[1]
    tpb, nctx_t = geom
    return pl.pallas_call(
        _final_norm_kernel,
        grid=(bsz, tpb - nctx_t),
        in_specs=[pl.BlockSpec((ROW_TILE, dm), lambda b, i: (b * tpb + nctx_t + i, 0)),
                  pl.BlockSpec((1, dm), lambda b, i: (0, 0))],
        out_specs=pl.BlockSpec((1, ROW_TILE, dm), lambda b, i: (b, i, 0)),
        out_shape=jax.ShapeDtypeStruct((bsz, t - CTX_LEN, dm), F32),
        name="final_norm",
    )(s, g[None])


DMA_UNROLL = 8


def _row_copy_waits(src_row, dst_row, sem, count):
    def body(_, carry):
        pltpu.make_async_copy(src_row, dst_row, sem).wait()
        return carry
    lax.fori_loop(0, count, body, 0)


def _scatter_kernel(dest_ref, meta_ref, s_ref, gain_ref, shift_ref, xb_ref, hbuf, zrow, sems, zsem, *, n_experts):
    i = pl.program_id(0)
    last = pl.num_programs(0) - 1
    slot = i % 2
    per_tile = TOP_K * ROW_TILE

    def wait_tile(sl):
        for _ in range(TOP_K):
            pltpu.make_async_copy(hbuf.at[sl], xb_ref.at[pl.ds(0, ROW_TILE), :], sems.at[sl]).wait()

    @pl.when(i >= 2)
    def _():
        wait_tile(slot)

    hbuf[slot] = _norm_mod(s_ref[...], gain_ref[0], shift_ref[0])

    def issue(r8, carry):
        for u in range(DMA_UNROLL):
            r = r8 * DMA_UNROLL + u
            for k in range(TOP_K):
                d = dest_ref[0, 0, TOP_K * r + k]
                pltpu.make_async_copy(hbuf.at[slot, pl.ds(r, 1), :], xb_ref.at[pl.ds(d, 1), :],
                                      sems.at[slot]).start(priority=(TOP_K * u + k) % 2)
        return carry
    lax.fori_loop(0, ROW_TILE // DMA_UNROLL, issue, 0)

    @pl.when(i == last)
    def _():
        @pl.when(last >= 1)
        def _():
            wait_tile(1 - slot)
        wait_tile(slot)
        zrow[...] = jnp.zeros_like(zrow)
        for e in range(n_experts):
            lo = meta_ref[2, e] + meta_ref[0, e]
            hi = meta_ref[2, e] + meta_ref[1, e]

            def pad_start(q, carry):
                pltpu.make_async_copy(zrow.at[pl.ds(0, 1), :], xb_ref.at[pl.ds(q, 1), :], zsem.at[0]).start()
                return carry
            lax.fori_loop(lo, hi, pad_start, 0)
            _row_copy_waits(zrow.at[pl.ds(0, 1), :], xb_ref.at[pl.ds(0, 1), :], zsem.at[0], hi - lo)
        end = meta_ref[2, n_experts - 1] + meta_ref[1, n_experts - 1]

        def tail_start(q, carry):
            pltpu.make_async_copy(zrow.at[pl.ds(0, 1), :], xb_ref.at[pl.ds(q, 1), :], zsem.at[0]).start()
            return carry
        lax.fori_loop(end, xb_ref.shape[0], tail_start, 0)
        _row_copy_waits(zrow.at[pl.ds(0, 1), :], xb_ref.at[pl.ds(0, 1), :], zsem.at[0], xb_ref.shape[0] - end)


def moe_scatter(s, gain, shift, geom, dest, meta, n_slots):
    n, dm = s.shape
    seg = _seg_map(*geom)
    nt = n // ROW_TILE
    return pl.pallas_call(
        functools.partial(_scatter_kernel, n_experts=meta.shape[1]),
        grid=(nt,),
        in_specs=[pl.BlockSpec((1, 1, TOP_K * ROW_TILE), lambda i: (i, 0, 0), memory_space=pltpu.SMEM),
                  pl.BlockSpec(memory_space=pltpu.SMEM),
                  pl.BlockSpec((ROW_TILE, dm), lambda i: (i, 0)),
                  pl.BlockSpec((1, 1, dm), lambda i: (seg(i), 0, 0)),
                  pl.BlockSpec((1, 1, dm), lambda i: (seg(i), 0, 0))],
        out_specs=pl.BlockSpec(memory_space=pl.ANY),
        out_shape=jax.ShapeDtypeStruct((n_slots, dm), F32),
        scratch_shapes=[pltpu.VMEM((2, ROW_TILE, dm), F32), pltpu.VMEM((8, dm), F32),
                        pltpu.SemaphoreType.DMA((2,)), pltpu.SemaphoreType.DMA((1,))],
        compiler_params=pltpu.CompilerParams(dimension_semantics=("arbitrary",), vmem_limit_bytes=VMEM_LIMIT),
        name="moe_scatter",
    )(dest.reshape(nt, 1, TOP_K * ROW_TILE), meta, s, gain, shift)


def _gather_combine_kernel(dcur_ref, dnxt_ref, g_ref, s_ref, gm_ref, yb_ref, o_ref, ybuf, sems):
    i = pl.program_id(0)
    nsteps = pl.num_programs(0)
    slot = i % 2

    def start_tile(dref, sl):
        def issue(r8, carry):
            for u in range(DMA_UNROLL):
                r = r8 * DMA_UNROLL + u
                for k in range(TOP_K):
                    d = dref[0, 0, TOP_K * r + k]
                    pltpu.make_async_copy(yb_ref.at[pl.ds(d, 1), :], ybuf.at[sl, k, pl.ds(r, 1), :],
                                          sems.at[sl]).start(priority=(TOP_K * u + k) % 2)
            return carry
        lax.fori_loop(0, ROW_TILE // DMA_UNROLL, issue, 0)

    @pl.when(i == 0)
    def _():
        start_tile(dcur_ref, 0)

    @pl.when(i + 1 < nsteps)
    def _():
        start_tile(dnxt_ref, 1 - slot)

    for k in range(TOP_K):
        pltpu.make_async_copy(yb_ref.at[pl.ds(0, ROW_TILE), :], ybuf.at[slot, k], sems.at[slot]).wait()
    g = g_ref[...]
    y = sum(ybuf[slot, k] * g[:, k:k + 1] for k in range(TOP_K))
    o_ref[...] = s_ref[...] + gm_ref[0] * y


def moe_gather_combine(yb, dest, gate, s, gm, geom):
    n, dm = s.shape
    seg = _seg_map(*geom)
    nt = n // ROW_TILE
    d3 = dest.reshape(nt, 1, TOP_K * ROW_TILE)
    row = pl.BlockSpec((ROW_TILE, dm), lambda i: (i, 0))
    return pl.pallas_call(
        _gather_combine_kernel,
        grid=(nt,),
        in_specs=[pl.BlockSpec((1, 1, TOP_K * ROW_TILE), lambda i: (i, 0, 0), memory_space=pltpu.SMEM),
                  pl.BlockSpec((1, 1, TOP_K * ROW_TILE), lambda i: (jnp.minimum(i + 1, nt - 1), 0, 0),
                               memory_space=pltpu.SMEM),
                  pl.BlockSpec((ROW_TILE, TOP_K), lambda i: (i, 0)), row,
                  pl.BlockSpec((1, 1, dm), lambda i: (seg(i), 0, 0)),
                  pl.BlockSpec(memory_space=pl.ANY)],
        out_specs=row,
        out_shape=jax.ShapeDtypeStruct((n, dm), F32),
        scratch_shapes=[pltpu.VMEM((2, TOP_K, ROW_TILE, dm), F32), pltpu.SemaphoreType.DMA((2,))],
        compiler_params=pltpu.CompilerParams(dimension_semantics=("arbitrary",), vmem_limit_bytes=VMEM_LIMIT),
        name="moe_gather_combine",
    )(d3, d3, gate, s, gm, yb)


def _rank_kernel(e_ref, rank_ref, cnt_ref, carry_scr, *, n_experts):
    @pl.when(pl.program_id(0) == 0)
    def _():
        carry_scr[...] = jnp.zeros_like(carry_scr)

    bl = e_ref.shape[2]
    e_row = e_ref[0]
    sub = lax.broadcasted_iota(jnp.int32, (n_experts, bl), 0)
    onehot = (sub == e_row).astype(F32)
    ri = lax.broadcasted_iota(jnp.int32, (bl, bl), 0)
    ci = lax.broadcasted_iota(jnp.int32, (bl, bl), 1)
    earlier = (ri < ci).astype(BF16)
    cum = jnp.dot(onehot.astype(BF16), earlier, preferred_element_type=F32)
    carry = carry_scr[...]
    rank_ref[0] = jnp.sum(onehot * (cum + carry[:, :1]), axis=0, keepdims=True).astype(jnp.int32)
    carry = carry + jnp.sum(onehot, axis=1, keepdims=True)
    carry_scr[...] = carry
    cnt_ref[...] = carry.astype(jnp.int32)


def assignment_ranks(flat_e, n_experts):
    n_assign = flat_e.shape[0]
    bl = _pick_tile(n_assign, (512, 256, 128))
    nblk = n_assign // bl
    rank, cnt = pl.pallas_call(
        functools.partial(_rank_kernel, n_experts=n_experts),
        grid=(nblk,),
        in_specs=[pl.BlockSpec((1, 1, bl), lambda i: (i, 0, 0))],
        out_specs=[pl.BlockSpec((1, 1, bl), lambda i: (i, 0, 0)),
                   pl.BlockSpec((n_experts, LANES), lambda i: (0, 0))],
        out_shape=[jax.ShapeDtypeStruct((nblk, 1, bl), jnp.int32),
                   jax.ShapeDtypeStruct((n_experts, LANES), jnp.int32)],
        scratch_shapes=[pltpu.VMEM((n_experts, LANES), F32)],
        compiler_params=pltpu.CompilerParams(dimension_semantics=("arbitrary",)),
        name="assignment_ranks",
    )(flat_e.reshape(nblk, 1, bl))
    return rank.reshape(n_assign), cnt[:, 0]


def _ffn_kernel(be_ref, x_ref, w1_ref, w3_ref, w2_ref, o_ref):
    del be_ref
    x = x_ref[...].astype(BF16)
    a = jnp.dot(x, w1_ref[0], preferred_element_type=F32)
    b = jnp.dot(x, w3_ref[0], preferred_element_type=F32)
    hid = (a * jax.nn.sigmoid(a) * b).astype(BF16)
    o_ref[...] = jnp.dot(hid, w2_ref[0], preferred_element_type=F32).astype(o_ref.dtype)


def expert_ffn(xb, block_expert, w1, w3, w2):
    nrows, dm = xb.shape
    f = w1.shape[2]
    nb = nrows // MOE_BLOCK
    return pl.pallas_call(
        _ffn_kernel,
        grid_spec=pltpu.PrefetchScalarGridSpec(
            num_scalar_prefetch=1,
            grid=(nb,),
            in_specs=[pl.BlockSpec((MOE_BLOCK, dm), lambda i, be: (i, 0)),
                      pl.BlockSpec((1, dm, f), lambda i, be: (be[i], 0, 0)),
                      pl.BlockSpec((1, dm, f), lambda i, be: (be[i], 0, 0)),
                      pl.BlockSpec((1, f, dm), lambda i, be: (be[i], 0, 0))],
            out_specs=pl.BlockSpec((MOE_BLOCK, dm), lambda i, be: (i, 0))),
        out_shape=jax.ShapeDtypeStruct((nrows, dm), F32),
        compiler_params=pltpu.CompilerParams(
            dimension_semantics=("arbitrary",), vmem_limit_bytes=VMEM_LIMIT),
        name="expert_ffn",
    )(block_expert, xb, w1.astype(BF16), w3.astype(BF16), w2.astype(BF16))


def _mlstm_layer(s, gain, shift, gate_mod, geom, bsz, t, w_in, w_gate, b_gate, conv, head_g, w_out):
    n, dm = s.shape
    heads = ML_HEADS
    z = norm_mod_mm(s, gain, shift, w_in, None, (None, None, None, "sigmoid"), geom)
    scale = jnp.concatenate([jnp.ones((dm,), F32), jnp.full((dm,), (dm // heads) ** -0.5, F32)])
    qk = conv_silu(z, conv, scale, 2 * dm, geom)
    ng = 4 * heads
    wg = jnp.pad(jnp.concatenate([w_gate[0], w_gate[1]], axis=1), ((0, 0), (0, LANES - ng)))
    bg = jnp.pad(jnp.concatenate([b_gate[0], b_gate[1]]), (0, LANES - ng))
    gates = norm_mod_mm(s, gain, shift, wg, bg, (None,), geom, out_dtype=F32)[:, :ng]
    gates = gates.reshape(bsz, t, 2, 2 * heads)
    gates = jnp.concatenate([gates[..., :heads], jax.nn.log_sigmoid(gates[..., heads:])], axis=-1)
    gc = jnp.moveaxis(gates, 2, 0).reshape(2, n, 2 * heads)
    gr = jnp.transpose(gates, (2, 0, 3, 1))
    h = mlstm_scan(qk, z, gc, gr, dm, bsz, t)
    return post_mm_residual(h, z, 3, s, head_g, gate_mod, w_out, heads, geom)


def _rwkv7_layer(s, gain, shift, gate_mod, geom, bsz, t, mu, w_rkv, w0, w1, w2, a0, a1, a2, g1, g2,
                 k_k, k_a, r_k, ln_w, ln_b, w_out):
    dm = s.shape[1]
    lw, kda, rvkg = rwkv_proj(s, gain, shift, geom, mu, w_rkv, w0, w1, w2, a0, a1, a2, g1, g2, k_k, k_a)
    o_f, o_b = rwkv_scan(lw, kda, rvkg, dm, bsz, t)
    return rwkv_post(o_f, o_b, rvkg, kda, s, ln_w, ln_b, r_k, gate_mod, w_out, geom)


def _hgrn2_layer(s, gain, shift, gate_mod, geom, bsz, t, layer_idx, w_in, w_f, b_f, lb_logits, head_g, w_out):
    dm = s.shape[1]
    z = norm_mod_mm(s, gain, shift, w_in, None, ("silu", None, "silu"), geom)
    p = jax.nn.softmax(lb_logits, axis=0)
    lb = jnp.cumsum(p, axis=0)[layer_idx] - p[0]
    aux = jnp.tile(jnp.stack([jnp.log(lb), jnp.log1p(-lb)]), (1, 2))
    log_f = norm_mod_mm(s, gain, shift, jnp.concatenate([w_f[0], w_f[1]], axis=1),
                        jnp.concatenate([b_f[0], b_f[1]]), ("logf", "logf"), geom, aux=aux, out_dtype=F32)
    o = hgrn_scan(z, log_f, dm, bsz, t)
    return post_mm_residual(o, z, 2, s, head_g, gate_mod, w_out, dm // HG_EXPAND, geom)


def _moe_layer(s, gain, shift, gate_mod, geom, router_w, router_b, w1, w3, w2):
    n_tok, d = s.shape
    n_experts = w1.shape[0]
    n_assign = n_tok * TOP_K
    e, g = norm_route(s, gain, shift, geom, router_w, router_b)
    flat_e = e.T.reshape(n_assign)
    rank, counts = assignment_ranks(flat_e, n_experts)
    padded = (counts + MOE_BLOCK - 1) // MOE_BLOCK * MOE_BLOCK
    end_pad = jnp.cumsum(padded)
    start_pad = end_pad - padded
    onehot = flat_e[:, None] == jnp.arange(n_experts, dtype=jnp.int32)[None, :]
    dest = jnp.sum(jnp.where(onehot, start_pad[None, :], 0), axis=1) + rank
    n_blocks = -(-n_assign // MOE_BLOCK) + n_experts
    block_start = jnp.arange(n_blocks, dtype=jnp.int32) * MOE_BLOCK
    block_expert = jnp.minimum(jnp.sum(end_pad[None, :] <= block_start[:, None], axis=1), n_experts - 1)
    meta = jnp.stack([counts, padded, start_pad]).astype(jnp.int32)
    xb = moe_scatter(s, gain, shift, geom, dest.astype(jnp.int32), meta, n_blocks * MOE_BLOCK)
    yb = expert_ffn(xb, block_expert.astype(jnp.int32), w1, w3, w2)
    return moe_gather_combine(yb, dest.astype(jnp.int32), g.T, s, gate_mod, geom)


def kernel(x, c, ctx, c_ctx, ada_w, ada_b, norm_mix, norm_ffn, norm_out, ml_w_in, ml_w_gate, ml_b_gate, ml_conv, ml_head_g, ml_w_out, rw_mu, rw_w_rkv, rw_w0, rw_w1, rw_w2, rw_a0, rw_a1, rw_a2, rw_g1, rw_g2, rw_k_k, rw_k_a, rw_r_k, rw_ln_w, rw_ln_b, rw_w_out, hg_w_in, hg_w_f, hg_b_f, hg_lb_logits, hg_head_g, hg_w_out, router_w, router_b, ex_w1, ex_w3, ex_w2):
    depth = ada_w.shape[0]
    bsz = x.shape[0]
    cond = jax.nn.silu(jnp.concatenate([c, c_ctx[None]], axis=0))
    cond = jnp.pad(cond, ((0, -(bsz + 1) % 8), (0, 0)))
    dm = x.shape[2]
    t = CTX_LEN + x.shape[1]
    n = bsz * t
    geom = (t // ROW_TILE, CTX_LEN // ROW_TILE)
    s = jnp.concatenate([ctx, x], axis=1).reshape(n, dm)
    for i in range(depth):
        mod = mm(cond, ada_w[i], bias=ada_b[i])
        mod_x = jnp.split(mod[:bsz, None, :], 6, axis=-1)
        mod_c = jnp.split(mod[bsz], 6, axis=-1)

        def table(idx):
            return jnp.stack([jnp.broadcast_to(mod_c[idx], (bsz, dm)), mod_x[idx][:, 0]], axis=1).reshape(2 * bsz, 1, dm)

        kind, j = i % N_MIXERS, i // N_MIXERS
        if kind == 2:
            s = _hgrn2_layer(s, norm_mix[i] * (1 + table(1)), table(0), table(2), geom, bsz, t, i,
                             hg_w_in[j], hg_w_f[j], hg_b_f[j], hg_lb_logits, hg_head_g[j], hg_w_out[j])
        elif kind == 0:
            s = _mlstm_layer(s, norm_mix[i] * (1 + table(1)), table(0), table(2), geom, bsz, t,
                             ml_w_in[j], ml_w_gate[j], ml_b_gate[j], ml_conv[j], ml_head_g[j], ml_w_out[j])
        else:
            s = _rwkv7_layer(s, norm_mix[i] * (1 + table(1)), table(0), table(2), geom, bsz, t,
                             rw_mu[j], rw_w_rkv[j], rw_w0[j], rw_w1[j], rw_w2[j], rw_a0[j],
                             rw_a1[j], rw_a2[j], rw_g1[j], rw_g2[j], rw_k_k[j], rw_k_a[j],
                             rw_r_k[j], rw_ln_w[j], rw_ln_b[j], rw_w_out[j])
        s = _moe_layer(s, norm_ffn[i] * (1 + table(4)), table(3), table(5), geom, router_w, router_b,
                       ex_w1[i], ex_w3[i], ex_w2[i])
    return final_norm(s, norm_out, bsz, t, geom)
```

```python
import functools

import jax
import jax.numpy as jnp
from jax import lax
from jax.experimental import pallas as pl
from jax.experimental.pallas import tpu as pltpu

F32 = jnp.float32
BF16 = jnp.bfloat16

GRID_W = 64
CTX_LEN = 256
N_MIXERS = 3
NORM_EPS = 1e-6
ML_HEADS = 8
RW_HEAD_DIM = 64
RW_GN_EPS = 64e-5
HG_EXPAND = 128
N_GROUPS = 4
TOP_K = 2
MOE_BLOCK = 512

LANES = 128
ML_CHUNK = 128
RW_CHUNK = 64
RW_PRE_CHUNKS = 4
HG_CHUNK = 64
HG_SUB = 16
HG_EXP_CLAMP = 80.0
VMEM_LIMIT = 48 * 1024 * 1024
VMEM_LIMIT_BIG = 56 * 1024 * 1024

NT = (((1,), (1,)), ((), ()))
NN = (((1,), (0,)), ((), ()))


def _dot(a, b, dims=NN, passes=1):
    a_hi = a.astype(BF16)
    b_hi = b.astype(BF16)
    out = lax.dot_general(a_hi, b_hi, dims, preferred_element_type=F32)
    if passes == 3:
        a_lo = (a - a_hi.astype(F32)).astype(BF16)
        b_lo = (b - b_hi.astype(F32)).astype(BF16)
        out = out + lax.dot_general(a_hi, b_lo, dims, preferred_element_type=F32)
        out = out + lax.dot_general(a_lo, b_hi, dims, preferred_element_type=F32)
    return out


def _cumsum_rows(x, reverse):
    n = x.shape[0]
    row = lax.broadcasted_iota(jnp.int32, x.shape, 0)
    s = 1
    while s < n:
        if reverse:
            x = x + jnp.where(row < n - s, pltpu.roll(x, n - s, axis=0), 0.0)
        else:
            x = x + jnp.where(row >= s, pltpu.roll(x, s, axis=0), 0.0)
        s *= 2
    return x


def _pick_tile(n, candidates):
    for c in candidates:
        if n % c == 0:
            return c
    raise ValueError(f"no tile for {n}")


def _scan_chunk_index(d, p, nctx, nc):
    rev = jnp.where(p < nctx, nctx - 1 - p, nc - 1 - (p - nctx))
    return jnp.where(d == 0, p, rev)


_ACTS = {
    None: lambda y: y,
    "sigmoid": jax.nn.sigmoid,
    "silu": lambda y: y * jax.nn.sigmoid(y),
    "tanh": jnp.tanh,
}


def _mm_kernel(x_ref, w_ref, b_ref, o_ref, *, act, precise):
    if precise:
        y = _dot(x_ref[...], w_ref[...], passes=3)
    else:
        y = jnp.dot(x_ref[...].astype(BF16), w_ref[...], preferred_element_type=F32)
    o_ref[...] = _ACTS[act](y + b_ref[...]).astype(o_ref.dtype)


def mm(x, w, bias=None, act=None, out_dtype=F32, precise=False):
    n, k = x.shape
    m = w.shape[1]
    tm = _pick_tile(n, (512, 256, 128, 64, 32, 16, 8))
    tn = m if m <= 1024 else _pick_tile(m, (1024, 512, 256, 128))
    if not precise:
        w = w.astype(BF16)
    if bias is None:
        bias = jnp.zeros((m,), F32)
    return pl.pallas_call(
        functools.partial(_mm_kernel, act=act, precise=precise),
        grid=(n // tm, m // tn),
        in_specs=[pl.BlockSpec((tm, k), lambda i, j: (i, 0)),
                  pl.BlockSpec((k, tn), lambda i, j: (0, j)),
                  pl.BlockSpec((1, tn), lambda i, j: (0, j))],
        out_specs=pl.BlockSpec((tm, tn), lambda i, j: (i, j)),
        out_shape=jax.ShapeDtypeStruct((n, m), out_dtype),
        compiler_params=pltpu.CompilerParams(vmem_limit_bytes=VMEM_LIMIT),
        name="mm",
    )(x, w, bias.reshape(1, m).astype(F32))


ROW_TILE = 256


def _log1p_exp_neg_abs(x):
    return jnp.log(1.0 + jnp.exp(-jnp.abs(x)))


def _log_sigmoid(y):
    return jnp.minimum(y, 0.0) - _log1p_exp_neg_abs(y)


def _softplus(x):
    return jnp.maximum(x, 0.0) + _log1p_exp_neg_abs(x)


def _logaddexp(a, b):
    return jnp.maximum(a, b) + _log1p_exp_neg_abs(a - b)


def _norm_mod(x, gain, shift):
    return x * lax.rsqrt(jnp.mean(x * x, axis=-1, keepdims=True) + NORM_EPS) * gain + shift


def _seg_map(tpb, nctx_t):
    def seg(i):
        return (i // tpb) * 2 + jnp.where(i % tpb < nctx_t, 0, 1)
    return seg


_EPILOGUES = {
    None: lambda y, aux: y,
    "sigmoid": lambda y, aux: jax.nn.sigmoid(y),
    "silu": lambda y, aux: y * jax.nn.sigmoid(y),
    "logf": lambda y, aux: _logaddexp(aux[0:1, :], aux[1:2, :] + _log_sigmoid(y)),
}


def _sub_tiles(n, most=4):
    return _pick_tile(n // ROW_TILE, tuple(range(most, 0, -1)))


def _nmm_kernel(s_ref, gain_ref, shift_ref, w_ref, b_ref, aux_ref, o_ref, h_scr, *, acts, sub, seg):
    j = pl.program_id(1)

    @pl.when(j == 0)
    def _():
        for k in range(sub):
            rows = pl.ds(k * ROW_TILE, ROW_TILE)
            sk = seg(pl.program_id(0) * sub + k)
            h_scr[rows, :] = _norm_mod(s_ref[rows, :], gain_ref[sk], shift_ref[sk]).astype(BF16)

    y = jnp.dot(h_scr[...], w_ref[...], preferred_element_type=F32) + b_ref[...]
    for act in sorted(set(acts), key=str):
        cols = [jj for jj, a in enumerate(acts) if a == act]
        if len(cols) == len(acts):
            o_ref[...] = _EPILOGUES[act](y, aux_ref[...]).astype(o_ref.dtype)
        else:
            @pl.when(functools.reduce(jnp.logical_or, [j == jj for jj in cols]))
            def _(act=act):
                o_ref[...] = _EPILOGUES[act](y, aux_ref[...]).astype(o_ref.dtype)


def norm_mod_mm(s, gain, shift, w, bias, acts, geom, aux=None, out_dtype=None):
    out_dtype = out_dtype or BF16
    n, k = s.shape
    m = w.shape[1]
    tn = m // len(acts)
    tpb, nctx_t = geom
    seg = _seg_map(tpb, nctx_t)
    sub = _sub_tiles(n)
    tm = sub * ROW_TILE
    if bias is None:
        bias = jnp.zeros((m,), F32)
    if aux is None:
        aux = jnp.zeros((2, m), F32)
    return pl.pallas_call(
        functools.partial(_nmm_kernel, acts=tuple(acts), sub=sub, seg=seg),
        grid=(n // tm, m // tn),
        in_specs=[pl.BlockSpec((tm, k), lambda i, j: (i, 0)),
                  pl.BlockSpec(gain.shape, lambda i, j: (0, 0, 0)),
                  pl.BlockSpec(shift.shape, lambda i, j: (0, 0, 0)),
                  pl.BlockSpec((k, tn), lambda i, j: (0, j)),
                  pl.BlockSpec((1, tn), lambda i, j: (0, j)),
                  pl.BlockSpec((2, tn), lambda i, j: (0, j))],
        out_specs=pl.BlockSpec((tm, tn), lambda i, j: (i, j)),
        out_shape=jax.ShapeDtypeStruct((n, m), out_dtype),
        scratch_shapes=[pltpu.VMEM((tm, k), BF16)],
        compiler_params=pltpu.CompilerParams(
            dimension_semantics=("arbitrary", "arbitrary"), vmem_limit_bytes=VMEM_LIMIT),
        name="norm_mod_mm",
    )(s, gain, shift, w.astype(BF16), bias.reshape(1, m).astype(F32), aux.astype(F32))


def _gated_residual_store(o_ref, s_ref, gm_ref, y, sub, seg):
    for k in range(sub):
        rows = pl.ds(k * ROW_TILE, ROW_TILE)
        gm = gm_ref[seg(pl.program_id(0) * sub + k)]
        o_ref[rows, :] = s_ref[rows, :] + gm * y[k * ROW_TILE:(k + 1) * ROW_TILE]


def _post_kernel(h_ref, g_ref, s_ref, hg_ref, gm_ref, w_ref, o_ref, *, heads, sub, seg):
    x = h_ref[0].astype(F32) + h_ref[1].astype(F32)
    hd = x.shape[1] // heads
    parts = []
    for h in range(heads):
        xh = x[:, h * hd:(h + 1) * hd]
        parts.append(xh * lax.rsqrt(jnp.mean(xh * xh, axis=-1, keepdims=True) + NORM_EPS))
    y = (jnp.concatenate(parts, axis=1) * hg_ref[...] * g_ref[...].astype(F32)).astype(BF16)
    _gated_residual_store(o_ref, s_ref, gm_ref, jnp.dot(y, w_ref[...], preferred_element_type=F32), sub, seg)


def post_mm_residual(h2, gate_arr, gate_block, s, head_g, gm, w_out, heads, geom):
    n, dm = s.shape
    seg = _seg_map(*geom)
    sub = _sub_tiles(n, most=2)
    tm = sub * ROW_TILE
    return pl.pallas_call(
        functools.partial(_post_kernel, heads=heads, sub=sub, seg=seg),
        grid=(n // tm,),
        in_specs=[pl.BlockSpec((2, tm, dm), lambda i: (0, i, 0)),
                  pl.BlockSpec((tm, dm), lambda i: (i, gate_block)),
                  pl.BlockSpec((tm, dm), lambda i: (i, 0)),
                  pl.BlockSpec((1, dm), lambda i: (0, 0)),
                  pl.BlockSpec(gm.shape, lambda i: (0, 0, 0)),
                  pl.BlockSpec((dm, dm), lambda i: (0, 0))],
        out_specs=pl.BlockSpec((tm, dm), lambda i: (i, 0)),
        out_shape=jax.ShapeDtypeStruct((n, dm), F32),
        compiler_params=pltpu.CompilerParams(dimension_semantics=("arbitrary",), vmem_limit_bytes=VMEM_LIMIT),
        name="post_mm_residual",
    )(h2, gate_arr, s, head_g.reshape(1, dm), gm, w_out.astype(BF16))


CONV_COLS = 512


def _conv_kernel(cur_ref, up_ref, dn_ref, w_ref, sc_ref, o_ref, *, tpb, nctx_t):
    ti = pl.program_id(0) % tpb
    is_ctx = ti < nctx_t
    no_up = jnp.logical_or(is_ctx, ti == nctx_t)
    no_dn = jnp.logical_or(is_ctx, ti == tpb - 1)
    x = cur_ref[...].astype(F32)
    up = jnp.where(no_up, 0.0, up_ref[...].astype(F32))
    dn = jnp.where(no_dn, 0.0, dn_ref[...].astype(F32))
    ext = jnp.concatenate([up, x, dn], axis=0)
    nr = ext.shape[0]
    ext_m = pltpu.roll(ext, 1, axis=0)
    ext_p = pltpu.roll(ext, nr - 1, axis=0)
    tpos = lax.broadcasted_iota(jnp.int32, (ROW_TILE, 1), 0)
    col = tpos % GRID_W
    left_ok = jnp.where(is_ctx, (tpos > 0).astype(F32), (col > 0).astype(F32))
    right_ok = jnp.where(is_ctx, (tpos < ROW_TILE - 1).astype(F32), (col < GRID_W - 1).astype(F32))
    vert = jnp.where(is_ctx, 0.0, 1.0)
    w = w_ref[...]
    acc = None
    for dr in (-1, 0, 1):
        base = GRID_W * (1 + dr)
        r3 = 3 * (dr + 1)
        term = (ext[base:base + ROW_TILE] * w[r3 + 1:r3 + 2]
                + ext_m[base:base + ROW_TILE] * w[r3:r3 + 1] * left_ok
                + ext_p[base:base + ROW_TILE] * w[r3 + 2:r3 + 3] * right_ok)
        if dr != 0:
            term = term * vert
        acc = term if acc is None else acc + term
    o_ref[...] = (acc * jax.nn.sigmoid(acc) * sc_ref[...]).astype(o_ref.dtype)


def conv_silu(z, conv_w, scale, width, geom):
    n = z.shape[0]
    tpb, nctx_t = geom
    assert nctx_t == 1 and ROW_TILE % GRID_W == 0
    hb = ROW_TILE // GRID_W
    last = n // GRID_W - 1
    return pl.pallas_call(
        functools.partial(_conv_kernel, tpb=tpb, nctx_t=nctx_t),
        grid=(n // ROW_TILE, width // CONV_COLS),
        in_specs=[pl.BlockSpec((ROW_TILE, CONV_COLS), lambda i, c: (i, c)),
                  pl.BlockSpec((GRID_W, CONV_COLS), lambda i, c: (jnp.maximum(i * hb - 1, 0), c)),
                  pl.BlockSpec((GRID_W, CONV_COLS), lambda i, c: (jnp.minimum((i + 1) * hb, last), c)),
                  pl.BlockSpec((9, CONV_COLS), lambda i, c: (0, c)),
                  pl.BlockSpec((1, CONV_COLS), lambda i, c: (0, c))],
        out_specs=pl.BlockSpec((ROW_TILE, CONV_COLS), lambda i, c: (i, c)),
        out_shape=jax.ShapeDtypeStruct((n, width), BF16),
        compiler_params=pltpu.CompilerParams(
            dimension_semantics=("arbitrary", "arbitrary"), vmem_limit_bytes=VMEM_LIMIT),
        name="conv_silu",
    )(z, z, z, conv_w.reshape(9, width).astype(F32), scale.reshape(1, width).astype(F32))


def _dot_exact01(a, b, lhs01=False):
    x = (b if lhs01 else a).astype(F32)
    out = None
    for _ in range(3):
        t = x.astype(BF16)
        x = x - t.astype(F32)
        p = lax.dot_general(a, t, NN, preferred_element_type=F32) if lhs01 else \
            lax.dot_general(t, b, NN, preferred_element_type=F32)
        out = p if out is None else out + p
    return out


def _cummax_rows(x, reverse):
    n = x.shape[0]
    row = lax.broadcasted_iota(jnp.int32, x.shape, 0)
    s = 1
    while s < n:
        if reverse:
            x = jnp.maximum(x, jnp.where(row < n - s, pltpu.roll(x, n - s, axis=0), -jnp.inf))
        else:
            x = jnp.maximum(x, jnp.where(row >= s, pltpu.roll(x, s, axis=0), -jnp.inf))
        s *= 2
    return x


def _ml_scan_kernel(q_ref, k_ref, v_ref, gc_ref, gr_ref, o_ref, z_scr, m_scr, *, heads):
    L = q_ref.shape[0]
    assert L == LANES
    d = pl.program_id(0)

    @pl.when(pl.program_id(2) == 0)
    def _():
        z_scr[...] = jnp.zeros_like(z_scr)
        m_scr[...] = jnp.zeros_like(m_scr)

    row = lax.broadcasted_iota(jnp.int32, (L, L), 0)
    col = lax.broadcasted_iota(jnp.int32, (L, L), 1)
    ones_blk = jnp.ones((L, LANES), BF16)

    def body(reverse):
        incl = (col >= row) if reverse else (col <= row)
        incl_t = (row >= col) if reverse else (row <= col)
        last = 0 if reverse else L - 1
        hs = range(heads)
        sls = [slice(h * LANES, (h + 1) * LANES) for h in hs]
        qk = [_dot(q_ref[:, sls[h]], k_ref[:, sls[h]], NT) for h in hs]
        qz = [_dot(q_ref[:, sls[h]], z_scr[h]) for h in hs]
        b_cols = _dot_exact01(incl.astype(BF16), gc_ref[0, :, heads:2 * heads], lhs01=True)
        b_rows = _dot_exact01(gr_ref[0, 0, heads:2 * heads, :], incl_t.astype(BF16))
        cols = jnp.concatenate([b_cols, gc_ref[0, :, 0:heads]], axis=1)
        pick = lax.broadcasted_iota(jnp.int32, (2 * heads, 2 * LANES), 0)
        lane2 = lax.broadcasted_iota(jnp.int32, (2 * heads, 2 * LANES), 1)
        stats = []
        for h in hs:
            sel = (pick == jnp.where(lane2 < LANES, h, heads + h)).astype(BF16)
            rep = _dot_exact01(cols, sel)
            b_rep, ig_rep = rep[:, :LANES], rep[:, LANES:]
            ig_row = gr_ref[0, 0, h:h + 1, :]
            b_row = b_rows[h:h + 1, :]
            m_prev = m_scr[h:h + 1, :]
            cmax = _cummax_rows(ig_rep - b_rep, reverse)
            dmat = jnp.where(incl, b_rep - (b_row - ig_row), -jnp.inf)
            inter = b_rep + m_prev
            m_t = jnp.maximum(inter, b_rep + cmax)
            b_last = b_rep[last:last + 1, :]
            m_new = jnp.maximum(b_last + m_prev, b_last + cmax[last:last + 1, :])
            w_k = jnp.exp(b_last - b_rep + ig_rep - m_new)
            w_prev = jnp.exp(b_last + m_prev - m_new)
            stats.append((jnp.exp(dmat - m_t), jnp.exp(inter - m_t), jnp.exp(-m_t), w_k, w_prev, m_new))
        kv = []
        for h in hs:
            w_k = stats[h][3]
            wv = jnp.concatenate([w_k * v_ref[:, sls[h]].astype(F32), w_k], axis=1)
            kv.append(_dot(k_ref[:, sls[h]].astype(F32).T, wv))
        s = [qk[h] * stats[h][0] for h in hs]
        sv = [_dot(s[h], jnp.concatenate([v_ref[:, sls[h]], ones_blk], axis=1)) for h in hs]
        for h in hs:
            _, w_inter, floor, _, w_prev, m_new = stats[h]
            w2 = jnp.concatenate([w_inter, w_inter], axis=1)
            full = sv[h] + w2 * qz[h]
            den = full[:, LANES:]
            o_ref[0, :, sls[h]] = (full[:, :LANES] / jnp.maximum(jnp.abs(den), floor)).astype(o_ref.dtype)
            z_scr[h] = jnp.concatenate([w_prev, w_prev], axis=1) * z_scr[h] + kv[h]
            m_scr[h:h + 1, :] = m_new

    @pl.when(d == 0)
    def _():
        body(False)

    @pl.when(d == 1)
    def _():
        body(True)


def mlstm_scan(qk, z, gc, gr, dm, bsz, t):
    n = qk.shape[0]
    heads = dm // LANES
    L = ML_CHUNK
    nc, nctx = t // L, CTX_LEN // L

    def row(d, b, p):
        return b * nc + _scan_chunk_index(d, p, nctx, nc)

    return pl.pallas_call(
        functools.partial(_ml_scan_kernel, heads=heads),
        grid=(2, bsz, nc),
        in_specs=[pl.BlockSpec((L, dm), lambda d, b, p: (row(d, b, p), 0)),
                  pl.BlockSpec((L, dm), lambda d, b, p: (row(d, b, p), 1)),
                  pl.BlockSpec((L, dm), lambda d, b, p: (row(d, b, p), 2)),
                  pl.BlockSpec((1, L, 2 * heads), lambda d, b, p: (d, row(d, b, p), 0)),
                  pl.BlockSpec((1, 1, 2 * heads, L),
                               lambda d, b, p: (d, b, 0, _scan_chunk_index(d, p, nctx, nc)))],
        out_specs=pl.BlockSpec((1, L, dm), lambda d, b, p: (d, row(d, b, p), 0)),
        out_shape=jax.ShapeDtypeStruct((2, n, dm), BF16),
        scratch_shapes=[pltpu.VMEM((heads, LANES, 2 * LANES), F32), pltpu.VMEM((heads, LANES), F32)],
        compiler_params=pltpu.CompilerParams(
            dimension_semantics=("arbitrary", "arbitrary", "arbitrary"), vmem_limit_bytes=VMEM_LIMIT),
        name="mlstm_scan",
    )(qk, qk, z, gc, gr)


RW_STATE_PASSES = 3


def _rw_scan_kernel(lw0_ref, lw1_ref, kd0_ref, kd1_ref, a0_ref, a1_ref, r0_ref, r1_ref, v0_ref, v1_ref,
                    kk0_ref, kk1_ref, of_ref, ob_ref, h_scr, rdp_scr, o0_scr, m_scr, ha_scr, *, nchunk):
    L = RW_CHUNK
    j = pl.program_id(2)

    @pl.when(j == 0)
    def _():
        for ref in (h_scr, rdp_scr, o0_scr, m_scr, ha_scr):
            ref[...] = jnp.zeros_like(ref)

    hs = [h_scr[0], h_scr[1]]

    def recurrence_step(k):
        for d, o_ref in ((0, of_ref), (1, ob_ref)):
            c = k if d == 0 else nchunk - 1 - k
            o_ref[pl.ds(c * L, L), :] = (_dot(rdp_scr[d, c], hs[d], passes=RW_STATE_PASSES)
                                         + o0_scr[d, c]).astype(o_ref.dtype)
            hs[d] = _dot(m_scr[d, c], hs[d], passes=RW_STATE_PASSES) + ha_scr[d, c]

    pending = list(range(nchunk))

    half = LANES // 2
    row = lax.broadcasted_iota(jnp.int32, (L, LANES), 0)
    col = lax.broadcasted_iota(jnp.int32, (L, LANES), 1) % half
    eye2 = (row == col).astype(F32)
    lane = lax.broadcasted_iota(jnp.int32, (1, LANES), 1)
    m0 = (lane < half).astype(F32)
    m1 = (lane >= half).astype(F32)
    r2 = lax.broadcasted_iota(jnp.int32, (LANES, LANES), 0)
    c2 = lax.broadcasted_iota(jnp.int32, (LANES, LANES), 1)
    same_head = (r2 // half) == (c2 // half)

    def stack(x):
        return jnp.concatenate([x * m0, x * m1], axis=0)

    chains = [(d, c) for c in range(nchunk) for d in range(2)]
    st = {}
    for d, c in chains:
        reverse = d == 1
        sl = pl.ds(c * L, L)
        lw = (lw0_ref, lw1_ref)[d][sl, :]
        k = (kd0_ref, kd1_ref)[d][sl, :].astype(F32)
        kk = (kk0_ref, kk1_ref)[d][sl, :].astype(F32)
        akk = kk * (a0_ref, a1_ref)[d][sl, :].astype(F32)
        g = _cumsum_rows(lw, reverse)
        ieg = jnp.exp(-g)
        g_last = g[0:1] if reverse else g[L - 1:L]
        dl = jnp.exp(g_last - g)
        st[d, c] = dict(kd=kk * jnp.exp(g - lw), rd=(r0_ref, r1_ref)[d][sl, :].astype(F32) * jnp.exp(g),
                        ai=akk * ieg, ki=k * ieg, ad=akk * dl, kdd=k * dl, eg_last=jnp.exp(g_last),
                        v=(v0_ref, v1_ref)[d][sl, :].astype(F32))
    recurrence_step(pending.pop(0))
    for d, c in chains:
        s = st[d, c]
        reverse = d == 1
        incl = (col >= row) if reverse else (col <= row)
        strict = (col > row) if reverse else (col < row)
        x = jnp.concatenate([s["kd"], s["rd"]], axis=0)
        rhs = jnp.concatenate([stack(s["ai"]), stack(s["ki"])], axis=0)
        sc = _dot(x, rhs, NT)
        s["a_ab"] = jnp.where(strict, sc[:L, :LANES], 0.0)
        s["a_ak"] = jnp.where(strict, sc[:L, LANES:], 0.0)
        s["b_ra"] = jnp.where(incl, sc[L:, :LANES], 0.0)
        s["b_rk"] = jnp.where(incl, sc[L:, LANES:], 0.0)
        s["tinv"] = eye2 - s["a_ab"]
        s["pw"] = s["a_ab"]
    span = 2
    while span < L:
        for d, c in chains:
            s = st[d, c]
            s["pw"] = _dot(s["pw"], stack(s["pw"]))
        for d, c in chains:
            s = st[d, c]
            s["tinv"] = _dot(s["tinv"], stack(eye2 + s["pw"]))
        if pending:
            recurrence_step(pending.pop(0))
        span *= 2
    while pending:
        recurrence_step(pending.pop(0))
    h_scr[0] = hs[0]
    h_scr[1] = hs[1]
    for d, c in chains:
        s = st[d, c]
        s["w"] = -_dot(s["tinv"], stack(s["a_ak"]))
        s["kdp"] = _dot(s["tinv"], stack(s["kd"]))
    for d, c in chains:
        s = st[d, c]
        s["vst"] = stack(s["v"])
        s["u0"] = _dot(s["w"], s["vst"])
    for d, c in chains:
        s = st[d, c]
        lhs = jnp.concatenate([s["b_ra"], s["b_rk"]], axis=1)
        rhs = jnp.concatenate([stack(s["u0"]), s["vst"]], axis=0)
        o0_scr[d, c] = _dot(lhs, rhs)
        rdp_scr[d, c] = s["rd"] - _dot(s["b_ra"], stack(s["kdp"]))
        diag = jnp.where(r2 == c2, s["eg_last"], 0.0)
        m_scr[d, c] = jnp.where(same_head, diag - _dot(s["ad"].T, s["kdp"]), 0.0)
        at = jnp.concatenate([s["ad"], s["kdd"]], axis=0).T
        ha_scr[d, c] = jnp.where(same_head, _dot(at, jnp.concatenate([s["u0"], s["v"]], axis=0)), 0.0)


def rwkv_scan(lw, kda, rvkg, dm, bsz, t):
    n = lw.shape[0]
    pairs = dm // LANES
    L = RW_CHUNK
    nchunk = RW_PRE_CHUNKS
    tb = nchunk * L
    nblk, nctx = t // tb, CTX_LEN // tb

    def block(d, b, j):
        return b * nblk + _scan_chunk_index(d, jnp.minimum(j, nblk - 1), nctx, nblk)

    def ispec(d, col):
        return pl.BlockSpec((tb, LANES), lambda b, p, j: (block(d, b, j), col * pairs + p))

    def ospec(d):
        return pl.BlockSpec((tb, LANES), lambda b, p, j: (block(d, b, jnp.maximum(j - 1, 0)), p))

    return pl.pallas_call(
        functools.partial(_rw_scan_kernel, nchunk=nchunk),
        grid=(bsz, pairs, nblk + 1),
        in_specs=[ispec(0, 0), ispec(1, 1), ispec(0, 0), ispec(1, 1), ispec(0, 2), ispec(1, 3),
                  ispec(0, 0), ispec(1, 0), ispec(0, 1), ispec(1, 1), ispec(0, 2), ispec(1, 2)],
        out_specs=[ospec(0), ospec(1)],
        out_shape=[jax.ShapeDtypeStruct((n, dm), BF16), jax.ShapeDtypeStruct((n, dm), BF16)],
        scratch_shapes=[pltpu.VMEM((2, LANES, LANES), F32), pltpu.VMEM((2, nchunk, L, LANES), F32),
                        pltpu.VMEM((2, nchunk, L, LANES), F32), pltpu.VMEM((2, nchunk, LANES, LANES), F32),
                        pltpu.VMEM((2, nchunk, LANES, LANES), F32)],
        compiler_params=pltpu.CompilerParams(
            dimension_semantics=("arbitrary", "arbitrary", "arbitrary"), vmem_limit_bytes=VMEM_LIMIT),
        name="rwkv_scan",
    )(lw, lw, kda, kda, kda, kda, rvkg, rvkg, rvkg, rvkg, rvkg, rvkg)


HALO_ROWS = 8


def _group_sum(x, width):
    r = lax.broadcasted_iota(jnp.int32, (LANES, LANES), 0) // width
    c = lax.broadcasted_iota(jnp.int32, (LANES, LANES), 1) // width
    ones = (r == c).astype(BF16)
    hi = x.astype(BF16)
    lo = (x - hi.astype(F32)).astype(BF16)
    parts = []
    for j in range(x.shape[1] // LANES):
        sl = slice(j * LANES, (j + 1) * LANES)
        parts.append(jnp.dot(hi[:, sl], ones, preferred_element_type=F32)
                     + jnp.dot(lo[:, sl], ones, preferred_element_type=F32))
    return jnp.concatenate(parts, axis=1)


def _rw_proj_kernel(s_ref, up_ref, dn_ref, gain_ref, shift_ref, mu_ref, wrkv_ref, w1_ref, w2_ref, w0_ref,
                    a1_ref, a2_ref, a0_ref, g1_ref, g2_ref, kk_ref, ka_ref,
                    lw_ref, kda_ref, rvkg_ref, *, tpb, nctx_t):
    tm, dm = s_ref.shape
    ti = pl.program_id(0) % tpb
    is_ctx = ti < nctx_t
    has_up = jnp.logical_not(jnp.logical_or(is_ctx, ti == nctx_t))
    has_dn = jnp.logical_not(jnp.logical_or(is_ctx, ti == tpb - 1))
    gain = gain_ref[0]
    shift = shift_ref[0]
    u = _norm_mod(s_ref[...], gain, shift)
    u_up = jnp.where(has_up, _norm_mod(up_ref[HALO_ROWS - 1:HALO_ROWS, :], gain, shift), 0.0)
    u_dn = jnp.where(has_dn, _norm_mod(dn_ref[0:1, :], gain, shift), 0.0)
    row = lax.broadcasted_iota(jnp.int32, (tm, 1), 0)
    u_m = jnp.where(row == 0, u_up, pltpu.roll(u, 1, axis=0))
    u_p = jnp.where(row == tm - 1, u_dn, pltpu.roll(u, tm - 1, axis=0))
    du = 0.5 * (u_m + u_p) - u
    mu = mu_ref[...]

    def mix(i):
        return (u + du * mu[i:i + 1]).astype(BF16)

    def dot(a, b):
        return jnp.dot(a.astype(BF16), b, preferred_element_type=F32)

    r = dot(mix(0), wrkv_ref[0])
    k = dot(mix(1), wrkv_ref[1])
    v = dot(mix(2), wrkv_ref[2])
    w_pre = dot(jnp.tanh(dot(mix(3), w1_ref[...])), w2_ref[...]) + w0_ref[...]
    lw_ref[...] = -jnp.exp(-_softplus(-w_pre) - 0.5)
    a = jax.nn.sigmoid(dot(dot(mix(4), a1_ref[...]), a2_ref[...]) + a0_ref[...])
    g = dot(jax.nn.sigmoid(dot(mix(5), g1_ref[...])), g2_ref[...])
    kk = k * kk_ref[...]
    kk = kk * lax.rsqrt(jnp.maximum(_group_sum(kk * kk, RW_HEAD_DIM), 1e-24))
    ka = ka_ref[...]
    for d in range(2):
        kda_ref[:, d * dm:(d + 1) * dm] = (k * (1.0 + (a[:, d * dm:(d + 1) * dm] - 1.0) * ka)).astype(kda_ref.dtype)
    kda_ref[:, 2 * dm:] = a.astype(kda_ref.dtype)
    for j, val in enumerate((r, v, kk, g)):
        rvkg_ref[:, j * dm:(j + 1) * dm] = val.astype(rvkg_ref.dtype)


def rwkv_proj(s, gain, shift, geom, mu, w_rkv, w0, w1, w2, a0, a1, a2, g1, g2, k_k, k_a):
    n, dm = s.shape
    tpb, nctx_t = geom
    assert nctx_t == 1
    seg = _seg_map(tpb, nctx_t)
    hb = ROW_TILE // HALO_ROWS
    last = n // HALO_ROWS - 1
    lora = w1.shape[2]

    def blockdiag(w):
        z = jnp.zeros_like(w[0])
        return jnp.concatenate([jnp.concatenate([w[0], z], axis=1), jnp.concatenate([z, w[1]], axis=1)], axis=0)

    consts = [jnp.pad(mu, ((0, HALO_ROWS - mu.shape[0]), (0, 0))), w_rkv.astype(BF16),
              jnp.concatenate([w1[0], w1[1]], axis=1).astype(BF16), blockdiag(w2).astype(BF16),
              jnp.concatenate([w0[0], w0[1]])[None],
              jnp.concatenate([a1[0], a1[1]], axis=1).astype(BF16), blockdiag(a2).astype(BF16),
              jnp.concatenate([a0[0], a0[1]])[None],
              g1.astype(BF16), g2.astype(BF16), k_k[None], k_a[None]]

    def const_spec(x):
        nd = x.ndim
        return pl.BlockSpec(x.shape, lambda i: (0,) * nd)

    return pl.pallas_call(
        functools.partial(_rw_proj_kernel, tpb=tpb, nctx_t=nctx_t),
        grid=(n // ROW_TILE,),
        in_specs=[pl.BlockSpec((ROW_TILE, dm), lambda i: (i, 0)),
                  pl.BlockSpec((HALO_ROWS, dm), lambda i: (jnp.maximum(i * hb - 1, 0), 0)),
                  pl.BlockSpec((HALO_ROWS, dm), lambda i: (jnp.minimum((i + 1) * hb, last), 0)),
                  pl.BlockSpec((1, 1, dm), lambda i: (seg(i), 0, 0)),
                  pl.BlockSpec((1, 1, dm), lambda i: (seg(i), 0, 0))] + [const_spec(x) for x in consts],
        out_specs=[pl.BlockSpec((ROW_TILE, 2 * dm), lambda i: (i, 0)),
                   pl.BlockSpec((ROW_TILE, 4 * dm), lambda i: (i, 0)),
                   pl.BlockSpec((ROW_TILE, 4 * dm), lambda i: (i, 0))],
        out_shape=[jax.ShapeDtypeStruct((n, 2 * dm), F32), jax.ShapeDtypeStruct((n, 4 * dm), BF16),
                   jax.ShapeDtypeStruct((n, 4 * dm), BF16)],
        compiler_params=pltpu.CompilerParams(dimension_semantics=("arbitrary",), vmem_limit_bytes=VMEM_LIMIT_BIG),
        name="rwkv_proj",
    )(s, s, s, gain, shift, *consts)


def _rw_post_kernel(of_ref, ob_ref, r_ref, v_ref, g_ref, k0_ref, k1_ref, s_ref, lnw_ref, lnb_ref, rk_ref, gm_ref,
                    w_ref, out_ref, *, sub, seg):
    o = of_ref[...].astype(F32) + ob_ref[...].astype(F32)
    inv = 1.0 / RW_HEAD_DIM
    mean = _group_sum(o, RW_HEAD_DIM) * inv
    oc = o - mean
    var = _group_sum(oc * oc, RW_HEAD_DIM) * inv
    xn = oc * lax.rsqrt(var + RW_GN_EPS) * lnw_ref[...] + lnb_ref[...]
    r = r_ref[...].astype(F32)
    ksum = k0_ref[...].astype(F32) + k1_ref[...].astype(F32)
    bonus = _group_sum(r * ksum * rk_ref[...], RW_HEAD_DIM) * v_ref[...].astype(F32)
    y = ((xn + bonus) * g_ref[...].astype(F32)).astype(BF16)
    _gated_residual_store(out_ref, s_ref, gm_ref, jnp.dot(y, w_ref[...], preferred_element_type=F32), sub, seg)


def rwkv_post(o_f, o_b, rvkg, kda, s, ln_w, ln_b, r_k, gm, w_out, geom):
    n, dm = s.shape
    seg = _seg_map(*geom)
    sub = _sub_tiles(n, most=2)
    tm = sub * ROW_TILE

    def col(block):
        return pl.BlockSpec((tm, dm), lambda i: (i, block))

    vec = pl.BlockSpec((1, dm), lambda i: (0, 0))
    return pl.pallas_call(
        functools.partial(_rw_post_kernel, sub=sub, seg=seg),
        grid=(n // tm,),
        in_specs=[col(0), col(0), col(0), col(1), col(3), col(0), col(1),
                  col(0), vec, vec, vec, pl.BlockSpec(gm.shape, lambda i: (0, 0, 0)),
                  pl.BlockSpec((dm, dm), lambda i: (0, 0))],
        out_specs=pl.BlockSpec((tm, dm), lambda i: (i, 0)),
        out_shape=jax.ShapeDtypeStruct((n, dm), F32),
        compiler_params=pltpu.CompilerParams(dimension_semantics=("arbitrary",), vmem_limit_bytes=VMEM_LIMIT),
        name="rwkv_post",
    )(o_f, o_b, rvkg, rvkg, rvkg, kda, kda, s, ln_w[None], ln_b[None], r_k[None], gm, w_out.astype(BF16))


def _hg_scan_kernel(q_ref, v_ref, lf_ref, o_ref, st_scr, *, heads):
    C = q_ref.shape[0]
    d = pl.program_id(0)
    nsub = C // HG_SUB

    @pl.when(pl.program_id(2) == 0)
    def _():
        st_scr[...] = jnp.zeros_like(st_scr)

    def body(reverse):
        last = 0 if reverse else C - 1
        hs = range(heads)
        sls = [slice(h * LANES, (h + 1) * LANES) for h in hs]
        g = [lf_ref[:, sls[h]] for h in hs]
        b = [_cumsum_rows(g[h], reverse) for h in hs]
        k = [-jnp.tanh(0.5 * g[h]) * (jnp.exp(g[h]) + 1.0) for h in hs]
        o_inter = [_dot(q_ref[:, sls[h]].astype(F32) * jnp.exp(b[h]), st_scr[h], NT) for h in hs]
        parts = [[None] * nsub for _ in hs]
        for i in range(nsub):
            r0 = i * HG_SUB
            lo, hi = (r0, C) if reverse else (0, r0 + HG_SUB)
            first = r0 + HG_SUB - 1 if reverse else r0
            row = lax.broadcasted_iota(jnp.int32, (HG_SUB, hi - lo), 0) + r0
            col = lax.broadcasted_iota(jnp.int32, (HG_SUB, hi - lo), 1) + lo
            keep = (col >= row) if reverse else (col <= row)
            att = []
            for h in hs:
                rho = b[h][first:first + 1, :] - g[h][first:first + 1, :]
                qi = q_ref[r0:r0 + HG_SUB, sls[h]].astype(F32) * jnp.exp(b[h][r0:r0 + HG_SUB] - rho)
                ki = k[h][lo:hi] * jnp.exp(jnp.minimum(rho - b[h][lo:hi], HG_EXP_CLAMP))
                att.append(jnp.where(keep, _dot(qi, ki, NT), 0.0))
            for h in hs:
                parts[h][i] = _dot(att[h], v_ref[lo:hi, sls[h]])
        for h in hs:
            o_ref[0, :, sls[h]] = (o_inter[h] + jnp.concatenate(parts[h], axis=0)).astype(o_ref.dtype)
        upd = []
        for h in hs:
            b_last = b[h][last:last + 1, :]
            upd.append((jnp.exp(b_last),
                        _dot(v_ref[:, sls[h]].astype(F32).T, k[h] * jnp.exp(b_last - b[h]))))
        for h in hs:
            st_scr[h] = st_scr[h] * upd[h][0] + upd[h][1]

    @pl.when(d == 0)
    def _():
        body(False)

    @pl.when(d == 1)
    def _():
        body(True)


def hgrn_scan(z, logf, dm, bsz, t):
    n = z.shape[0]
    heads = dm // LANES
    C = HG_CHUNK
    nc, nctx = t // C, CTX_LEN // C

    def row(d, b, p):
        return b * nc + _scan_chunk_index(d, p, nctx, nc)

    return pl.pallas_call(
        functools.partial(_hg_scan_kernel, heads=heads),
        grid=(2, bsz, nc),
        in_specs=[pl.BlockSpec((C, dm), lambda d, b, p: (row(d, b, p), 0)),
                  pl.BlockSpec((C, dm), lambda d, b, p: (row(d, b, p), 1)),
                  pl.BlockSpec((C, dm), lambda d, b, p: (row(d, b, p), d))],
        out_specs=pl.BlockSpec((1, C, dm), lambda d, b, p: (d, row(d, b, p), 0)),
        out_shape=jax.ShapeDtypeStruct((2, n, dm), BF16),
        scratch_shapes=[pltpu.VMEM((heads, LANES, LANES), F32)],
        compiler_params=pltpu.CompilerParams(
            dimension_semantics=("arbitrary", "arbitrary", "arbitrary"), vmem_limit_bytes=VMEM_LIMIT),
        name="hgrn_scan",
    )(z, z, logf)


def _first_argmax(vals):
    best, idx = vals[0], jnp.zeros(vals[0].shape, jnp.int32)
    for i in range(1, len(vals)):
        better = vals[i] > best
        best = jnp.where(better, vals[i], best)
        idx = jnp.where(better, i, idx)
    return best, idx


def _router_kernel(s_ref, gain_ref, shift_ref, wt_ref, b_ref, e_ref, g_ref, *, n_groups, top_k):
    n_experts = wt_ref.shape[0]
    per = n_experts // n_groups
    h = _norm_mod(s_ref[...], gain_ref[0], shift_ref[0])
    aff = jax.nn.sigmoid(_dot(wt_ref[...], h, NT, passes=3))
    sel = aff + b_ref[...]
    a = [aff[e:e + 1, :] for e in range(n_experts)]
    s = [sel[e:e + 1, :] for e in range(n_experts)]
    neg = jnp.full_like(s[0], -jnp.inf)
    scores = []
    for g in range(n_groups):
        grp = s[g * per:(g + 1) * per]
        m1, i1 = _first_argmax(grp)
        m2, _ = _first_argmax([jnp.where(i1 == j, neg, grp[j]) for j in range(per)])
        scores.append(m1 + m2)
    _, best = _first_argmax(scores)

    def in_best(rows):
        out = []
        for j in range(per):
            x = rows[j]
            for g in range(1, n_groups):
                x = jnp.where(best == g, rows[g * per + j], x)
            out.append(x)
        return out

    sb, ab = in_best(s), in_best(a)
    picked, chosen = [], []
    cand = sb
    for _ in range(top_k):
        _, i = _first_argmax(cand)
        c = ab[0]
        for j in range(1, per):
            c = jnp.where(i == j, ab[j], c)
        picked.append(i)
        chosen.append(c)
        cand = [jnp.where(i == j, neg, cand[j]) for j in range(per)]
    total = functools.reduce(jnp.add, chosen)
    for kk_ in range(top_k):
        e_ref[kk_:kk_ + 1, :] = best * per + picked[kk_]
        g_ref[kk_:kk_ + 1, :] = chosen[kk_] / total


def norm_route(s, gain, shift, geom, router_w, router_b):
    n, k = s.shape
    n_experts = router_w.shape[1]
    tm = ROW_TILE
    seg = _seg_map(*geom)
    return pl.pallas_call(
        functools.partial(_router_kernel, n_groups=N_GROUPS, top_k=TOP_K),
        grid=(n // tm,),
        in_specs=[pl.BlockSpec((tm, k), lambda i: (i, 0)),
                  pl.BlockSpec((1, 1, k), lambda i: (seg(i), 0, 0)),
                  pl.BlockSpec((1, 1, k), lambda i: (seg(i), 0, 0)),
                  pl.BlockSpec((n_experts, k), lambda i: (0, 0)),
                  pl.BlockSpec((n_experts, 1), lambda i: (0, 0))],
        out_specs=[pl.BlockSpec((TOP_K, tm), lambda i: (0, i)), pl.BlockSpec((TOP_K, tm), lambda i: (0, i))],
        out_shape=[jax.ShapeDtypeStruct((TOP_K, n), jnp.int32), jax.ShapeDtypeStruct((TOP_K, n), F32)],
        compiler_params=pltpu.CompilerParams(dimension_semantics=("arbitrary",), vmem_limit_bytes=VMEM_LIMIT),
        name="norm_route",
    )(s, gain, shift, router_w.T, router_b.reshape(n_experts, 1).astype(F32))


def _final_norm_kernel(s_ref, g_ref, o_ref):
    x = s_ref[...]
    o_ref[0] = x * lax.rsqrt(jnp.mean(x * x, axis=-1, keepdims=True) + NORM_EPS) * g_ref[...]


def final_norm(s, g, bsz, t, geom):
    dm = s.shape[1]
    tpb, nctx_t = geom
    return pl.pallas_call(
        _final_norm_kernel,
        grid=(bsz, tpb - nctx_t),
        in_specs=[pl.BlockSpec((ROW_TILE, dm), lambda b, i: (b * tpb + nctx_t + i, 0)),
                  pl.BlockSpec((1, dm), lambda b, i: (0, 0))],
        out_specs=pl.BlockSpec((1, ROW_TILE, dm), lambda b, i: (b, i, 0)),
        out_shape=jax.ShapeDtypeStruct((bsz, t - CTX_LEN, dm), F32),
        name="final_norm",
    )(s, g[None])


DMA_UNROLL = 8


def _row_copy_waits(src_row, dst_row, sem, count):
    def body(_, carry):
        pltpu.make_async_copy(src_row, dst_row, sem).wait()
        return carry
    lax.fori_loop(0, count, body, 0)


def _scatter_kernel(dest_ref, meta_ref, s_ref, gain_ref, shift_ref, xb_ref, hbuf, zrow, sems, zsem, *, n_experts):
    i = pl.program_id(0)
    last = pl.num_programs(0) - 1
    slot = i % 2
    per_tile = TOP_K * ROW_TILE

    def wait_tile(sl):
        for _ in range(TOP_K):
            pltpu.make_async_copy(hbuf.at[sl], xb_ref.at[pl.ds(0, ROW_TILE), :], sems.at[sl]).wait()

    @pl.when(i >= 2)
    def _():
        wait_tile(slot)

    hbuf[slot] = _norm_mod(s_ref[...], gain_ref[0], shift_ref[0])

    def issue(r8, carry):
        for u in range(DMA_UNROLL):
            r = r8 * DMA_UNROLL + u
            for k in range(TOP_K):
                d = dest_ref[0, 0, TOP_K * r + k]
                pltpu.make_async_copy(hbuf.at[slot, pl.ds(r, 1), :], xb_ref.at[pl.ds(d, 1), :],
                                      sems.at[slot]).start(priority=(TOP_K * u + k) % 2)
        return carry
    lax.fori_loop(0, ROW_TILE // DMA_UNROLL, issue, 0)

    @pl.when(i == last)
    def _():
        @pl.when(last >= 1)
        def _():
            wait_tile(1 - slot)
        wait_tile(slot)
        zrow[...] = jnp.zeros_like(zrow)
        for e in range(n_experts):
            lo = meta_ref[2, e] + meta_ref[0, e]
            hi = meta_ref[2, e] + meta_ref[1, e]

            def pad_start(q, carry):
                pltpu.make_async_copy(zrow.at[pl.ds(0, 1), :], xb_ref.at[pl.ds(q, 1), :], zsem.at[0]).start()
                return carry
            lax.fori_loop(lo, hi, pad_start, 0)
            _row_copy_waits(zrow.at[pl.ds(0, 1), :], xb_ref.at[pl.ds(0, 1), :], zsem.at[0], hi - lo)
        end = meta_ref[2, n_experts - 1] + meta_ref[1, n_experts - 1]

        def tail_start(q, carry):
            pltpu.make_async_copy(zrow.at[pl.ds(0, 1), :], xb_ref.at[pl.ds(q, 1), :], zsem.at[0]).start()
            return carry
        lax.fori_loop(end, xb_ref.shape[0], tail_start, 0)
        _row_copy_waits(zrow.at[pl.ds(0, 1), :], xb_ref.at[pl.ds(0, 1), :], zsem.at[0], xb_ref.shape[0] - end)


def moe_scatter(s, gain, shift, geom, dest, meta, n_slots):
    n, dm = s.shape
    seg = _seg_map(*geom)
    nt = n // ROW_TILE
    return pl.pallas_call(
        functools.partial(_scatter_kernel, n_experts=meta.shape[1]),
        grid=(nt,),
        in_specs=[pl.BlockSpec((1, 1, TOP_K * ROW_TILE), lambda i: (i, 0, 0), memory_space=pltpu.SMEM),
                  pl.BlockSpec(memory_space=pltpu.SMEM),
                  pl.BlockSpec((ROW_TILE, dm), lambda i: (i, 0)),
                  pl.BlockSpec((1, 1, dm), lambda i: (seg(i), 0, 0)),
                  pl.BlockSpec((1, 1, dm), lambda i: (seg(i), 0, 0))],
        out_specs=pl.BlockSpec(memory_space=pl.ANY),
        out_shape=jax.ShapeDtypeStruct((n_slots, dm), F32),
        scratch_shapes=[pltpu.VMEM((2, ROW_TILE, dm), F32), pltpu.VMEM((8, dm), F32),
                        pltpu.SemaphoreType.DMA((2,)), pltpu.SemaphoreType.DMA((1,))],
        compiler_params=pltpu.CompilerParams(dimension_semantics=("arbitrary",), vmem_limit_bytes=VMEM_LIMIT),
        name="moe_scatter",
    )(dest.reshape(nt, 1, TOP_K * ROW_TILE), meta, s, gain, shift)


def _gather_combine_kernel(dcur_ref, dnxt_ref, g_ref, s_ref, gm_ref, yb_ref, o_ref, ybuf, sems):
    i = pl.program_id(0)
    nsteps = pl.num_programs(0)
    slot = i % 2

    def start_tile(dref, sl):
        def issue(r8, carry):
            for u in range(DMA_UNROLL):
                r = r8 * DMA_UNROLL + u
                for k in range(TOP_K):
                    d = dref[0, 0, TOP_K * r + k]
                    pltpu.make_async_copy(yb_ref.at[pl.ds(d, 1), :], ybuf.at[sl, k, pl.ds(r, 1), :],
                                          sems.at[sl]).start(priority=(TOP_K * u + k) % 2)
            return carry
        lax.fori_loop(0, ROW_TILE // DMA_UNROLL, issue, 0)

    @pl.when(i == 0)
    def _():
        start_tile(dcur_ref, 0)

    @pl.when(i + 1 < nsteps)
    def _():
        start_tile(dnxt_ref, 1 - slot)

    for k in range(TOP_K):
        pltpu.make_async_copy(yb_ref.at[pl.ds(0, ROW_TILE), :], ybuf.at[slot, k], sems.at[slot]).wait()
    g = g_ref[...]
    y = sum(ybuf[slot, k] * g[:, k:k + 1] for k in range(TOP_K))
    o_ref[...] = s_ref[...] + gm_ref[0] * y


def moe_gather_combine(yb, dest, gate, s, gm, geom):
    n, dm = s.shape
    seg = _seg_map(*geom)
    nt = n // ROW_TILE
    d3 = dest.reshape(nt, 1, TOP_K * ROW_TILE)
    row = pl.BlockSpec((ROW_TILE, dm), lambda i: (i, 0))
    return pl.pallas_call(
        _gather_combine_kernel,
        grid=(nt,),
        in_specs=[pl.BlockSpec((1, 1, TOP_K * ROW_TILE), lambda i: (i, 0, 0), memory_space=pltpu.SMEM),
                  pl.BlockSpec((1, 1, TOP_K * ROW_TILE), lambda i: (jnp.minimum(i + 1, nt - 1), 0, 0),
                               memory_space=pltpu.SMEM),
                  pl.BlockSpec((ROW_TILE, TOP_K), lambda i: (i, 0)), row,
                  pl.BlockSpec((1, 1, dm), lambda i: (seg(i), 0, 0)),
                  pl.BlockSpec(memory_space=pl.ANY)],
        out_specs=row,
        out_shape=jax.ShapeDtypeStruct((n, dm), F32),
        scratch_shapes=[pltpu.VMEM((2, TOP_K, ROW_TILE, dm), F32), pltpu.SemaphoreType.DMA((2,))],
        compiler_params=pltpu.CompilerParams(dimension_semantics=("arbitrary",), vmem_limit_bytes=VMEM_LIMIT),
        name="moe_gather_combine",
    )(d3, d3, gate, s, gm, yb)


def _rank_kernel(e_ref, rank_ref, cnt_ref, carry_scr, *, n_experts):
    @pl.when(pl.program_id(0) == 0)
    def _():
        carry_scr[...] = jnp.zeros_like(carry_scr)

    bl = e_ref.shape[2]
    e_row = e_ref[0]
    sub = lax.broadcasted_iota(jnp.int32, (n_experts, bl), 0)
    onehot = (sub == e_row).astype(F32)
    ri = lax.broadcasted_iota(jnp.int32, (bl, bl), 0)
    ci = lax.broadcasted_iota(jnp.int32, (bl, bl), 1)
    earlier = (ri < ci).astype(BF16)
    cum = jnp.dot(onehot.astype(BF16), earlier, preferred_element_type=F32)
    carry = carry_scr[...]
    rank_ref[0] = jnp.sum(onehot * (cum + carry[:, :1]), axis=0, keepdims=True).astype(jnp.int32)
    carry = carry + jnp.sum(onehot, axis=1, keepdims=True)
    carry_scr[...] = carry
    cnt_ref[...] = carry.astype(jnp.int32)


def assignment_ranks(flat_e, n_experts):
    n_assign = flat_e.shape[0]
    bl = _pick_tile(n_assign, (512, 256, 128))
    nblk = n_assign // bl
    rank, cnt = pl.pallas_call(
        functools.partial(_rank_kernel, n_experts=n_experts),
        grid=(nblk,),
        in_specs=[pl.BlockSpec((1, 1, bl), lambda i: (i, 0, 0))],
        out_specs=[pl.BlockSpec((1, 1, bl), lambda i: (i, 0, 0)),
                   pl.BlockSpec((n_experts, LANES), lambda i: (0, 0))],
        out_shape=[jax.ShapeDtypeStruct((nblk, 1, bl), jnp.int32),
                   jax.ShapeDtypeStruct((n_experts, LANES), jnp.int32)],
        scratch_shapes=[pltpu.VMEM((n_experts, LANES), F32)],
        compiler_params=pltpu.CompilerParams(dimension_semantics=("arbitrary",)),
        name="assignment_ranks",
    )(flat_e.reshape(nblk, 1, bl))
    return rank.reshape(n_assign), cnt[:, 0]


def _ffn_kernel(be_ref, x_ref, w1_ref, w3_ref, w2_ref, o_ref):
    del be_ref
    x = x_ref[...].astype(BF16)
    a = jnp.dot(x, w1_ref[0], preferred_element_type=F32)
    b = jnp.dot(x, w3_ref[0], preferred_element_type=F32)
    hid = (a * jax.nn.sigmoid(a) * b).astype(BF16)
    o_ref[...] = jnp.dot(hid, w2_ref[0], preferred_element_type=F32).astype(o_ref.dtype)


def expert_ffn(xb, block_expert, w1, w3, w2):
    nrows, dm = xb.shape
    f = w1.shape[2]
    nb = nrows // MOE_BLOCK
    return pl.pallas_call(
        _ffn_kernel,
        grid_spec=pltpu.PrefetchScalarGridSpec(
            num_scalar_prefetch=1,
            grid=(nb,),
            in_specs=[pl.BlockSpec((MOE_BLOCK, dm), lambda i, be: (i, 0)),
                      pl.BlockSpec((1, dm, f), lambda i, be: (be[i], 0, 0)),
                      pl.BlockSpec((1, dm, f), lambda i, be: (be[i], 0, 0)),
                      pl.BlockSpec((1, f, dm), lambda i, be: (be[i], 0, 0))],
            out_specs=pl.BlockSpec((MOE_BLOCK, dm), lambda i, be: (i, 0))),
        out_shape=jax.ShapeDtypeStruct((nrows, dm), F32),
        compiler_params=pltpu.CompilerParams(
            dimension_semantics=("arbitrary",), vmem_limit_bytes=VMEM_LIMIT),
        name="expert_ffn",
    )(block_expert, xb, w1.astype(BF16), w3.astype(BF16), w2.astype(BF16))


def _mlstm_layer(s, gain, shift, gate_mod, geom, bsz, t, w_in, w_gate, b_gate, conv, head_g, w_out):
    n, dm = s.shape
    heads = ML_HEADS
    z = norm_mod_mm(s, gain, shift, w_in, None, (None, None, None, "sigmoid"), geom)
    scale = jnp.concatenate([jnp.ones((dm,), F32), jnp.full((dm,), (dm // heads) ** -0.5, F32)])
    qk = conv_silu(z, conv, scale, 2 * dm, geom)
    ng = 4 * heads
    wg = jnp.pad(jnp.concatenate([w_gate[0], w_gate[1]], axis=1), ((0, 0), (0, LANES - ng)))
    bg = jnp.pad(jnp.concatenate([b_gate[0], b_gate[1]]), (0, LANES - ng))
    gates = norm_mod_mm(s, gain, shift, wg, bg, (None,), geom, out_dtype=F32)[:, :ng]
    gates = gates.reshape(bsz, t, 2, 2 * heads)
    gates = jnp.concatenate([gates[..., :heads], jax.nn.log_sigmoid(gates[..., heads:])], axis=-1)
    gc = jnp.moveaxis(gates, 2, 0).reshape(2, n, 2 * heads)
    gr = jnp.transpose(gates, (2, 0, 3, 1))
    h = mlstm_scan(qk, z, gc, gr, dm, bsz, t)
    return post_mm_residual(h, z, 3, s, head_g, gate_mod, w_out, heads, geom)


def _rwkv7_layer(s, gain, shift, gate_mod, geom, bsz, t, mu, w_rkv, w0, w1, w2, a0, a1, a2, g1, g2,
                 k_k, k_a, r_k, ln_w, ln_b, w_out):
    dm = s.shape[1]
    lw, kda, rvkg = rwkv_proj(s, gain, shift, geom, mu, w_rkv, w0, w1, w2, a0, a1, a2, g1, g2, k_k, k_a)
    o_f, o_b = rwkv_scan(lw, kda, rvkg, dm, bsz, t)
    return rwkv_post(o_f, o_b, rvkg, kda, s, ln_w, ln_b, r_k, gate_mod, w_out, geom)


def _hgrn2_layer(s, gain, shift, gate_mod, geom, bsz, t, layer_idx, w_in, w_f, b_f, lb_logits, head_g, w_out):
    dm = s.shape[1]
    z = norm_mod_mm(s, gain, shift, w_in, None, ("silu", None, "silu"), geom)
    p = jax.nn.softmax(lb_logits, axis=0)
    lb = jnp.cumsum(p, axis=0)[layer_idx] - p[0]
    aux = jnp.tile(jnp.stack([jnp.log(lb), jnp.log1p(-lb)]), (1, 2))
    log_f = norm_mod_mm(s, gain, shift, jnp.concatenate([w_f[0], w_f[1]], axis=1),
                        jnp.concatenate([b_f[0], b_f[1]]), ("logf", "logf"), geom, aux=aux, out_dtype=F32)
    o = hgrn_scan(z, log_f, dm, bsz, t)
    return post_mm_residual(o, z, 2, s, head_g, gate_mod, w_out, dm // HG_EXPAND, geom)


def _moe_layer(s, gain, shift, gate_mod, geom, router_w, router_b, w1, w3, w2):
    n_tok, d = s.shape
    n_experts = w1.shape[0]
    n_assign = n_tok * TOP_K
    e, g = norm_route(s, gain, shift, geom, router_w, router_b)
    flat_e = e.T.reshape(n_assign)
    rank, counts = assignment_ranks(flat_e, n_experts)
    padded = (counts + MOE_BLOCK - 1) // MOE_BLOCK * MOE_BLOCK
    end_pad = jnp.cumsum(padded)
    start_pad = end_pad - padded
    onehot = flat_e[:, None] == jnp.arange(n_experts, dtype=jnp.int32)[None, :]
    dest = jnp.sum(jnp.where(onehot, start_pad[None, :], 0), axis=1) + rank
    n_blocks = -(-n_assign // MOE_BLOCK) + n_experts
    block_start = jnp.arange(n_blocks, dtype=jnp.int32) * MOE_BLOCK
    block_expert = jnp.minimum(jnp.sum(end_pad[None, :] <= block_start[:, None], axis=1), n_experts - 1)
    meta = jnp.stack([counts, padded, start_pad]).astype(jnp.int32)
    xb = moe_scatter(s, gain, shift, geom, dest.astype(jnp.int32), meta, n_blocks * MOE_BLOCK)
    yb = expert_ffn(xb, block_expert.astype(jnp.int32), w1, w3, w2)
    return moe_gather_combine(yb, dest.astype(jnp.int32), g.T, s, gate_mod, geom)


def kernel(x, c, ctx, c_ctx, ada_w, ada_b, norm_mix, norm_ffn, norm_out, ml_w_in, ml_w_gate, ml_b_gate, ml_conv, ml_head_g, ml_w_out, rw_mu, rw_w_rkv, rw_w0, rw_w1, rw_w2, rw_a0, rw_a1, rw_a2, rw_g1, rw_g2, rw_k_k, rw_k_a, rw_r_k, rw_ln_w, rw_ln_b, rw_w_out, hg_w_in, hg_w_f, hg_b_f, hg_lb_logits, hg_head_g, hg_w_out, router_w, router_b, ex_w1, ex_w3, ex_w2):
    depth = ada_w.shape[0]
    bsz = x.shape[0]
    cond = jax.nn.silu(jnp.concatenate([c, c_ctx[None]], axis=0))
    cond = jnp.pad(cond, ((0, -(bsz + 1) % 8), (0, 0)))
    dm = x.shape[2]
    t = CTX_LEN + x.shape[1]
    n = bsz * t
    geom = (t // ROW_TILE, CTX_LEN // ROW_TILE)
    s = jnp.concatenate([ctx, x], axis=1).reshape(n, dm)
    for i in range(depth):
        mod = mm(cond, ada_w[i], bias=ada_b[i])
        mod_x = jnp.split(mod[:bsz, None, :], 6, axis=-1)
        mod_c = jnp.split(mod[bsz], 6, axis=-1)

        def table(idx):
            return jnp.stack([jnp.broadcast_to(mod_c[idx], (bsz, dm)), mod_x[idx][:, 0]], axis=1).reshape(2 * bsz, 1, dm)

        kind, j = i % N_MIXERS, i // N_MIXERS
        if kind == 2:
            s = _hgrn2_layer(s, norm_mix[i] * (1 + table(1)), table(0), table(2), geom, bsz, t, i,
                             hg_w_in[j], hg_w_f[j], hg_b_f[j], hg_lb_logits, hg_head_g[j], hg_w_out[j])
        elif kind == 0:
            s = _mlstm_layer(s, norm_mix[i] * (1 + table(1)), table(0), table(2), geom, bsz, t,
                             ml_w_in[j], ml_w_gate[j], ml_b_gate[j], ml_conv[j], ml_head_g[j], ml_w_out[j])
        else:
            s = _rwkv7_layer(s, norm_mix[i] * (1 + table(1)), table(0), table(2), geom, bsz, t,
                             rw_mu[j], rw_w_rkv[j], rw_w0[j], rw_w1[j], rw_w2[j], rw_a0[j],
                             rw_a1[j], rw_a2[j], rw_g1[j], rw_g2[j], rw_k_k[j], rw_k_a[j],
                             rw_r_k[j], rw_ln_w[j], rw_ln_b[j], rw_w_out[j])
        s = _moe_layer(s, norm_ffn[i] * (1 + table(4)), table(3), table(5), geom, router_w, router_b,
                       ex_w1[i], ex_w3[i], ex_w2[i])
    return final_norm(s, norm_out, bsz, t, geom)
```

```python
import functools

import jax
import jax.numpy as jnp
from jax import lax
from jax.experimental import pallas as pl
from jax.experimental.pallas import tpu as pltpu

F32 = jnp.float32
BF16 = jnp.bfloat16

GRID_W = 64
CTX_LEN = 256
N_MIXERS = 3
NORM_EPS = 1e-6
ML_HEADS = 8
RW_HEAD_DIM = 64
RW_GN_EPS = 64e-5
HG_EXPAND = 128
N_GROUPS = 4
TOP_K = 2
MOE_BLOCK = 512

LANES = 128
ML_CHUNK = 128
RW_CHUNK = 64
RW_PRE_CHUNKS = 4
RW_PAIRS_PER_STEP = 2
HG_CHUNK = 64
HG_SUB = 16
HG_EXP_CLAMP = 80.0
VMEM_LIMIT = 48 * 1024 * 1024
VMEM_LIMIT_BIG = 56 * 1024 * 1024

NT = (((1,), (1,)), ((), ()))
NN = (((1,), (0,)), ((), ()))


def _dot(a, b, dims=NN, passes=1):
    a_hi = a.astype(BF16)
    b_hi = b.astype(BF16)
    out = lax.dot_general(a_hi, b_hi, dims, preferred_element_type=F32)
    if passes == 3:
        a_lo = (a - a_hi.astype(F32)).astype(BF16)
        b_lo = (b - b_hi.astype(F32)).astype(BF16)
        out = out + lax.dot_general(a_hi, b_lo, dims, preferred_element_type=F32)
        out = out + lax.dot_general(a_lo, b_hi, dims, preferred_element_type=F32)
    return out


def _cumsum_rows(x, reverse):
    n = x.shape[0]
    row = lax.broadcasted_iota(jnp.int32, x.shape, 0)
    s = 1
    while s < n:
        if reverse:
            x = x + jnp.where(row < n - s, pltpu.roll(x, n - s, axis=0), 0.0)
        else:
            x = x + jnp.where(row >= s, pltpu.roll(x, s, axis=0), 0.0)
        s *= 2
    return x


def _pick_tile(n, candidates):
    for c in candidates:
        if n % c == 0:
            return c
    raise ValueError(f"no tile for {n}")


def _scan_chunk_index(d, p, nctx, nc):
    rev = jnp.where(p < nctx, nctx - 1 - p, nc - 1 - (p - nctx))
    return jnp.where(d == 0, p, rev)


_ACTS = {
    None: lambda y: y,
    "sigmoid": jax.nn.sigmoid,
    "silu": lambda y: y * jax.nn.sigmoid(y),
    "tanh": jnp.tanh,
}


def _mm_kernel(x_ref, w_ref, b_ref, o_ref, *, act, precise):
    if precise:
        y = _dot(x_ref[...], w_ref[...], passes=3)
    else:
        y = jnp.dot(x_ref[...].astype(BF16), w_ref[...], preferred_element_type=F32)
    o_ref[...] = _ACTS[act](y + b_ref[...]).astype(o_ref.dtype)


def mm(x, w, bias=None, act=None, out_dtype=F32, precise=False):
    n, k = x.shape
    m = w.shape[1]
    tm = _pick_tile(n, (512, 256, 128, 64, 32, 16, 8))
    tn = m if m <= 1024 else _pick_tile(m, (1024, 512, 256, 128))
    if not precise:
        w = w.astype(BF16)
    if bias is None:
        bias = jnp.zeros((m,), F32)
    return pl.pallas_call(
        functools.partial(_mm_kernel, act=act, precise=precise),
        grid=(n // tm, m // tn),
        in_specs=[pl.BlockSpec((tm, k), lambda i, j: (i, 0)),
                  pl.BlockSpec((k, tn), lambda i, j: (0, j)),
                  pl.BlockSpec((1, tn), lambda i, j: (0, j))],
        out_specs=pl.BlockSpec((tm, tn), lambda i, j: (i, j)),
        out_shape=jax.ShapeDtypeStruct((n, m), out_dtype),
        compiler_params=pltpu.CompilerParams(vmem_limit_bytes=VMEM_LIMIT),
        name="mm",
    )(x, w, bias.reshape(1, m).astype(F32))


ROW_TILE = 256


def _log1p_exp_neg_abs(x):
    return jnp.log(1.0 + jnp.exp(-jnp.abs(x)))


def _log_sigmoid(y):
    return jnp.minimum(y, 0.0) - _log1p_exp_neg_abs(y)


def _softplus(x):
    return jnp.maximum(x, 0.0) + _log1p_exp_neg_abs(x)


def _logaddexp(a, b):
    return jnp.maximum(a, b) + _log1p_exp_neg_abs(a - b)


def _norm_mod(x, gain, shift):
    return x * lax.rsqrt(jnp.mean(x * x, axis=-1, keepdims=True) + NORM_EPS) * gain + shift


def _seg_map(tpb, nctx_t):
    def seg(i):
        return (i // tpb) * 2 + jnp.where(i % tpb < nctx_t, 0, 1)
    return seg


_EPILOGUES = {
    None: lambda y, aux: y,
    "sigmoid": lambda y, aux: jax.nn.sigmoid(y),
    "silu": lambda y, aux: y * jax.nn.sigmoid(y),
    "logf": lambda y, aux: _logaddexp(aux[0:1, :], aux[1:2, :] + _log_sigmoid(y)),
}


def _sub_tiles(n, most=4):
    return _pick_tile(n // ROW_TILE, tuple(range(most, 0, -1)))


def _nmm_kernel(s_ref, gain_ref, shift_ref, w_ref, b_ref, aux_ref, o_ref, h_scr, *, acts, sub, seg):
    j = pl.program_id(1)

    @pl.when(j == 0)
    def _():
        for k in range(sub):
            rows = pl.ds(k * ROW_TILE, ROW_TILE)
            sk = seg(pl.program_id(0) * sub + k)
            h_scr[rows, :] = _norm_mod(s_ref[rows, :], gain_ref[sk], shift_ref[sk]).astype(BF16)

    y = jnp.dot(h_scr[...], w_ref[...], preferred_element_type=F32) + b_ref[...]
    for act in sorted(set(acts), key=str):
        cols = [jj for jj, a in enumerate(acts) if a == act]
        if len(cols) == len(acts):
            o_ref[...] = _EPILOGUES[act](y, aux_ref[...]).astype(o_ref.dtype)
        else:
            @pl.when(functools.reduce(jnp.logical_or, [j == jj for jj in cols]))
            def _(act=act):
                o_ref[...] = _EPILOGUES[act](y, aux_ref[...]).astype(o_ref.dtype)


def norm_mod_mm(s, gain, shift, w, bias, acts, geom, aux=None, out_dtype=None):
    out_dtype = out_dtype or BF16
    n, k = s.shape
    m = w.shape[1]
    tn = m // len(acts)
    tpb, nctx_t = geom
    seg = _seg_map(tpb, nctx_t)
    sub = _sub_tiles(n)
    tm = sub * ROW_TILE
    if bias is None:
        bias = jnp.zeros((m,), F32)
    if aux is None:
        aux = jnp.zeros((2, m), F32)
    return pl.pallas_call(
        functools.partial(_nmm_kernel, acts=tuple(acts), sub=sub, seg=seg),
        grid=(n // tm, m // tn),
        in_specs=[pl.BlockSpec((tm, k), lambda i, j: (i, 0)),
                  pl.BlockSpec(gain.shape, lambda i, j: (0, 0, 0)),
                  pl.BlockSpec(shift.shape, lambda i, j: (0, 0, 0)),
                  pl.BlockSpec((k, tn), lambda i, j: (0, j)),
                  pl.BlockSpec((1, tn), lambda i, j: (0, j)),
                  pl.BlockSpec((2, tn), lambda i, j: (0, j))],
        out_specs=pl.BlockSpec((tm, tn), lambda i, j: (i, j)),
        out_shape=jax.ShapeDtypeStruct((n, m), out_dtype),
        scratch_shapes=[pltpu.VMEM((tm, k), BF16)],
        compiler_params=pltpu.CompilerParams(
            dimension_semantics=("arbitrary", "arbitrary"), vmem_limit_bytes=VMEM_LIMIT),
        name="norm_mod_mm",
    )(s, gain, shift, w.astype(BF16), bias.reshape(1, m).astype(F32), aux.astype(F32))


def _gated_residual_store(o_ref, s_ref, gm_ref, y, sub, seg):
    for k in range(sub):
        rows = pl.ds(k * ROW_TILE, ROW_TILE)
        gm = gm_ref[seg(pl.program_id(0) * sub + k)]
        o_ref[rows, :] = s_ref[rows, :] + gm * y[k * ROW_TILE:(k + 1) * ROW_TILE]


def _post_kernel(h_ref, g_ref, s_ref, hg_ref, gm_ref, w_ref, o_ref, *, heads, sub, seg):
    x = h_ref[0].astype(F32) + h_ref[1].astype(F32)
    hd = x.shape[1] // heads
    parts = []
    for h in range(heads):
        xh = x[:, h * hd:(h + 1) * hd]
        parts.append(xh * lax.rsqrt(jnp.mean(xh * xh, axis=-1, keepdims=True) + NORM_EPS))
    y = (jnp.concatenate(parts, axis=1) * hg_ref[...] * g_ref[...].astype(F32)).astype(BF16)
    _gated_residual_store(o_ref, s_ref, gm_ref, jnp.dot(y, w_ref[...], preferred_element_type=F32), sub, seg)


def post_mm_residual(h2, gate_arr, gate_block, s, head_g, gm, w_out, heads, geom):
    n, dm = s.shape
    seg = _seg_map(*geom)
    sub = _sub_tiles(n, most=2)
    tm = sub * ROW_TILE
    return pl.pallas_call(
        functools.partial(_post_kernel, heads=heads, sub=sub, seg=seg),
        grid=(n // tm,),
        in_specs=[pl.BlockSpec((2, tm, dm), lambda i: (0, i, 0)),
                  pl.BlockSpec((tm, dm), lambda i: (i, gate_block)),
                  pl.BlockSpec((tm, dm), lambda i: (i, 0)),
                  pl.BlockSpec((1, dm), lambda i: (0, 0)),
                  pl.BlockSpec(gm.shape, lambda i: (0, 0, 0)),
                  pl.BlockSpec((dm, dm), lambda i: (0, 0))],
        out_specs=pl.BlockSpec((tm, dm), lambda i: (i, 0)),
        out_shape=jax.ShapeDtypeStruct((n, dm), F32),
        compiler_params=pltpu.CompilerParams(dimension_semantics=("arbitrary",), vmem_limit_bytes=VMEM_LIMIT),
        name="post_mm_residual",
    )(h2, gate_arr, s, head_g.reshape(1, dm), gm, w_out.astype(BF16))


CONV_COLS = 512


def _conv_kernel(cur_ref, up_ref, dn_ref, w_ref, sc_ref, o_ref, *, tpb, nctx_t):
    ti = pl.program_id(0) % tpb
    is_ctx = ti < nctx_t
    no_up = jnp.logical_or(is_ctx, ti == nctx_t)
    no_dn = jnp.logical_or(is_ctx, ti == tpb - 1)
    x = cur_ref[...].astype(F32)
    up = jnp.where(no_up, 0.0, up_ref[...].astype(F32))
    dn = jnp.where(no_dn, 0.0, dn_ref[...].astype(F32))
    ext = jnp.concatenate([up, x, dn], axis=0)
    nr = ext.shape[0]
    ext_m = pltpu.roll(ext, 1, axis=0)
    ext_p = pltpu.roll(ext, nr - 1, axis=0)
    tpos = lax.broadcasted_iota(jnp.int32, (ROW_TILE, 1), 0)
    col = tpos % GRID_W
    left_ok = jnp.where(is_ctx, (tpos > 0).astype(F32), (col > 0).astype(F32))
    right_ok = jnp.where(is_ctx, (tpos < ROW_TILE - 1).astype(F32), (col < GRID_W - 1).astype(F32))
    vert = jnp.where(is_ctx, 0.0, 1.0)
    w = w_ref[...]
    acc = None
    for dr in (-1, 0, 1):
        base = GRID_W * (1 + dr)
        r3 = 3 * (dr + 1)
        term = (ext[base:base + ROW_TILE] * w[r3 + 1:r3 + 2]
                + ext_m[base:base + ROW_TILE] * w[r3:r3 + 1] * left_ok
                + ext_p[base:base + ROW_TILE] * w[r3 + 2:r3 + 3] * right_ok)
        if dr != 0:
            term = term * vert
        acc = term if acc is None else acc + term
    o_ref[...] = (acc * jax.nn.sigmoid(acc) * sc_ref[...]).astype(o_ref.dtype)


def conv_silu(z, conv_w, scale, width, geom):
    n = z.shape[0]
    tpb, nctx_t = geom
    assert nctx_t == 1 and ROW_TILE % GRID_W == 0
    hb = ROW_TILE // GRID_W
    last = n // GRID_W - 1
    return pl.pallas_call(
        functools.partial(_conv_kernel, tpb=tpb, nctx_t=nctx_t),
        grid=(n // ROW_TILE, width // CONV_COLS),
        in_specs=[pl.BlockSpec((ROW_TILE, CONV_COLS), lambda i, c: (i, c)),
                  pl.BlockSpec((GRID_W, CONV_COLS), lambda i, c: (jnp.maximum(i * hb - 1, 0), c)),
                  pl.BlockSpec((GRID_W, CONV_COLS), lambda i, c: (jnp.minimum((i + 1) * hb, last), c)),
                  pl.BlockSpec((9, CONV_COLS), lambda i, c: (0, c)),
                  pl.BlockSpec((1, CONV_COLS), lambda i, c: (0, c))],
        out_specs=pl.BlockSpec((ROW_TILE, CONV_COLS), lambda i, c: (i, c)),
        out_shape=jax.ShapeDtypeStruct((n, width), BF16),
        compiler_params=pltpu.CompilerParams(
            dimension_semantics=("arbitrary", "arbitrary"), vmem_limit_bytes=VMEM_LIMIT),
        name="conv_silu",
    )(z, z, z, conv_w.reshape(9, width).astype(F32), scale.reshape(1, width).astype(F32))


def _dot_exact01(a, b, lhs01=False):
    x = (b if lhs01 else a).astype(F32)
    out = None
    for _ in range(3):
        t = x.astype(BF16)
        x = x - t.astype(F32)
        p = lax.dot_general(a, t, NN, preferred_element_type=F32) if lhs01 else \
            lax.dot_general(t, b, NN, preferred_element_type=F32)
        out = p if out is None else out + p
    return out


def _cummax_rows(x, reverse):
    n = x.shape[0]
    row = lax.broadcasted_iota(jnp.int32, x.shape, 0)
    s = 1
    while s < n:
        if reverse:
            x = jnp.maximum(x, jnp.where(row < n - s, pltpu.roll(x, n - s, axis=0), -jnp.inf))
        else:
            x = jnp.maximum(x, jnp.where(row >= s, pltpu.roll(x, s, axis=0), -jnp.inf))
        s *= 2
    return x


def _ml_scan_kernel(q_ref, k_ref, v_ref, gc_ref, gr_ref, o_ref, z_scr, m_scr, *, heads):
    L = q_ref.shape[0]
    assert L == LANES
    d = pl.program_id(0)

    @pl.when(pl.program_id(2) == 0)
    def _():
        z_scr[...] = jnp.zeros_like(z_scr)
        m_scr[...] = jnp.zeros_like(m_scr)

    row = lax.broadcasted_iota(jnp.int32, (L, L), 0)
    col = lax.broadcasted_iota(jnp.int32, (L, L), 1)
    ones_blk = jnp.ones((L, LANES), BF16)

    def body(reverse):
        incl = (col >= row) if reverse else (col <= row)
        incl_t = (row >= col) if reverse else (row <= col)
        last = 0 if reverse else L - 1
        hs = range(heads)
        sls = [slice(h * LANES, (h + 1) * LANES) for h in hs]
        qk = [_dot(q_ref[:, sls[h]], k_ref[:, sls[h]], NT) for h in hs]
        qz = [_dot(q_ref[:, sls[h]], z_scr[h]) for h in hs]
        b_cols = _dot_exact01(incl.astype(BF16), gc_ref[0, :, heads:2 * heads], lhs01=True)
        b_rows = _dot_exact01(gr_ref[0, 0, heads:2 * heads, :], incl_t.astype(BF16))
        cols = jnp.concatenate([b_cols, gc_ref[0, :, 0:heads]], axis=1)
        pick = lax.broadcasted_iota(jnp.int32, (2 * heads, 2 * LANES), 0)
        lane2 = lax.broadcasted_iota(jnp.int32, (2 * heads, 2 * LANES), 1)
        stats = []
        for h in hs:
            sel = (pick == jnp.where(lane2 < LANES, h, heads + h)).astype(BF16)
            rep = _dot_exact01(cols, sel)
            b_rep, ig_rep = rep[:, :LANES], rep[:, LANES:]
            ig_row = gr_ref[0, 0, h:h + 1, :]
            b_row = b_rows[h:h + 1, :]
            m_prev = m_scr[h:h + 1, :]
            cmax = _cummax_rows(ig_rep - b_rep, reverse)
            dmat = jnp.where(incl, b_rep - (b_row - ig_row), -jnp.inf)
            inter = b_rep + m_prev
            m_t = jnp.maximum(inter, b_rep + cmax)
            b_last = b_rep[last:last + 1, :]
            m_new = jnp.maximum(b_last + m_prev, b_last + cmax[last:last + 1, :])
            w_k = jnp.exp(b_last - b_rep + ig_rep - m_new)
            w_prev = jnp.exp(b_last + m_prev - m_new)
            stats.append((jnp.exp(dmat - m_t), jnp.exp(inter - m_t), jnp.exp(-m_t), w_k, w_prev, m_new))
        kv = []
        for h in hs:
            w_k = stats[h][3]
            wv = jnp.concatenate([w_k * v_ref[:, sls[h]].astype(F32), w_k], axis=1)
            kv.append(_dot(k_ref[:, sls[h]].astype(F32).T, wv))
        s = [qk[h] * stats[h][0] for h in hs]
        sv = [_dot(s[h], jnp.concatenate([v_ref[:, sls[h]], ones_blk], axis=1)) for h in hs]
        for h in hs:
            _, w_inter, floor, _, w_prev, m_new = stats[h]
            w2 = jnp.concatenate([w_inter, w_inter], axis=1)
            full = sv[h] + w2 * qz[h]
            den = full[:, LANES:]
            o_ref[0, :, sls[h]] = (full[:, :LANES] / jnp.maximum(jnp.abs(den), floor)).astype(o_ref.dtype)
            z_scr[h] = jnp.concatenate([w_prev, w_prev], axis=1) * z_scr[h] + kv[h]
            m_scr[h:h + 1, :] = m_new

    @pl.when(d == 0)
    def _():
        body(False)

    @pl.when(d == 1)
    def _():
        body(True)


def mlstm_scan(qk, z, gc, gr, dm, bsz, t):
    n = qk.shape[0]
    heads = dm // LANES
    L = ML_CHUNK
    nc, nctx = t // L, CTX_LEN // L

    def row(d, b, p):
        return b * nc + _scan_chunk_index(d, p, nctx, nc)

    return pl.pallas_call(
        functools.partial(_ml_scan_kernel, heads=heads),
        grid=(2, bsz, nc),
        in_specs=[pl.BlockSpec((L, dm), lambda d, b, p: (row(d, b, p), 0)),
                  pl.BlockSpec((L, dm), lambda d, b, p: (row(d, b, p), 1)),
                  pl.BlockSpec((L, dm), lambda d, b, p: (row(d, b, p), 2)),
                  pl.BlockSpec((1, L, 2 * heads), lambda d, b, p: (d, row(d, b, p), 0)),
                  pl.BlockSpec((1, 1, 2 * heads, L),
                               lambda d, b, p: (d, b, 0, _scan_chunk_index(d, p, nctx, nc)))],
        out_specs=pl.BlockSpec((1, L, dm), lambda d, b, p: (d, row(d, b, p), 0)),
        out_shape=jax.ShapeDtypeStruct((2, n, dm), BF16),
        scratch_shapes=[pltpu.VMEM((heads, LANES, 2 * LANES), F32), pltpu.VMEM((heads, LANES), F32)],
        compiler_params=pltpu.CompilerParams(
            dimension_semantics=("arbitrary", "arbitrary", "arbitrary"), vmem_limit_bytes=VMEM_LIMIT),
        name="mlstm_scan",
    )(qk, qk, z, gc, gr)


RW_STATE_PASSES = 3


def _rw_scan_kernel(lw0_ref, lw1_ref, kd0_ref, kd1_ref, a0_ref, a1_ref, r0_ref, r1_ref, v0_ref, v1_ref,
                    kk0_ref, kk1_ref, of_ref, ob_ref, h_scr, rdp_scr, o0_scr, m_scr, ha_scr, *, nchunk, npair):
    L = RW_CHUNK
    j = pl.program_id(2)

    @pl.when(j == 0)
    def _():
        for ref in (h_scr, rdp_scr, o0_scr, m_scr, ha_scr):
            ref[...] = jnp.zeros_like(ref)

    qls = [slice(q * LANES, (q + 1) * LANES) for q in range(npair)]
    hs = {(q, d): h_scr[q, d] for q in range(npair) for d in range(2)}

    def recurrence_step(k):
        for q in range(npair):
            for d, o_ref in ((0, of_ref), (1, ob_ref)):
                c = k if d == 0 else nchunk - 1 - k
                o_ref[pl.ds(c * L, L), qls[q]] = (_dot(rdp_scr[q, d, c], hs[q, d], passes=RW_STATE_PASSES)
                                                  + o0_scr[q, d, c]).astype(o_ref.dtype)
                hs[q, d] = _dot(m_scr[q, d, c], hs[q, d], passes=RW_STATE_PASSES) + ha_scr[q, d, c]

    pending = list(range(nchunk))

    half = LANES // 2
    row = lax.broadcasted_iota(jnp.int32, (L, LANES), 0)
    col = lax.broadcasted_iota(jnp.int32, (L, LANES), 1) % half
    eye2 = (row == col).astype(F32)
    lane = lax.broadcasted_iota(jnp.int32, (1, LANES), 1)
    m0 = (lane < half).astype(BF16)
    m1 = (lane >= half).astype(BF16)
    r2 = lax.broadcasted_iota(jnp.int32, (LANES, LANES), 0)
    c2 = lax.broadcasted_iota(jnp.int32, (LANES, LANES), 1)
    same_head = (r2 // half) == (c2 // half)

    def stack(x):
        xb = x.astype(BF16)
        return jnp.concatenate([xb * m0, xb * m1], axis=0)

    chains = [(q, d, c) for c in range(nchunk) for q in range(npair) for d in range(2)]
    st = {}
    for q, d, c in chains:
        reverse = d == 1
        sl = pl.ds(c * L, L)
        lw = (lw0_ref, lw1_ref)[d][sl, qls[q]]
        k = (kd0_ref, kd1_ref)[d][sl, qls[q]].astype(F32)
        kk = (kk0_ref, kk1_ref)[d][sl, qls[q]].astype(F32)
        akk = kk * (a0_ref, a1_ref)[d][sl, qls[q]].astype(F32)
        g = _cumsum_rows(lw, reverse)
        ieg = jnp.exp(-g)
        g_last = g[0:1] if reverse else g[L - 1:L]
        dl = jnp.exp(g_last - g)
        st[q, d, c] = dict(kd=kk * jnp.exp(g - lw), rd=(r0_ref, r1_ref)[d][sl, qls[q]].astype(F32) * jnp.exp(g),
                           ai=akk * ieg, ki=k * ieg, ad=akk * dl, kdd=k * dl, eg_last=jnp.exp(g_last),
                           v=(v0_ref, v1_ref)[d][sl, qls[q]].astype(F32))
    recurrence_step(pending.pop(0))
    for q, d, c in chains:
        s = st[q, d, c]
        reverse = d == 1
        incl = (col >= row) if reverse else (col <= row)
        strict = (col > row) if reverse else (col < row)
        x = jnp.concatenate([s["kd"], s["rd"]], axis=0)
        rhs = jnp.concatenate([stack(s["ai"]), stack(s["ki"])], axis=0)
        sc = _dot(x, rhs, NT)
        s["a_ab"] = jnp.where(strict, sc[:L, :LANES], 0.0)
        s["a_ak"] = jnp.where(strict, sc[:L, LANES:], 0.0)
        s["b_ra"] = jnp.where(incl, sc[L:, :LANES], 0.0)
        s["b_rk"] = jnp.where(incl, sc[L:, LANES:], 0.0)
        s["tinv"] = eye2 - s["a_ab"]
        s["pw"] = s["a_ab"]
    span = 2
    while span < L:
        for key in chains:
            s = st[key]
            s["pw"] = _dot(s["pw"], stack(s["pw"]))
        for key in chains:
            s = st[key]
            s["tinv"] = _dot(s["tinv"], stack(eye2 + s["pw"]))
        if pending:
            recurrence_step(pending.pop(0))
        span *= 2
    while pending:
        recurrence_step(pending.pop(0))
    for (q, d), h in hs.items():
        h_scr[q, d] = h
    for key in chains:
        s = st[key]
        s["w"] = -_dot(s["tinv"], stack(s["a_ak"]))
        s["kdp"] = _dot(s["tinv"], stack(s["kd"]))
    for key in chains:
        s = st[key]
        s["vst"] = stack(s["v"])
        s["u0"] = _dot(s["w"], s["vst"])
    for key in chains:
        s = st[key]
        lhs = jnp.concatenate([s["b_ra"], s["b_rk"]], axis=1)
        rhs = jnp.concatenate([stack(s["u0"]), s["vst"]], axis=0)
        o0_scr[key] = _dot(lhs, rhs)
        rdp_scr[key] = s["rd"] - _dot(s["b_ra"], stack(s["kdp"]))
        diag = jnp.where(r2 == c2, s["eg_last"], 0.0)
        m_scr[key] = jnp.where(same_head, diag - _dot(s["ad"].T, s["kdp"]), 0.0)
        at = jnp.concatenate([s["ad"], s["kdd"]], axis=0).T
        ha_scr[key] = jnp.where(same_head, _dot(at, jnp.concatenate([s["u0"], s["v"]], axis=0)), 0.0)


def rwkv_scan(lw, kda, rvkg, dm, bsz, t):
    n = lw.shape[0]
    pairs = dm // LANES
    L = RW_CHUNK
    nchunk = RW_PRE_CHUNKS
    tb = nchunk * L
    nblk, nctx = t // tb, CTX_LEN // tb

    def block(d, b, j):
        return b * nblk + _scan_chunk_index(d, jnp.minimum(j, nblk - 1), nctx, nblk)

    npair = RW_PAIRS_PER_STEP
    width = npair * LANES
    groups = pairs // npair

    def ispec(d, col):
        return pl.BlockSpec((tb, width), lambda b, p, j: (block(d, b, j), col * groups + p))

    def ospec(d):
        return pl.BlockSpec((tb, width), lambda b, p, j: (block(d, b, jnp.maximum(j - 1, 0)), p))

    return pl.pallas_call(
        functools.partial(_rw_scan_kernel, nchunk=nchunk, npair=npair),
        grid=(bsz, groups, nblk + 1),
        in_specs=[ispec(0, 0), ispec(1, 1), ispec(0, 0), ispec(1, 1), ispec(0, 2), ispec(1, 3),
                  ispec(0, 0), ispec(1, 0), ispec(0, 1), ispec(1, 1), ispec(0, 2), ispec(1, 2)],
        out_specs=[ospec(0), ospec(1)],
        out_shape=[jax.ShapeDtypeStruct((n, dm), BF16), jax.ShapeDtypeStruct((n, dm), BF16)],
        scratch_shapes=[pltpu.VMEM((npair, 2, LANES, LANES), F32), pltpu.VMEM((npair, 2, nchunk, L, LANES), F32),
                        pltpu.VMEM((npair, 2, nchunk, L, LANES), F32),
                        pltpu.VMEM((npair, 2, nchunk, LANES, LANES), F32),
                        pltpu.VMEM((npair, 2, nchunk, LANES, LANES), F32)],
        compiler_params=pltpu.CompilerParams(
            dimension_semantics=("arbitrary", "arbitrary", "arbitrary"), vmem_limit_bytes=VMEM_LIMIT),
        name="rwkv_scan",
    )(lw, lw, kda, kda, kda, kda, rvkg, rvkg, rvkg, rvkg, rvkg, rvkg)


HALO_ROWS = 8


def _group_sum(x, width):
    r = lax.broadcasted_iota(jnp.int32, (LANES, LANES), 0) // width
    c = lax.broadcasted_iota(jnp.int32, (LANES, LANES), 1) // width
    ones = (r == c).astype(BF16)
    hi = x.astype(BF16)
    lo = (x - hi.astype(F32)).astype(BF16)
    parts = []
    for j in range(x.shape[1] // LANES):
        sl = slice(j * LANES, (j + 1) * LANES)
        parts.append(jnp.dot(hi[:, sl], ones, preferred_element_type=F32)
                     + jnp.dot(lo[:, sl], ones, preferred_element_type=F32))
    return jnp.concatenate(parts, axis=1)


def _rw_proj_kernel(s_ref, up_ref, dn_ref, gain_ref, shift_ref, mu_ref, wrkv_ref, w1_ref, w2_ref, w0_ref,
                    a1_ref, a2_ref, a0_ref, g1_ref, g2_ref, kk_ref, ka_ref,
                    lw_ref, kda_ref, rvkg_ref, *, tpb, nctx_t):
    tm, dm = s_ref.shape
    ti = pl.program_id(0) % tpb
    is_ctx = ti < nctx_t
    has_up = jnp.logical_not(jnp.logical_or(is_ctx, ti == nctx_t))
    has_dn = jnp.logical_not(jnp.logical_or(is_ctx, ti == tpb - 1))
    gain = gain_ref[0]
    shift = shift_ref[0]
    u = _norm_mod(s_ref[...], gain, shift)
    u_up = jnp.where(has_up, _norm_mod(up_ref[HALO_ROWS - 1:HALO_ROWS, :], gain, shift), 0.0)
    u_dn = jnp.where(has_dn, _norm_mod(dn_ref[0:1, :], gain, shift), 0.0)
    row = lax.broadcasted_iota(jnp.int32, (tm, 1), 0)
    u_m = jnp.where(row == 0, u_up, pltpu.roll(u, 1, axis=0))
    u_p = jnp.where(row == tm - 1, u_dn, pltpu.roll(u, tm - 1, axis=0))
    du = 0.5 * (u_m + u_p) - u
    mu = mu_ref[...]

    def mix(i):
        return (u + du * mu[i:i + 1]).astype(BF16)

    def dot(a, b):
        return jnp.dot(a.astype(BF16), b, preferred_element_type=F32)

    r = dot(mix(0), wrkv_ref[0])
    k = dot(mix(1), wrkv_ref[1])
    v = dot(mix(2), wrkv_ref[2])
    w_pre = dot(jnp.tanh(dot(mix(3), w1_ref[...])), w2_ref[...]) + w0_ref[...]
    lw_ref[...] = -jnp.exp(-_softplus(-w_pre) - 0.5)
    a = jax.nn.sigmoid(dot(dot(mix(4), a1_ref[...]), a2_ref[...]) + a0_ref[...])
    g = dot(jax.nn.sigmoid(dot(mix(5), g1_ref[...])), g2_ref[...])
    kk = k * kk_ref[...]
    kk = kk * lax.rsqrt(jnp.maximum(_group_sum(kk * kk, RW_HEAD_DIM), 1e-24))
    ka = ka_ref[...]
    for d in range(2):
        kda_ref[:, d * dm:(d + 1) * dm] = (k * (1.0 + (a[:, d * dm:(d + 1) * dm] - 1.0) * ka)).astype(kda_ref.dtype)
    kda_ref[:, 2 * dm:] = a.astype(kda_ref.dtype)
    for j, val in enumerate((r, v, kk, g)):
        rvkg_ref[:, j * dm:(j + 1) * dm] = val.astype(rvkg_ref.dtype)


def rwkv_proj(s, gain, shift, geom, mu, w_rkv, w0, w1, w2, a0, a1, a2, g1, g2, k_k, k_a):
    n, dm = s.shape
    tpb, nctx_t = geom
    assert nctx_t == 1
    seg = _seg_map(tpb, nctx_t)
    hb = ROW_TILE // HALO_ROWS
    last = n // HALO_ROWS - 1
    lora = w1.shape[2]

    def blockdiag(w):
        z = jnp.zeros_like(w[0])
        return jnp.concatenate([jnp.concatenate([w[0], z], axis=1), jnp.concatenate([z, w[1]], axis=1)], axis=0)

    consts = [jnp.pad(mu, ((0, HALO_ROWS - mu.shape[0]), (0, 0))), w_rkv.astype(BF16),
              jnp.concatenate([w1[0], w1[1]], axis=1).astype(BF16), blockdiag(w2).astype(BF16),
              jnp.concatenate([w0[0], w0[1]])[None],
              jnp.concatenate([a1[0], a1[1]], axis=1).astype(BF16), blockdiag(a2).astype(BF16),
              jnp.concatenate([a0[0], a0[1]])[None],
              g1.astype(BF16), g2.astype(BF16), k_k[None], k_a[None]]

    def const_spec(x):
        nd = x.ndim
        return pl.BlockSpec(x.shape, lambda i: (0,) * nd)

    return pl.pallas_call(
        functools.partial(_rw_proj_kernel, tpb=tpb, nctx_t=nctx_t),
        grid=(n // ROW_TILE,),
        in_specs=[pl.BlockSpec((ROW_TILE, dm), lambda i: (i, 0)),
                  pl.BlockSpec((HALO_ROWS, dm), lambda i: (jnp.maximum(i * hb - 1, 0), 0)),
                  pl.BlockSpec((HALO_ROWS, dm), lambda i: (jnp.minimum((i + 1) * hb, last), 0)),
                  pl.BlockSpec((1, 1, dm), lambda i: (seg(i), 0, 0)),
                  pl.BlockSpec((1, 1, dm), lambda i: (seg(i), 0, 0))] + [const_spec(x) for x in consts],
        out_specs=[pl.BlockSpec((ROW_TILE, 2 * dm), lambda i: (i, 0)),
                   pl.BlockSpec((ROW_TILE, 4 * dm), lambda i: (i, 0)),
                   pl.BlockSpec((ROW_TILE, 4 * dm), lambda i: (i, 0))],
        out_shape=[jax.ShapeDtypeStruct((n, 2 * dm), F32), jax.ShapeDtypeStruct((n, 4 * dm), BF16),
                   jax.ShapeDtypeStruct((n, 4 * dm), BF16)],
        compiler_params=pltpu.CompilerParams(dimension_semantics=("arbitrary",), vmem_limit_bytes=VMEM_LIMIT_BIG),
        name="rwkv_proj",
    )(s, s, s, gain, shift, *consts)


def _rw_post_kernel(of_ref, ob_ref, r_ref, v_ref, g_ref, k0_ref, k1_ref, s_ref, lnw_ref, lnb_ref, rk_ref, gm_ref,
                    w_ref, out_ref, *, sub, seg):
    o = of_ref[...].astype(F32) + ob_ref[...].astype(F32)
    inv = 1.0 / RW_HEAD_DIM
    mean = _group_sum(o, RW_HEAD_DIM) * inv
    oc = o - mean
    var = _group_sum(oc * oc, RW_HEAD_DIM) * inv
    xn = oc * lax.rsqrt(var + RW_GN_EPS) * lnw_ref[...] + lnb_ref[...]
    r = r_ref[...].astype(F32)
    ksum = k0_ref[...].astype(F32) + k1_ref[...].astype(F32)
    bonus = _group_sum(r * ksum * rk_ref[...], RW_HEAD_DIM) * v_ref[...].astype(F32)
    y = ((xn + bonus) * g_ref[...].astype(F32)).astype(BF16)
    _gated_residual_store(out_ref, s_ref, gm_ref, jnp.dot(y, w_ref[...], preferred_element_type=F32), sub, seg)


def rwkv_post(o_f, o_b, rvkg, kda, s, ln_w, ln_b, r_k, gm, w_out, geom):
    n, dm = s.shape
    seg = _seg_map(*geom)
    sub = _sub_tiles(n, most=2)
    tm = sub * ROW_TILE

    def col(block):
        return pl.BlockSpec((tm, dm), lambda i: (i, block))

    vec = pl.BlockSpec((1, dm), lambda i: (0, 0))
    return pl.pallas_call(
        functools.partial(_rw_post_kernel, sub=sub, seg=seg),
        grid=(n // tm,),
        in_specs=[col(0), col(0), col(0), col(1), col(3), col(0), col(1),
                  col(0), vec, vec, vec, pl.BlockSpec(gm.shape, lambda i: (0, 0, 0)),
                  pl.BlockSpec((dm, dm), lambda i: (0, 0))],
        out_specs=pl.BlockSpec((tm, dm), lambda i: (i, 0)),
        out_shape=jax.ShapeDtypeStruct((n, dm), F32),
        compiler_params=pltpu.CompilerParams(dimension_semantics=("arbitrary",), vmem_limit_bytes=VMEM_LIMIT),
        name="rwkv_post",
    )(o_f, o_b, rvkg, rvkg, rvkg, kda, kda, s, ln_w[None], ln_b[None], r_k[None], gm, w_out.astype(BF16))


def _hg_scan_kernel(q_ref, v_ref, lf_ref, o_ref, st_scr, *, heads):
    C = q_ref.shape[0]
    d = pl.program_id(0)
    nsub = C // HG_SUB

    @pl.when(pl.program_id(2) == 0)
    def _():
        st_scr[...] = jnp.zeros_like(st_scr)

    def body(reverse):
        last = 0 if reverse else C - 1
        hs = range(heads)
        sls = [slice(h * LANES, (h + 1) * LANES) for h in hs]
        g = [lf_ref[:, sls[h]] for h in hs]
        b = [_cumsum_rows(g[h], reverse) for h in hs]
        k = [-jnp.tanh(0.5 * g[h]) * (jnp.exp(g[h]) + 1.0) for h in hs]
        o_inter = [_dot(q_ref[:, sls[h]].astype(F32) * jnp.exp(b[h]), st_scr[h], NT) for h in hs]
        parts = [[None] * nsub for _ in hs]
        for i in range(nsub):
            r0 = i * HG_SUB
            lo, hi = (r0, C) if reverse else (0, r0 + HG_SUB)
            first = r0 + HG_SUB - 1 if reverse else r0
            row = lax.broadcasted_iota(jnp.int32, (HG_SUB, hi - lo), 0) + r0
            col = lax.broadcasted_iota(jnp.int32, (HG_SUB, hi - lo), 1) + lo
            keep = (col >= row) if reverse else (col <= row)
            att = []
            for h in hs:
                rho = b[h][first:first + 1, :] - g[h][first:first + 1, :]
                qi = q_ref[r0:r0 + HG_SUB, sls[h]].astype(F32) * jnp.exp(b[h][r0:r0 + HG_SUB] - rho)
                ki = k[h][lo:hi] * jnp.exp(jnp.minimum(rho - b[h][lo:hi], HG_EXP_CLAMP))
                att.append(jnp.where(keep, _dot(qi, ki, NT), 0.0))
            for h in hs:
                parts[h][i] = _dot(att[h], v_ref[lo:hi, sls[h]])
        for h in hs:
            o_ref[0, :, sls[h]] = (o_inter[h] + jnp.concatenate(parts[h], axis=0)).astype(o_ref.dtype)
        upd = []
        for h in hs:
            b_last = b[h][last:last + 1, :]
            upd.append((jnp.exp(b_last),
                        _dot(v_ref[:, sls[h]].astype(F32).T, k[h] * jnp.exp(b_last - b[h]))))
        for h in hs:
            st_scr[h] = st_scr[h] * upd[h][0] + upd[h][1]

    @pl.when(d == 0)
    def _():
        body(False)

    @pl.when(d == 1)
    def _():
        body(True)


def hgrn_scan(z, logf, dm, bsz, t):
    n = z.shape[0]
    heads = dm // LANES
    C = HG_CHUNK
    nc, nctx = t // C, CTX_LEN // C

    def row(d, b, p):
        return b * nc + _scan_chunk_index(d, p, nctx, nc)

    return pl.pallas_call(
        functools.partial(_hg_scan_kernel, heads=heads),
        grid=(2, bsz, nc),
        in_specs=[pl.BlockSpec((C, dm), lambda d, b, p: (row(d, b, p), 0)),
                  pl.BlockSpec((C, dm), lambda d, b, p: (row(d, b, p), 1)),
                  pl.BlockSpec((C, dm), lambda d, b, p: (row(d, b, p), d))],
        out_specs=pl.BlockSpec((1, C, dm), lambda d, b, p: (d, row(d, b, p), 0)),
        out_shape=jax.ShapeDtypeStruct((2, n, dm), BF16),
        scratch_shapes=[pltpu.VMEM((heads, LANES, LANES), F32)],
        compiler_params=pltpu.CompilerParams(
            dimension_semantics=("arbitrary", "arbitrary", "arbitrary"), vmem_limit_bytes=VMEM_LIMIT),
        name="hgrn_scan",
    )(z, z, logf)


def _first_argmax(vals):
    best, idx = vals[0], jnp.zeros(vals[0].shape, jnp.int32)
    for i in range(1, len(vals)):
        better = vals[i] > best
        best = jnp.where(better, vals[i], best)
        idx = jnp.where(better, i, idx)
    return best, idx


def _router_kernel(s_ref, gain_ref, shift_ref, wt_ref, b_ref, e_ref, g_ref, *, n_groups, top_k):
    n_experts = wt_ref.shape[0]
    per = n_experts // n_groups
    h = _norm_mod(s_ref[...], gain_ref[0], shift_ref[0])
    aff = jax.nn.sigmoid(_dot(wt_ref[...], h, NT, passes=3))
    sel = aff + b_ref[...]
    a = [aff[e:e + 1, :] for e in range(n_experts)]
    s = [sel[e:e + 1, :] for e in range(n_experts)]
    neg = jnp.full_like(s[0], -jnp.inf)
    scores = []
    for g in range(n_groups):
        grp = s[g * per:(g + 1) * per]
        m1, i1 = _first_argmax(grp)
        m2, _ = _first_argmax([jnp.where(i1 == j, neg, grp[j]) for j in range(per)])
        scores.append(m1 + m2)
    _, best = _first_argmax(scores)

    def in_best(rows):
        out = []
        for j in range(per):
            x = rows[j]
            for g in range(1, n_groups):
                x = jnp.where(best == g, rows[g * per + j], x)
            out.append(x)
        return out

    sb, ab = in_best(s), in_best(a)
    picked, chosen = [], []
    cand = sb
    for _ in range(top_k):
        _, i = _first_argmax(cand)
        c = ab[0]
        for j in range(1, per):
            c = jnp.where(i == j, ab[j], c)
        picked.append(i)
        chosen.append(c)
        cand = [jnp.where(i == j, neg, cand[j]) for j in range(per)]
    total = functools.reduce(jnp.add, chosen)
    for kk_ in range(top_k):
        e_ref[kk_:kk_ + 1, :] = best * per + picked[kk_]
        g_ref[kk_:kk_ + 1, :] = chosen[kk_] / total


def norm_route(s, gain, shift, geom, router_w, router_b):
    n, k = s.shape
    n_experts = router_w.shape[1]
    tm = ROW_TILE
    seg = _seg_map(*geom)
    return pl.pallas_call(
        functools.partial(_router_kernel, n_groups=N_GROUPS, top_k=TOP_K),
        grid=(n // tm,),
        in_specs=[pl.BlockSpec((tm, k), lambda i: (i, 0)),
                  pl.BlockSpec((1, 1, k), lambda i: (seg(i), 0, 0)),
                  pl.BlockSpec((1, 1, k), lambda i: (seg(i), 0, 0)),
                  pl.BlockSpec((n_experts, k), lambda i: (0, 0)),
                  pl.BlockSpec((n_experts, 1), lambda i: (0, 0))],
        out_specs=[pl.BlockSpec((TOP_K, tm), lambda i: (0, i)), pl.BlockSpec((TOP_K, tm), lambda i: (0, i))],
        out_shape=[jax.ShapeDtypeStruct((TOP_K, n), jnp.int32), jax.ShapeDtypeStruct((TOP_K, n), F32)],
        compiler_params=pltpu.CompilerParams(dimension_semantics=("arbitrary",), vmem_limit_bytes=VMEM_LIMIT),
        name="norm_route",
    )(s, gain, shift, router_w.T, router_b.reshape(n_experts, 1).astype(F32))


def _final_norm_kernel(s_ref, g_ref, o_ref):
    x = s_ref[...]
    o_ref[0] = x * lax.rsqrt(jnp.mean(x * x, axis=-1, keepdims=True) + NORM_EPS) * g_ref[...]


def final_norm(s, g, bsz, t, geom):
    dm = s.shape[1]
    tpb, nctx_t = geom
    return pl.pallas_call(
        _final_norm_kernel,
        grid=(bsz, tpb - nctx_t),
        in_specs=[pl.BlockSpec((ROW_TILE, dm), lambda b, i: (b * tpb + nctx_t + i, 0)),
                  pl.BlockSpec((1, dm), lambda b, i: (0, 0))],
        out_specs=pl.BlockSpec((1, ROW_TILE, dm), lambda b, i: (b, i, 0)),
        out_shape=jax.ShapeDtypeStruct((bsz, t - CTX_LEN, dm), F32),
        name="final_norm",
    )(s, g[None])


DMA_UNROLL = 8


def _row_copy_waits(src_row, dst_row, sem, count):
    def body(_, carry):
        pltpu.make_async_copy(src_row, dst_row, sem).wait()
        return carry
    lax.fori_loop(0, count, body, 0)


def _scatter_kernel(dest_ref, meta_ref, s_ref, gain_ref, shift_ref, xb_ref, hbuf, zrow, sems, zsem, *, n_experts):
    i = pl.program_id(0)
    last = pl.num_programs(0) - 1
    slot = i % 2
    per_tile = TOP_K * ROW_TILE

    def wait_tile(sl):
        for _ in range(TOP_K):
            pltpu.make_async_copy(hbuf.at[sl], xb_ref.at[pl.ds(0, ROW_TILE), :], sems.at[sl]).wait()

    @pl.when(i >= 2)
    def _():
        wait_tile(slot)

    hbuf[slot] = _norm_mod(s_ref[...], gain_ref[0], shift_ref[0])

    def issue(r8, carry):
        for u in range(DMA_UNROLL):
            r = r8 * DMA_UNROLL + u
            for k in range(TOP_K):
                d = dest_ref[0, 0, TOP_K * r + k]
                pltpu.make_async_copy(hbuf.at[slot, pl.ds(r, 1), :], xb_ref.at[pl.ds(d, 1), :],
                                      sems.at[slot]).start(priority=(TOP_K * u + k) % 2)
        return carry
    lax.fori_loop(0, ROW_TILE // DMA_UNROLL, issue, 0)

    @pl.when(i == last)
    def _():
        @pl.when(last >= 1)
        def _():
            wait_tile(1 - slot)
        wait_tile(slot)
        zrow[...] = jnp.zeros_like(zrow)
        for e in range(n_experts):
            lo = meta_ref[2, e] + meta_ref[0, e]
            hi = meta_ref[2, e] + meta_ref[1, e]

            def pad_start(q, carry):
                pltpu.make_async_copy(zrow.at[pl.ds(0, 1), :], xb_ref.at[pl.ds(q, 1), :], zsem.at[0]).start()
                return carry
            lax.fori_loop(lo, hi, pad_start, 0)
            _row_copy_waits(zrow.at[pl.ds(0, 1), :], xb_ref.at[pl.ds(0, 1), :], zsem.at[0], hi - lo)
        end = meta_ref[2, n_experts - 1] + meta_ref[1, n_experts - 1]

        def tail_start(q, carry):
            pltpu.make_async_copy(zrow.at[pl.ds(0, 1), :], xb_ref.at[pl.ds(q, 1), :], zsem.at[0]).start()
            return carry
        lax.fori_loop(end, xb_ref.shape[0], tail_start, 0)
        _row_copy_waits(zrow.at[pl.ds(0, 1), :], xb_ref.at[pl.ds(0, 1), :], zsem.at[0], xb_ref.shape[0] - end)


def moe_scatter(s, gain, shift, geom, dest, meta, n_slots):
    n, dm = s.shape
    seg = _seg_map(*geom)
    nt = n // ROW_TILE
    return pl.pallas_call(
        functools.partial(_scatter_kernel, n_experts=meta.shape[1]),
        grid=(nt,),
        in_specs=[pl.BlockSpec((1, 1, TOP_K * ROW_TILE), lambda i: (i, 0, 0), memory_space=pltpu.SMEM),
                  pl.BlockSpec(memory_space=pltpu.SMEM),
                  pl.BlockSpec((ROW_TILE, dm), lambda i: (i, 0)),
                  pl.BlockSpec((1, 1, dm), lambda i: (seg(i), 0, 0)),
                  pl.BlockSpec((1, 1, dm), lambda i: (seg(i), 0, 0))],
        out_specs=pl.BlockSpec(memory_space=pl.ANY),
        out_shape=jax.ShapeDtypeStruct((n_slots, dm), F32),
        scratch_shapes=[pltpu.VMEM((2, ROW_TILE, dm), F32), pltpu.VMEM((8, dm), F32),
                        pltpu.SemaphoreType.DMA((2,)), pltpu.SemaphoreType.DMA((1,))],
        compiler_params=pltpu.CompilerParams(dimension_semantics=("arbitrary",), vmem_limit_bytes=VMEM_LIMIT),
        name="moe_scatter",
    )(dest.reshape(nt, 1, TOP_K * ROW_TILE), meta, s, gain, shift)


def _gather_combine_kernel(dcur_ref, dnxt_ref, g_ref, s_ref, gm_ref, yb_ref, o_ref, ybuf, sems):
    i = pl.program_id(0)
    nsteps = pl.num_programs(0)
    slot = i % 2

    def start_tile(dref, sl):
        def issue(r8, carry):
            for u in range(DMA_UNROLL):
                r = r8 * DMA_UNROLL + u
                for k in range(TOP_K):
                    d = dref[0, 0, TOP_K * r + k]
                    pltpu.make_async_copy(yb_ref.at[pl.ds(d, 1), :], ybuf.at[sl, k, pl.ds(r, 1), :],
                                          sems.at[sl]).start(priority=(TOP_K * u + k) % 2)
            return carry
        lax.fori_loop(0, ROW_TILE // DMA_UNROLL, issue, 0)

    @pl.when(i == 0)
    def _():
        start_tile(dcur_ref, 0)

    @pl.when(i + 1 < nsteps)
    def _():
        start_tile(dnxt_ref, 1 - slot)

    for k in range(TOP_K):
        pltpu.make_async_copy(yb_ref.at[pl.ds(0, ROW_TILE), :], ybuf.at[slot, k], sems.at[slot]).wait()
    g = g_ref[...]
    y = sum(ybuf[slot, k] * g[:, k:k + 1] for k in range(TOP_K))
    o_ref[...] = s_ref[...] + gm_ref[0] * y


def moe_gather_combine(yb, dest, gate, s, gm, geom):
    n, dm = s.shape
    seg = _seg_map(*geom)
    nt = n // ROW_TILE
    d3 = dest.reshape(nt, 1, TOP_K * ROW_TILE)
    row = pl.BlockSpec((ROW_TILE, dm), lambda i: (i, 0))
    return pl.pallas_call(
        _gather_combine_kernel,
        grid=(nt,),
        in_specs=[pl.BlockSpec((1, 1, TOP_K * ROW_TILE), lambda i: (i, 0, 0), memory_space=pltpu.SMEM),
                  pl.BlockSpec((1, 1, TOP_K * ROW_TILE), lambda i: (jnp.minimum(i + 1, nt - 1), 0, 0),
                               memory_space=pltpu.SMEM),
                  pl.BlockSpec((ROW_TILE, TOP_K), lambda i: (i, 0)), row,
                  pl.BlockSpec((1, 1, dm), lambda i: (seg(i), 0, 0)),
                  pl.BlockSpec(memory_space=pl.ANY)],
        out_specs=row,
        out_shape=jax.ShapeDtypeStruct((n, dm), F32),
        scratch_shapes=[pltpu.VMEM((2, TOP_K, ROW_TILE, dm), F32), pltpu.SemaphoreType.DMA((2,))],
        compiler_params=pltpu.CompilerParams(dimension_semantics=("arbitrary",), vmem_limit_bytes=VMEM_LIMIT),
        name="moe_gather_combine",
    )(d3, d3, gate, s, gm, yb)


def _rank_kernel(e_ref, rank_ref, cnt_ref, carry_scr, *, n_experts):
    @pl.when(pl.program_id(0) == 0)
    def _():
        carry_scr[...] = jnp.zeros_like(carry_scr)

    bl = e_ref.shape[2]
    e_row = e_ref[0]
    sub = lax.broadcasted_iota(jnp.int32, (n_experts, bl), 0)
    onehot = (sub == e_row).astype(F32)
    ri = lax.broadcasted_iota(jnp.int32, (bl, bl), 0)
    ci = lax.broadcasted_iota(jnp.int32, (bl, bl), 1)
    earlier = (ri < ci).astype(BF16)
    cum = jnp.dot(onehot.astype(BF16), earlier, preferred_element_type=F32)
    carry = carry_scr[...]
    rank_ref[0] = jnp.sum(onehot * (cum + carry[:, :1]), axis=0, keepdims=True).astype(jnp.int32)
    carry = carry + jnp.sum(onehot, axis=1, keepdims=True)
    carry_scr[...] = carry
    cnt_ref[...] = carry.astype(jnp.int32)


def assignment_ranks(flat_e, n_experts):
    n_assign = flat_e.shape[0]
    bl = _pick_tile(n_assign, (512, 256, 128))
    nblk = n_assign // bl
    rank, cnt = pl.pallas_call(
        functools.partial(_rank_kernel, n_experts=n_experts),
        grid=(nblk,),
        in_specs=[pl.BlockSpec((1, 1, bl), lambda i: (i, 0, 0))],
        out_specs=[pl.BlockSpec((1, 1, bl), lambda i: (i, 0, 0)),
                   pl.BlockSpec((n_experts, LANES), lambda i: (0, 0))],
        out_shape=[jax.ShapeDtypeStruct((nblk, 1, bl), jnp.int32),
                   jax.ShapeDtypeStruct((n_experts, LANES), jnp.int32)],
        scratch_shapes=[pltpu.VMEM((n_experts, LANES), F32)],
        compiler_params=pltpu.CompilerParams(dimension_semantics=("arbitrary",)),
        name="assignment_ranks",
    )(flat_e.reshape(nblk, 1, bl))
    return rank.reshape(n_assign), cnt[:, 0]


def _ffn_kernel(be_ref, x_ref, w1_ref, w3_ref, w2_ref, o_ref):
    del be_ref
    x = x_ref[...].astype(BF16)
    a = jnp.dot(x, w1_ref[0], preferred_element_type=F32)
    b = jnp.dot(x, w3_ref[0], preferred_element_type=F32)
    hid = (a * jax.nn.sigmoid(a) * b).astype(BF16)
    o_ref[...] = jnp.dot(hid, w2_ref[0], preferred_element_type=F32).astype(o_ref.dtype)


def expert_ffn(xb, block_expert, w1, w3, w2):
    nrows, dm = xb.shape
    f = w1.shape[2]
    nb = nrows // MOE_BLOCK
    return pl.pallas_call(
        _ffn_kernel,
        grid_spec=pltpu.PrefetchScalarGridSpec(
            num_scalar_prefetch=1,
            grid=(nb,),
            in_specs=[pl.BlockSpec((MOE_BLOCK, dm), lambda i, be: (i, 0)),
                      pl.BlockSpec((1, dm, f), lambda i, be: (be[i], 0, 0)),
                      pl.BlockSpec((1, dm, f), lambda i, be: (be[i], 0, 0)),
                      pl.BlockSpec((1, f, dm), lambda i, be: (be[i], 0, 0))],
            out_specs=pl.BlockSpec((MOE_BLOCK, dm), lambda i, be: (i, 0))),
        out_shape=jax.ShapeDtypeStruct((nrows, dm), F32),
        compiler_params=pltpu.CompilerParams(
            dimension_semantics=("arbitrary",), vmem_limit_bytes=VMEM_LIMIT),
        name="expert_ffn",
    )(block_expert, xb, w1.astype(BF16), w3.astype(BF16), w2.astype(BF16))


def _mlstm_layer(s, gain, shift, gate_mod, geom, bsz, t, w_in, w_gate, b_gate, conv, head_g, w_out):
    n, dm = s.shape
    heads = ML_HEADS
    z = norm_mod_mm(s, gain, shift, w_in, None, (None, None, None, "sigmoid"), geom)
    scale = jnp.concatenate([jnp.ones((dm,), F32), jnp.full((dm,), (dm // heads) ** -0.5, F32)])
    qk = conv_silu(z, conv, scale, 2 * dm, geom)
    ng = 4 * heads
    wg = jnp.pad(jnp.concatenate([w_gate[0], w_gate[1]], axis=1), ((0, 0), (0, LANES - ng)))
    bg = jnp.pad(jnp.concatenate([b_gate[0], b_gate[1]]), (0, LANES - ng))
    gates = norm_mod_mm(s, gain, shift, wg, bg, (None,), geom, out_dtype=F32)[:, :ng]
    gates = gates.reshape(bsz, t, 2, 2 * heads)
    gates = jnp.concatenate([gates[..., :heads], jax.nn.log_sigmoid(gates[..., heads:])], axis=-1)
    gc = jnp.moveaxis(gates, 2, 0).reshape(2, n, 2 * heads)
    gr = jnp.transpose(gates, (2, 0, 3, 1))
    h = mlstm_scan(qk, z, gc, gr, dm, bsz, t)
    return post_mm_residual(h, z, 3, s, head_g, gate_mod, w_out, heads, geom)


def _rwkv7_layer(s, gain, shift, gate_mod, geom, bsz, t, mu, w_rkv, w0, w1, w2, a0, a1, a2, g1, g2,
                 k_k, k_a, r_k, ln_w, ln_b, w_out):
    dm = s.shape[1]
    lw, kda, rvkg = rwkv_proj(s, gain, shift, geom, mu, w_rkv, w0, w1, w2, a0, a1, a2, g1, g2, k_k, k_a)
    o_f, o_b = rwkv_scan(lw, kda, rvkg, dm, bsz, t)
    return rwkv_post(o_f, o_b, rvkg, kda, s, ln_w, ln_b, r_k, gate_mod, w_out, geom)


def _hgrn2_layer(s, gain, shift, gate_mod, geom, bsz, t, layer_idx, w_in, w_f, b_f, lb_logits, head_g, w_out):
    dm = s.shape[1]
    z = norm_mod_mm(s, gain, shift, w_in, None, ("silu", None, "silu"), geom)
    p = jax.nn.softmax(lb_logits, axis=0)
    lb = jnp.cumsum(p, axis=0)[layer_idx] - p[0]
    aux = jnp.tile(jnp.stack([jnp.log(lb), jnp.log1p(-lb)]), (1, 2))
    log_f = norm_mod_mm(s, gain, shift, jnp.concatenate([w_f[0], w_f[1]], axis=1),
                        jnp.concatenate([b_f[0], b_f[1]]), ("logf", "logf"), geom, aux=aux, out_dtype=F32)
    o = hgrn_scan(z, log_f, dm, bsz, t)
    return post_mm_residual(o, z, 2, s, head_g, gate_mod, w_out, dm // HG_EXPAND, geom)


def _moe_layer(s, gain, shift, gate_mod, geom, router_w, router_b, w1, w3, w2):
    n_tok, d = s.shape
    n_experts = w1.shape[0]
    n_assign = n_tok * TOP_K
    e, g = norm_route(s, gain, shift, geom, router_w, router_b)
    flat_e = e.T.reshape(n_assign)
    rank, counts = assignment_ranks(flat_e, n_experts)
    padded = (counts + MOE_BLOCK - 1) // MOE_BLOCK * MOE_BLOCK
    end_pad = jnp.cumsum(padded)
    start_pad = end_pad - padded
    onehot = flat_e[:, None] == jnp.arange(n_experts, dtype=jnp.int32)[None, :]
    dest = jnp.sum(jnp.where(onehot, start_pad[None, :], 0), axis=1) + rank
    n_blocks = -(-n_assign // MOE_BLOCK) + n_experts
    block_start = jnp.arange(n_blocks, dtype=jnp.int32) * MOE_BLOCK
    block_expert = jnp.minimum(jnp.sum(end_pad[None, :] <= block_start[:, None], axis=1), n_experts - 1)
    meta = jnp.stack([counts, padded, start_pad]).astype(jnp.int32)
    xb = moe_scatter(s, gain, shift, geom, dest.astype(jnp.int32), meta, n_blocks * MOE_BLOCK)
    yb = expert_ffn(xb, block_expert.astype(jnp.int32), w1, w3, w2)
    return moe_gather_combine(yb, dest.astype(jnp.int32), g.T, s, gate_mod, geom)


def kernel(x, c, ctx, c_ctx, ada_w, ada_b, norm_mix, norm_ffn, norm_out, ml_w_in, ml_w_gate, ml_b_gate, ml_conv, ml_head_g, ml_w_out, rw_mu, rw_w_rkv, rw_w0, rw_w1, rw_w2, rw_a0, rw_a1, rw_a2, rw_g1, rw_g2, rw_k_k, rw_k_a, rw_r_k, rw_ln_w, rw_ln_b, rw_w_out, hg_w_in, hg_w_f, hg_b_f, hg_lb_logits, hg_head_g, hg_w_out, router_w, router_b, ex_w1, ex_w3, ex_w2):
    depth = ada_w.shape[0]
    bsz = x.shape[0]
    cond = jax.nn.silu(jnp.concatenate([c, c_ctx[None]], axis=0))
    cond = jnp.pad(cond, ((0, -(bsz + 1) % 8), (0, 0)))
    dm = x.shape[2]
    t = CTX_LEN + x.shape[1]
    n = bsz * t
    geom = (t // ROW_TILE, CTX_LEN // ROW_TILE)
    s = jnp.concatenate([ctx, x], axis=1).reshape(n, dm)
    for i in range(depth):
        mod = mm(cond, ada_w[i], bias=ada_b[i])
        mod_x = jnp.split(mod[:bsz, None, :], 6, axis=-1)
        mod_c = jnp.split(mod[bsz], 6, axis=-1)

        def table(idx):
            return jnp.stack([jnp.broadcast_to(mod_c[idx], (bsz, dm)), mod_x[idx][:, 0]], axis=1).reshape(2 * bsz, 1, dm)

        kind, j = i % N_MIXERS, i // N_MIXERS
        if kind == 2:
            s = _hgrn2_layer(s, norm_mix[i] * (1 + table(1)), table(0), table(2), geom, bsz, t, i,
                             hg_w_in[j], hg_w_f[j], hg_b_f[j], hg_lb_logits, hg_head_g[j], hg_w_out[j])
        elif kind == 0:
            s = _mlstm_layer(s, norm_mix[i] * (1 + table(1)), table(0), table(2), geom, bsz, t,
                             ml_w_in[j], ml_w_gate[j], ml_b_gate[j], ml_conv[j], ml_head_g[j], ml_w_out[j])
        else:
            s = _rwkv7_layer(s, norm_mix[i] * (1 + table(1)), table(0), table(2), geom, bsz, t,
                             rw_mu[j], rw_w_rkv[j], rw_w0[j], rw_w1[j], rw_w2[j], rw_a0[j],
                             rw_a1[j], rw_a2[j], rw_g1[j], rw_g2[j], rw_k_k[j], rw_k_a[j],
                             rw_r_k[j], rw_ln_w[j], rw_ln_b[j], rw_w_out[j])
        s = _moe_layer(s, norm_ffn[i] * (1 + table(4)), table(3), table(5), geom, router_w, router_b,
                       ex_w1[i], ex_w3[i], ex_w2[i])
    return final_norm(s, norm_out, bsz, t, geom)
```

```python
import functools

import jax
import jax.numpy as jnp
from jax import lax
from jax.experimental import pallas as pl
from jax.experimental.pallas import tpu as pltpu

F32 = jnp.float32
BF16 = jnp.bfloat16

GRID_W = 64
CTX_LEN = 256
N_MIXERS = 3
NORM_EPS = 1e-6
ML_HEADS = 8
RW_HEAD_DIM = 64
RW_GN_EPS = 64e-5
HG_EXPAND = 128
N_GROUPS = 4
TOP_K = 2
MOE_BLOCK = 512

LANES = 128
ML_CHUNK = 128
RW_CHUNK = 64
RW_PRE_CHUNKS = 4
RW_PAIRS_PER_STEP = 2
HG_CHUNK = 128
HG_SUB = 16
HG_EXP_CLAMP = 80.0
VMEM_LIMIT = 48 * 1024 * 1024
VMEM_LIMIT_BIG = 56 * 1024 * 1024

NT = (((1,), (1,)), ((), ()))
NN = (((1,), (0,)), ((), ()))


def _dot(a, b, dims=NN, passes=1):
    a_hi = a.astype(BF16)
    b_hi = b.astype(BF16)
    out = lax.dot_general(a_hi, b_hi, dims, preferred_element_type=F32)
    if passes == 3:
        a_lo = (a - a_hi.astype(F32)).astype(BF16)
        b_lo = (b - b_hi.astype(F32)).astype(BF16)
        out = out + lax.dot_general(a_hi, b_lo, dims, preferred_element_type=F32)
        out = out + lax.dot_general(a_lo, b_hi, dims, preferred_element_type=F32)
    return out


def _dot_exact01(a, b, lhs01=False):
    x = (b if lhs01 else a).astype(F32)
    out = None
    for _ in range(3):
        t = x.astype(BF16)
        x = x - t.astype(F32)
        p = lax.dot_general(a, t, NN, preferred_element_type=F32) if lhs01 else \
            lax.dot_general(t, b, NN, preferred_element_type=F32)
        out = p if out is None else out + p
    return out


def _cumsum_rows(x, reverse):
    n = x.shape[0]
    row = lax.broadcasted_iota(jnp.int32, x.shape, 0)
    s = 1
    while s < n:
        if reverse:
            x = x + jnp.where(row < n - s, pltpu.roll(x, n - s, axis=0), 0.0)
        else:
            x = x + jnp.where(row >= s, pltpu.roll(x, s, axis=0), 0.0)
        s *= 2
    return x


def _pick_tile(n, candidates):
    for c in candidates:
        if n % c == 0:
            return c
    raise ValueError(f"no tile for {n}")


def _scan_chunk_index(d, p, nctx, nc):
    rev = jnp.where(p < nctx, nctx - 1 - p, nc - 1 - (p - nctx))
    return jnp.where(d == 0, p, rev)


_ACTS = {
    None: lambda y: y,
    "sigmoid": jax.nn.sigmoid,
    "silu": lambda y: y * jax.nn.sigmoid(y),
    "tanh": jnp.tanh,
}


def _mm_kernel(x_ref, w_ref, b_ref, o_ref, *, act, precise):
    if precise:
        y = _dot(x_ref[...], w_ref[...], passes=3)
    else:
        y = jnp.dot(x_ref[...].astype(BF16), w_ref[...], preferred_element_type=F32)
    o_ref[...] = _ACTS[act](y + b_ref[...]).astype(o_ref.dtype)


def mm(x, w, bias=None, act=None, out_dtype=F32, precise=False):
    n, k = x.shape
    m = w.shape[1]
    tm = _pick_tile(n, (512, 256, 128, 64, 32, 16, 8))
    tn = m if m <= 1024 else _pick_tile(m, (1024, 512, 256, 128))
    if not precise:
        w = w.astype(BF16)
    if bias is None:
        bias = jnp.zeros((m,), F32)
    return pl.pallas_call(
        functools.partial(_mm_kernel, act=act, precise=precise),
        grid=(n // tm, m // tn),
        in_specs=[pl.BlockSpec((tm, k), lambda i, j: (i, 0)),
                  pl.BlockSpec((k, tn), lambda i, j: (0, j)),
                  pl.BlockSpec((1, tn), lambda i, j: (0, j))],
        out_specs=pl.BlockSpec((tm, tn), lambda i, j: (i, j)),
        out_shape=jax.ShapeDtypeStruct((n, m), out_dtype),
        compiler_params=pltpu.CompilerParams(vmem_limit_bytes=VMEM_LIMIT),
        name="mm",
    )(x, w, bias.reshape(1, m).astype(F32))


ROW_TILE = 256


def _log1p_exp_neg_abs(x):
    return jnp.log(1.0 + jnp.exp(-jnp.abs(x)))


def _log_sigmoid(y):
    return jnp.minimum(y, 0.0) - _log1p_exp_neg_abs(y)


def _softplus(x):
    return jnp.maximum(x, 0.0) + _log1p_exp_neg_abs(x)


def _logaddexp(a, b):
    return jnp.maximum(a, b) + _log1p_exp_neg_abs(a - b)


def _norm_mod(x, gain, shift):
    return x * lax.rsqrt(jnp.mean(x * x, axis=-1, keepdims=True) + NORM_EPS) * gain + shift


def _seg_map(tpb, nctx_t):
    def seg(i):
        return (i // tpb) * 2 + jnp.where(i % tpb < nctx_t, 0, 1)
    return seg


_EPILOGUES = {
    None: lambda y, aux: y,
    "sigmoid": lambda y, aux: jax.nn.sigmoid(y),
    "silu": lambda y, aux: y * jax.nn.sigmoid(y),
    "logf": lambda y, aux: _logaddexp(aux[0:1, :], aux[1:2, :] + _log_sigmoid(y)),
}


def _sub_tiles(n, most=4):
    return _pick_tile(n // ROW_TILE, tuple(range(most, 0, -1)))


def _nmm_kernel(s_ref, gain_ref, shift_ref, w_ref, b_ref, aux_ref, o_ref, h_scr, *, acts, sub, seg):
    j = pl.program_id(1)

    @pl.when(j == 0)
    def _():
        for k in range(sub):
            rows = pl.ds(k * ROW_TILE, ROW_TILE)
            sk = seg(pl.program_id(0) * sub + k)
            h_scr[rows, :] = _norm_mod(s_ref[rows, :], gain_ref[sk], shift_ref[sk]).astype(BF16)

    y = jnp.dot(h_scr[...], w_ref[...], preferred_element_type=F32) + b_ref[...]
    for act in sorted(set(acts), key=str):
        cols = [jj for jj, a in enumerate(acts) if a == act]
        if len(cols) == len(acts):
            o_ref[...] = _EPILOGUES[act](y, aux_ref[...]).astype(o_ref.dtype)
        else:
            @pl.when(functools.reduce(jnp.logical_or, [j == jj for jj in cols]))
            def _(act=act):
                o_ref[...] = _EPILOGUES[act](y, aux_ref[...]).astype(o_ref.dtype)


def norm_mod_mm(s, gain, shift, w, bias, acts, geom, aux=None, out_dtype=None):
    out_dtype = out_dtype or BF16
    n, k = s.shape
    m = w.shape[1]
    tn = m // len(acts)
    tpb, nctx_t = geom
    seg = _seg_map(tpb, nctx_t)
    sub = _sub_tiles(n)
    tm = sub * ROW_TILE
    if bias is None:
        bias = jnp.zeros((m,), F32)
    if aux is None:
        aux = jnp.zeros((2, m), F32)
    return pl.pallas_call(
        functools.partial(_nmm_kernel, acts=tuple(acts), sub=sub, seg=seg),
        grid=(n // tm, m // tn),
        in_specs=[pl.BlockSpec((tm, k), lambda i, j: (i, 0)),
                  pl.BlockSpec(gain.shape, lambda i, j: (0, 0, 0)),
                  pl.BlockSpec(shift.shape, lambda i, j: (0, 0, 0)),
                  pl.BlockSpec((k, tn), lambda i, j: (0, j)),
                  pl.BlockSpec((1, tn), lambda i, j: (0, j)),
                  pl.BlockSpec((2, tn), lambda i, j: (0, j))],
        out_specs=pl.BlockSpec((tm, tn), lambda i, j: (i, j)),
        out_shape=jax.ShapeDtypeStruct((n, m), out_dtype),
        scratch_shapes=[pltpu.VMEM((tm, k), BF16)],
        compiler_params=pltpu.CompilerParams(
            dimension_semantics=("arbitrary", "arbitrary"), vmem_limit_bytes=VMEM_LIMIT),
        name="norm_mod_mm",
    )(s, gain, shift, w.astype(BF16), bias.reshape(1, m).astype(F32), aux.astype(F32))


def _gated_residual_store(o_ref, s_ref, gm_ref, y, sub, seg):
    for k in range(sub):
        rows = pl.ds(k * ROW_TILE, ROW_TILE)
        gm = gm_ref[seg(pl.program_id(0) * sub + k)]
        o_ref[rows, :] = s_ref[rows, :] + gm * y[k * ROW_TILE:(k + 1) * ROW_TILE]


def _post_kernel(h_ref, g_ref, s_ref, hg_ref, gm_ref, w_ref, o_ref, *, heads, sub, seg):
    x = h_ref[0].astype(F32) + h_ref[1].astype(F32)
    hd = x.shape[1] // heads
    parts = []
    for h in range(heads):
        xh = x[:, h * hd:(h + 1) * hd]
        parts.append(xh * lax.rsqrt(jnp.mean(xh * xh, axis=-1, keepdims=True) + NORM_EPS))
    y = (jnp.concatenate(parts, axis=1) * hg_ref[...] * g_ref[...].astype(F32)).astype(BF16)
    _gated_residual_store(o_ref, s_ref, gm_ref, jnp.dot(y, w_ref[...], preferred_element_type=F32), sub, seg)


def post_mm_residual(h2, gate_arr, gate_block, s, head_g, gm, w_out, heads, geom):
    n, dm = s.shape
    seg = _seg_map(*geom)
    sub = _sub_tiles(n, most=2)
    tm = sub * ROW_TILE
    return pl.pallas_call(
        functools.partial(_post_kernel, heads=heads, sub=sub, seg=seg),
        grid=(n // tm,),
        in_specs=[pl.BlockSpec((2, tm, dm), lambda i: (0, i, 0)),
                  pl.BlockSpec((tm, dm), lambda i: (i, gate_block)),
                  pl.BlockSpec((tm, dm), lambda i: (i, 0)),
                  pl.BlockSpec((1, dm), lambda i: (0, 0)),
                  pl.BlockSpec(gm.shape, lambda i: (0, 0, 0)),
                  pl.BlockSpec((dm, dm), lambda i: (0, 0))],
        out_specs=pl.BlockSpec((tm, dm), lambda i: (i, 0)),
        out_shape=jax.ShapeDtypeStruct((n, dm), F32),
        compiler_params=pltpu.CompilerParams(dimension_semantics=("arbitrary",), vmem_limit_bytes=VMEM_LIMIT),
        name="post_mm_residual",
    )(h2, gate_arr, s, head_g.reshape(1, dm), gm, w_out.astype(BF16))


CONV_COLS = 512


def _conv_kernel(cur_ref, up_ref, dn_ref, w_ref, sc_ref, o_ref, *, tpb, nctx_t):
    ti = pl.program_id(0) % tpb
    is_ctx = ti < nctx_t
    no_up = jnp.logical_or(is_ctx, ti == nctx_t)
    no_dn = jnp.logical_or(is_ctx, ti == tpb - 1)
    x = cur_ref[...].astype(F32)
    up = jnp.where(no_up, 0.0, up_ref[...].astype(F32))
    dn = jnp.where(no_dn, 0.0, dn_ref[...].astype(F32))
    ext = jnp.concatenate([up, x, dn], axis=0)
    nr = ext.shape[0]
    ext_m = pltpu.roll(ext, 1, axis=0)
    ext_p = pltpu.roll(ext, nr - 1, axis=0)
    tpos = lax.broadcasted_iota(jnp.int32, (ROW_TILE, 1), 0)
    col = tpos % GRID_W
    left_ok = jnp.where(is_ctx, (tpos > 0).astype(F32), (col > 0).astype(F32))
    right_ok = jnp.where(is_ctx, (tpos < ROW_TILE - 1).astype(F32), (col < GRID_W - 1).astype(F32))
    vert = jnp.where(is_ctx, 0.0, 1.0)
    w = w_ref[...]
    sums = [None, None, None]
    for dr in (-1, 0, 1):
        base = GRID_W * (1 + dr)
        wr = w[3 * (dr + 1):3 * (dr + 2)] * (1.0 if dr == 0 else vert)
        for dc, src in enumerate((ext_m, ext, ext_p)):
            term = src[base:base + ROW_TILE] * wr[dc:dc + 1]
            sums[dc] = term if sums[dc] is None else sums[dc] + term
    acc = sums[1] + left_ok * sums[0] + right_ok * sums[2]
    o_ref[...] = (acc * jax.nn.sigmoid(acc) * sc_ref[...]).astype(o_ref.dtype)


def conv_silu(z, conv_w, scale, width, geom):
    n = z.shape[0]
    tpb, nctx_t = geom
    assert nctx_t == 1 and ROW_TILE % GRID_W == 0
    hb = ROW_TILE // GRID_W
    last = n // GRID_W - 1
    return pl.pallas_call(
        functools.partial(_conv_kernel, tpb=tpb, nctx_t=nctx_t),
        grid=(n // ROW_TILE, width // CONV_COLS),
        in_specs=[pl.BlockSpec((ROW_TILE, CONV_COLS), lambda i, c: (i, c)),
                  pl.BlockSpec((GRID_W, CONV_COLS), lambda i, c: (jnp.maximum(i * hb - 1, 0), c)),
                  pl.BlockSpec((GRID_W, CONV_COLS), lambda i, c: (jnp.minimum((i + 1) * hb, last), c)),
                  pl.BlockSpec((9, CONV_COLS), lambda i, c: (0, c)),
                  pl.BlockSpec((1, CONV_COLS), lambda i, c: (0, c))],
        out_specs=pl.BlockSpec((ROW_TILE, CONV_COLS), lambda i, c: (i, c)),
        out_shape=jax.ShapeDtypeStruct((n, width), BF16),
        compiler_params=pltpu.CompilerParams(
            dimension_semantics=("arbitrary", "arbitrary"), vmem_limit_bytes=VMEM_LIMIT),
        name="conv_silu",
    )(z, z, z, conv_w.reshape(9, width).astype(F32), scale.reshape(1, width).astype(F32))


def _cummax_rows(x, reverse):
    n = x.shape[0]
    row = lax.broadcasted_iota(jnp.int32, x.shape, 0)
    s = 1
    while s < n:
        if reverse:
            x = jnp.maximum(x, jnp.where(row < n - s, pltpu.roll(x, n - s, axis=0), -jnp.inf))
        else:
            x = jnp.maximum(x, jnp.where(row >= s, pltpu.roll(x, s, axis=0), -jnp.inf))
        s *= 2
    return x


def _ml_scan_kernel(q_ref, k_ref, v_ref, gc_ref, gr_ref, o_ref, z_scr, m_scr, *, heads):
    L = q_ref.shape[0]
    assert L == LANES
    d = pl.program_id(0)

    @pl.when(pl.program_id(2) == 0)
    def _():
        z_scr[...] = jnp.zeros_like(z_scr)
        m_scr[...] = jnp.zeros_like(m_scr)

    row = lax.broadcasted_iota(jnp.int32, (L, L), 0)
    col = lax.broadcasted_iota(jnp.int32, (L, L), 1)
    ones_blk = jnp.ones((L, LANES), BF16)

    def body(reverse):
        incl = (col >= row) if reverse else (col <= row)
        incl_t = (row >= col) if reverse else (row <= col)
        last = 0 if reverse else L - 1
        hs = range(heads)
        sls = [slice(h * LANES, (h + 1) * LANES) for h in hs]
        qk = [_dot(q_ref[:, sls[h]], k_ref[:, sls[h]], NT) for h in hs]
        qz = [_dot(q_ref[:, sls[h]], z_scr[h]) for h in hs]
        b_cols = _dot_exact01(incl.astype(BF16), gc_ref[0, :, heads:2 * heads], lhs01=True)
        b_rows = _dot_exact01(gr_ref[0, 0, heads:2 * heads, :], incl_t.astype(BF16))
        cols = jnp.concatenate([b_cols, gc_ref[0, :, 0:heads]], axis=1)
        pick = lax.broadcasted_iota(jnp.int32, (2 * heads, 2 * LANES), 0)
        lane2 = lax.broadcasted_iota(jnp.int32, (2 * heads, 2 * LANES), 1)
        stats = []
        for h in hs:
            sel = (pick == jnp.where(lane2 < LANES, h, heads + h)).astype(BF16)
            rep = _dot_exact01(cols, sel)
            b_rep, ig_rep = rep[:, :LANES], rep[:, LANES:]
            ig_row = gr_ref[0, 0, h:h + 1, :]
            b_row = b_rows[h:h + 1, :]
            m_prev = m_scr[h:h + 1, :]
            cmax = _cummax_rows(ig_rep - b_rep, reverse)
            dmat = jnp.where(incl, b_rep - (b_row - ig_row), -jnp.inf)
            inter = b_rep + m_prev
            m_t = jnp.maximum(inter, b_rep + cmax)
            b_last = b_rep[last:last + 1, :]
            m_new = jnp.maximum(b_last + m_prev, b_last + cmax[last:last + 1, :])
            w_k = jnp.exp(b_last - b_rep + ig_rep - m_new)
            w_prev = jnp.exp(b_last + m_prev - m_new)
            stats.append((jnp.exp(dmat - m_t), jnp.exp(inter - m_t), jnp.exp(-m_t), w_k, w_prev, m_new))
        kv = []
        for h in hs:
            w_k = stats[h][3]
            wv = jnp.concatenate([w_k * v_ref[:, sls[h]].astype(F32), w_k], axis=1)
            kv.append(_dot(k_ref[:, sls[h]].astype(F32).T, wv))
        s = [qk[h] * stats[h][0] for h in hs]
        sv = [_dot(s[h], jnp.concatenate([v_ref[:, sls[h]], ones_blk], axis=1)) for h in hs]
        for h in hs:
            _, w_inter, floor, _, w_prev, m_new = stats[h]
            w2 = jnp.concatenate([w_inter, w_inter], axis=1)
            full = sv[h] + w2 * qz[h]
            den = full[:, LANES:]
            o_ref[0, :, sls[h]] = (full[:, :LANES] / jnp.maximum(jnp.abs(den), floor)).astype(o_ref.dtype)
            z_scr[h] = jnp.concatenate([w_prev, w_prev], axis=1) * z_scr[h] + kv[h]
            m_scr[h:h + 1, :] = m_new

    @pl.when(d == 0)
    def _():
        body(False)

    @pl.when(d == 1)
    def _():
        body(True)


def mlstm_scan(qk, z, gc, gr, dm, bsz, t):
    n = qk.shape[0]
    heads = dm // LANES
    L = ML_CHUNK
    nc, nctx = t // L, CTX_LEN // L

    def row(d, b, p):
        return b * nc + _scan_chunk_index(d, p, nctx, nc)

    return pl.pallas_call(
        functools.partial(_ml_scan_kernel, heads=heads),
        grid=(2, bsz, nc),
        in_specs=[pl.BlockSpec((L, dm), lambda d, b, p: (row(d, b, p), 0)),
                  pl.BlockSpec((L, dm), lambda d, b, p: (row(d, b, p), 1)),
                  pl.BlockSpec((L, dm), lambda d, b, p: (row(d, b, p), 2)),
                  pl.BlockSpec((1, L, 2 * heads), lambda d, b, p: (d, row(d, b, p), 0)),
                  pl.BlockSpec((1, 1, 2 * heads, L),
                               lambda d, b, p: (d, b, 0, _scan_chunk_index(d, p, nctx, nc)))],
        out_specs=pl.BlockSpec((1, L, dm), lambda d, b, p: (d, row(d, b, p), 0)),
        out_shape=jax.ShapeDtypeStruct((2, n, dm), BF16),
        scratch_shapes=[pltpu.VMEM((heads, LANES, 2 * LANES), F32), pltpu.VMEM((heads, LANES), F32)],
        compiler_params=pltpu.CompilerParams(
            dimension_semantics=("arbitrary", "arbitrary", "arbitrary"), vmem_limit_bytes=VMEM_LIMIT),
        name="mlstm_scan",
    )(qk, qk, z, gc, gr)


RW_STATE_PASSES = 3


def _rw_scan_kernel(lw0_ref, lw1_ref, kd0_ref, kd1_ref, a0_ref, a1_ref, r0_ref, r1_ref, v0_ref, v1_ref,
                    kk0_ref, kk1_ref, of_ref, ob_ref, h_scr, rdp_scr, o0_scr, m_scr, ha_scr, *, nchunk, npair):
    L = RW_CHUNK
    j = pl.program_id(2)

    @pl.when(j == 0)
    def _():
        for ref in (h_scr, rdp_scr, o0_scr, m_scr, ha_scr):
            ref[...] = jnp.zeros_like(ref)

    qls = [slice(q * LANES, (q + 1) * LANES) for q in range(npair)]
    hs = {(q, d): h_scr[q, d] for q in range(npair) for d in range(2)}

    def recurrence_step(k):
        for q in range(npair):
            for d, o_ref in ((0, of_ref), (1, ob_ref)):
                c = k if d == 0 else nchunk - 1 - k
                o_ref[pl.ds(c * L, L), qls[q]] = (_dot(rdp_scr[q, d, c], hs[q, d], passes=RW_STATE_PASSES)
                                                  + o0_scr[q, d, c]).astype(o_ref.dtype)
                hs[q, d] = _dot(m_scr[q, d, c], hs[q, d], passes=RW_STATE_PASSES) + ha_scr[q, d, c]

    pending = list(range(nchunk))

    half = LANES // 2
    row = lax.broadcasted_iota(jnp.int32, (L, LANES), 0)
    col = lax.broadcasted_iota(jnp.int32, (L, LANES), 1) % half
    eye2 = (row == col).astype(F32)
    lane = lax.broadcasted_iota(jnp.int32, (1, LANES), 1)
    m0 = (lane < half).astype(BF16)
    m1 = (lane >= half).astype(BF16)
    r2 = lax.broadcasted_iota(jnp.int32, (LANES, LANES), 0)
    c2 = lax.broadcasted_iota(jnp.int32, (LANES, LANES), 1)
    same_head = (r2 // half) == (c2 // half)

    def stack(x):
        xb = x.astype(BF16)
        return jnp.concatenate([xb * m0, xb * m1], axis=0)

    chains = [(q, d, c) for c in range(nchunk) for q in range(npair) for d in range(2)]
    st = {}
    for q, d, c in chains:
        reverse = d == 1
        sl = pl.ds(c * L, L)
        lw = (lw0_ref, lw1_ref)[d][sl, qls[q]]
        k = (kd0_ref, kd1_ref)[d][sl, qls[q]].astype(F32)
        kk = (kk0_ref, kk1_ref)[d][sl, qls[q]].astype(F32)
        akk = kk * (a0_ref, a1_ref)[d][sl, qls[q]].astype(F32)
        g = _cumsum_rows(lw, reverse)
        ieg = jnp.exp(-g)
        g_last = g[0:1] if reverse else g[L - 1:L]
        dl = jnp.exp(g_last - g)
        st[q, d, c] = dict(kd=kk * jnp.exp(g - lw), rd=(r0_ref, r1_ref)[d][sl, qls[q]].astype(F32) * jnp.exp(g),
                           ai=akk * ieg, ki=k * ieg, ad=akk * dl, kdd=k * dl, eg_last=jnp.exp(g_last),
                           v=(v0_ref, v1_ref)[d][sl, qls[q]].astype(F32))
    recurrence_step(pending.pop(0))
    for q, d, c in chains:
        s = st[q, d, c]
        reverse = d == 1
        incl = (col >= row) if reverse else (col <= row)
        strict = (col > row) if reverse else (col < row)
        x = jnp.concatenate([s["kd"], s["rd"]], axis=0)
        rhs = jnp.concatenate([stack(s["ai"]), stack(s["ki"])], axis=0)
        sc = _dot(x, rhs, NT)
        s["a_ab"] = jnp.where(strict, sc[:L, :LANES], 0.0)
        s["a_ak"] = jnp.where(strict, sc[:L, LANES:], 0.0)
        s["b_ra"] = jnp.where(incl, sc[L:, :LANES], 0.0)
        s["b_rk"] = jnp.where(incl, sc[L:, LANES:], 0.0)
        s["tinv"] = eye2 - s["a_ab"]
        s["pw"] = s["a_ab"]
    span = 2
    while span < L:
        for key in chains:
            s = st[key]
            s["pw"] = _dot(s["pw"], stack(s["pw"]))
        for key in chains:
            s = st[key]
            s["tinv"] = _dot(s["tinv"], stack(eye2 + s["pw"]))
        if pending:
            recurrence_step(pending.pop(0))
        span *= 2
    while pending:
        recurrence_step(pending.pop(0))
    for (q, d), h in hs.items():
        h_scr[q, d] = h
    for key in chains:
        s = st[key]
        s["w"] = -_dot(s["tinv"], stack(s["a_ak"]))
        s["kdp"] = _dot(s["tinv"], stack(s["kd"]))
    for key in chains:
        s = st[key]
        s["vst"] = stack(s["v"])
        s["u0"] = _dot(s["w"], s["vst"])
    for key in chains:
        s = st[key]
        lhs = jnp.concatenate([s["b_ra"], s["b_rk"]], axis=1)
        rhs = jnp.concatenate([stack(s["u0"]), s["vst"]], axis=0)
        o0_scr[key] = _dot(lhs, rhs)
        rdp_scr[key] = s["rd"] - _dot(s["b_ra"], stack(s["kdp"]))
        diag = jnp.where(r2 == c2, s["eg_last"], 0.0)
        m_scr[key] = jnp.where(same_head, diag - _dot(s["ad"].T, s["kdp"]), 0.0)
        at = jnp.concatenate([s["ad"], s["kdd"]], axis=0).T
        ha_scr[key] = jnp.where(same_head, _dot(at, jnp.concatenate([s["u0"], s["v"]], axis=0)), 0.0)


def rwkv_scan(lw, kda, rvkg, dm, bsz, t):
    n = lw.shape[0]
    pairs = dm // LANES
    L = RW_CHUNK
    nchunk = RW_PRE_CHUNKS
    tb = nchunk * L
    nblk, nctx = t // tb, CTX_LEN // tb

    def block(d, b, j):
        return b * nblk + _scan_chunk_index(d, jnp.minimum(j, nblk - 1), nctx, nblk)

    npair = RW_PAIRS_PER_STEP
    width = npair * LANES
    groups = pairs // npair

    def ispec(d, col):
        return pl.BlockSpec((tb, width), lambda b, p, j: (block(d, b, j), col * groups + p))

    def ospec(d):
        return pl.BlockSpec((tb, width), lambda b, p, j: (block(d, b, jnp.maximum(j - 1, 0)), p))

    return pl.pallas_call(
        functools.partial(_rw_scan_kernel, nchunk=nchunk, npair=npair),
        grid=(bsz, groups, nblk + 1),
        in_specs=[ispec(0, 0), ispec(1, 1), ispec(0, 0), ispec(1, 1), ispec(0, 2), ispec(1, 3),
                  ispec(0, 0), ispec(1, 0), ispec(0, 1), ispec(1, 1), ispec(0, 2), ispec(1, 2)],
        out_specs=[ospec(0), ospec(1)],
        out_shape=[jax.ShapeDtypeStruct((n, dm), BF16), jax.ShapeDtypeStruct((n, dm), BF16)],
        scratch_shapes=[pltpu.VMEM((npair, 2, LANES, LANES), F32), pltpu.VMEM((npair, 2, nchunk, L, LANES), F32),
                        pltpu.VMEM((npair, 2, nchunk, L, LANES), F32),
                        pltpu.VMEM((npair, 2, nchunk, LANES, LANES), F32),
                        pltpu.VMEM((npair, 2, nchunk, LANES, LANES), F32)],
        compiler_params=pltpu.CompilerParams(
            dimension_semantics=("arbitrary", "arbitrary", "arbitrary"), vmem_limit_bytes=VMEM_LIMIT),
        name="rwkv_scan",
    )(lw, lw, kda, kda, kda, kda, rvkg, rvkg, rvkg, rvkg, rvkg, rvkg)


HALO_ROWS = 8


def _group_sum(x, width):
    r = lax.broadcasted_iota(jnp.int32, (LANES, LANES), 0) // width
    c = lax.broadcasted_iota(jnp.int32, (LANES, LANES), 1) // width
    ones = (r == c).astype(BF16)
    hi = x.astype(BF16)
    lo = (x - hi.astype(F32)).astype(BF16)
    parts = []
    for j in range(x.shape[1] // LANES):
        sl = slice(j * LANES, (j + 1) * LANES)
        parts.append(jnp.dot(hi[:, sl], ones, preferred_element_type=F32)
                     + jnp.dot(lo[:, sl], ones, preferred_element_type=F32))
    return jnp.concatenate(parts, axis=1)


def _rw_proj_kernel(s_ref, up_ref, dn_ref, gain_ref, shift_ref, mu_ref, wrkv_ref, w1_ref, w2_ref, w0_ref,
                    a1_ref, a2_ref, a0_ref, g1_ref, g2_ref, kk_ref, ka_ref,
                    lw_ref, kda_ref, rvkg_ref, *, tpb, nctx_t):
    tm, dm = s_ref.shape
    ti = pl.program_id(0) % tpb
    is_ctx = ti < nctx_t
    has_up = jnp.logical_not(jnp.logical_or(is_ctx, ti == nctx_t))
    has_dn = jnp.logical_not(jnp.logical_or(is_ctx, ti == tpb - 1))
    gain = gain_ref[0]
    shift = shift_ref[0]
    u = _norm_mod(s_ref[...], gain, shift)
    u_up = jnp.where(has_up, _norm_mod(up_ref[HALO_ROWS - 1:HALO_ROWS, :], gain, shift), 0.0)
    u_dn = jnp.where(has_dn, _norm_mod(dn_ref[0:1, :], gain, shift), 0.0)
    row = lax.broadcasted_iota(jnp.int32, (tm, 1), 0)
    u_m = jnp.where(row == 0, u_up, pltpu.roll(u, 1, axis=0))
    u_p = jnp.where(row == tm - 1, u_dn, pltpu.roll(u, tm - 1, axis=0))
    du = 0.5 * (u_m + u_p) - u
    mu = mu_ref[...]

    def mix(i):
        return (u + du * mu[i:i + 1]).astype(BF16)

    def dot(a, b):
        return jnp.dot(a.astype(BF16), b, preferred_element_type=F32)

    r = dot(mix(0), wrkv_ref[0])
    k = dot(mix(1), wrkv_ref[1])
    v = dot(mix(2), wrkv_ref[2])
    w_pre = dot(jnp.tanh(dot(mix(3), w1_ref[...])), w2_ref[...]) + w0_ref[...]
    lw_ref[...] = -jnp.exp(-_softplus(-w_pre) - 0.5)
    a = jax.nn.sigmoid(dot(dot(mix(4), a1_ref[...]), a2_ref[...]) + a0_ref[...])
    g = dot(jax.nn.sigmoid(dot(mix(5), g1_ref[...])), g2_ref[...])
    kk = k * kk_ref[...]
    kk = kk * lax.rsqrt(jnp.maximum(_group_sum(kk * kk, RW_HEAD_DIM), 1e-24))
    ka = ka_ref[...]
    for d in range(2):
        kda_ref[:, d * dm:(d + 1) * dm] = (k * (1.0 + (a[:, d * dm:(d + 1) * dm] - 1.0) * ka)).astype(kda_ref.dtype)
    kda_ref[:, 2 * dm:] = a.astype(kda_ref.dtype)
    for j, val in enumerate((r, v, kk, g)):
        rvkg_ref[:, j * dm:(j + 1) * dm] = val.astype(rvkg_ref.dtype)


def rwkv_proj(s, gain, shift, geom, mu, w_rkv, w0, w1, w2, a0, a1, a2, g1, g2, k_k, k_a):
    n, dm = s.shape
    tpb, nctx_t = geom
    assert nctx_t == 1
    seg = _seg_map(tpb, nctx_t)
    hb = ROW_TILE // HALO_ROWS
    last = n // HALO_ROWS - 1
    lora = w1.shape[2]

    def blockdiag(w):
        z = jnp.zeros_like(w[0])
        return jnp.concatenate([jnp.concatenate([w[0], z], axis=1), jnp.concatenate([z, w[1]], axis=1)], axis=0)

    consts = [jnp.pad(mu, ((0, HALO_ROWS - mu.shape[0]), (0, 0))), w_rkv.astype(BF16),
              jnp.concatenate([w1[0], w1[1]], axis=1).astype(BF16), blockdiag(w2).astype(BF16),
              jnp.concatenate([w0[0], w0[1]])[None],
              jnp.concatenate([a1[0], a1[1]], axis=1).astype(BF16), blockdiag(a2).astype(BF16),
              jnp.concatenate([a0[0], a0[1]])[None],
              g1.astype(BF16), g2.astype(BF16), k_k[None], k_a[None]]

    def const_spec(x):
        nd = x.ndim
        return pl.BlockSpec(x.shape, lambda i: (0,) * nd)

    return pl.pallas_call(
        functools.partial(_rw_proj_kernel, tpb=tpb, nctx_t=nctx_t),
        grid=(n // ROW_TILE,),
        in_specs=[pl.BlockSpec((ROW_TILE, dm), lambda i: (i, 0)),
                  pl.BlockSpec((HALO_ROWS, dm), lambda i: (jnp.maximum(i * hb - 1, 0), 0)),
                  pl.BlockSpec((HALO_ROWS, dm), lambda i: (jnp.minimum((i + 1) * hb, last), 0)),
                  pl.BlockSpec((1, 1, dm), lambda i: (seg(i), 0, 0)),
                  pl.BlockSpec((1, 1, dm), lambda i: (seg(i), 0, 0))] + [const_spec(x) for x in consts],
        out_specs=[pl.BlockSpec((ROW_TILE, 2 * dm), lambda i: (i, 0)),
                   pl.BlockSpec((ROW_TILE, 4 * dm), lambda i: (i, 0)),
                   pl.BlockSpec((ROW_TILE, 4 * dm), lambda i: (i, 0))],
        out_shape=[jax.ShapeDtypeStruct((n, 2 * dm), F32), jax.ShapeDtypeStruct((n, 4 * dm), BF16),
                   jax.ShapeDtypeStruct((n, 4 * dm), BF16)],
        compiler_params=pltpu.CompilerParams(dimension_semantics=("arbitrary",), vmem_limit_bytes=VMEM_LIMIT_BIG),
        name="rwkv_proj",
    )(s, s, s, gain, shift, *consts)


def _rw_post_kernel(of_ref, ob_ref, r_ref, v_ref, g_ref, k0_ref, k1_ref, s_ref, lnw_ref, lnb_ref, rk_ref, gm_ref,
                    w_ref, out_ref, *, sub, seg):
    o = of_ref[...].astype(F32) + ob_ref[...].astype(F32)
    inv = 1.0 / RW_HEAD_DIM
    mean = _group_sum(o, RW_HEAD_DIM) * inv
    oc = o - mean
    var = _group_sum(oc * oc, RW_HEAD_DIM) * inv
    xn = oc * lax.rsqrt(var + RW_GN_EPS) * lnw_ref[...] + lnb_ref[...]
    r = r_ref[...].astype(F32)
    ksum = k0_ref[...].astype(F32) + k1_ref[...].astype(F32)
    bonus = _group_sum(r * ksum * rk_ref[...], RW_HEAD_DIM) * v_ref[...].astype(F32)
    y = ((xn + bonus) * g_ref[...].astype(F32)).astype(BF16)
    _gated_residual_store(out_ref, s_ref, gm_ref, jnp.dot(y, w_ref[...], preferred_element_type=F32), sub, seg)


def rwkv_post(o_f, o_b, rvkg, kda, s, ln_w, ln_b, r_k, gm, w_out, geom):
    n, dm = s.shape
    seg = _seg_map(*geom)
    sub = _sub_tiles(n, most=2)
    tm = sub * ROW_TILE

    def col(block):
        return pl.BlockSpec((tm, dm), lambda i: (i, block))

    vec = pl.BlockSpec((1, dm), lambda i: (0, 0))
    return pl.pallas_call(
        functools.partial(_rw_post_kernel, sub=sub, seg=seg),
        grid=(n // tm,),
        in_specs=[col(0), col(0), col(0), col(1), col(3), col(0), col(1),
                  col(0), vec, vec, vec, pl.BlockSpec(gm.shape, lambda i: (0, 0, 0)),
                  pl.BlockSpec((dm, dm), lambda i: (0, 0))],
        out_specs=pl.BlockSpec((tm, dm), lambda i: (i, 0)),
        out_shape=jax.ShapeDtypeStruct((n, dm), F32),
        compiler_params=pltpu.CompilerParams(dimension_semantics=("arbitrary",), vmem_limit_bytes=VMEM_LIMIT),
        name="rwkv_post",
    )(o_f, o_b, rvkg, rvkg, rvkg, kda, kda, s, ln_w[None], ln_b[None], r_k[None], gm, w_out.astype(BF16))


def _hg_scan_kernel(q_ref, v_ref, lf_ref, o_ref, st_scr, *, heads):
    C = q_ref.shape[0]
    d = pl.program_id(0)
    nsub = C // HG_SUB

    @pl.when(pl.program_id(2) == 0)
    def _():
        st_scr[...] = jnp.zeros_like(st_scr)

    def body(reverse):
        last = 0 if reverse else C - 1
        hs = range(heads)
        sls = [slice(h * LANES, (h + 1) * LANES) for h in hs]
        g = [lf_ref[:, sls[h]] for h in hs]
        b = [_cumsum_rows(g[h], reverse) for h in hs]
        k = [-jnp.tanh(0.5 * g[h]) * (jnp.exp(g[h]) + 1.0) for h in hs]
        o_inter = [_dot(q_ref[:, sls[h]].astype(F32) * jnp.exp(b[h]), st_scr[h], NT) for h in hs]
        parts = [[None] * nsub for _ in hs]
        for i in range(nsub):
            r0 = i * HG_SUB
            lo, hi = (r0, C) if reverse else (0, r0 + HG_SUB)
            first = r0 + HG_SUB - 1 if reverse else r0
            row = lax.broadcasted_iota(jnp.int32, (HG_SUB, hi - lo), 0) + r0
            col = lax.broadcasted_iota(jnp.int32, (HG_SUB, hi - lo), 1) + lo
            keep = (col >= row) if reverse else (col <= row)
            att = []
            for h in hs:
                rho = b[h][first:first + 1, :] - g[h][first:first + 1, :]
                qi = q_ref[r0:r0 + HG_SUB, sls[h]].astype(F32) * jnp.exp(b[h][r0:r0 + HG_SUB] - rho)
                ki = k[h][lo:hi] * jnp.exp(jnp.minimum(rho - b[h][lo:hi], HG_EXP_CLAMP))
                att.append(jnp.where(keep, _dot(qi, ki, NT), 0.0))
            for h in hs:
                parts[h][i] = _dot(att[h], v_ref[lo:hi, sls[h]])
        for h in hs:
            o_ref[0, :, sls[h]] = (o_inter[h] + jnp.concatenate(parts[h], axis=0)).astype(o_ref.dtype)
        upd = []
        for h in hs:
            b_last = b[h][last:last + 1, :]
            upd.append((jnp.exp(b_last),
                        _dot(v_ref[:, sls[h]].astype(F32).T, k[h] * jnp.exp(b_last - b[h]))))
        for h in hs:
            st_scr[h] = st_scr[h] * upd[h][0] + upd[h][1]

    @pl.when(d == 0)
    def _():
        body(False)

    @pl.when(d == 1)
    def _():
        body(True)


def hgrn_scan(z, logf, dm, bsz, t):
    n = z.shape[0]
    heads = dm // LANES
    C = HG_CHUNK
    nc, nctx = t // C, CTX_LEN // C

    def row(d, b, p):
        return b * nc + _scan_chunk_index(d, p, nctx, nc)

    return pl.pallas_call(
        functools.partial(_hg_scan_kernel, heads=heads),
        grid=(2, bsz, nc),
        in_specs=[pl.BlockSpec((C, dm), lambda d, b, p: (row(d, b, p), 0)),
                  pl.BlockSpec((C, dm), lambda d, b, p: (row(d, b, p), 1)),
                  pl.BlockSpec((C, dm), lambda d, b, p: (row(d, b, p), d))],
        out_specs=pl.BlockSpec((1, C, dm), lambda d, b, p: (d, row(d, b, p), 0)),
        out_shape=jax.ShapeDtypeStruct((2, n, dm), BF16),
        scratch_shapes=[pltpu.VMEM((heads, LANES, LANES), F32)],
        compiler_params=pltpu.CompilerParams(
            dimension_semantics=("arbitrary", "arbitrary", "arbitrary"), vmem_limit_bytes=VMEM_LIMIT),
        name="hgrn_scan",
    )(z, z, logf)


def _first_argmax(vals):
    best, idx = vals[0], jnp.zeros(vals[0].shape, jnp.int32)
    for i in range(1, len(vals)):
        better = vals[i] > best
        best = jnp.where(better, vals[i], best)
        idx = jnp.where(better, i, idx)
    return best, idx


def _router_kernel(s_ref, gain_ref, shift_ref, wt_ref, b_ref, e_ref, g_ref, rank_ref, cnt_ref, carry_scr, *,
                   n_groups, top_k):
    n_experts = wt_ref.shape[0]
    per = n_experts // n_groups
    h = _norm_mod(s_ref[...], gain_ref[0], shift_ref[0])
    aff = jax.nn.sigmoid(_dot(wt_ref[...], h, NT, passes=3))
    sel = aff + b_ref[...]
    a = [aff[e:e + 1, :] for e in range(n_experts)]
    s = [sel[e:e + 1, :] for e in range(n_experts)]
    neg = jnp.full_like(s[0], -jnp.inf)
    scores = []
    for g in range(n_groups):
        grp = s[g * per:(g + 1) * per]
        m1, i1 = _first_argmax(grp)
        m2, _ = _first_argmax([jnp.where(i1 == j, neg, grp[j]) for j in range(per)])
        scores.append(m1 + m2)
    _, best = _first_argmax(scores)

    def in_best(rows):
        out = []
        for j in range(per):
            x = rows[j]
            for g in range(1, n_groups):
                x = jnp.where(best == g, rows[g * per + j], x)
            out.append(x)
        return out

    sb, ab = in_best(s), in_best(a)
    picked, chosen = [], []
    cand = sb
    for _ in range(top_k):
        _, i = _first_argmax(cand)
        c = ab[0]
        for j in range(1, per):
            c = jnp.where(i == j, ab[j], c)
        picked.append(i)
        chosen.append(c)
        cand = [jnp.where(i == j, neg, cand[j]) for j in range(per)]
    total = functools.reduce(jnp.add, chosen)
    experts = [best * per + picked[kk_] for kk_ in range(top_k)]
    for kk_ in range(top_k):
        e_ref[kk_:kk_ + 1, :] = experts[kk_]
        g_ref[kk_:kk_ + 1, :] = chosen[kk_] / total

    @pl.when(pl.program_id(0) == 0)
    def _():
        carry_scr[...] = jnp.zeros_like(carry_scr)

    tm = s_ref.shape[0]
    sub = lax.broadcasted_iota(jnp.int32, (n_experts, tm), 0)
    onehots = [(sub == ex).astype(F32) for ex in experts]
    tot = functools.reduce(jnp.add, onehots)
    ri = lax.broadcasted_iota(jnp.int32, (tm, tm), 0)
    ci = lax.broadcasted_iota(jnp.int32, (tm, tm), 1)
    earlier = (ri < ci).astype(BF16)
    before = jnp.dot(tot.astype(BF16), earlier, preferred_element_type=F32) + carry_scr[:, 0:1]
    seen = before
    for kk_ in range(top_k):
        rank_ref[kk_:kk_ + 1, :] = jnp.sum(onehots[kk_] * seen, axis=0, keepdims=True).astype(jnp.int32)
        seen = seen + onehots[kk_]
    carry = carry_scr[...] + jnp.sum(tot, axis=1, keepdims=True)
    carry_scr[...] = carry
    cnt_ref[...] = carry.astype(jnp.int32)


def norm_route(s, gain, shift, geom, router_w, router_b):
    n, k = s.shape
    n_experts = router_w.shape[1]
    tm = ROW_TILE
    seg = _seg_map(*geom)
    return pl.pallas_call(
        functools.partial(_router_kernel, n_groups=N_GROUPS, top_k=TOP_K),
        grid=(n // tm,),
        in_specs=[pl.BlockSpec((tm, k), lambda i: (i, 0)),
                  pl.BlockSpec((1, 1, k), lambda i: (seg(i), 0, 0)),
                  pl.BlockSpec((1, 1, k), lambda i: (seg(i), 0, 0)),
                  pl.BlockSpec((n_experts, k), lambda i: (0, 0)),
                  pl.BlockSpec((n_experts, 1), lambda i: (0, 0))],
        out_specs=[pl.BlockSpec((TOP_K, tm), lambda i: (0, i)), pl.BlockSpec((TOP_K, tm), lambda i: (0, i)),
                   pl.BlockSpec((TOP_K, tm), lambda i: (0, i)), pl.BlockSpec((n_experts, LANES), lambda i: (0, 0))],
        out_shape=[jax.ShapeDtypeStruct((TOP_K, n), jnp.int32), jax.ShapeDtypeStruct((TOP_K, n), F32),
                   jax.ShapeDtypeStruct((TOP_K, n), jnp.int32), jax.ShapeDtypeStruct((n_experts, LANES), jnp.int32)],
        scratch_shapes=[pltpu.VMEM((n_experts, LANES), F32)],
        compiler_params=pltpu.CompilerParams(dimension_semantics=("arbitrary",), vmem_limit_bytes=VMEM_LIMIT),
        name="norm_route",
    )(s, gain, shift, router_w.T, router_b.reshape(n_experts, 1).astype(F32))


def _final_norm_kernel(s_ref, g_ref, o_ref):
    x = s_ref[...]
    o_ref[0] = x * lax.rsqrt(jnp.mean(x * x, axis=-1, keepdims=True) + NORM_EPS) * g_ref[...]


def final_norm(s, g, bsz, t, geom):
    dm = s.shape[1]
    tpb, nctx_t = geom
    return pl.pallas_call(
        _final_norm_kernel,
        grid=(bsz, tpb - nctx_t),
        in_specs=[pl.BlockSpec((ROW_TILE, dm), lambda b, i: (b * tpb + nctx_t + i, 0)),
                  pl.BlockSpec((1, dm), lambda b, i: (0, 0))],
        out_specs=pl.BlockSpec((1, ROW_TILE, dm), lambda b, i: (b, i, 0)),
        out_shape=jax.ShapeDtypeStruct((bsz, t - CTX_LEN, dm), F32),
        name="final_norm",
    )(s, g[None])


DMA_UNROLL = 8


def _row_copy_waits(src_row, dst_row, sem, count):
    def body(_, carry):
        pltpu.make_async_copy(src_row, dst_row, sem).wait()
        return carry
    lax.fori_loop(0, count, body, 0)


def _scatter_kernel(dest_ref, meta_ref, s_ref, gain_ref, shift_ref, xb_ref, hbuf, zrow, sems, zsem, *, n_experts):
    i = pl.program_id(0)
    last = pl.num_programs(0) - 1
    slot = i % 2
    per_tile = TOP_K * ROW_TILE

    def wait_tile(sl):
        for _ in range(TOP_K):
            pltpu.make_async_copy(hbuf.at[sl], xb_ref.at[pl.ds(0, ROW_TILE), :], sems.at[sl]).wait()

    @pl.when(i >= 2)
    def _():
        wait_tile(slot)

    hbuf[slot] = _norm_mod(s_ref[...], gain_ref[0], shift_ref[0])

    def issue(r8, carry):
        for u in range(DMA_UNROLL):
            r = r8 * DMA_UNROLL + u
            for k in range(TOP_K):
                d = dest_ref[0, 0, TOP_K * r + k]
                pltpu.make_async_copy(hbuf.at[slot, pl.ds(r, 1), :], xb_ref.at[pl.ds(d, 1), :],
                                      sems.at[slot]).start(priority=(TOP_K * u + k) % 2)
        return carry
    lax.fori_loop(0, ROW_TILE // DMA_UNROLL, issue, 0)

    @pl.when(i == last)
    def _():
        @pl.when(last >= 1)
        def _():
            wait_tile(1 - slot)
        wait_tile(slot)
        zrow[...] = jnp.zeros_like(zrow)
        for e in range(n_experts):
            lo = meta_ref[2, e] + meta_ref[0, e]
            hi = meta_ref[2, e] + meta_ref[1, e]

            def pad_start(q, carry):
                pltpu.make_async_copy(zrow.at[pl.ds(0, 1), :], xb_ref.at[pl.ds(q, 1), :], zsem.at[0]).start()
                return carry
            lax.fori_loop(lo, hi, pad_start, 0)
            _row_copy_waits(zrow.at[pl.ds(0, 1), :], xb_ref.at[pl.ds(0, 1), :], zsem.at[0], hi - lo)
        end = meta_ref[2, n_experts - 1] + meta_ref[1, n_experts - 1]

        def tail_start(q, carry):
            pltpu.make_async_copy(zrow.at[pl.ds(0, 1), :], xb_ref.at[pl.ds(q, 1), :], zsem.at[0]).start()
            return carry
        lax.fori_loop(end, xb_ref.shape[0], tail_start, 0)
        _row_copy_waits(zrow.at[pl.ds(0, 1), :], xb_ref.at[pl.ds(0, 1), :], zsem.at[0], xb_ref.shape[0] - end)


def moe_scatter(s, gain, shift, geom, dest, meta, n_slots):
    n, dm = s.shape
    seg = _seg_map(*geom)
    nt = n // ROW_TILE
    return pl.pallas_call(
        functools.partial(_scatter_kernel, n_experts=meta.shape[1]),
        grid=(nt,),
        in_specs=[pl.BlockSpec((1, 1, TOP_K * ROW_TILE), lambda i: (i, 0, 0), memory_space=pltpu.SMEM),
                  pl.BlockSpec(memory_space=pltpu.SMEM),
                  pl.BlockSpec((ROW_TILE, dm), lambda i: (i, 0)),
                  pl.BlockSpec((1, 1, dm), lambda i: (seg(i), 0, 0)),
                  pl.BlockSpec((1, 1, dm), lambda i: (seg(i), 0, 0))],
        out_specs=pl.BlockSpec(memory_space=pl.ANY),
        out_shape=jax.ShapeDtypeStruct((n_slots, dm), F32),
        scratch_shapes=[pltpu.VMEM((2, ROW_TILE, dm), F32), pltpu.VMEM((8, dm), F32),
                        pltpu.SemaphoreType.DMA((2,)), pltpu.SemaphoreType.DMA((1,))],
        compiler_params=pltpu.CompilerParams(dimension_semantics=("arbitrary",), vmem_limit_bytes=VMEM_LIMIT),
        name="moe_scatter",
    )(dest.reshape(nt, 1, TOP_K * ROW_TILE), meta, s, gain, shift)


def _gather_combine_kernel(dcur_ref, dnxt_ref, g_ref, s_ref, gm_ref, yb_ref, o_ref, ybuf, sems):
    i = pl.program_id(0)
    nsteps = pl.num_programs(0)
    slot = i % 2

    def start_tile(dref, sl):
        def issue(r8, carry):
            for u in range(DMA_UNROLL):
                r = r8 * DMA_UNROLL + u
                for k in range(TOP_K):
                    d = dref[0, 0, TOP_K * r + k]
                    pltpu.make_async_copy(yb_ref.at[pl.ds(d, 1), :], ybuf.at[sl, k, pl.ds(r, 1), :],
                                          sems.at[sl]).start(priority=(TOP_K * u + k) % 2)
            return carry
        lax.fori_loop(0, ROW_TILE // DMA_UNROLL, issue, 0)

    @pl.when(i == 0)
    def _():
        start_tile(dcur_ref, 0)

    @pl.when(i + 1 < nsteps)
    def _():
        start_tile(dnxt_ref, 1 - slot)

    for k in range(TOP_K):
        pltpu.make_async_copy(yb_ref.at[pl.ds(0, ROW_TILE), :], ybuf.at[slot, k], sems.at[slot]).wait()
    g = g_ref[...]
    y = sum(ybuf[slot, k] * g[:, k:k + 1] for k in range(TOP_K))
    o_ref[...] = s_ref[...] + gm_ref[0] * y


def moe_gather_combine(yb, dest, gate, s, gm, geom):
    n, dm = s.shape
    seg = _seg_map(*geom)
    nt = n // ROW_TILE
    d3 = dest.reshape(nt, 1, TOP_K * ROW_TILE)
    row = pl.BlockSpec((ROW_TILE, dm), lambda i: (i, 0))
    return pl.pallas_call(
        _gather_combine_kernel,
        grid=(nt,),
        in_specs=[pl.BlockSpec((1, 1, TOP_K * ROW_TILE), lambda i: (i, 0, 0), memory_space=pltpu.SMEM),
                  pl.BlockSpec((1, 1, TOP_K * ROW_TILE), lambda i: (jnp.minimum(i + 1, nt - 1), 0, 0),
                               memory_space=pltpu.SMEM),
                  pl.BlockSpec((ROW_TILE, TOP_K), lambda i: (i, 0)), row,
                  pl.BlockSpec((1, 1, dm), lambda i: (seg(i), 0, 0)),
                  pl.BlockSpec(memory_space=pl.ANY)],
        out_specs=row,
        out_shape=jax.ShapeDtypeStruct((n, dm), F32),
        scratch_shapes=[pltpu.VMEM((2, TOP_K, ROW_TILE, dm), F32), pltpu.SemaphoreType.DMA((2,))],
        compiler_params=pltpu.CompilerParams(dimension_semantics=("arbitrary",), vmem_limit_bytes=VMEM_LIMIT),
        name="moe_gather_combine",
    )(d3, d3, gate, s, gm, yb)


def _ffn_kernel(be_ref, x_ref, w1_ref, w3_ref, w2_ref, o_ref):
    del be_ref
    x = x_ref[...].astype(BF16)
    a = jnp.dot(x, w1_ref[0], preferred_element_type=F32)
    b = jnp.dot(x, w3_ref[0], preferred_element_type=F32)
    hid = (a * jax.nn.sigmoid(a) * b).astype(BF16)
    o_ref[...] = jnp.dot(hid, w2_ref[0], preferred_element_type=F32).astype(o_ref.dtype)


def expert_ffn(xb, block_expert, w1, w3, w2):
    nrows, dm = xb.shape
    f = w1.shape[2]
    nb = nrows // MOE_BLOCK
    return pl.pallas_call(
        _ffn_kernel,
        grid_spec=pltpu.PrefetchScalarGridSpec(
            num_scalar_prefetch=1,
            grid=(nb,),
            in_specs=[pl.BlockSpec((MOE_BLOCK, dm), lambda i, be: (i, 0)),
                      pl.BlockSpec((1, dm, f), lambda i, be: (be[i], 0, 0)),
                      pl.BlockSpec((1, dm, f), lambda i, be: (be[i], 0, 0)),
                      pl.BlockSpec((1, f, dm), lambda i, be: (be[i], 0, 0))],
            out_specs=pl.BlockSpec((MOE_BLOCK, dm), lambda i, be: (i, 0))),
        out_shape=jax.ShapeDtypeStruct((nrows, dm), F32),
        compiler_params=pltpu.CompilerParams(
            dimension_semantics=("arbitrary",), vmem_limit_bytes=VMEM_LIMIT),
        name="expert_ffn",
    )(block_expert, xb, w1.astype(BF16), w3.astype(BF16), w2.astype(BF16))


def _mlstm_layer(s, gain, shift, gate_mod, geom, bsz, t, w_in, w_gate, b_gate, conv, head_g, w_out):
    n, dm = s.shape
    heads = ML_HEADS
    z = norm_mod_mm(s, gain, shift, w_in, None, (None, None, None, "sigmoid"), geom)
    scale = jnp.concatenate([jnp.ones((dm,), F32), jnp.full((dm,), (dm // heads) ** -0.5, F32)])
    qk = conv_silu(z, conv, scale, 2 * dm, geom)
    ng = 4 * heads
    wg = jnp.pad(jnp.concatenate([w_gate[0], w_gate[1]], axis=1), ((0, 0), (0, LANES - ng)))
    bg = jnp.pad(jnp.concatenate([b_gate[0], b_gate[1]]), (0, LANES - ng))
    gates = norm_mod_mm(s, gain, shift, wg, bg, (None,), geom, out_dtype=F32)[:, :ng]
    gates = gates.reshape(bsz, t, 2, 2 * heads)
    gates = jnp.concatenate([gates[..., :heads], jax.nn.log_sigmoid(gates[..., heads:])], axis=-1)
    gc = jnp.moveaxis(gates, 2, 0).reshape(2, n, 2 * heads)
    gr = jnp.transpose(gates, (2, 0, 3, 1))
    h = mlstm_scan(qk, z, gc, gr, dm, bsz, t)
    return post_mm_residual(h, z, 3, s, head_g, gate_mod, w_out, heads, geom)


def _rwkv7_layer(s, gain, shift, gate_mod, geom, bsz, t, mu, w_rkv, w0, w1, w2, a0, a1, a2, g1, g2,
                 k_k, k_a, r_k, ln_w, ln_b, w_out):
    dm = s.shape[1]
    lw, kda, rvkg = rwkv_proj(s, gain, shift, geom, mu, w_rkv, w0, w1, w2, a0, a1, a2, g1, g2, k_k, k_a)
    o_f, o_b = rwkv_scan(lw, kda, rvkg, dm, bsz, t)
    return rwkv_post(o_f, o_b, rvkg, kda, s, ln_w, ln_b, r_k, gate_mod, w_out, geom)


def _hgrn2_layer(s, gain, shift, gate_mod, geom, bsz, t, layer_idx, w_in, w_f, b_f, lb_logits, head_g, w_out):
    dm = s.shape[1]
    z = norm_mod_mm(s, gain, shift, w_in, None, ("silu", None, "silu"), geom)
    p = jax.nn.softmax(lb_logits, axis=0)
    lb = jnp.cumsum(p, axis=0)[layer_idx] - p[0]
    aux = jnp.tile(jnp.stack([jnp.log(lb), jnp.log1p(-lb)]), (1, 2))
    log_f = norm_mod_mm(s, gain, shift, jnp.concatenate([w_f[0], w_f[1]], axis=1),
                        jnp.concatenate([b_f[0], b_f[1]]), ("logf", "logf"), geom, aux=aux, out_dtype=F32)
    o = hgrn_scan(z, log_f, dm, bsz, t)
    return post_mm_residual(o, z, 2, s, head_g, gate_mod, w_out, dm // HG_EXPAND, geom)


def _moe_layer(s, gain, shift, gate_mod, geom, router_w, router_b, w1, w3, w2):
    n_tok, d = s.shape
    n_experts = w1.shape[0]
    n_assign = n_tok * TOP_K
    e, g, rank, cnt = norm_route(s, gain, shift, geom, router_w, router_b)
    flat_e = e.T.reshape(n_assign)
    rank = rank.T.reshape(n_assign)
    counts = cnt[:, 0]
    padded = (counts + MOE_BLOCK - 1) // MOE_BLOCK * MOE_BLOCK
    end_pad = jnp.cumsum(padded)
    start_pad = end_pad - padded
    onehot = flat_e[:, None] == jnp.arange(n_experts, dtype=jnp.int32)[None, :]
    dest = jnp.sum(jnp.where(onehot, start_pad[None, :], 0), axis=1) + rank
    n_blocks = -(-n_assign // MOE_BLOCK) + n_experts
    block_start = jnp.arange(n_blocks, dtype=jnp.int32) * MOE_BLOCK
    block_expert = jnp.minimum(jnp.sum(end_pad[None, :] <= block_start[:, None], axis=1), n_experts - 1)
    meta = jnp.stack([counts, padded, start_pad]).astype(jnp.int32)
    xb = moe_scatter(s, gain, shift, geom, dest.astype(jnp.int32), meta, n_blocks * MOE_BLOCK)
    yb = expert_ffn(xb, block_expert.astype(jnp.int32), w1, w3, w2)
    return moe_gather_combine(yb, dest.astype(jnp.int32), g.T, s, gate_mod, geom)


def kernel(x, c, ctx, c_ctx, ada_w, ada_b, norm_mix, norm_ffn, norm_out, ml_w_in, ml_w_gate, ml_b_gate, ml_conv, ml_head_g, ml_w_out, rw_mu, rw_w_rkv, rw_w0, rw_w1, rw_w2, rw_a0, rw_a1, rw_a2, rw_g1, rw_g2, rw_k_k, rw_k_a, rw_r_k, rw_ln_w, rw_ln_b, rw_w_out, hg_w_in, hg_w_f, hg_b_f, hg_lb_logits, hg_head_g, hg_w_out, router_w, router_b, ex_w1, ex_w3, ex_w2):
    depth = ada_w.shape[0]
    bsz = x.shape[0]
    cond = jax.nn.silu(jnp.concatenate([c, c_ctx[None]], axis=0))
    cond = jnp.pad(cond, ((0, -(bsz + 1) % 8), (0, 0)))
    dm = x.shape[2]
    t = CTX_LEN + x.shape[1]
    n = bsz * t
    geom = (t // ROW_TILE, CTX_LEN // ROW_TILE)
    s = jnp.concatenate([ctx, x], axis=1).reshape(n, dm)
    for i in range(depth):
        mod = mm(cond, ada_w[i], bias=ada_b[i])
        mod_x = jnp.split(mod[:bsz, None, :], 6, axis=-1)
        mod_c = jnp.split(mod[bsz], 6, axis=-1)

        def table(idx):
            return jnp.stack([jnp.broadcast_to(mod_c[idx], (bsz, dm)), mod_x[idx][:, 0]], axis=1).reshape(2 * bsz, 1, dm)

        kind, j = i % N_MIXERS, i // N_MIXERS
        if kind == 2:
            s = _hgrn2_layer(s, norm_mix[i] * (1 + table(1)), table(0), table(2), geom, bsz, t, i,
                             hg_w_in[j], hg_w_f[j], hg_b_f[j], hg_lb_logits, hg_head_g[j], hg_w_out[j])
        elif kind == 0:
            s = _mlstm_layer(s, norm_mix[i] * (1 + table(1)), table(0), table(2), geom, bsz, t,
                             ml_w_in[j], ml_w_gate[j], ml_b_gate[j], ml_conv[j], ml_head_g[j], ml_w_out[j])
        else:
            s = _rwkv7_layer(s, norm_mix[i] * (1 + table(1)), table(0), table(2), geom, bsz, t,
                             rw_mu[j], rw_w_rkv[j], rw_w0[j], rw_w1[j], rw_w2[j], rw_a0[j],
                             rw_a1[j], rw_a2[j], rw_g1[j], rw_g2[j], rw_k_k[j], rw_k_a[j],
                             rw_r_k[j], rw_ln_w[j], rw_ln_b[j], rw_w_out[j])
        s = _moe_layer(s, norm_ffn[i] * (1 + table(4)), table(3), table(5), geom, router_w, router_b,
                       ex_w1[i], ex_w3[i], ex_w2[i])
    return final_norm(s, norm_out, bsz, t, geom)
```

```python
import functools

import jax
import jax.numpy as jnp
from jax import lax
from jax.experimental import pallas as pl
from jax.experimental.pallas import tpu as pltpu

F32 = jnp.float32
BF16 = jnp.bfloat16

GRID_W = 64
CTX_LEN = 256
N_MIXERS = 3
NORM_EPS = 1e-6
ML_HEADS = 8
RW_HEAD_DIM = 64
RW_GN_EPS = 64e-5
HG_EXPAND = 128
N_GROUPS = 4
TOP_K = 2
MOE_BLOCK = 512

LANES = 128
ML_CHUNK = 128
RW_CHUNK = 64
RW_PRE_CHUNKS = 4
RW_PAIRS_PER_STEP = 2
HG_CHUNK = 128
HG_SUB = 16
HG_EXP_CLAMP = 80.0
VMEM_LIMIT = 48 * 1024 * 1024
VMEM_LIMIT_BIG = 56 * 1024 * 1024

NT = (((1,), (1,)), ((), ()))
NN = (((1,), (0,)), ((), ()))


def _dot(a, b, dims=NN, passes=1):
    a_hi = a.astype(BF16)
    b_hi = b.astype(BF16)
    out = lax.dot_general(a_hi, b_hi, dims, preferred_element_type=F32)
    if passes == 3:
        a_lo = (a - a_hi.astype(F32)).astype(BF16)
        b_lo = (b - b_hi.astype(F32)).astype(BF16)
        out = out + lax.dot_general(a_hi, b_lo, dims, preferred_element_type=F32)
        out = out + lax.dot_general(a_lo, b_hi, dims, preferred_element_type=F32)
    return out


def _dot_exact01(a, b, lhs01=False):
    x = (b if lhs01 else a).astype(F32)
    out = None
    for _ in range(3):
        t = x.astype(BF16)
        x = x - t.astype(F32)
        p = lax.dot_general(a, t, NN, preferred_element_type=F32) if lhs01 else \
            lax.dot_general(t, b, NN, preferred_element_type=F32)
        out = p if out is None else out + p
    return out


def _cumsum_rows(x, reverse):
    n = x.shape[0]
    row = lax.broadcasted_iota(jnp.int32, x.shape, 0)
    s = 1
    while s < n:
        if reverse:
            x = x + jnp.where(row < n - s, pltpu.roll(x, n - s, axis=0), 0.0)
        else:
            x = x + jnp.where(row >= s, pltpu.roll(x, s, axis=0), 0.0)
        s *= 2
    return x


def _pick_tile(n, candidates):
    for c in candidates:
        if n % c == 0:
            return c
    raise ValueError(f"no tile for {n}")


def _scan_chunk_index(d, p, nctx, nc):
    rev = jnp.where(p < nctx, nctx - 1 - p, nc - 1 - (p - nctx))
    return jnp.where(d == 0, p, rev)


_ACTS = {
    None: lambda y: y,
    "sigmoid": jax.nn.sigmoid,
    "silu": lambda y: y * jax.nn.sigmoid(y),
    "tanh": jnp.tanh,
}


def _mm_kernel(x_ref, w_ref, b_ref, o_ref, *, act, precise):
    if precise:
        y = _dot(x_ref[...], w_ref[...], passes=3)
    else:
        y = jnp.dot(x_ref[...].astype(BF16), w_ref[...], preferred_element_type=F32)
    o_ref[...] = _ACTS[act](y + b_ref[...]).astype(o_ref.dtype)


def mm(x, w, bias=None, act=None, out_dtype=F32, precise=False):
    n, k = x.shape
    m = w.shape[1]
    tm = _pick_tile(n, (512, 256, 128, 64, 32, 16, 8))
    tn = m if m <= 1024 else _pick_tile(m, (1024, 512, 256, 128))
    if not precise:
        w = w.astype(BF16)
    if bias is None:
        bias = jnp.zeros((m,), F32)
    return pl.pallas_call(
        functools.partial(_mm_kernel, act=act, precise=precise),
        grid=(n // tm, m // tn),
        in_specs=[pl.BlockSpec((tm, k), lambda i, j: (i, 0)),
                  pl.BlockSpec((k, tn), lambda i, j: (0, j)),
                  pl.BlockSpec((1, tn), lambda i, j: (0, j))],
        out_specs=pl.BlockSpec((tm, tn), lambda i, j: (i, j)),
        out_shape=jax.ShapeDtypeStruct((n, m), out_dtype),
        compiler_params=pltpu.CompilerParams(vmem_limit_bytes=VMEM_LIMIT),
        name="mm",
    )(x, w, bias.reshape(1, m).astype(F32))


ROW_TILE = 256


def _log1p_exp_neg_abs(x):
    return jnp.log(1.0 + jnp.exp(-jnp.abs(x)))


def _log_sigmoid(y):
    return jnp.minimum(y, 0.0) - _log1p_exp_neg_abs(y)


def _softplus(x):
    return jnp.maximum(x, 0.0) + _log1p_exp_neg_abs(x)


def _logaddexp(a, b):
    return jnp.maximum(a, b) + _log1p_exp_neg_abs(a - b)


def _norm_mod(x, gain, shift):
    return x * lax.rsqrt(jnp.mean(x * x, axis=-1, keepdims=True) + NORM_EPS) * gain + shift


def _seg_map(tpb, nctx_t):
    def seg(i):
        return (i // tpb) * 2 + jnp.where(i % tpb < nctx_t, 0, 1)
    return seg


_EPILOGUES = {
    None: lambda y, aux: y,
    "sigmoid": lambda y, aux: jax.nn.sigmoid(y),
    "silu": lambda y, aux: y * jax.nn.sigmoid(y),
    "logf": lambda y, aux: _logaddexp(aux[0:1, :], aux[1:2, :] + _log_sigmoid(y)),
}


def _sub_tiles(n, most=4):
    return _pick_tile(n // ROW_TILE, tuple(range(most, 0, -1)))


def _nmm_kernel(s_ref, gain_ref, shift_ref, w_ref, b_ref, aux_ref, o_ref, h_scr, *, acts, sub, seg):
    j = pl.program_id(1)

    @pl.when(j == 0)
    def _():
        for k in range(sub):
            rows = pl.ds(k * ROW_TILE, ROW_TILE)
            sk = seg(pl.program_id(0) * sub + k)
            h_scr[rows, :] = _norm_mod(s_ref[rows, :], gain_ref[sk], shift_ref[sk]).astype(BF16)

    y = jnp.dot(h_scr[...], w_ref[...], preferred_element_type=F32) + b_ref[...]
    for act in sorted(set(acts), key=str):
        cols = [jj for jj, a in enumerate(acts) if a == act]
        if len(cols) == len(acts):
            o_ref[...] = _EPILOGUES[act](y, aux_ref[...]).astype(o_ref.dtype)
        else:
            @pl.when(functools.reduce(jnp.logical_or, [j == jj for jj in cols]))
            def _(act=act):
                o_ref[...] = _EPILOGUES[act](y, aux_ref[...]).astype(o_ref.dtype)


def norm_mod_mm(s, gain, shift, w, bias, acts, geom, aux=None, out_dtype=None):
    out_dtype = out_dtype or BF16
    n, k = s.shape
    m = w.shape[1]
    tn = m // len(acts)
    tpb, nctx_t = geom
    seg = _seg_map(tpb, nctx_t)
    sub = _sub_tiles(n)
    tm = sub * ROW_TILE
    if bias is None:
        bias = jnp.zeros((m,), F32)
    if aux is None:
        aux = jnp.zeros((2, m), F32)
    return pl.pallas_call(
        functools.partial(_nmm_kernel, acts=tuple(acts), sub=sub, seg=seg),
        grid=(n // tm, m // tn),
        in_specs=[pl.BlockSpec((tm, k), lambda i, j: (i, 0)),
                  pl.BlockSpec(gain.shape, lambda i, j: (0, 0, 0)),
                  pl.BlockSpec(shift.shape, lambda i, j: (0, 0, 0)),
                  pl.BlockSpec((k, tn), lambda i, j: (0, j)),
                  pl.BlockSpec((1, tn), lambda i, j: (0, j)),
                  pl.BlockSpec((2, tn), lambda i, j: (0, j))],
        out_specs=pl.BlockSpec((tm, tn), lambda i, j: (i, j)),
        out_shape=jax.ShapeDtypeStruct((n, m), out_dtype),
        scratch_shapes=[pltpu.VMEM((tm, k), BF16)],
        compiler_params=pltpu.CompilerParams(
            dimension_semantics=("arbitrary", "arbitrary"), vmem_limit_bytes=VMEM_LIMIT),
        name="norm_mod_mm",
    )(s, gain, shift, w.astype(BF16), bias.reshape(1, m).astype(F32), aux.astype(F32))


def _gated_residual_store(o_ref, s_ref, gm_ref, y, sub, seg):
    for k in range(sub):
        rows = pl.ds(k * ROW_TILE, ROW_TILE)
        gm = gm_ref[seg(pl.program_id(0) * sub + k)]
        o_ref[rows, :] = s_ref[rows, :] + gm * y[k * ROW_TILE:(k + 1) * ROW_TILE]


def _post_kernel(h_ref, g_ref, s_ref, hg_ref, gm_ref, w_ref, o_ref, *, heads, sub, seg):
    x = h_ref[0].astype(F32) + h_ref[1].astype(F32)
    hd = x.shape[1] // heads
    parts = []
    for h in range(heads):
        xh = x[:, h * hd:(h + 1) * hd]
        parts.append(xh * lax.rsqrt(jnp.mean(xh * xh, axis=-1, keepdims=True) + NORM_EPS))
    y = (jnp.concatenate(parts, axis=1) * hg_ref[...] * g_ref[...].astype(F32)).astype(BF16)
    _gated_residual_store(o_ref, s_ref, gm_ref, jnp.dot(y, w_ref[...], preferred_element_type=F32), sub, seg)


def post_mm_residual(h2, gate_arr, gate_block, s, head_g, gm, w_out, heads, geom):
    n, dm = s.shape
    seg = _seg_map(*geom)
    sub = _sub_tiles(n, most=2)
    tm = sub * ROW_TILE
    return pl.pallas_call(
        functools.partial(_post_kernel, heads=heads, sub=sub, seg=seg),
        grid=(n // tm,),
        in_specs=[pl.BlockSpec((2, tm, dm), lambda i: (0, i, 0)),
                  pl.BlockSpec((tm, dm), lambda i: (i, gate_block)),
                  pl.BlockSpec((tm, dm), lambda i: (i, 0)),
                  pl.BlockSpec((1, dm), lambda i: (0, 0)),
                  pl.BlockSpec(gm.shape, lambda i: (0, 0, 0)),
                  pl.BlockSpec((dm, dm), lambda i: (0, 0))],
        out_specs=pl.BlockSpec((tm, dm), lambda i: (i, 0)),
        out_shape=jax.ShapeDtypeStruct((n, dm), F32),
        compiler_params=pltpu.CompilerParams(dimension_semantics=("arbitrary",), vmem_limit_bytes=VMEM_LIMIT),
        name="post_mm_residual",
    )(h2, gate_arr, s, head_g.reshape(1, dm), gm, w_out.astype(BF16))


CONV_COLS = 512


def _conv_kernel(cur_ref, up_ref, dn_ref, w_ref, sc_ref, o_ref, *, tpb, nctx_t):
    ti = pl.program_id(0) % tpb
    is_ctx = ti < nctx_t
    no_up = jnp.logical_or(is_ctx, ti == nctx_t)
    no_dn = jnp.logical_or(is_ctx, ti == tpb - 1)
    x = cur_ref[...].astype(F32)
    up = jnp.where(no_up, 0.0, up_ref[...].astype(F32))
    dn = jnp.where(no_dn, 0.0, dn_ref[...].astype(F32))
    ext = jnp.concatenate([up, x, dn], axis=0)
    tpos = lax.broadcasted_iota(jnp.int32, (ROW_TILE, 1), 0)
    col = tpos % GRID_W
    left_ok = jnp.where(is_ctx, (tpos > 0).astype(F32), (col > 0).astype(F32))
    right_ok = jnp.where(is_ctx, (tpos < ROW_TILE - 1).astype(F32), (col < GRID_W - 1).astype(F32))
    vert = jnp.where(is_ctx, 0.0, 1.0)
    w = w_ref[...]
    sums = [None, None, None]
    for dr in (-1, 0, 1):
        base = GRID_W * (1 + dr)
        wr = w[3 * (dr + 1):3 * (dr + 2)] * (1.0 if dr == 0 else vert)
        for dc in range(3):
            term = ext[base:base + ROW_TILE] * wr[dc:dc + 1]
            sums[dc] = term if sums[dc] is None else sums[dc] + term
    acc = (sums[1] + left_ok * pltpu.roll(sums[0], 1, axis=0)
           + right_ok * pltpu.roll(sums[2], ROW_TILE - 1, axis=0))
    o_ref[...] = (acc * jax.nn.sigmoid(acc) * sc_ref[...]).astype(o_ref.dtype)


def conv_silu(z, conv_w, scale, width, geom):
    n = z.shape[0]
    tpb, nctx_t = geom
    assert nctx_t == 1 and ROW_TILE % GRID_W == 0
    hb = ROW_TILE // GRID_W
    last = n // GRID_W - 1
    return pl.pallas_call(
        functools.partial(_conv_kernel, tpb=tpb, nctx_t=nctx_t),
        grid=(n // ROW_TILE, width // CONV_COLS),
        in_specs=[pl.BlockSpec((ROW_TILE, CONV_COLS), lambda i, c: (i, c)),
                  pl.BlockSpec((GRID_W, CONV_COLS), lambda i, c: (jnp.maximum(i * hb - 1, 0), c)),
                  pl.BlockSpec((GRID_W, CONV_COLS), lambda i, c: (jnp.minimum((i + 1) * hb, last), c)),
                  pl.BlockSpec((9, CONV_COLS), lambda i, c: (0, c)),
                  pl.BlockSpec((1, CONV_COLS), lambda i, c: (0, c))],
        out_specs=pl.BlockSpec((ROW_TILE, CONV_COLS), lambda i, c: (i, c)),
        out_shape=jax.ShapeDtypeStruct((n, width), BF16),
        compiler_params=pltpu.CompilerParams(
            dimension_semantics=("arbitrary", "arbitrary"), vmem_limit_bytes=VMEM_LIMIT),
        name="conv_silu",
    )(z, z, z, conv_w.reshape(9, width).astype(F32), scale.reshape(1, width).astype(F32))


def _cummax_rows(x, reverse):
    n = x.shape[0]
    row = lax.broadcasted_iota(jnp.int32, x.shape, 0)
    s = 1
    while s < n:
        if reverse:
            x = jnp.maximum(x, jnp.where(row < n - s, pltpu.roll(x, n - s, axis=0), -jnp.inf))
        else:
            x = jnp.maximum(x, jnp.where(row >= s, pltpu.roll(x, s, axis=0), -jnp.inf))
        s *= 2
    return x


def _ml_scan_kernel(q_ref, k_ref, v_ref, gc_ref, gr_ref, o_ref, z_scr, m_scr, *, heads):
    L = q_ref.shape[0]
    assert L == LANES
    d = pl.program_id(0)

    @pl.when(pl.program_id(2) == 0)
    def _():
        z_scr[...] = jnp.zeros_like(z_scr)
        m_scr[...] = jnp.zeros_like(m_scr)

    row = lax.broadcasted_iota(jnp.int32, (L, L), 0)
    col = lax.broadcasted_iota(jnp.int32, (L, L), 1)
    ones_blk = jnp.ones((L, LANES), BF16)

    def body(reverse):
        incl = (col >= row) if reverse else (col <= row)
        incl_t = (row >= col) if reverse else (row <= col)
        last = 0 if reverse else L - 1
        hs = range(heads)
        sls = [slice(h * LANES, (h + 1) * LANES) for h in hs]
        qk = [_dot(q_ref[:, sls[h]], k_ref[:, sls[h]], NT) for h in hs]
        qz = [_dot(q_ref[:, sls[h]], z_scr[h]) for h in hs]
        b_cols = _dot_exact01(incl.astype(BF16), gc_ref[0, :, heads:2 * heads], lhs01=True)
        b_rows = _dot_exact01(gr_ref[0, 0, heads:2 * heads, :], incl_t.astype(BF16))
        cols = jnp.concatenate([b_cols, gc_ref[0, :, 0:heads]], axis=1)
        pick = lax.broadcasted_iota(jnp.int32, (2 * heads, 2 * LANES), 0)
        lane2 = lax.broadcasted_iota(jnp.int32, (2 * heads, 2 * LANES), 1)
        stats = []
        for h in hs:
            sel = (pick == jnp.where(lane2 < LANES, h, heads + h)).astype(BF16)
            rep = _dot_exact01(cols, sel)
            b_rep, ig_rep = rep[:, :LANES], rep[:, LANES:]
            ig_row = gr_ref[0, 0, h:h + 1, :]
            b_row = b_rows[h:h + 1, :]
            m_prev = m_scr[h:h + 1, :]
            cmax = _cummax_rows(ig_rep - b_rep, reverse)
            dmat = jnp.where(incl, b_rep - (b_row - ig_row), -jnp.inf)
            inter = b_rep + m_prev
            m_t = jnp.maximum(inter, b_rep + cmax)
            b_last = b_rep[last:last + 1, :]
            m_new = jnp.maximum(b_last + m_prev, b_last + cmax[last:last + 1, :])
            w_k = jnp.exp(b_last - b_rep + ig_rep - m_new)
            w_prev = jnp.exp(b_last + m_prev - m_new)
            stats.append((jnp.exp(dmat - m_t), jnp.exp(inter - m_t), jnp.exp(-m_t), w_k, w_prev, m_new))
        kv = []
        for h in hs:
            w_k = stats[h][3]
            wv = jnp.concatenate([w_k * v_ref[:, sls[h]].astype(F32), w_k], axis=1)
            kv.append(_dot(k_ref[:, sls[h]].astype(F32).T, wv))
        s = [qk[h] * stats[h][0] for h in hs]
        sv = [_dot(s[h], jnp.concatenate([v_ref[:, sls[h]], ones_blk], axis=1)) for h in hs]
        for h in hs:
            _, w_inter, floor, _, w_prev, m_new = stats[h]
            w2 = jnp.concatenate([w_inter, w_inter], axis=1)
            full = sv[h] + w2 * qz[h]
            den = full[:, LANES:]
            o_ref[0, :, sls[h]] = (full[:, :LANES] / jnp.maximum(jnp.abs(den), floor)).astype(o_ref.dtype)
            z_scr[h] = jnp.concatenate([w_prev, w_prev], axis=1) * z_scr[h] + kv[h]
            m_scr[h:h + 1, :] = m_new

    @pl.when(d == 0)
    def _():
        body(False)

    @pl.when(d == 1)
    def _():
        body(True)


def mlstm_scan(qk, z, gc, gr, dm, bsz, t):
    n = qk.shape[0]
    heads = dm // LANES
    L = ML_CHUNK
    nc, nctx = t // L, CTX_LEN // L

    def row(d, b, p):
        return b * nc + _scan_chunk_index(d, p, nctx, nc)

    return pl.pallas_call(
        functools.partial(_ml_scan_kernel, heads=heads),
        grid=(2, bsz, nc),
        in_specs=[pl.BlockSpec((L, dm), lambda d, b, p: (row(d, b, p), 0)),
                  pl.BlockSpec((L, dm), lambda d, b, p: (row(d, b, p), 1)),
                  pl.BlockSpec((L, dm), lambda d, b, p: (row(d, b, p), 2)),
                  pl.BlockSpec((1, L, 2 * heads), lambda d, b, p: (d, row(d, b, p), 0)),
                  pl.BlockSpec((1, 1, 2 * heads, L),
                               lambda d, b, p: (d, b, 0, _scan_chunk_index(d, p, nctx, nc)))],
        out_specs=pl.BlockSpec((1, L, dm), lambda d, b, p: (d, row(d, b, p), 0)),
        out_shape=jax.ShapeDtypeStruct((2, n, dm), BF16),
        scratch_shapes=[pltpu.VMEM((heads, LANES, 2 * LANES), F32), pltpu.VMEM((heads, LANES), F32)],
        compiler_params=pltpu.CompilerParams(
            dimension_semantics=("arbitrary", "arbitrary", "arbitrary"), vmem_limit_bytes=VMEM_LIMIT),
        name="mlstm_scan",
    )(qk, qk, z, gc, gr)


RW_STATE_PASSES = 3


def _rw_scan_kernel(lw0_ref, lw1_ref, kd0_ref, kd1_ref, a0_ref, a1_ref, r0_ref, r1_ref, v0_ref, v1_ref,
                    kk0_ref, kk1_ref, of_ref, ob_ref, h_scr, rdp_scr, o0_scr, m_scr, ha_scr, *, nchunk, npair):
    L = RW_CHUNK
    j = pl.program_id(2)

    @pl.when(j == 0)
    def _():
        for ref in (h_scr, rdp_scr, o0_scr, m_scr, ha_scr):
            ref[...] = jnp.zeros_like(ref)

    qls = [slice(q * LANES, (q + 1) * LANES) for q in range(npair)]
    hs = {(q, d): h_scr[q, d] for q in range(npair) for d in range(2)}

    def recurrence_step(k):
        for q in range(npair):
            for d, o_ref in ((0, of_ref), (1, ob_ref)):
                c = k if d == 0 else nchunk - 1 - k
                o_ref[pl.ds(c * L, L), qls[q]] = (_dot(rdp_scr[q, d, c], hs[q, d], passes=RW_STATE_PASSES)
                                                  + o0_scr[q, d, c]).astype(o_ref.dtype)
                hs[q, d] = _dot(m_scr[q, d, c], hs[q, d], passes=RW_STATE_PASSES) + ha_scr[q, d, c]

    pending = list(range(nchunk))

    half = LANES // 2
    row = lax.broadcasted_iota(jnp.int32, (L, LANES), 0)
    col = lax.broadcasted_iota(jnp.int32, (L, LANES), 1) % half
    eye2 = (row == col).astype(F32)
    lane = lax.broadcasted_iota(jnp.int32, (1, LANES), 1)
    m0 = (lane < half).astype(BF16)
    m1 = (lane >= half).astype(BF16)
    r2 = lax.broadcasted_iota(jnp.int32, (LANES, LANES), 0)
    c2 = lax.broadcasted_iota(jnp.int32, (LANES, LANES), 1)
    same_head = (r2 // half) == (c2 // half)

    def stack(x):
        xb = x.astype(BF16)
        return jnp.concatenate([xb * m0, xb * m1], axis=0)

    chains = [(q, d, c) for c in range(nchunk) for q in range(npair) for d in range(2)]
    st = {}
    for q, d, c in chains:
        reverse = d == 1
        sl = pl.ds(c * L, L)
        lw = (lw0_ref, lw1_ref)[d][sl, qls[q]]
        k = (kd0_ref, kd1_ref)[d][sl, qls[q]].astype(F32)
        kk = (kk0_ref, kk1_ref)[d][sl, qls[q]].astype(F32)
        akk = kk * (a0_ref, a1_ref)[d][sl, qls[q]].astype(F32)
        g = _cumsum_rows(lw, reverse)
        ieg = jnp.exp(-g)
        g_last = g[0:1] if reverse else g[L - 1:L]
        dl = jnp.exp(g_last - g)
        st[q, d, c] = dict(kd=kk * jnp.exp(g - lw), rd=(r0_ref, r1_ref)[d][sl, qls[q]].astype(F32) * jnp.exp(g),
                           ai=akk * ieg, ki=k * ieg, ad=akk * dl, kdd=k * dl, eg_last=jnp.exp(g_last),
                           v=(v0_ref, v1_ref)[d][sl, qls[q]].astype(F32))
    recurrence_step(pending.pop(0))
    for q, d, c in chains:
        s = st[q, d, c]
        reverse = d == 1
        incl = (col >= row) if reverse else (col <= row)
        strict = (col > row) if reverse else (col < row)
        x = jnp.concatenate([s["kd"], s["rd"]], axis=0)
        rhs = jnp.concatenate([stack(s["ai"]), stack(s["ki"])], axis=0)
        sc = _dot(x, rhs, NT)
        s["a_ab"] = jnp.where(strict, sc[:L, :LANES], 0.0)
        s["a_ak"] = jnp.where(strict, sc[:L, LANES:], 0.0)
        s["b_ra"] = jnp.where(incl, sc[L:, :LANES], 0.0)
        s["b_rk"] = jnp.where(incl, sc[L:, LANES:], 0.0)
        s["tinv"] = eye2 - s["a_ab"]
        s["pw"] = s["a_ab"]
    span = 2
    while span < L:
        for key in chains:
            s = st[key]
            s["pw"] = _dot(s["pw"], stack(s["pw"]))
        for key in chains:
            s = st[key]
            s["tinv"] = _dot(s["tinv"], stack(eye2 + s["pw"]))
        if pending:
            recurrence_step(pending.pop(0))
        span *= 2
    while pending:
        recurrence_step(pending.pop(0))
    for (q, d), h in hs.items():
        h_scr[q, d] = h
    for key in chains:
        s = st[key]
        s["w"] = -_dot(s["tinv"], stack(s["a_ak"]))
        s["kdp"] = _dot(s["tinv"], stack(s["kd"]))
    for key in chains:
        s = st[key]
        s["vst"] = stack(s["v"])
        s["u0"] = _dot(s["w"], s["vst"])
    for key in chains:
        s = st[key]
        lhs = jnp.concatenate([s["b_ra"], s["b_rk"]], axis=1)
        rhs = jnp.concatenate([stack(s["u0"]), s["vst"]], axis=0)
        o0_scr[key] = _dot(lhs, rhs)
        rdp_scr[key] = s["rd"] - _dot(s["b_ra"], stack(s["kdp"]))
        diag = jnp.where(r2 == c2, s["eg_last"], 0.0)
        m_scr[key] = jnp.where(same_head, diag - _dot(s["ad"].T, s["kdp"]), 0.0)
        at = jnp.concatenate([s["ad"], s["kdd"]], axis=0).T
        ha_scr[key] = jnp.where(same_head, _dot(at, jnp.concatenate([s["u0"], s["v"]], axis=0)), 0.0)


def rwkv_scan(lw, kda, rvkg, dm, bsz, t):
    n = lw.shape[0]
    pairs = dm // LANES
    L = RW_CHUNK
    nchunk = RW_PRE_CHUNKS
    tb = nchunk * L
    nblk, nctx = t // tb, CTX_LEN // tb

    def block(d, b, j):
        return b * nblk + _scan_chunk_index(d, jnp.minimum(j, nblk - 1), nctx, nblk)

    npair = RW_PAIRS_PER_STEP
    width = npair * LANES
    groups = pairs // npair

    def ispec(d, col):
        return pl.BlockSpec((tb, width), lambda b, p, j: (block(d, b, j), col * groups + p))

    def ospec(d):
        return pl.BlockSpec((tb, width), lambda b, p, j: (block(d, b, jnp.maximum(j - 1, 0)), p))

    return pl.pallas_call(
        functools.partial(_rw_scan_kernel, nchunk=nchunk, npair=npair),
        grid=(bsz, groups, nblk + 1),
        in_specs=[ispec(0, 0), ispec(1, 1), ispec(0, 0), ispec(1, 1), ispec(0, 2), ispec(1, 3),
                  ispec(0, 0), ispec(1, 0), ispec(0, 1), ispec(1, 1), ispec(0, 2), ispec(1, 2)],
        out_specs=[ospec(0), ospec(1)],
        out_shape=[jax.ShapeDtypeStruct((n, dm), BF16), jax.ShapeDtypeStruct((n, dm), BF16)],
        scratch_shapes=[pltpu.VMEM((npair, 2, LANES, LANES), F32), pltpu.VMEM((npair, 2, nchunk, L, LANES), F32),
                        pltpu.VMEM((npair, 2, nchunk, L, LANES), F32),
                        pltpu.VMEM((npair, 2, nchunk, LANES, LANES), F32),
                        pltpu.VMEM((npair, 2, nchunk, LANES, LANES), F32)],
        compiler_params=pltpu.CompilerParams(
            dimension_semantics=("arbitrary", "arbitrary", "arbitrary"), vmem_limit_bytes=VMEM_LIMIT),
        name="rwkv_scan",
    )(lw, lw, kda, kda, kda, kda, rvkg, rvkg, rvkg, rvkg, rvkg, rvkg)


HALO_ROWS = 8


def _group_sum(x, width):
    r = lax.broadcasted_iota(jnp.int32, (LANES, LANES), 0) // width
    c = lax.broadcasted_iota(jnp.int32, (LANES, LANES), 1) // width
    ones = (r == c).astype(BF16)
    hi = x.astype(BF16)
    lo = (x - hi.astype(F32)).astype(BF16)
    parts = []
    for j in range(x.shape[1] // LANES):
        sl = slice(j * LANES, (j + 1) * LANES)
        parts.append(jnp.dot(hi[:, sl], ones, preferred_element_type=F32)
                     + jnp.dot(lo[:, sl], ones, preferred_element_type=F32))
    return jnp.concatenate(parts, axis=1)


def _rw_proj_kernel(s_ref, up_ref, dn_ref, gain_ref, shift_ref, mu_ref, wrkv_ref, w1_ref, w2_ref, w0_ref,
                    a1_ref, a2_ref, a0_ref, g1_ref, g2_ref, kk_ref, ka_ref,
                    lw_ref, kda_ref, rvkg_ref, *, tpb, nctx_t):
    tm, dm = s_ref.shape
    ti = pl.program_id(0) % tpb
    is_ctx = ti < nctx_t
    has_up = jnp.logical_not(jnp.logical_or(is_ctx, ti == nctx_t))
    has_dn = jnp.logical_not(jnp.logical_or(is_ctx, ti == tpb - 1))
    gain = gain_ref[0]
    shift = shift_ref[0]
    u = _norm_mod(s_ref[...], gain, shift)
    u_up = jnp.where(has_up, _norm_mod(up_ref[HALO_ROWS - 1:HALO_ROWS, :], gain, shift), 0.0)
    u_dn = jnp.where(has_dn, _norm_mod(dn_ref[0:1, :], gain, shift), 0.0)
    row = lax.broadcasted_iota(jnp.int32, (tm, 1), 0)
    u_m = jnp.where(row == 0, u_up, pltpu.roll(u, 1, axis=0))
    u_p = jnp.where(row == tm - 1, u_dn, pltpu.roll(u, tm - 1, axis=0))
    du = 0.5 * (u_m + u_p) - u
    mu = mu_ref[...]

    def mix(i):
        return (u + du * mu[i:i + 1]).astype(BF16)

    def dot(a, b):
        return jnp.dot(a.astype(BF16), b, preferred_element_type=F32)

    r = dot(mix(0), wrkv_ref[0])
    k = dot(mix(1), wrkv_ref[1])
    v = dot(mix(2), wrkv_ref[2])
    w_pre = dot(jnp.tanh(dot(mix(3), w1_ref[...])), w2_ref[...]) + w0_ref[...]
    lw_ref[...] = -jnp.exp(-_softplus(-w_pre) - 0.5)
    a = jax.nn.sigmoid(dot(dot(mix(4), a1_ref[...]), a2_ref[...]) + a0_ref[...])
    g = dot(jax.nn.sigmoid(dot(mix(5), g1_ref[...])), g2_ref[...])
    kk = k * kk_ref[...]
    kk = kk * lax.rsqrt(jnp.maximum(_group_sum(kk * kk, RW_HEAD_DIM), 1e-24))
    ka = ka_ref[...]
    for d in range(2):
        kda_ref[:, d * dm:(d + 1) * dm] = (k * (1.0 + (a[:, d * dm:(d + 1) * dm] - 1.0) * ka)).astype(kda_ref.dtype)
    kda_ref[:, 2 * dm:] = a.astype(kda_ref.dtype)
    for j, val in enumerate((r, v, kk, g)):
        rvkg_ref[:, j * dm:(j + 1) * dm] = val.astype(rvkg_ref.dtype)


def rwkv_proj(s, gain, shift, geom, mu, w_rkv, w0, w1, w2, a0, a1, a2, g1, g2, k_k, k_a):
    n, dm = s.shape
    tpb, nctx_t = geom
    assert nctx_t == 1
    seg = _seg_map(tpb, nctx_t)
    hb = ROW_TILE // HALO_ROWS
    last = n // HALO_ROWS - 1
    lora = w1.shape[2]

    def blockdiag(w):
        z = jnp.zeros_like(w[0])
        return jnp.concatenate([jnp.concatenate([w[0], z], axis=1), jnp.concatenate([z, w[1]], axis=1)], axis=0)

    consts = [jnp.pad(mu, ((0, HALO_ROWS - mu.shape[0]), (0, 0))), w_rkv.astype(BF16),
              jnp.concatenate([w1[0], w1[1]], axis=1).astype(BF16), blockdiag(w2).astype(BF16),
              jnp.concatenate([w0[0], w0[1]])[None],
              jnp.concatenate([a1[0], a1[1]], axis=1).astype(BF16), blockdiag(a2).astype(BF16),
              jnp.concatenate([a0[0], a0[1]])[None],
              g1.astype(BF16), g2.astype(BF16), k_k[None], k_a[None]]

    def const_spec(x):
        nd = x.ndim
        return pl.BlockSpec(x.shape, lambda i: (0,) * nd)

    return pl.pallas_call(
        functools.partial(_rw_proj_kernel, tpb=tpb, nctx_t=nctx_t),
        grid=(n // ROW_TILE,),
        in_specs=[pl.BlockSpec((ROW_TILE, dm), lambda i: (i, 0)),
                  pl.BlockSpec((HALO_ROWS, dm), lambda i: (jnp.maximum(i * hb - 1, 0), 0)),
                  pl.BlockSpec((HALO_ROWS, dm), lambda i: (jnp.minimum((i + 1) * hb, last), 0)),
                  pl.BlockSpec((1, 1, dm), lambda i: (seg(i), 0, 0)),
                  pl.BlockSpec((1, 1, dm), lambda i: (seg(i), 0, 0))] + [const_spec(x) for x in consts],
        out_specs=[pl.BlockSpec((ROW_TILE, 2 * dm), lambda i: (i, 0)),
                   pl.BlockSpec((ROW_TILE, 4 * dm), lambda i: (i, 0)),
                   pl.BlockSpec((ROW_TILE, 4 * dm), lambda i: (i, 0))],
        out_shape=[jax.ShapeDtypeStruct((n, 2 * dm), F32), jax.ShapeDtypeStruct((n, 4 * dm), BF16),
                   jax.ShapeDtypeStruct((n, 4 * dm), BF16)],
        compiler_params=pltpu.CompilerParams(dimension_semantics=("arbitrary",), vmem_limit_bytes=VMEM_LIMIT_BIG),
        name="rwkv_proj",
    )(s, s, s, gain, shift, *consts)


def _rw_post_kernel(of_ref, ob_ref, r_ref, v_ref, g_ref, k0_ref, k1_ref, s_ref, lnw_ref, lnb_ref, rk_ref, gm_ref,
                    w_ref, out_ref, *, sub, seg):
    o = of_ref[...].astype(F32) + ob_ref[...].astype(F32)
    inv = 1.0 / RW_HEAD_DIM
    mean = _group_sum(o, RW_HEAD_DIM) * inv
    oc = o - mean
    var = _group_sum(oc * oc, RW_HEAD_DIM) * inv
    xn = oc * lax.rsqrt(var + RW_GN_EPS) * lnw_ref[...] + lnb_ref[...]
    r = r_ref[...].astype(F32)
    ksum = k0_ref[...].astype(F32) + k1_ref[...].astype(F32)
    bonus = _group_sum(r * ksum * rk_ref[...], RW_HEAD_DIM) * v_ref[...].astype(F32)
    y = ((xn + bonus) * g_ref[...].astype(F32)).astype(BF16)
    _gated_residual_store(out_ref, s_ref, gm_ref, jnp.dot(y, w_ref[...], preferred_element_type=F32), sub, seg)


def rwkv_post(o_f, o_b, rvkg, kda, s, ln_w, ln_b, r_k, gm, w_out, geom):
    n, dm = s.shape
    seg = _seg_map(*geom)
    sub = _sub_tiles(n, most=2)
    tm = sub * ROW_TILE

    def col(block):
        return pl.BlockSpec((tm, dm), lambda i: (i, block))

    vec = pl.BlockSpec((1, dm), lambda i: (0, 0))
    return pl.pallas_call(
        functools.partial(_rw_post_kernel, sub=sub, seg=seg),
        grid=(n // tm,),
        in_specs=[col(0), col(0), col(0), col(1), col(3), col(0), col(1),
                  col(0), vec, vec, vec, pl.BlockSpec(gm.shape, lambda i: (0, 0, 0)),
                  pl.BlockSpec((dm, dm), lambda i: (0, 0))],
        out_specs=pl.BlockSpec((tm, dm), lambda i: (i, 0)),
        out_shape=jax.ShapeDtypeStruct((n, dm), F32),
        compiler_params=pltpu.CompilerParams(dimension_semantics=("arbitrary",), vmem_limit_bytes=VMEM_LIMIT),
        name="rwkv_post",
    )(o_f, o_b, rvkg, rvkg, rvkg, kda, kda, s, ln_w[None], ln_b[None], r_k[None], gm, w_out.astype(BF16))


def _hg_scan_kernel(q_ref, v_ref, lf_ref, o_ref, st_scr, *, heads):
    C = q_ref.shape[0]
    d = pl.program_id(0)
    nsub = C // HG_SUB

    @pl.when(pl.program_id(2) == 0)
    def _():
        st_scr[...] = jnp.zeros_like(st_scr)

    def body(reverse):
        last = 0 if reverse else C - 1
        hs = range(heads)
        sls = [slice(h * LANES, (h + 1) * LANES) for h in hs]
        g = [lf_ref[:, sls[h]] for h in hs]
        b = [_cumsum_rows(g[h], reverse) for h in hs]
        k = [-jnp.tanh(0.5 * g[h]) * (jnp.exp(g[h]) + 1.0) for h in hs]
        o_inter = [_dot(q_ref[:, sls[h]].astype(F32) * jnp.exp(b[h]), st_scr[h], NT) for h in hs]
        parts = [[None] * nsub for _ in hs]
        for i in range(nsub):
            r0 = i * HG_SUB
            lo, hi = (r0, C) if reverse else (0, r0 + HG_SUB)
            first = r0 + HG_SUB - 1 if reverse else r0
            row = lax.broadcasted_iota(jnp.int32, (HG_SUB, hi - lo), 0) + r0
            col = lax.broadcasted_iota(jnp.int32, (HG_SUB, hi - lo), 1) + lo
            keep = (col >= row) if reverse else (col <= row)
            att = []
            for h in hs:
                rho = b[h][first:first + 1, :] - g[h][first:first + 1, :]
                qi = q_ref[r0:r0 + HG_SUB, sls[h]].astype(F32) * jnp.exp(b[h][r0:r0 + HG_SUB] - rho)
                ki = k[h][lo:hi] * jnp.exp(jnp.minimum(rho - b[h][lo:hi], HG_EXP_CLAMP))
                att.append(jnp.where(keep, _dot(qi, ki, NT), 0.0))
            for h in hs:
                parts[h][i] = _dot(att[h], v_ref[lo:hi, sls[h]])
        for h in hs:
            o_ref[0, :, sls[h]] = (o_inter[h] + jnp.concatenate(parts[h], axis=0)).astype(o_ref.dtype)
        upd = []
        for h in hs:
            b_last = b[h][last:last + 1, :]
            upd.append((jnp.exp(b_last),
                        _dot(v_ref[:, sls[h]].astype(F32).T, k[h] * jnp.exp(b_last - b[h]))))
        for h in hs:
            st_scr[h] = st_scr[h] * upd[h][0] + upd[h][1]

    @pl.when(d == 0)
    def _():
        body(False)

    @pl.when(d == 1)
    def _():
        body(True)


def hgrn_scan(z, logf, dm, bsz, t):
    n = z.shape[0]
    heads = dm // LANES
    C = HG_CHUNK
    nc, nctx = t // C, CTX_LEN // C

    def row(d, b, p):
        return b * nc + _scan_chunk_index(d, p, nctx, nc)

    return pl.pallas_call(
        functools.partial(_hg_scan_kernel, heads=heads),
        grid=(2, bsz, nc),
        in_specs=[pl.BlockSpec((C, dm), lambda d, b, p: (row(d, b, p), 0)),
                  pl.BlockSpec((C, dm), lambda d, b, p: (row(d, b, p), 1)),
                  pl.BlockSpec((C, dm), lambda d, b, p: (row(d, b, p), d))],
        out_specs=pl.BlockSpec((1, C, dm), lambda d, b, p: (d, row(d, b, p), 0)),
        out_shape=jax.ShapeDtypeStruct((2, n, dm), BF16),
        scratch_shapes=[pltpu.VMEM((heads, LANES, LANES), F32)],
        compiler_params=pltpu.CompilerParams(
            dimension_semantics=("arbitrary", "arbitrary", "arbitrary"), vmem_limit_bytes=VMEM_LIMIT),
        name="hgrn_scan",
    )(z, z, logf)


def _first_argmax(vals):
    best, idx = vals[0], jnp.zeros(vals[0].shape, jnp.int32)
    for i in range(1, len(vals)):
        better = vals[i] > best
        best = jnp.where(better, vals[i], best)
        idx = jnp.where(better, i, idx)
    return best, idx


def _router_kernel(s_ref, gain_ref, shift_ref, wt_ref, b_ref, e_ref, g_ref, rank_ref, cnt_ref, carry_scr, *,
                   n_groups, top_k):
    n_experts = wt_ref.shape[0]
    per = n_experts // n_groups
    h = _norm_mod(s_ref[...], gain_ref[0], shift_ref[0])
    aff = jax.nn.sigmoid(_dot(wt_ref[...], h, NT, passes=3))
    sel = aff + b_ref[...]
    a = [aff[e:e + 1, :] for e in range(n_experts)]
    s = [sel[e:e + 1, :] for e in range(n_experts)]
    neg = jnp.full_like(s[0], -jnp.inf)
    scores = []
    for g in range(n_groups):
        grp = s[g * per:(g + 1) * per]
        m1, i1 = _first_argmax(grp)
        m2, _ = _first_argmax([jnp.where(i1 == j, neg, grp[j]) for j in range(per)])
        scores.append(m1 + m2)
    _, best = _first_argmax(scores)

    def in_best(rows):
        out = []
        for j in range(per):
            x = rows[j]
            for g in range(1, n_groups):
                x = jnp.where(best == g, rows[g * per + j], x)
            out.append(x)
        return out

    sb, ab = in_best(s), in_best(a)
    picked, chosen = [], []
    cand = sb
    for _ in range(top_k):
        _, i = _first_argmax(cand)
        c = ab[0]
        for j in range(1, per):
            c = jnp.where(i == j, ab[j], c)
        picked.append(i)
        chosen.append(c)
        cand = [jnp.where(i == j, neg, cand[j]) for j in range(per)]
    total = functools.reduce(jnp.add, chosen)
    experts = [best * per + picked[kk_] for kk_ in range(top_k)]
    for kk_ in range(top_k):
        e_ref[kk_:kk_ + 1, :] = experts[kk_]
        g_ref[kk_:kk_ + 1, :] = chosen[kk_] / total

    @pl.when(pl.program_id(0) == 0)
    def _():
        carry_scr[...] = jnp.zeros_like(carry_scr)

    tm = s_ref.shape[0]
    sub = lax.broadcasted_iota(jnp.int32, (n_experts, tm), 0)
    onehots = [(sub == ex).astype(F32) for ex in experts]
    tot = functools.reduce(jnp.add, onehots)
    ri = lax.broadcasted_iota(jnp.int32, (tm, tm), 0)
    ci = lax.broadcasted_iota(jnp.int32, (tm, tm), 1)
    earlier = (ri < ci).astype(BF16)
    before = jnp.dot(tot.astype(BF16), earlier, preferred_element_type=F32) + carry_scr[:, 0:1]
    seen = before
    for kk_ in range(top_k):
        rank_ref[kk_:kk_ + 1, :] = jnp.sum(onehots[kk_] * seen, axis=0, keepdims=True).astype(jnp.int32)
        seen = seen + onehots[kk_]
    carry = carry_scr[...] + jnp.sum(tot, axis=1, keepdims=True)
    carry_scr[...] = carry
    cnt_ref[...] = carry.astype(jnp.int32)


def norm_route(s, gain, shift, geom, router_w, router_b):
    n, k = s.shape
    n_experts = router_w.shape[1]
    tm = ROW_TILE
    seg = _seg_map(*geom)
    return pl.pallas_call(
        functools.partial(_router_kernel, n_groups=N_GROUPS, top_k=TOP_K),
        grid=(n // tm,),
        in_specs=[pl.BlockSpec((tm, k), lambda i: (i, 0)),
                  pl.BlockSpec((1, 1, k), lambda i: (seg(i), 0, 0)),
                  pl.BlockSpec((1, 1, k), lambda i: (seg(i), 0, 0)),
                  pl.BlockSpec((n_experts, k), lambda i: (0, 0)),
                  pl.BlockSpec((n_experts, 1), lambda i: (0, 0))],
        out_specs=[pl.BlockSpec((TOP_K, tm), lambda i: (0, i)), pl.BlockSpec((TOP_K, tm), lambda i: (0, i)),
                   pl.BlockSpec((TOP_K, tm), lambda i: (0, i)), pl.BlockSpec((n_experts, LANES), lambda i: (0, 0))],
        out_shape=[jax.ShapeDtypeStruct((TOP_K, n), jnp.int32), jax.ShapeDtypeStruct((TOP_K, n), F32),
                   jax.ShapeDtypeStruct((TOP_K, n), jnp.int32), jax.ShapeDtypeStruct((n_experts, LANES), jnp.int32)],
        scratch_shapes=[pltpu.VMEM((n_experts, LANES), F32)],
        compiler_params=pltpu.CompilerParams(dimension_semantics=("arbitrary",), vmem_limit_bytes=VMEM_LIMIT),
        name="norm_route",
    )(s, gain, shift, router_w.T, router_b.reshape(n_experts, 1).astype(F32))


def _final_norm_kernel(s_ref, g_ref, o_ref):
    x = s_ref[...]
    o_ref[0] = x * lax.rsqrt(jnp.mean(x * x, axis=-1, keepdims=True) + NORM_EPS) * g_ref[...]


def final_norm(s, g, bsz, t, geom):
    dm = s.shape[1]
    tpb, nctx_t = geom
    return pl.pallas_call(
        _final_norm_kernel,
        grid=(bsz, tpb - nctx_t),
        in_specs=[pl.BlockSpec((ROW_TILE, dm), lambda b, i: (b * tpb + nctx_t + i, 0)),
                  pl.BlockSpec((1, dm), lambda b, i: (0, 0))],
        out_specs=pl.BlockSpec((1, ROW_TILE, dm), lambda b, i: (b, i, 0)),
        out_shape=jax.ShapeDtypeStruct((bsz, t - CTX_LEN, dm), F32),
        name="final_norm",
    )(s, g[None])


DMA_UNROLL = 8


def _row_copy_waits(src_row, dst_row, sem, count):
    def body(_, carry):
        pltpu.make_async_copy(src_row, dst_row, sem).wait()
        return carry
    lax.fori_loop(0, count, body, 0)


def _scatter_kernel(dest_ref, meta_ref, s_ref, gain_ref, shift_ref, xb_ref, hbuf, zrow, sems, zsem, *, n_experts):
    i = pl.program_id(0)
    last = pl.num_programs(0) - 1
    slot = i % 2
    per_tile = TOP_K * ROW_TILE

    def wait_tile(sl):
        for _ in range(TOP_K):
            pltpu.make_async_copy(hbuf.at[sl], xb_ref.at[pl.ds(0, ROW_TILE), :], sems.at[sl]).wait()

    @pl.when(i >= 2)
    def _():
        wait_tile(slot)

    hbuf[slot] = _norm_mod(s_ref[...], gain_ref[0], shift_ref[0])

    def issue(r8, carry):
        for u in range(DMA_UNROLL):
            r = r8 * DMA_UNROLL + u
            for k in range(TOP_K):
                d = dest_ref[0, 0, TOP_K * r + k]
                pltpu.make_async_copy(hbuf.at[slot, pl.ds(r, 1), :], xb_ref.at[pl.ds(d, 1), :],
                                      sems.at[slot]).start(priority=(TOP_K * u + k) % 2)
        return carry
    lax.fori_loop(0, ROW_TILE // DMA_UNROLL, issue, 0)

    @pl.when(i == last)
    def _():
        @pl.when(last >= 1)
        def _():
            wait_tile(1 - slot)
        wait_tile(slot)
        zrow[...] = jnp.zeros_like(zrow)
        for e in range(n_experts):
            lo = meta_ref[2, e] + meta_ref[0, e]
            hi = meta_ref[2, e] + meta_ref[1, e]

            def pad_start(q, carry):
                pltpu.make_async_copy(zrow.at[pl.ds(0, 1), :], xb_ref.at[pl.ds(q, 1), :], zsem.at[0]).start()
                return carry
            lax.fori_loop(lo, hi, pad_start, 0)
            _row_copy_waits(zrow.at[pl.ds(0, 1), :], xb_ref.at[pl.ds(0, 1), :], zsem.at[0], hi - lo)
        end = meta_ref[2, n_experts - 1] + meta_ref[1, n_experts - 1]

        def tail_start(q, carry):
            pltpu.make_async_copy(zrow.at[pl.ds(0, 1), :], xb_ref.at[pl.ds(q, 1), :], zsem.at[0]).start()
            return carry
        lax.fori_loop(end, xb_ref.shape[0], tail_start, 0)
        _row_copy_waits(zrow.at[pl.ds(0, 1), :], xb_ref.at[pl.ds(0, 1), :], zsem.at[0], xb_ref.shape[0] - end)


def moe_scatter(s, gain, shift, geom, dest, meta, n_slots):
    n, dm = s.shape
    seg = _seg_map(*geom)
    nt = n // ROW_TILE
    return pl.pallas_call(
        functools.partial(_scatter_kernel, n_experts=meta.shape[1]),
        grid=(nt,),
        in_specs=[pl.BlockSpec((1, 1, TOP_K * ROW_TILE), lambda i: (i, 0, 0), memory_space=pltpu.SMEM),
                  pl.BlockSpec(memory_space=pltpu.SMEM),
                  pl.BlockSpec((ROW_TILE, dm), lambda i: (i, 0)),
                  pl.BlockSpec((1, 1, dm), lambda i: (seg(i), 0, 0)),
                  pl.BlockSpec((1, 1, dm), lambda i: (seg(i), 0, 0))],
        out_specs=pl.BlockSpec(memory_space=pl.ANY),
        out_shape=jax.ShapeDtypeStruct((n_slots, dm), F32),
        scratch_shapes=[pltpu.VMEM((2, ROW_TILE, dm), F32), pltpu.VMEM((8, dm), F32),
                        pltpu.SemaphoreType.DMA((2,)), pltpu.SemaphoreType.DMA((1,))],
        compiler_params=pltpu.CompilerParams(dimension_semantics=("arbitrary",), vmem_limit_bytes=VMEM_LIMIT),
        name="moe_scatter",
    )(dest.reshape(nt, 1, TOP_K * ROW_TILE), meta, s, gain, shift)


def _gather_combine_kernel(dcur_ref, dnxt_ref, g_ref, s_ref, gm_ref, yb_ref, o_ref, ybuf, sems):
    i = pl.program_id(0)
    nsteps = pl.num_programs(0)
    slot = i % 2

    def start_tile(dref, sl):
        def issue(r8, carry):
            for u in range(DMA_UNROLL):
                r = r8 * DMA_UNROLL + u
                for k in range(TOP_K):
                    d = dref[0, 0, TOP_K * r + k]
                    pltpu.make_async_copy(yb_ref.at[pl.ds(d, 1), :], ybuf.at[sl, k, pl.ds(r, 1), :],
                                          sems.at[sl]).start(priority=(TOP_K * u + k) % 2)
            return carry
        lax.fori_loop(0, ROW_TILE // DMA_UNROLL, issue, 0)

    @pl.when(i == 0)
    def _():
        start_tile(dcur_ref, 0)

    @pl.when(i + 1 < nsteps)
    def _():
        start_tile(dnxt_ref, 1 - slot)

    for k in range(TOP_K):
        pltpu.make_async_copy(yb_ref.at[pl.ds(0, ROW_TILE), :], ybuf.at[slot, k], sems.at[slot]).wait()
    g = g_ref[...]
    y = sum(ybuf[slot, k] * g[:, k:k + 1] for k in range(TOP_K))
    o_ref[...] = s_ref[...] + gm_ref[0] * y


def moe_gather_combine(yb, dest, gate, s, gm, geom):
    n, dm = s.shape
    seg = _seg_map(*geom)
    nt = n // ROW_TILE
    d3 = dest.reshape(nt, 1, TOP_K * ROW_TILE)
    row = pl.BlockSpec((ROW_TILE, dm), lambda i: (i, 0))
    return pl.pallas_call(
        _gather_combine_kernel,
        grid=(nt,),
        in_specs=[pl.BlockSpec((1, 1, TOP_K * ROW_TILE), lambda i: (i, 0, 0), memory_space=pltpu.SMEM),
                  pl.BlockSpec((1, 1, TOP_K * ROW_TILE), lambda i: (jnp.minimum(i + 1, nt - 1), 0, 0),
                               memory_space=pltpu.SMEM),
                  pl.BlockSpec((ROW_TILE, TOP_K), lambda i: (i, 0)), row,
                  pl.BlockSpec((1, 1, dm), lambda i: (seg(i), 0, 0)),
                  pl.BlockSpec(memory_space=pl.ANY)],
        out_specs=row,
        out_shape=jax.ShapeDtypeStruct((n, dm), F32),
        scratch_shapes=[pltpu.VMEM((2, TOP_K, ROW_TILE, dm), F32), pltpu.SemaphoreType.DMA((2,))],
        compiler_params=pltpu.CompilerParams(dimension_semantics=("arbitrary",), vmem_limit_bytes=VMEM_LIMIT),
        name="moe_gather_combine",
    )(d3, d3, gate, s, gm, yb)


def _ffn_kernel(be_ref, x_ref, w1_ref, w3_ref, w2_ref, o_ref):
    del be_ref
    x = x_ref[...].astype(BF16)
    a = jnp.dot(x, w1_ref[0], preferred_element_type=F32)
    b = jnp.dot(x, w3_ref[0], preferred_element_type=F32)
    hid = (a * jax.nn.sigmoid(a) * b).astype(BF16)
    o_ref[...] = jnp.dot(hid, w2_ref[0], preferred_element_type=F32).astype(o_ref.dtype)


def expert_ffn(xb, block_expert, w1, w3, w2):
    nrows, dm = xb.shape
    f = w1.shape[2]
    nb = nrows // MOE_BLOCK
    return pl.pallas_call(
        _ffn_kernel,
        grid_spec=pltpu.PrefetchScalarGridSpec(
            num_scalar_prefetch=1,
            grid=(nb,),
            in_specs=[pl.BlockSpec((MOE_BLOCK, dm), lambda i, be: (i, 0)),
                      pl.BlockSpec((1, dm, f), lambda i, be: (be[i], 0, 0)),
                      pl.BlockSpec((1, dm, f), lambda i, be: (be[i], 0, 0)),
                      pl.BlockSpec((1, f, dm), lambda i, be: (be[i], 0, 0))],
            out_specs=pl.BlockSpec((MOE_BLOCK, dm), lambda i, be: (i, 0))),
        out_shape=jax.ShapeDtypeStruct((nrows, dm), F32),
        compiler_params=pltpu.CompilerParams(
            dimension_semantics=("arbitrary",), vmem_limit_bytes=VMEM_LIMIT),
        name="expert_ffn",
    )(block_expert, xb, w1.astype(BF16), w3.astype(BF16), w2.astype(BF16))


def _mlstm_layer(s, gain, shift, gate_mod, geom, bsz, t, w_in, w_gate, b_gate, conv, head_g, w_out):
    n, dm = s.shape
    heads = ML_HEADS
    z = norm_mod_mm(s, gain, shift, w_in, None, (None, None, None, "sigmoid"), geom)
    scale = jnp.concatenate([jnp.ones((dm,), F32), jnp.full((dm,), (dm // heads) ** -0.5, F32)])
    qk = conv_silu(z, conv, scale, 2 * dm, geom)
    ng = 4 * heads
    wg = jnp.pad(jnp.concatenate([w_gate[0], w_gate[1]], axis=1), ((0, 0), (0, LANES - ng)))
    bg = jnp.pad(jnp.concatenate([b_gate[0], b_gate[1]]), (0, LANES - ng))
    gates = norm_mod_mm(s, gain, shift, wg, bg, (None,), geom, out_dtype=F32)[:, :ng]
    gates = gates.reshape(bsz, t, 2, 2 * heads)
    gates = jnp.concatenate([gates[..., :heads], jax.nn.log_sigmoid(gates[..., heads:])], axis=-1)
    gc = jnp.moveaxis(gates, 2, 0).reshape(2, n, 2 * heads)
    gr = jnp.transpose(gates, (2, 0, 3, 1))
    h = mlstm_scan(qk, z, gc, gr, dm, bsz, t)
    return post_mm_residual(h, z, 3, s, head_g, gate_mod, w_out, heads, geom)


def _rwkv7_layer(s, gain, shift, gate_mod, geom, bsz, t, mu, w_rkv, w0, w1, w2, a0, a1, a2, g1, g2,
                 k_k, k_a, r_k, ln_w, ln_b, w_out):
    dm = s.shape[1]
    lw, kda, rvkg = rwkv_proj(s, gain, shift, geom, mu, w_rkv, w0, w1, w2, a0, a1, a2, g1, g2, k_k, k_a)
    o_f, o_b = rwkv_scan(lw, kda, rvkg, dm, bsz, t)
    return rwkv_post(o_f, o_b, rvkg, kda, s, ln_w, ln_b, r_k, gate_mod, w_out, geom)


def _hgrn2_layer(s, gain, shift, gate_mod, geom, bsz, t, layer_idx, w_in, w_f, b_f, lb_logits, head_g, w_out):
    dm = s.shape[1]
    z = norm_mod_mm(s, gain, shift, w_in, None, ("silu", None, "silu"), geom)
    p = jax.nn.softmax(lb_logits, axis=0)
    lb = jnp.cumsum(p, axis=0)[layer_idx] - p[0]
    aux = jnp.tile(jnp.stack([jnp.log(lb), jnp.log1p(-lb)]), (1, 2))
    log_f = norm_mod_mm(s, gain, shift, jnp.concatenate([w_f[0], w_f[1]], axis=1),
                        jnp.concatenate([b_f[0], b_f[1]]), ("logf", "logf"), geom, aux=aux, out_dtype=F32)
    o = hgrn_scan(z, log_f, dm, bsz, t)
    return post_mm_residual(o, z, 2, s, head_g, gate_mod, w_out, dm // HG_EXPAND, geom)


def _moe_layer(s, gain, shift, gate_mod, geom, router_w, router_b, w1, w3, w2):
    n_tok, d = s.shape
    n_experts = w1.shape[0]
    n_assign = n_tok * TOP_K
    e, g, rank, cnt = norm_route(s, gain, shift, geom, router_w, router_b)
    flat_e = e.T.reshape(n_assign)
    rank = rank.T.reshape(n_assign)
    counts = cnt[:, 0]
    padded = (counts + MOE_BLOCK - 1) // MOE_BLOCK * MOE_BLOCK
    end_pad = jnp.cumsum(padded)
    start_pad = end_pad - padded
    onehot = flat_e[:, None] == jnp.arange(n_experts, dtype=jnp.int32)[None, :]
    dest = jnp.sum(jnp.where(onehot, start_pad[None, :], 0), axis=1) + rank
    n_blocks = -(-n_assign // MOE_BLOCK) + n_experts
    block_start = jnp.arange(n_blocks, dtype=jnp.int32) * MOE_BLOCK
    block_expert = jnp.minimum(jnp.sum(end_pad[None, :] <= block_start[:, None], axis=1), n_experts - 1)
    meta = jnp.stack([counts, padded, start_pad]).astype(jnp.int32)
    xb = moe_scatter(s, gain, shift, geom, dest.astype(jnp.int32), meta, n_blocks * MOE_BLOCK)
    yb = expert_ffn(xb, block_expert.astype(jnp.int32), w1, w3, w2)
    return moe_gather_combine(yb, dest.astype(jnp.int32), g.T, s, gate_mod, geom)


def kernel(x, c, ctx, c_ctx, ada_w, ada_b, norm_mix, norm_ffn, norm_out, ml_w_in, ml_w_gate, ml_b_gate, ml_conv, ml_head_g, ml_w_out, rw_mu, rw_w_rkv, rw_w0, rw_w1, rw_w2, rw_a0, rw_a1, rw_a2, rw_g1, rw_g2, rw_k_k, rw_k_a, rw_r_k, rw_ln_w, rw_ln_b, rw_w_out, hg_w_in, hg_w_f, hg_b_f, hg_lb_logits, hg_head_g, hg_w_out, router_w, router_b, ex_w1, ex_w3, ex_w2):
    depth = ada_w.shape[0]
    bsz = x.shape[0]
    cond = jax.nn.silu(jnp.concatenate([c, c_ctx[None]], axis=0))
    cond = jnp.pad(cond, ((0, -(bsz + 1) % 8), (0, 0)))
    dm = x.shape[2]
    t = CTX_LEN + x.shape[1]
    n = bsz * t
    geom = (t // ROW_TILE, CTX_LEN // ROW_TILE)
    s = jnp.concatenate([ctx, x], axis=1).reshape(n, dm)
    for i in range(depth):
        mod = mm(cond, ada_w[i], bias=ada_b[i])
        mod_x = jnp.split(mod[:bsz, None, :], 6, axis=-1)
        mod_c = jnp.split(mod[bsz], 6, axis=-1)

        def table(idx):
            return jnp.stack([jnp.broadcast_to(mod_c[idx], (bsz, dm)), mod_x[idx][:, 0]], axis=1).reshape(2 * bsz, 1, dm)

        kind, j = i % N_MIXERS, i // N_MIXERS
        if kind == 2:
            s = _hgrn2_layer(s, norm_mix[i] * (1 + table(1)), table(0), table(2), geom, bsz, t, i,
                             hg_w_in[j], hg_w_f[j], hg_b_f[j], hg_lb_logits, hg_head_g[j], hg_w_out[j])
        elif kind == 0:
            s = _mlstm_layer(s, norm_mix[i] * (1 + table(1)), table(0), table(2), geom, bsz, t,
                             ml_w_in[j], ml_w_gate[j], ml_b_gate[j], ml_conv[j], ml_head_g[j], ml_w_out[j])
        else:
            s = _rwkv7_layer(s, norm_mix[i] * (1 + table(1)), table(0), table(2), geom, bsz, t,
                             rw_mu[j], rw_w_rkv[j], rw_w0[j], rw_w1[j], rw_w2[j], rw_a0[j],
                             rw_a1[j], rw_a2[j], rw_g1[j], rw_g2[j], rw_k_k[j], rw_k_a[j],
                             rw_r_k[j], rw_ln_w[j], rw_ln_b[j], rw_w_out[j])
        s = _moe_layer(s, norm_ffn[i] * (1 + table(4)), table(3), table(5), geom, router_w, router_b,
                       ex_w1[i], ex_w3[i], ex_w2[i])
    return final_norm(s, norm_out, bsz, t, geom)
```

```python
import functools

import jax
import jax.numpy as jnp
from jax import lax
from jax.experimental import pallas as pl
from jax.experimental.pallas import tpu as pltpu

F32 = jnp.float32
BF16 = jnp.bfloat16

GRID_W = 64
CTX_LEN = 256
N_MIXERS = 3
NORM_EPS = 1e-6
ML_HEADS = 8
RW_HEAD_DIM = 64
RW_GN_EPS = 64e-5
HG_EXPAND = 128
N_GROUPS = 4
TOP_K = 2
MOE_BLOCK = 512

LANES = 128
ML_CHUNK = 128
RW_CHUNK = 64
RW_PRE_CHUNKS = 4
RW_PAIRS_PER_STEP = 4
HG_CHUNK = 128
HG_SUB = 16
HG_EXP_CLAMP = 80.0
VMEM_LIMIT = 48 * 1024 * 1024
VMEM_LIMIT_BIG = 56 * 1024 * 1024

NT = (((1,), (1,)), ((), ()))
NN = (((1,), (0,)), ((), ()))


def _dot(a, b, dims=NN, passes=1):
    a_hi = a.astype(BF16)
    b_hi = b.astype(BF16)
    out = lax.dot_general(a_hi, b_hi, dims, preferred_element_type=F32)
    if passes == 3:
        a_lo = (a - a_hi.astype(F32)).astype(BF16)
        b_lo = (b - b_hi.astype(F32)).astype(BF16)
        out = out + lax.dot_general(a_hi, b_lo, dims, preferred_element_type=F32)
        out = out + lax.dot_general(a_lo, b_hi, dims, preferred_element_type=F32)
    return out


def _dot_exact01(a, b, lhs01=False):
    x = (b if lhs01 else a).astype(F32)
    out = None
    for _ in range(3):
        t = x.astype(BF16)
        x = x - t.astype(F32)
        p = lax.dot_general(a, t, NN, preferred_element_type=F32) if lhs01 else \
            lax.dot_general(t, b, NN, preferred_element_type=F32)
        out = p if out is None else out + p
    return out


def _cumsum_rows(x, reverse):
    n = x.shape[0]
    row = lax.broadcasted_iota(jnp.int32, x.shape, 0)
    s = 1
    while s < n:
        if reverse:
            x = x + jnp.where(row < n - s, pltpu.roll(x, n - s, axis=0), 0.0)
        else:
            x = x + jnp.where(row >= s, pltpu.roll(x, s, axis=0), 0.0)
        s *= 2
    return x


def _pick_tile(n, candidates):
    for c in candidates:
        if n % c == 0:
            return c
    raise ValueError(f"no tile for {n}")


def _scan_chunk_index(d, p, nctx, nc):
    rev = jnp.where(p < nctx, nctx - 1 - p, nc - 1 - (p - nctx))
    return jnp.where(d == 0, p, rev)


_ACTS = {
    None: lambda y: y,
    "sigmoid": jax.nn.sigmoid,
    "silu": lambda y: y * jax.nn.sigmoid(y),
    "tanh": jnp.tanh,
}


def _mm_kernel(x_ref, w_ref, b_ref, o_ref, *, act, precise):
    if precise:
        y = _dot(x_ref[...], w_ref[...], passes=3)
    else:
        y = jnp.dot(x_ref[...].astype(BF16), w_ref[...], preferred_element_type=F32)
    o_ref[...] = _ACTS[act](y + b_ref[...]).astype(o_ref.dtype)


def mm(x, w, bias=None, act=None, out_dtype=F32, precise=False):
    n, k = x.shape
    m = w.shape[1]
    tm = _pick_tile(n, (512, 256, 128, 64, 32, 16, 8))
    tn = m if m <= 1024 else _pick_tile(m, (1024, 512, 256, 128))
    if not precise:
        w = w.astype(BF16)
    if bias is None:
        bias = jnp.zeros((m,), F32)
    return pl.pallas_call(
        functools.partial(_mm_kernel, act=act, precise=precise),
        grid=(n // tm, m // tn),
        in_specs=[pl.BlockSpec((tm, k), lambda i, j: (i, 0)),
                  pl.BlockSpec((k, tn), lambda i, j: (0, j)),
                  pl.BlockSpec((1, tn), lambda i, j: (0, j))],
        out_specs=pl.BlockSpec((tm, tn), lambda i, j: (i, j)),
        out_shape=jax.ShapeDtypeStruct((n, m), out_dtype),
        compiler_params=pltpu.CompilerParams(vmem_limit_bytes=VMEM_LIMIT),
        name="mm",
    )(x, w, bias.reshape(1, m).astype(F32))


ROW_TILE = 256


def _log1p_exp_neg_abs(x):
    return jnp.log(1.0 + jnp.exp(-jnp.abs(x)))


def _log_sigmoid(y):
    return jnp.minimum(y, 0.0) - _log1p_exp_neg_abs(y)


def _softplus(x):
    return jnp.maximum(x, 0.0) + _log1p_exp_neg_abs(x)


def _logaddexp(a, b):
    return jnp.maximum(a, b) + _log1p_exp_neg_abs(a - b)


def _norm_mod(x, gain, shift):
    return x * lax.rsqrt(jnp.mean(x * x, axis=-1, keepdims=True) + NORM_EPS) * gain + shift


def _seg_map(tpb, nctx_t):
    def seg(i):
        return (i // tpb) * 2 + jnp.where(i % tpb < nctx_t, 0, 1)
    return seg


_EPILOGUES = {
    None: lambda y, aux: y,
    "sigmoid": lambda y, aux: jax.nn.sigmoid(y),
    "silu": lambda y, aux: y * jax.nn.sigmoid(y),
    "logf": lambda y, aux: _logaddexp(aux[0:1, :], aux[1:2, :] + _log_sigmoid(y)),
}


def _sub_tiles(n, most=4):
    return _pick_tile(n // ROW_TILE, tuple(range(most, 0, -1)))


def _nmm_kernel(s_ref, gain_ref, shift_ref, w_ref, b_ref, aux_ref, o_ref, h_scr, *, acts, sub, seg):
    j = pl.program_id(1)

    @pl.when(j == 0)
    def _():
        for k in range(sub):
            rows = pl.ds(k * ROW_TILE, ROW_TILE)
            sk = seg(pl.program_id(0) * sub + k)
            h_scr[rows, :] = _norm_mod(s_ref[rows, :], gain_ref[sk], shift_ref[sk]).astype(BF16)

    y = jnp.dot(h_scr[...], w_ref[...], preferred_element_type=F32) + b_ref[...]
    for act in sorted(set(acts), key=str):
        cols = [jj for jj, a in enumerate(acts) if a == act]
        if len(cols) == len(acts):
            o_ref[...] = _EPILOGUES[act](y, aux_ref[...]).astype(o_ref.dtype)
        else:
            @pl.when(functools.reduce(jnp.logical_or, [j == jj for jj in cols]))
            def _(act=act):
                o_ref[...] = _EPILOGUES[act](y, aux_ref[...]).astype(o_ref.dtype)


def norm_mod_mm(s, gain, shift, w, bias, acts, geom, aux=None, out_dtype=None):
    out_dtype = out_dtype or BF16
    n, k = s.shape
    m = w.shape[1]
    tn = m // len(acts)
    tpb, nctx_t = geom
    seg = _seg_map(tpb, nctx_t)
    sub = _sub_tiles(n)
    tm = sub * ROW_TILE
    if bias is None:
        bias = jnp.zeros((m,), F32)
    if aux is None:
        aux = jnp.zeros((2, m), F32)
    return pl.pallas_call(
        functools.partial(_nmm_kernel, acts=tuple(acts), sub=sub, seg=seg),
        grid=(n // tm, m // tn),
        in_specs=[pl.BlockSpec((tm, k), lambda i, j: (i, 0)),
                  pl.BlockSpec(gain.shape, lambda i, j: (0, 0, 0)),
                  pl.BlockSpec(shift.shape, lambda i, j: (0, 0, 0)),
                  pl.BlockSpec((k, tn), lambda i, j: (0, j)),
                  pl.BlockSpec((1, tn), lambda i, j: (0, j)),
                  pl.BlockSpec((2, tn), lambda i, j: (0, j))],
        out_specs=pl.BlockSpec((tm, tn), lambda i, j: (i, j)),
        out_shape=jax.ShapeDtypeStruct((n, m), out_dtype),
        scratch_shapes=[pltpu.VMEM((tm, k), BF16)],
        compiler_params=pltpu.CompilerParams(
            dimension_semantics=("arbitrary", "arbitrary"), vmem_limit_bytes=VMEM_LIMIT),
        name="norm_mod_mm",
    )(s, gain, shift, w.astype(BF16), bias.reshape(1, m).astype(F32), aux.astype(F32))


def _gated_residual_store(o_ref, s_ref, gm_ref, y, sub, seg):
    for k in range(sub):
        rows = pl.ds(k * ROW_TILE, ROW_TILE)
        gm = gm_ref[seg(pl.program_id(0) * sub + k)]
        o_ref[rows, :] = s_ref[rows, :] + gm * y[k * ROW_TILE:(k + 1) * ROW_TILE]


def _post_kernel(h_ref, g_ref, s_ref, hg_ref, gm_ref, w_ref, o_ref, *, heads, sub, seg):
    x = h_ref[0].astype(F32) + h_ref[1].astype(F32)
    hd = x.shape[1] // heads
    parts = []
    for h in range(heads):
        xh = x[:, h * hd:(h + 1) * hd]
        parts.append(xh * lax.rsqrt(jnp.mean(xh * xh, axis=-1, keepdims=True) + NORM_EPS))
    y = (jnp.concatenate(parts, axis=1) * hg_ref[...] * g_ref[...].astype(F32)).astype(BF16)
    _gated_residual_store(o_ref, s_ref, gm_ref, jnp.dot(y, w_ref[...], preferred_element_type=F32), sub, seg)


def post_mm_residual(h2, gate_arr, gate_block, s, head_g, gm, w_out, heads, geom):
    n, dm = s.shape
    seg = _seg_map(*geom)
    sub = _sub_tiles(n, most=2)
    tm = sub * ROW_TILE
    return pl.pallas_call(
        functools.partial(_post_kernel, heads=heads, sub=sub, seg=seg),
        grid=(n // tm,),
        in_specs=[pl.BlockSpec((2, tm, dm), lambda i: (0, i, 0)),
                  pl.BlockSpec((tm, dm), lambda i: (i, gate_block)),
                  pl.BlockSpec((tm, dm), lambda i: (i, 0)),
                  pl.BlockSpec((1, dm), lambda i: (0, 0)),
                  pl.BlockSpec(gm.shape, lambda i: (0, 0, 0)),
                  pl.BlockSpec((dm, dm), lambda i: (0, 0))],
        out_specs=pl.BlockSpec((tm, dm), lambda i: (i, 0)),
        out_shape=jax.ShapeDtypeStruct((n, dm), F32),
        compiler_params=pltpu.CompilerParams(dimension_semantics=("arbitrary",), vmem_limit_bytes=VMEM_LIMIT),
        name="post_mm_residual",
    )(h2, gate_arr, s, head_g.reshape(1, dm), gm, w_out.astype(BF16))


CONV_COLS = 512


def _conv_kernel(cur_ref, up_ref, dn_ref, w_ref, sc_ref, o_ref, *, tpb, nctx_t):
    ti = pl.program_id(0) % tpb
    is_ctx = ti < nctx_t
    no_up = jnp.logical_or(is_ctx, ti == nctx_t)
    no_dn = jnp.logical_or(is_ctx, ti == tpb - 1)
    x = cur_ref[...].astype(F32)
    up = jnp.where(no_up, 0.0, up_ref[...].astype(F32))
    dn = jnp.where(no_dn, 0.0, dn_ref[...].astype(F32))
    ext = jnp.concatenate([up, x, dn], axis=0)
    tpos = lax.broadcasted_iota(jnp.int32, (ROW_TILE, 1), 0)
    col = tpos % GRID_W
    left_ok = jnp.where(is_ctx, (tpos > 0).astype(F32), (col > 0).astype(F32))
    right_ok = jnp.where(is_ctx, (tpos < ROW_TILE - 1).astype(F32), (col < GRID_W - 1).astype(F32))
    vert = jnp.where(is_ctx, 0.0, 1.0)
    w = w_ref[...]
    sums = [None, None, None]
    for dr in (-1, 0, 1):
        base = GRID_W * (1 + dr)
        wr = w[3 * (dr + 1):3 * (dr + 2)] * (1.0 if dr == 0 else vert)
        for dc in range(3):
            term = ext[base:base + ROW_TILE] * wr[dc:dc + 1]
            sums[dc] = term if sums[dc] is None else sums[dc] + term
    acc = (sums[1] + left_ok * pltpu.roll(sums[0], 1, axis=0)
           + right_ok * pltpu.roll(sums[2], ROW_TILE - 1, axis=0))
    o_ref[...] = (acc * jax.nn.sigmoid(acc) * sc_ref[...]).astype(o_ref.dtype)


def conv_silu(z, conv_w, scale, width, geom):
    n = z.shape[0]
    tpb, nctx_t = geom
    assert nctx_t == 1 and ROW_TILE % GRID_W == 0
    hb = ROW_TILE // GRID_W
    last = n // GRID_W - 1
    return pl.pallas_call(
        functools.partial(_conv_kernel, tpb=tpb, nctx_t=nctx_t),
        grid=(n // ROW_TILE, width // CONV_COLS),
        in_specs=[pl.BlockSpec((ROW_TILE, CONV_COLS), lambda i, c: (i, c)),
                  pl.BlockSpec((GRID_W, CONV_COLS), lambda i, c: (jnp.maximum(i * hb - 1, 0), c)),
                  pl.BlockSpec((GRID_W, CONV_COLS), lambda i, c: (jnp.minimum((i + 1) * hb, last), c)),
                  pl.BlockSpec((9, CONV_COLS), lambda i, c: (0, c)),
                  pl.BlockSpec((1, CONV_COLS), lambda i, c: (0, c))],
        out_specs=pl.BlockSpec((ROW_TILE, CONV_COLS), lambda i, c: (i, c)),
        out_shape=jax.ShapeDtypeStruct((n, width), BF16),
        compiler_params=pltpu.CompilerParams(
            dimension_semantics=("arbitrary", "arbitrary"), vmem_limit_bytes=VMEM_LIMIT),
        name="conv_silu",
    )(z, z, z, conv_w.reshape(9, width).astype(F32), scale.reshape(1, width).astype(F32))


def _cummax_rows(x, reverse):
    n = x.shape[0]
    row = lax.broadcasted_iota(jnp.int32, x.shape, 0)
    s = 1
    while s < n:
        if reverse:
            x = jnp.maximum(x, jnp.where(row < n - s, pltpu.roll(x, n - s, axis=0), -jnp.inf))
        else:
            x = jnp.maximum(x, jnp.where(row >= s, pltpu.roll(x, s, axis=0), -jnp.inf))
        s *= 2
    return x


def _ml_scan_kernel(q_ref, k_ref, v_ref, gc_ref, gr_ref, o_ref, z_scr, m_scr, *, heads):
    L = q_ref.shape[0]
    assert L == LANES
    d = pl.program_id(0)

    @pl.when(pl.program_id(2) == 0)
    def _():
        z_scr[...] = jnp.zeros_like(z_scr)
        m_scr[...] = jnp.zeros_like(m_scr)

    row = lax.broadcasted_iota(jnp.int32, (L, L), 0)
    col = lax.broadcasted_iota(jnp.int32, (L, L), 1)
    ones_blk = jnp.ones((L, LANES), BF16)

    def body(reverse):
        incl = (col >= row) if reverse else (col <= row)
        incl_t = (row >= col) if reverse else (row <= col)
        last = 0 if reverse else L - 1
        hs = range(heads)
        sls = [slice(h * LANES, (h + 1) * LANES) for h in hs]
        qk = [_dot(q_ref[:, sls[h]], k_ref[:, sls[h]], NT) for h in hs]
        qz = [_dot(q_ref[:, sls[h]], z_scr[h]) for h in hs]
        b_cols = _dot_exact01(incl.astype(BF16), gc_ref[0, :, heads:2 * heads], lhs01=True)
        b_rows = _dot_exact01(gr_ref[0, 0, heads:2 * heads, :], incl_t.astype(BF16))
        cols = jnp.concatenate([b_cols, gc_ref[0, :, 0:heads]], axis=1)
        pick = lax.broadcasted_iota(jnp.int32, (2 * heads, 2 * LANES), 0)
        lane2 = lax.broadcasted_iota(jnp.int32, (2 * heads, 2 * LANES), 1)
        stats = []
        for h in hs:
            sel = (pick == jnp.where(lane2 < LANES, h, heads + h)).astype(BF16)
            rep = _dot_exact01(cols, sel)
            b_rep, ig_rep = rep[:, :LANES], rep[:, LANES:]
            ig_row = gr_ref[0, 0, h:h + 1, :]
            b_row = b_rows[h:h + 1, :]
            m_prev = m_scr[h:h + 1, :]
            cmax = _cummax_rows(ig_rep - b_rep, reverse)
            dmat = jnp.where(incl, b_rep - (b_row - ig_row), -jnp.inf)
            inter = b_rep + m_prev
            m_t = jnp.maximum(inter, b_rep + cmax)
            b_last = b_rep[last:last + 1, :]
            m_new = jnp.maximum(b_last + m_prev, b_last + cmax[last:last + 1, :])
            w_k = jnp.exp(b_last - b_rep + ig_rep - m_new)
            w_prev = jnp.exp(b_last + m_prev - m_new)
            stats.append((jnp.exp(dmat - m_t), jnp.exp(inter - m_t), jnp.exp(-m_t), w_k, w_prev, m_new))
        kv = []
        for h in hs:
            w_k = stats[h][3]
            wv = jnp.concatenate([w_k * v_ref[:, sls[h]].astype(F32), w_k], axis=1)
            kv.append(_dot(k_ref[:, sls[h]].astype(F32).T, wv))
        s = [qk[h] * stats[h][0] for h in hs]
        sv = [_dot(s[h], jnp.concatenate([v_ref[:, sls[h]], ones_blk], axis=1)) for h in hs]
        for h in hs:
            _, w_inter, floor, _, w_prev, m_new = stats[h]
            w2 = jnp.concatenate([w_inter, w_inter], axis=1)
            full = sv[h] + w2 * qz[h]
            den = full[:, LANES:]
            o_ref[0, :, sls[h]] = (full[:, :LANES] / jnp.maximum(jnp.abs(den), floor)).astype(o_ref.dtype)
            z_scr[h] = jnp.concatenate([w_prev, w_prev], axis=1) * z_scr[h] + kv[h]
            m_scr[h:h + 1, :] = m_new

    @pl.when(d == 0)
    def _():
        body(False)

    @pl.when(d == 1)
    def _():
        body(True)


def mlstm_scan(qk, z, gc, gr, dm, bsz, t):
    n = qk.shape[0]
    heads = dm // LANES
    L = ML_CHUNK
    nc, nctx = t // L, CTX_LEN // L

    def row(d, b, p):
        return b * nc + _scan_chunk_index(d, p, nctx, nc)

    return pl.pallas_call(
        functools.partial(_ml_scan_kernel, heads=heads),
        grid=(2, bsz, nc),
        in_specs=[pl.BlockSpec((L, dm), lambda d, b, p: (row(d, b, p), 0)),
                  pl.BlockSpec((L, dm), lambda d, b, p: (row(d, b, p), 1)),
                  pl.BlockSpec((L, dm), lambda d, b, p: (row(d, b, p), 2)),
                  pl.BlockSpec((1, L, 2 * heads), lambda d, b, p: (d, row(d, b, p), 0)),
                  pl.BlockSpec((1, 1, 2 * heads, L),
                               lambda d, b, p: (d, b, 0, _scan_chunk_index(d, p, nctx, nc)))],
        out_specs=pl.BlockSpec((1, L, dm), lambda d, b, p: (d, row(d, b, p), 0)),
        out_shape=jax.ShapeDtypeStruct((2, n, dm), BF16),
        scratch_shapes=[pltpu.VMEM((heads, LANES, 2 * LANES), F32), pltpu.VMEM((heads, LANES), F32)],
        compiler_params=pltpu.CompilerParams(
            dimension_semantics=("arbitrary", "arbitrary", "arbitrary"), vmem_limit_bytes=VMEM_LIMIT),
        name="mlstm_scan",
    )(qk, qk, z, gc, gr)


RW_STATE_PASSES = 3


def _rw_scan_kernel(lw0_ref, lw1_ref, kd0_ref, kd1_ref, a0_ref, a1_ref, r0_ref, r1_ref, v0_ref, v1_ref,
                    kk0_ref, kk1_ref, of_ref, ob_ref, h_scr, rdp_scr, o0_scr, m_scr, ha_scr, *, nchunk, npair):
    L = RW_CHUNK
    j = pl.program_id(2)

    @pl.when(j == 0)
    def _():
        for ref in (h_scr, rdp_scr, o0_scr, m_scr, ha_scr):
            ref[...] = jnp.zeros_like(ref)

    qls = [slice(q * LANES, (q + 1) * LANES) for q in range(npair)]
    hs = {(q, d): h_scr[q, d] for q in range(npair) for d in range(2)}

    def recurrence_step(k):
        for q in range(npair):
            for d, o_ref in ((0, of_ref), (1, ob_ref)):
                c = k if d == 0 else nchunk - 1 - k
                o_ref[pl.ds(c * L, L), qls[q]] = (_dot(rdp_scr[q, d, c], hs[q, d], passes=RW_STATE_PASSES)
                                                  + o0_scr[q, d, c]).astype(o_ref.dtype)
                hs[q, d] = _dot(m_scr[q, d, c], hs[q, d], passes=RW_STATE_PASSES) + ha_scr[q, d, c]

    pending = list(range(nchunk))

    half = LANES // 2
    row = lax.broadcasted_iota(jnp.int32, (L, LANES), 0)
    col = lax.broadcasted_iota(jnp.int32, (L, LANES), 1) % half
    eye2 = (row == col).astype(F32)
    lane = lax.broadcasted_iota(jnp.int32, (1, LANES), 1)
    m0 = (lane < half).astype(BF16)
    m1 = (lane >= half).astype(BF16)
    r2 = lax.broadcasted_iota(jnp.int32, (LANES, LANES), 0)
    c2 = lax.broadcasted_iota(jnp.int32, (LANES, LANES), 1)
    same_head = (r2 // half) == (c2 // half)

    def stack(x):
        xb = x.astype(BF16)
        return jnp.concatenate([xb * m0, xb * m1], axis=0)

    chains = [(q, d, c) for c in range(nchunk) for q in range(npair) for d in range(2)]
    st = {}
    for q, d, c in chains:
        reverse = d == 1
        sl = pl.ds(c * L, L)
        lw = (lw0_ref, lw1_ref)[d][sl, qls[q]]
        k = (kd0_ref, kd1_ref)[d][sl, qls[q]].astype(F32)
        kk = (kk0_ref, kk1_ref)[d][sl, qls[q]].astype(F32)
        akk = kk * (a0_ref, a1_ref)[d][sl, qls[q]].astype(F32)
        g = _cumsum_rows(lw, reverse)
        ieg = jnp.exp(-g)
        g_last = g[0:1] if reverse else g[L - 1:L]
        dl = jnp.exp(g_last - g)
        st[q, d, c] = dict(kd=kk * jnp.exp(g - lw), rd=(r0_ref, r1_ref)[d][sl, qls[q]].astype(F32) * jnp.exp(g),
                           ai=akk * ieg, ki=k * ieg, ad=akk * dl, kdd=k * dl, eg_last=jnp.exp(g_last),
                           v=(v0_ref, v1_ref)[d][sl, qls[q]].astype(F32))
    recurrence_step(pending.pop(0))
    for q, d, c in chains:
        s = st[q, d, c]
        reverse = d == 1
        incl = (col >= row) if reverse else (col <= row)
        strict = (col > row) if reverse else (col < row)
        x = jnp.concatenate([s["kd"], s["rd"]], axis=0)
        rhs = jnp.concatenate([stack(s["ai"]), stack(s["ki"])], axis=0)
        sc = _dot(x, rhs, NT)
        s["a_ab"] = jnp.where(strict, sc[:L, :LANES], 0.0)
        s["a_ak"] = jnp.where(strict, sc[:L, LANES:], 0.0)
        s["b_ra"] = jnp.where(incl, sc[L:, :LANES], 0.0)
        s["b_rk"] = jnp.where(incl, sc[L:, LANES:], 0.0)
        s["tinv"] = eye2 - s["a_ab"]
        s["pw"] = s["a_ab"]
    span = 2
    while span < L:
        for key in chains:
            s = st[key]
            s["pw"] = _dot(s["pw"], stack(s["pw"]))
        for key in chains:
            s = st[key]
            s["tinv"] = _dot(s["tinv"], stack(eye2 + s["pw"]))
        if pending:
            recurrence_step(pending.pop(0))
        span *= 2
    while pending:
        recurrence_step(pending.pop(0))
    for (q, d), h in hs.items():
        h_scr[q, d] = h
    for key in chains:
        s = st[key]
        s["w"] = -_dot(s["tinv"], stack(s["a_ak"]))
        s["kdp"] = _dot(s["tinv"], stack(s["kd"]))
    for key in chains:
        s = st[key]
        s["vst"] = stack(s["v"])
        s["u0"] = _dot(s["w"], s["vst"])
    for key in chains:
        s = st[key]
        lhs = jnp.concatenate([s["b_ra"], s["b_rk"]], axis=1)
        rhs = jnp.concatenate([stack(s["u0"]), s["vst"]], axis=0)
        o0_scr[key] = _dot(lhs, rhs)
        rdp_scr[key] = s["rd"] - _dot(s["b_ra"], stack(s["kdp"]))
        diag = jnp.where(r2 == c2, s["eg_last"], 0.0)
        m_scr[key] = jnp.where(same_head, diag - _dot(s["ad"].T, s["kdp"]), 0.0)
        at = jnp.concatenate([s["ad"], s["kdd"]], axis=0).T
        ha_scr[key] = jnp.where(same_head, _dot(at, jnp.concatenate([s["u0"], s["v"]], axis=0)), 0.0)


def rwkv_scan(lw, kda, rvkg, dm, bsz, t):
    n = lw.shape[0]
    pairs = dm // LANES
    L = RW_CHUNK
    nchunk = RW_PRE_CHUNKS
    tb = nchunk * L
    nblk, nctx = t // tb, CTX_LEN // tb

    def block(d, b, j):
        return b * nblk + _scan_chunk_index(d, jnp.minimum(j, nblk - 1), nctx, nblk)

    npair = RW_PAIRS_PER_STEP
    width = npair * LANES
    groups = pairs // npair

    def ispec(d, col):
        return pl.BlockSpec((tb, width), lambda b, p, j: (block(d, b, j), col * groups + p))

    def ospec(d):
        return pl.BlockSpec((tb, width), lambda b, p, j: (block(d, b, jnp.maximum(j - 1, 0)), p))

    return pl.pallas_call(
        functools.partial(_rw_scan_kernel, nchunk=nchunk, npair=npair),
        grid=(bsz, groups, nblk + 1),
        in_specs=[ispec(0, 0), ispec(1, 1), ispec(0, 0), ispec(1, 1), ispec(0, 2), ispec(1, 3),
                  ispec(0, 0), ispec(1, 0), ispec(0, 1), ispec(1, 1), ispec(0, 2), ispec(1, 2)],
        out_specs=[ospec(0), ospec(1)],
        out_shape=[jax.ShapeDtypeStruct((n, dm), BF16), jax.ShapeDtypeStruct((n, dm), BF16)],
        scratch_shapes=[pltpu.VMEM((npair, 2, LANES, LANES), F32), pltpu.VMEM((npair, 2, nchunk, L, LANES), F32),
                        pltpu.VMEM((npair, 2, nchunk, L, LANES), F32),
                        pltpu.VMEM((npair, 2, nchunk, LANES, LANES), F32),
                        pltpu.VMEM((npair, 2, nchunk, LANES, LANES), F32)],
        compiler_params=pltpu.CompilerParams(
            dimension_semantics=("arbitrary", "arbitrary", "arbitrary"), vmem_limit_bytes=VMEM_LIMIT),
        name="rwkv_scan",
    )(lw, lw, kda, kda, kda, kda, rvkg, rvkg, rvkg, rvkg, rvkg, rvkg)


HALO_ROWS = 8


def _group_sum(x, width):
    r = lax.broadcasted_iota(jnp.int32, (LANES, LANES), 0) // width
    c = lax.broadcasted_iota(jnp.int32, (LANES, LANES), 1) // width
    ones = (r == c).astype(BF16)
    hi = x.astype(BF16)
    lo = (x - hi.astype(F32)).astype(BF16)
    parts = []
    for j in range(x.shape[1] // LANES):
        sl = slice(j * LANES, (j + 1) * LANES)
        parts.append(jnp.dot(hi[:, sl], ones, preferred_element_type=F32)
                     + jnp.dot(lo[:, sl], ones, preferred_element_type=F32))
    return jnp.concatenate(parts, axis=1)


def _rw_proj_kernel(s_ref, up_ref, dn_ref, gain_ref, shift_ref, mu_ref, wrkv_ref, w1_ref, w2_ref, w0_ref,
                    a1_ref, a2_ref, a0_ref, g1_ref, g2_ref, kk_ref, ka_ref,
                    lw_ref, kda_ref, rvkg_ref, *, tpb, nctx_t):
    tm, dm = s_ref.shape
    ti = pl.program_id(0) % tpb
    is_ctx = ti < nctx_t
    has_up = jnp.logical_not(jnp.logical_or(is_ctx, ti == nctx_t))
    has_dn = jnp.logical_not(jnp.logical_or(is_ctx, ti == tpb - 1))
    gain = gain_ref[0]
    shift = shift_ref[0]
    u = _norm_mod(s_ref[...], gain, shift)
    u_up = jnp.where(has_up, _norm_mod(up_ref[HALO_ROWS - 1:HALO_ROWS, :], gain, shift), 0.0)
    u_dn = jnp.where(has_dn, _norm_mod(dn_ref[0:1, :], gain, shift), 0.0)
    row = lax.broadcasted_iota(jnp.int32, (tm, 1), 0)
    u_m = jnp.where(row == 0, u_up, pltpu.roll(u, 1, axis=0))
    u_p = jnp.where(row == tm - 1, u_dn, pltpu.roll(u, tm - 1, axis=0))
    du = 0.5 * (u_m + u_p) - u
    mu = mu_ref[...]

    def mix(i):
        return (u + du * mu[i:i + 1]).astype(BF16)

    def dot(a, b):
        return jnp.dot(a.astype(BF16), b, preferred_element_type=F32)

    r = dot(mix(0), wrkv_ref[0])
    k = dot(mix(1), wrkv_ref[1])
    v = dot(mix(2), wrkv_ref[2])
    w_pre = dot(jnp.tanh(dot(mix(3), w1_ref[...])), w2_ref[...]) + w0_ref[...]
    lw_ref[...] = -jnp.exp(-_softplus(-w_pre) - 0.5)
    a = jax.nn.sigmoid(dot(dot(mix(4), a1_ref[...]), a2_ref[...]) + a0_ref[...])
    g = dot(jax.nn.sigmoid(dot(mix(5), g1_ref[...])), g2_ref[...])
    kk = k * kk_ref[...]
    kk = kk * lax.rsqrt(jnp.maximum(_group_sum(kk * kk, RW_HEAD_DIM), 1e-24))
    ka = ka_ref[...]
    for d in range(2):
        kda_ref[:, d * dm:(d + 1) * dm] = (k * (1.0 + (a[:, d * dm:(d + 1) * dm] - 1.0) * ka)).astype(kda_ref.dtype)
    kda_ref[:, 2 * dm:] = a.astype(kda_ref.dtype)
    for j, val in enumerate((r, v, kk, g)):
        rvkg_ref[:, j * dm:(j + 1) * dm] = val.astype(rvkg_ref.dtype)


def rwkv_proj(s, gain, shift, geom, mu, w_rkv, w0, w1, w2, a0, a1, a2, g1, g2, k_k, k_a):
    n, dm = s.shape
    tpb, nctx_t = geom
    assert nctx_t == 1
    seg = _seg_map(tpb, nctx_t)
    hb = ROW_TILE // HALO_ROWS
    last = n // HALO_ROWS - 1
    lora = w1.shape[2]

    def blockdiag(w):
        z = jnp.zeros_like(w[0])
        return jnp.concatenate([jnp.concatenate([w[0], z], axis=1), jnp.concatenate([z, w[1]], axis=1)], axis=0)

    consts = [jnp.pad(mu, ((0, HALO_ROWS - mu.shape[0]), (0, 0))), w_rkv.astype(BF16),
              jnp.concatenate([w1[0], w1[1]], axis=1).astype(BF16), blockdiag(w2).astype(BF16),
              jnp.concatenate([w0[0], w0[1]])[None],
              jnp.concatenate([a1[0], a1[1]], axis=1).astype(BF16), blockdiag(a2).astype(BF16),
              jnp.concatenate([a0[0], a0[1]])[None],
              g1.astype(BF16), g2.astype(BF16), k_k[None], k_a[None]]

    def const_spec(x):
        nd = x.ndim
        return pl.BlockSpec(x.shape, lambda i: (0,) * nd)

    return pl.pallas_call(
        functools.partial(_rw_proj_kernel, tpb=tpb, nctx_t=nctx_t),
        grid=(n // ROW_TILE,),
        in_specs=[pl.BlockSpec((ROW_TILE, dm), lambda i: (i, 0)),
                  pl.BlockSpec((HALO_ROWS, dm), lambda i: (jnp.maximum(i * hb - 1, 0), 0)),
                  pl.BlockSpec((HALO_ROWS, dm), lambda i: (jnp.minimum((i + 1) * hb, last), 0)),
                  pl.BlockSpec((1, 1, dm), lambda i: (seg(i), 0, 0)),
                  pl.BlockSpec((1, 1, dm), lambda i: (seg(i), 0, 0))] + [const_spec(x) for x in consts],
        out_specs=[pl.BlockSpec((ROW_TILE, 2 * dm), lambda i: (i, 0)),
                   pl.BlockSpec((ROW_TILE, 4 * dm), lambda i: (i, 0)),
                   pl.BlockSpec((ROW_TILE, 4 * dm), lambda i: (i, 0))],
        out_shape=[jax.ShapeDtypeStruct((n, 2 * dm), F32), jax.ShapeDtypeStruct((n, 4 * dm), BF16),
                   jax.ShapeDtypeStruct((n, 4 * dm), BF16)],
        compiler_params=pltpu.CompilerParams(dimension_semantics=("arbitrary",), vmem_limit_bytes=VMEM_LIMIT_BIG),
        name="rwkv_proj",
    )(s, s, s, gain, shift, *consts)


def _rw_post_kernel(of_ref, ob_ref, r_ref, v_ref, g_ref, k0_ref, k1_ref, s_ref, lnw_ref, lnb_ref, rk_ref, gm_ref,
                    w_ref, out_ref, *, sub, seg):
    o = of_ref[...].astype(F32) + ob_ref[...].astype(F32)
    inv = 1.0 / RW_HEAD_DIM
    mean = _group_sum(o, RW_HEAD_DIM) * inv
    oc = o - mean
    var = _group_sum(oc * oc, RW_HEAD_DIM) * inv
    xn = oc * lax.rsqrt(var + RW_GN_EPS) * lnw_ref[...] + lnb_ref[...]
    r = r_ref[...].astype(F32)
    ksum = k0_ref[...].astype(F32) + k1_ref[...].astype(F32)
    bonus = _group_sum(r * ksum * rk_ref[...], RW_HEAD_DIM) * v_ref[...].astype(F32)
    y = ((xn + bonus) * g_ref[...].astype(F32)).astype(BF16)
    _gated_residual_store(out_ref, s_ref, gm_ref, jnp.dot(y, w_ref[...], preferred_element_type=F32), sub, seg)


def rwkv_post(o_f, o_b, rvkg, kda, s, ln_w, ln_b, r_k, gm, w_out, geom):
    n, dm = s.shape
    seg = _seg_map(*geom)
    sub = _sub_tiles(n, most=2)
    tm = sub * ROW_TILE

    def col(block):
        return pl.BlockSpec((tm, dm), lambda i: (i, block))

    vec = pl.BlockSpec((1, dm), lambda i: (0, 0))
    return pl.pallas_call(
        functools.partial(_rw_post_kernel, sub=sub, seg=seg),
        grid=(n // tm,),
        in_specs=[col(0), col(0), col(0), col(1), col(3), col(0), col(1),
                  col(0), vec, vec, vec, pl.BlockSpec(gm.shape, lambda i: (0, 0, 0)),
                  pl.BlockSpec((dm, dm), lambda i: (0, 0))],
        out_specs=pl.BlockSpec((tm, dm), lambda i: (i, 0)),
        out_shape=jax.ShapeDtypeStruct((n, dm), F32),
        compiler_params=pltpu.CompilerParams(dimension_semantics=("arbitrary",), vmem_limit_bytes=VMEM_LIMIT),
        name="rwkv_post",
    )(o_f, o_b, rvkg, rvkg, rvkg, kda, kda, s, ln_w[None], ln_b[None], r_k[None], gm, w_out.astype(BF16))


def _hg_scan_kernel(q_ref, v_ref, lf_ref, o_ref, st_scr, *, heads):
    C = q_ref.shape[0]
    d = pl.program_id(0)
    nsub = C // HG_SUB

    @pl.when(pl.program_id(2) == 0)
    def _():
        st_scr[...] = jnp.zeros_like(st_scr)

    def body(reverse):
        last = 0 if reverse else C - 1
        hs = range(heads)
        sls = [slice(h * LANES, (h + 1) * LANES) for h in hs]
        g = [lf_ref[:, sls[h]] for h in hs]
        b = [_cumsum_rows(g[h], reverse) for h in hs]
        k = [-jnp.tanh(0.5 * g[h]) * (jnp.exp(g[h]) + 1.0) for h in hs]
        o_inter = [_dot(q_ref[:, sls[h]].astype(F32) * jnp.exp(b[h]), st_scr[h], NT) for h in hs]
        parts = [[None] * nsub for _ in hs]
        for i in range(nsub):
            r0 = i * HG_SUB
            lo, hi = (r0, C) if reverse else (0, r0 + HG_SUB)
            first = r0 + HG_SUB - 1 if reverse else r0
            row = lax.broadcasted_iota(jnp.int32, (HG_SUB, hi - lo), 0) + r0
            col = lax.broadcasted_iota(jnp.int32, (HG_SUB, hi - lo), 1) + lo
            keep = (col >= row) if reverse else (col <= row)
            att = []
            for h in hs:
                rho = b[h][first:first + 1, :] - g[h][first:first + 1, :]
                qi = q_ref[r0:r0 + HG_SUB, sls[h]].astype(F32) * jnp.exp(b[h][r0:r0 + HG_SUB] - rho)
                ki = k[h][lo:hi] * jnp.exp(jnp.minimum(rho - b[h][lo:hi], HG_EXP_CLAMP))
                att.append(jnp.where(keep, _dot(qi, ki, NT), 0.0))
            for h in hs:
                parts[h][i] = _dot(att[h], v_ref[lo:hi, sls[h]])
        for h in hs:
            o_ref[0, :, sls[h]] = (o_inter[h] + jnp.concatenate(parts[h], axis=0)).astype(o_ref.dtype)
        upd = []
        for h in hs:
            b_last = b[h][last:last + 1, :]
            upd.append((jnp.exp(b_last),
                        _dot(v_ref[:, sls[h]].astype(F32).T, k[h] * jnp.exp(b_last - b[h]))))
        for h in hs:
            st_scr[h] = st_scr[h] * upd[h][0] + upd[h][1]

    @pl.when(d == 0)
    def _():
        body(False)

    @pl.when(d == 1)
    def _():
        body(True)


def hgrn_scan(z, logf, dm, bsz, t):
    n = z.shape[0]
    heads = dm // LANES
    C = HG_CHUNK
    nc, nctx = t // C, CTX_LEN // C

    def row(d, b, p):
        return b * nc + _scan_chunk_index(d, p, nctx, nc)

    return pl.pallas_call(
        functools.partial(_hg_scan_kernel, heads=heads),
        grid=(2, bsz, nc),
        in_specs=[pl.BlockSpec((C, dm), lambda d, b, p: (row(d, b, p), 0)),
                  pl.BlockSpec((C, dm), lambda d, b, p: (row(d, b, p), 1)),
                  pl.BlockSpec((C, dm), lambda d, b, p: (row(d, b, p), d))],
        out_specs=pl.BlockSpec((1, C, dm), lambda d, b, p: (d, row(d, b, p), 0)),
        out_shape=jax.ShapeDtypeStruct((2, n, dm), BF16),
        scratch_shapes=[pltpu.VMEM((heads, LANES, LANES), F32)],
        compiler_params=pltpu.CompilerParams(
            dimension_semantics=("arbitrary", "arbitrary", "arbitrary"), vmem_limit_bytes=VMEM_LIMIT),
        name="hgrn_scan",
    )(z, z, logf)


def _first_argmax(vals):
    best, idx = vals[0], jnp.zeros(vals[0].shape, jnp.int32)
    for i in range(1, len(vals)):
        better = vals[i] > best
        best = jnp.where(better, vals[i], best)
        idx = jnp.where(better, i, idx)
    return best, idx


def _router_kernel(s_ref, gain_ref, shift_ref, wt_ref, b_ref, e_ref, g_ref, rank_ref, cnt_ref, carry_scr, *,
                   n_groups, top_k):
    n_experts = wt_ref.shape[0]
    per = n_experts // n_groups
    h = _norm_mod(s_ref[...], gain_ref[0], shift_ref[0])
    aff = jax.nn.sigmoid(_dot(wt_ref[...], h, NT, passes=3))
    sel = aff + b_ref[...]
    a = [aff[e:e + 1, :] for e in range(n_experts)]
    s = [sel[e:e + 1, :] for e in range(n_experts)]
    neg = jnp.full_like(s[0], -jnp.inf)
    scores = []
    for g in range(n_groups):
        grp = s[g * per:(g + 1) * per]
        m1, i1 = _first_argmax(grp)
        m2, _ = _first_argmax([jnp.where(i1 == j, neg, grp[j]) for j in range(per)])
        scores.append(m1 + m2)
    _, best = _first_argmax(scores)

    def in_best(rows):
        out = []
        for j in range(per):
            x = rows[j]
            for g in range(1, n_groups):
                x = jnp.where(best == g, rows[g * per + j], x)
            out.append(x)
        return out

    sb, ab = in_best(s), in_best(a)
    picked, chosen = [], []
    cand = sb
    for _ in range(top_k):
        _, i = _first_argmax(cand)
        c = ab[0]
        for j in range(1, per):
            c = jnp.where(i == j, ab[j], c)
        picked.append(i)
        chosen.append(c)
        cand = [jnp.where(i == j, neg, cand[j]) for j in range(per)]
    total = functools.reduce(jnp.add, chosen)
    experts = [best * per + picked[kk_] for kk_ in range(top_k)]
    for kk_ in range(top_k):
        e_ref[kk_:kk_ + 1, :] = experts[kk_]
        g_ref[kk_:kk_ + 1, :] = chosen[kk_] / total

    @pl.when(pl.program_id(0) == 0)
    def _():
        carry_scr[...] = jnp.zeros_like(carry_scr)

    tm = s_ref.shape[0]
    sub = lax.broadcasted_iota(jnp.int32, (n_experts, tm), 0)
    onehots = [(sub == ex).astype(F32) for ex in experts]
    tot = functools.reduce(jnp.add, onehots)
    ri = lax.broadcasted_iota(jnp.int32, (tm, tm), 0)
    ci = lax.broadcasted_iota(jnp.int32, (tm, tm), 1)
    earlier = (ri < ci).astype(BF16)
    before = jnp.dot(tot.astype(BF16), earlier, preferred_element_type=F32) + carry_scr[:, 0:1]
    seen = before
    for kk_ in range(top_k):
        rank_ref[kk_:kk_ + 1, :] = jnp.sum(onehots[kk_] * seen, axis=0, keepdims=True).astype(jnp.int32)
        seen = seen + onehots[kk_]
    carry = carry_scr[...] + jnp.sum(tot, axis=1, keepdims=True)
    carry_scr[...] = carry
    cnt_ref[...] = carry.astype(jnp.int32)


def norm_route(s, gain, shift, geom, router_w, router_b):
    n, k = s.shape
    n_experts = router_w.shape[1]
    tm = ROW_TILE
    seg = _seg_map(*geom)
    return pl.pallas_call(
        functools.partial(_router_kernel, n_groups=N_GROUPS, top_k=TOP_K),
        grid=(n // tm,),
        in_specs=[pl.BlockSpec((tm, k), lambda i: (i, 0)),
                  pl.BlockSpec((1, 1, k), lambda i: (seg(i), 0, 0)),
                  pl.BlockSpec((1, 1, k), lambda i: (seg(i), 0, 0)),
                  pl.BlockSpec((n_experts, k), lambda i: (0, 0)),
                  pl.BlockSpec((n_experts, 1), lambda i: (0, 0))],
        out_specs=[pl.BlockSpec((TOP_K, tm), lambda i: (0, i)), pl.BlockSpec((TOP_K, tm), lambda i: (0, i)),
                   pl.BlockSpec((TOP_K, tm), lambda i: (0, i)), pl.BlockSpec((n_experts, LANES), lambda i: (0, 0))],
        out_shape=[jax.ShapeDtypeStruct((TOP_K, n), jnp.int32), jax.ShapeDtypeStruct((TOP_K, n), F32),
                   jax.ShapeDtypeStruct((TOP_K, n), jnp.int32), jax.ShapeDtypeStruct((n_experts, LANES), jnp.int32)],
        scratch_shapes=[pltpu.VMEM((n_experts, LANES), F32)],
        compiler_params=pltpu.CompilerParams(dimension_semantics=("arbitrary",), vmem_limit_bytes=VMEM_LIMIT),
        name="norm_route",
    )(s, gain, shift, router_w.T, router_b.reshape(n_experts, 1).astype(F32))


def _final_norm_kernel(s_ref, g_ref, o_ref):
    x = s_ref[...]
    o_ref[0] = x * lax.rsqrt(jnp.mean(x * x, axis=-1, keepdims=True) + NORM_EPS) * g_ref[...]


def final_norm(s, g, bsz, t, geom):
    dm = s.shape[1]
    tpb, nctx_t = geom
    return pl.pallas_call(
        _final_norm_kernel,
        grid=(bsz, tpb - nctx_t),
        in_specs=[pl.BlockSpec((ROW_TILE, dm), lambda b, i: (b * tpb + nctx_t + i, 0)),
                  pl.BlockSpec((1, dm), lambda b, i: (0, 0))],
        out_specs=pl.BlockSpec((1, ROW_TILE, dm), lambda b, i: (b, i, 0)),
        out_shape=jax.ShapeDtypeStruct((bsz, t - CTX_LEN, dm), F32),
        name="final_norm",
    )(s, g[None])


DMA_UNROLL = 8


def _row_copy_waits(src_row, dst_row, sem, count):
    def body(_, carry):
        pltpu.make_async_copy(src_row, dst_row, sem).wait()
        return carry
    lax.fori_loop(0, count, body, 0)


def _scatter_kernel(dest_ref, meta_ref, s_ref, gain_ref, shift_ref, xb_ref, hbuf, zrow, sems, zsem, *, n_experts):
    i = pl.program_id(0)
    last = pl.num_programs(0) - 1
    slot = i % 2
    per_tile = TOP_K * ROW_TILE

    def wait_tile(sl):
        for _ in range(TOP_K):
            pltpu.make_async_copy(hbuf.at[sl], xb_ref.at[pl.ds(0, ROW_TILE), :], sems.at[sl]).wait()

    @pl.when(i >= 2)
    def _():
        wait_tile(slot)

    hbuf[slot] = _norm_mod(s_ref[...], gain_ref[0], shift_ref[0])

    def issue(r8, carry):
        for u in range(DMA_UNROLL):
            r = r8 * DMA_UNROLL + u
            for k in range(TOP_K):
                d = dest_ref[0, 0, TOP_K * r + k]
                pltpu.make_async_copy(hbuf.at[slot, pl.ds(r, 1), :], xb_ref.at[pl.ds(d, 1), :],
                                      sems.at[slot]).start(priority=(TOP_K * u + k) % 2)
        return carry
    lax.fori_loop(0, ROW_TILE // DMA_UNROLL, issue, 0)

    @pl.when(i == last)
    def _():
        @pl.when(last >= 1)
        def _():
            wait_tile(1 - slot)
        wait_tile(slot)
        zrow[...] = jnp.zeros_like(zrow)
        for e in range(n_experts):
            lo = meta_ref[2, e] + meta_ref[0, e]
            hi = meta_ref[2, e] + meta_ref[1, e]

            def pad_start(q, carry):
                pltpu.make_async_copy(zrow.at[pl.ds(0, 1), :], xb_ref.at[pl.ds(q, 1), :], zsem.at[0]).start()
                return carry
            lax.fori_loop(lo, hi, pad_start, 0)
            _row_copy_waits(zrow.at[pl.ds(0, 1), :], xb_ref.at[pl.ds(0, 1), :], zsem.at[0], hi - lo)
        end = meta_ref[2, n_experts - 1] + meta_ref[1, n_experts - 1]

        def tail_start(q, carry):
            pltpu.make_async_copy(zrow.at[pl.ds(0, 1), :], xb_ref.at[pl.ds(q, 1), :], zsem.at[0]).start()
            return carry
        lax.fori_loop(end, xb_ref.shape[0], tail_start, 0)
        _row_copy_waits(zrow.at[pl.ds(0, 1), :], xb_ref.at[pl.ds(0, 1), :], zsem.at[0], xb_ref.shape[0] - end)


def moe_scatter(s, gain, shift, geom, dest, meta, n_slots):
    n, dm = s.shape
    seg = _seg_map(*geom)
    nt = n // ROW_TILE
    return pl.pallas_call(
        functools.partial(_scatter_kernel, n_experts=meta.shape[1]),
        grid=(nt,),
        in_specs=[pl.BlockSpec((1, 1, TOP_K * ROW_TILE), lambda i: (i, 0, 0), memory_space=pltpu.SMEM),
                  pl.BlockSpec(memory_space=pltpu.SMEM),
                  pl.BlockSpec((ROW_TILE, dm), lambda i: (i, 0)),
                  pl.BlockSpec((1, 1, dm), lambda i: (seg(i), 0, 0)),
                  pl.BlockSpec((1, 1, dm), lambda i: (seg(i), 0, 0))],
        out_specs=pl.BlockSpec(memory_space=pl.ANY),
        out_shape=jax.ShapeDtypeStruct((n_slots, dm), F32),
        scratch_shapes=[pltpu.VMEM((2, ROW_TILE, dm), F32), pltpu.VMEM((8, dm), F32),
                        pltpu.SemaphoreType.DMA((2,)), pltpu.SemaphoreType.DMA((1,))],
        compiler_params=pltpu.CompilerParams(dimension_semantics=("arbitrary",), vmem_limit_bytes=VMEM_LIMIT),
        name="moe_scatter",
    )(dest.reshape(nt, 1, TOP_K * ROW_TILE), meta, s, gain, shift)


def _gather_combine_kernel(dcur_ref, dnxt_ref, g_ref, s_ref, gm_ref, yb_ref, o_ref, ybuf, sems):
    i = pl.program_id(0)
    nsteps = pl.num_programs(0)
    slot = i % 2

    def start_tile(dref, sl):
        def issue(r8, carry):
            for u in range(DMA_UNROLL):
                r = r8 * DMA_UNROLL + u
                for k in range(TOP_K):
                    d = dref[0, 0, TOP_K * r + k]
                    pltpu.make_async_copy(yb_ref.at[pl.ds(d, 1), :], ybuf.at[sl, k, pl.ds(r, 1), :],
                                          sems.at[sl]).start(priority=(TOP_K * u + k) % 2)
            return carry
        lax.fori_loop(0, ROW_TILE // DMA_UNROLL, issue, 0)

    @pl.when(i == 0)
    def _():
        start_tile(dcur_ref, 0)

    @pl.when(i + 1 < nsteps)
    def _():
        start_tile(dnxt_ref, 1 - slot)

    for k in range(TOP_K):
        pltpu.make_async_copy(yb_ref.at[pl.ds(0, ROW_TILE), :], ybuf.at[slot, k], sems.at[slot]).wait()
    g = g_ref[...]
    y = sum(ybuf[slot, k] * g[:, k:k + 1] for k in range(TOP_K))
    o_ref[...] = s_ref[...] + gm_ref[0] * y


def moe_gather_combine(yb, dest, gate, s, gm, geom):
    n, dm = s.shape
    seg = _seg_map(*geom)
    nt = n // ROW_TILE
    d3 = dest.reshape(nt, 1, TOP_K * ROW_TILE)
    row = pl.BlockSpec((ROW_TILE, dm), lambda i: (i, 0))
    return pl.pallas_call(
        _gather_combine_kernel,
        grid=(nt,),
        in_specs=[pl.BlockSpec((1, 1, TOP_K * ROW_TILE), lambda i: (i, 0, 0), memory_space=pltpu.SMEM),
                  pl.BlockSpec((1, 1, TOP_K * ROW_TILE), lambda i: (jnp.minimum(i + 1, nt - 1), 0, 0),
                               memory_space=pltpu.SMEM),
                  pl.BlockSpec((ROW_TILE, TOP_K), lambda i: (i, 0)), row,
                  pl.BlockSpec((1, 1, dm), lambda i: (seg(i), 0, 0)),
                  pl.BlockSpec(memory_space=pl.ANY)],
        out_specs=row,
        out_shape=jax.ShapeDtypeStruct((n, dm), F32),
        scratch_shapes=[pltpu.VMEM((2, TOP_K, ROW_TILE, dm), F32), pltpu.SemaphoreType.DMA((2,))],
        compiler_params=pltpu.CompilerParams(dimension_semantics=("arbitrary",), vmem_limit_bytes=VMEM_LIMIT),
        name="moe_gather_combine",
    )(d3, d3, gate, s, gm, yb)


def _ffn_kernel(be_ref, x_ref, w1_ref, w3_ref, w2_ref, o_ref):
    del be_ref
    x = x_ref[...].astype(BF16)
    a = jnp.dot(x, w1_ref[0], preferred_element_type=F32)
    b = jnp.dot(x, w3_ref[0], preferred_element_type=F32)
    hid = (a * jax.nn.sigmoid(a) * b).astype(BF16)
    o_ref[...] = jnp.dot(hid, w2_ref[0], preferred_element_type=F32).astype(o_ref.dtype)


def expert_ffn(xb, block_expert, w1, w3, w2):
    nrows, dm = xb.shape
    f = w1.shape[2]
    nb = nrows // MOE_BLOCK
    return pl.pallas_call(
        _ffn_kernel,
        grid_spec=pltpu.PrefetchScalarGridSpec(
            num_scalar_prefetch=1,
            grid=(nb,),
            in_specs=[pl.BlockSpec((MOE_BLOCK, dm), lambda i, be: (i, 0)),
                      pl.BlockSpec((1, dm, f), lambda i, be: (be[i], 0, 0)),
                      pl.BlockSpec((1, dm, f), lambda i, be: (be[i], 0, 0)),
                      pl.BlockSpec((1, f, dm), lambda i, be: (be[i], 0, 0))],
            out_specs=pl.BlockSpec((MOE_BLOCK, dm), lambda i, be: (i, 0))),
        out_shape=jax.ShapeDtypeStruct((nrows, dm), F32),
        compiler_params=pltpu.CompilerParams(
            dimension_semantics=("arbitrary",), vmem_limit_bytes=VMEM_LIMIT),
        name="expert_ffn",
    )(block_expert, xb, w1.astype(BF16), w3.astype(BF16), w2.astype(BF16))


def _mlstm_layer(s, gain, shift, gate_mod, geom, bsz, t, w_in, w_gate, b_gate, conv, head_g, w_out):
    n, dm = s.shape
    heads = ML_HEADS
    z = norm_mod_mm(s, gain, shift, w_in, None, (None, None, None, "sigmoid"), geom)
    scale = jnp.concatenate([jnp.ones((dm,), F32), jnp.full((dm,), (dm // heads) ** -0.5, F32)])
    qk = conv_silu(z, conv, scale, 2 * dm, geom)
    ng = 4 * heads
    wg = jnp.pad(jnp.concatenate([w_gate[0], w_gate[1]], axis=1), ((0, 0), (0, LANES - ng)))
    bg = jnp.pad(jnp.concatenate([b_gate[0], b_gate[1]]), (0, LANES - ng))
    gates = norm_mod_mm(s, gain, shift, wg, bg, (None,), geom, out_dtype=F32)[:, :ng]
    gates = gates.reshape(bsz, t, 2, 2 * heads)
    gates = jnp.concatenate([gates[..., :heads], jax.nn.log_sigmoid(gates[..., heads:])], axis=-1)
    gc = jnp.moveaxis(gates, 2, 0).reshape(2, n, 2 * heads)
    gr = jnp.transpose(gates, (2, 0, 3, 1))
    h = mlstm_scan(qk, z, gc, gr, dm, bsz, t)
    return post_mm_residual(h, z, 3, s, head_g, gate_mod, w_out, heads, geom)


def _rwkv7_layer(s, gain, shift, gate_mod, geom, bsz, t, mu, w_rkv, w0, w1, w2, a0, a1, a2, g1, g2,
                 k_k, k_a, r_k, ln_w, ln_b, w_out):
    dm = s.shape[1]
    lw, kda, rvkg = rwkv_proj(s, gain, shift, geom, mu, w_rkv, w0, w1, w2, a0, a1, a2, g1, g2, k_k, k_a)
    o_f, o_b = rwkv_scan(lw, kda, rvkg, dm, bsz, t)
    return rwkv_post(o_f, o_b, rvkg, kda, s, ln_w, ln_b, r_k, gate_mod, w_out, geom)


def _hgrn2_layer(s, gain, shift, gate_mod, geom, bsz, t, layer_idx, w_in, w_f, b_f, lb_logits, head_g, w_out):
    dm = s.shape[1]
    z = norm_mod_mm(s, gain, shift, w_in, None, ("silu", None, "silu"), geom)
    p = jax.nn.softmax(lb_logits, axis=0)
    lb = jnp.cumsum(p, axis=0)[layer_idx] - p[0]
    aux = jnp.tile(jnp.stack([jnp.log(lb), jnp.log1p(-lb)]), (1, 2))
    log_f = norm_mod_mm(s, gain, shift, jnp.concatenate([w_f[0], w_f[1]], axis=1),
                        jnp.concatenate([b_f[0], b_f[1]]), ("logf", "logf"), geom, aux=aux, out_dtype=F32)
    o = hgrn_scan(z, log_f, dm, bsz, t)
    return post_mm_residual(o, z, 2, s, head_g, gate_mod, w_out, dm // HG_EXPAND, geom)


def _moe_layer(s, gain, shift, gate_mod, geom, router_w, router_b, w1, w3, w2):
    n_tok, d = s.shape
    n_experts = w1.shape[0]
    n_assign = n_tok * TOP_K
    e, g, rank, cnt = norm_route(s, gain, shift, geom, router_w, router_b)
    flat_e = e.T.reshape(n_assign)
    rank = rank.T.reshape(n_assign)
    counts = cnt[:, 0]
    padded = (counts + MOE_BLOCK - 1) // MOE_BLOCK * MOE_BLOCK
    end_pad = jnp.cumsum(padded)
    start_pad = end_pad - padded
    onehot = flat_e[:, None] == jnp.arange(n_experts, dtype=jnp.int32)[None, :]
    dest = jnp.sum(jnp.where(onehot, start_pad[None, :], 0), axis=1) + rank
    n_blocks = -(-n_assign // MOE_BLOCK) + n_experts
    block_start = jnp.arange(n_blocks, dtype=jnp.int32) * MOE_BLOCK
    block_expert = jnp.minimum(jnp.sum(end_pad[None, :] <= block_start[:, None], axis=1), n_experts - 1)
    meta = jnp.stack([counts, padded, start_pad]).astype(jnp.int32)
    xb = moe_scatter(s, gain, shift, geom, dest.astype(jnp.int32), meta, n_blocks * MOE_BLOCK)
    yb = expert_ffn(xb, block_expert.astype(jnp.int32), w1, w3, w2)
    return moe_gather_combine(yb, dest.astype(jnp.int32), g.T, s, gate_mod, geom)


def kernel(x, c, ctx, c_ctx, ada_w, ada_b, norm_mix, norm_ffn, norm_out, ml_w_in, ml_w_gate, ml_b_gate, ml_conv, ml_head_g, ml_w_out, rw_mu, rw_w_rkv, rw_w0, rw_w1, rw_w2, rw_a0, rw_a1, rw_a2, rw_g1, rw_g2, rw_k_k, rw_k_a, rw_r_k, rw_ln_w, rw_ln_b, rw_w_out, hg_w_in, hg_w_f, hg_b_f, hg_lb_logits, hg_head_g, hg_w_out, router_w, router_b, ex_w1, ex_w3, ex_w2):
    depth = ada_w.shape[0]
    bsz = x.shape[0]
    cond = jax.nn.silu(jnp.concatenate([c, c_ctx[None]], axis=0))
    cond = jnp.pad(cond, ((0, -(bsz + 1) % 8), (0, 0)))
    dm = x.shape[2]
    t = CTX_LEN + x.shape[1]
    n = bsz * t
    geom = (t // ROW_TILE, CTX_LEN // ROW_TILE)
    s = jnp.concatenate([ctx, x], axis=1).reshape(n, dm)
    for i in range(depth):
        mod = mm(cond, ada_w[i], bias=ada_b[i])
        mod_x = jnp.split(mod[:bsz, None, :], 6, axis=-1)
        mod_c = jnp.split(mod[bsz], 6, axis=-1)

        def table(idx):
            return jnp.stack([jnp.broadcast_to(mod_c[idx], (bsz, dm)), mod_x[idx][:, 0]], axis=1).reshape(2 * bsz, 1, dm)

        kind, j = i % N_MIXERS, i // N_MIXERS
        if kind == 2:
            s = _hgrn2_layer(s, norm_mix[i] * (1 + table(1)), table(0), table(2), geom, bsz, t, i,
                             hg_w_in[j], hg_w_f[j], hg_b_f[j], hg_lb_logits, hg_head_g[j], hg_w_out[j])
        elif kind == 0:
            s = _mlstm_layer(s, norm_mix[i] * (1 + table(1)), table(0), table(2), geom, bsz, t,
                             ml_w_in[j], ml_w_gate[j], ml_b_gate[j], ml_conv[j], ml_head_g[j], ml_w_out[j])
        else:
            s = _rwkv7_layer(s, norm_mix[i] * (1 + table(1)), table(0), table(2), geom, bsz, t,
                             rw_mu[j], rw_w_rkv[j], rw_w0[j], rw_w1[j], rw_w2[j], rw_a0[j],
                             rw_a1[j], rw_a2[j], rw_g1[j], rw_g2[j], rw_k_k[j], rw_k_a[j],
                             rw_r_k[j], rw_ln_w[j], rw_ln_b[j], rw_w_out[j])
        s = _moe_layer(s, norm_ffn[i] * (1 + table(4)), table(3), table(5), geom, router_w, router_b,
                       ex_w1[i], ex_w3[i], ex_w2[i])
    return final_norm(s, norm_out, bsz, t, geom)
```

```python
import functools

import jax
import jax.numpy as jnp
from jax import lax
from jax.experimental import pallas as pl
from jax.experimental.pallas import tpu as pltpu

F32 = jnp.float32
BF16 = jnp.bfloat16

GRID_W = 64
CTX_LEN = 256
N_MIXERS = 3
NORM_EPS = 1e-6
ML_HEADS = 8
RW_HEAD_DIM = 64
RW_GN_EPS = 64e-5
HG_EXPAND = 128
N_GROUPS = 4
TOP_K = 2
MOE_BLOCK = 512

LANES = 128
ML_CHUNK = 128
RW_CHUNK = 64
RW_PRE_CHUNKS = 4
RW_PAIRS_PER_STEP = 4
HG_CHUNK = 128
HG_SUB = 16
HG_EXP_CLAMP = 80.0
VMEM_LIMIT = 48 * 1024 * 1024
VMEM_LIMIT_BIG = 56 * 1024 * 1024

NT = (((1,), (1,)), ((), ()))
NN = (((1,), (0,)), ((), ()))


def _dot(a, b, dims=NN, passes=1):
    a_hi = a.astype(BF16)
    b_hi = b.astype(BF16)
    out = lax.dot_general(a_hi, b_hi, dims, preferred_element_type=F32)
    if passes == 3:
        a_lo = (a - a_hi.astype(F32)).astype(BF16)
        b_lo = (b - b_hi.astype(F32)).astype(BF16)
        out = out + lax.dot_general(a_hi, b_lo, dims, preferred_element_type=F32)
        out = out + lax.dot_general(a_lo, b_hi, dims, preferred_element_type=F32)
    return out


def _dot_exact01(a, b, lhs01=False):
    x = (b if lhs01 else a).astype(F32)
    out = None
    for _ in range(3):
        t = x.astype(BF16)
        x = x - t.astype(F32)
        p = lax.dot_general(a, t, NN, preferred_element_type=F32) if lhs01 else \
            lax.dot_general(t, b, NN, preferred_element_type=F32)
        out = p if out is None else out + p
    return out


def _cumsum_rows(x, reverse):
    n = x.shape[0]
    row = lax.broadcasted_iota(jnp.int32, x.shape, 0)
    s = 1
    while s < n:
        if reverse:
            x = x + jnp.where(row < n - s, pltpu.roll(x, n - s, axis=0), 0.0)
        else:
            x = x + jnp.where(row >= s, pltpu.roll(x, s, axis=0), 0.0)
        s *= 2
    return x


def _pick_tile(n, candidates):
    for c in candidates:
        if n % c == 0:
            return c
    raise ValueError(f"no tile for {n}")


def _scan_chunk_index(d, p, nctx, nc):
    rev = jnp.where(p < nctx, nctx - 1 - p, nc - 1 - (p - nctx))
    return jnp.where(d == 0, p, rev)


_ACTS = {
    None: lambda y: y,
    "sigmoid": jax.nn.sigmoid,
    "silu": lambda y: y * jax.nn.sigmoid(y),
    "tanh": jnp.tanh,
}


def _mm_kernel(x_ref, w_ref, b_ref, o_ref, *, act, precise):
    if precise:
        y = _dot(x_ref[...], w_ref[...], passes=3)
    else:
        y = jnp.dot(x_ref[...].astype(BF16), w_ref[...], preferred_element_type=F32)
    o_ref[...] = _ACTS[act](y + b_ref[...]).astype(o_ref.dtype)


def mm(x, w, bias=None, act=None, out_dtype=F32, precise=False):
    n, k = x.shape
    m = w.shape[1]
    tm = _pick_tile(n, (512, 256, 128, 64, 32, 16, 8))
    tn = m if m <= 1024 else _pick_tile(m, (1024, 512, 256, 128))
    if not precise:
        w = w.astype(BF16)
    if bias is None:
        bias = jnp.zeros((m,), F32)
    return pl.pallas_call(
        functools.partial(_mm_kernel, act=act, precise=precise),
        grid=(n // tm, m // tn),
        in_specs=[pl.BlockSpec((tm, k), lambda i, j: (i, 0)),
                  pl.BlockSpec((k, tn), lambda i, j: (0, j)),
                  pl.BlockSpec((1, tn), lambda i, j: (0, j))],
        out_specs=pl.BlockSpec((tm, tn), lambda i, j: (i, j)),
        out_shape=jax.ShapeDtypeStruct((n, m), out_dtype),
        compiler_params=pltpu.CompilerParams(vmem_limit_bytes=VMEM_LIMIT),
        name="mm",
    )(x, w, bias.reshape(1, m).astype(F32))


ROW_TILE = 256


def _log1p_exp_neg_abs(x):
    return jnp.log(1.0 + jnp.exp(-jnp.abs(x)))


def _log_sigmoid(y):
    return jnp.minimum(y, 0.0) - _log1p_exp_neg_abs(y)


def _softplus(x):
    return jnp.maximum(x, 0.0) + _log1p_exp_neg_abs(x)


def _logaddexp(a, b):
    return jnp.maximum(a, b) + _log1p_exp_neg_abs(a - b)


def _norm_mod(x, gain, shift):
    return x * lax.rsqrt(jnp.mean(x * x, axis=-1, keepdims=True) + NORM_EPS) * gain + shift


def _seg_map(tpb, nctx_t):
    def seg(i):
        return (i // tpb) * 2 + jnp.where(i % tpb < nctx_t, 0, 1)
    return seg


_EPILOGUES = {
    None: lambda y, aux: y,
    "sigmoid": lambda y, aux: jax.nn.sigmoid(y),
    "silu": lambda y, aux: y * jax.nn.sigmoid(y),
    "logf": lambda y, aux: _logaddexp(aux[0:1, :], aux[1:2, :] + _log_sigmoid(y)),
}


def _sub_tiles(n, most=4):
    return _pick_tile(n // ROW_TILE, tuple(range(most, 0, -1)))


def _nmm_kernel(s_ref, gain_ref, shift_ref, w_ref, b_ref, aux_ref, o_ref, h_scr, *, acts, sub, seg):
    j = pl.program_id(1)

    @pl.when(j == 0)
    def _():
        for k in range(sub):
            rows = pl.ds(k * ROW_TILE, ROW_TILE)
            sk = seg(pl.program_id(0) * sub + k)
            h_scr[rows, :] = _norm_mod(s_ref[rows, :], gain_ref[sk], shift_ref[sk]).astype(BF16)

    y = jnp.dot(h_scr[...], w_ref[...], preferred_element_type=F32) + b_ref[...]
    for act in sorted(set(acts), key=str):
        cols = [jj for jj, a in enumerate(acts) if a == act]
        if len(cols) == len(acts):
            o_ref[...] = _EPILOGUES[act](y, aux_ref[...]).astype(o_ref.dtype)
        else:
            @pl.when(functools.reduce(jnp.logical_or, [j == jj for jj in cols]))
            def _(act=act):
                o_ref[...] = _EPILOGUES[act](y, aux_ref[...]).astype(o_ref.dtype)


def norm_mod_mm(s, gain, shift, w, bias, acts, geom, aux=None, out_dtype=None):
    out_dtype = out_dtype or BF16
    n, k = s.shape
    m = w.shape[1]
    tn = m // len(acts)
    tpb, nctx_t = geom
    seg = _seg_map(tpb, nctx_t)
    sub = _sub_tiles(n)
    tm = sub * ROW_TILE
    if bias is None:
        bias = jnp.zeros((m,), F32)
    if aux is None:
        aux = jnp.zeros((2, m), F32)
    return pl.pallas_call(
        functools.partial(_nmm_kernel, acts=tuple(acts), sub=sub, seg=seg),
        grid=(n // tm, m // tn),
        in_specs=[pl.BlockSpec((tm, k), lambda i, j: (i, 0)),
                  pl.BlockSpec(gain.shape, lambda i, j: (0, 0, 0)),
                  pl.BlockSpec(shift.shape, lambda i, j: (0, 0, 0)),
                  pl.BlockSpec((k, tn), lambda i, j: (0, j)),
                  pl.BlockSpec((1, tn), lambda i, j: (0, j)),
                  pl.BlockSpec((2, tn), lambda i, j: (0, j))],
        out_specs=pl.BlockSpec((tm, tn), lambda i, j: (i, j)),
        out_shape=jax.ShapeDtypeStruct((n, m), out_dtype),
        scratch_shapes=[pltpu.VMEM((tm, k), BF16)],
        compiler_params=pltpu.CompilerParams(
            dimension_semantics=("arbitrary", "arbitrary"), vmem_limit_bytes=VMEM_LIMIT),
        name="norm_mod_mm",
    )(s, gain, shift, w.astype(BF16), bias.reshape(1, m).astype(F32), aux.astype(F32))


def _gated_residual_store(o_ref, s_ref, gm_ref, y, sub, seg):
    for k in range(sub):
        rows = pl.ds(k * ROW_TILE, ROW_TILE)
        gm = gm_ref[seg(pl.program_id(0) * sub + k)]
        o_ref[rows, :] = s_ref[rows, :] + gm * y[k * ROW_TILE:(k + 1) * ROW_TILE]


def _post_kernel(h_ref, g_ref, s_ref, hg_ref, gm_ref, w_ref, o_ref, *, heads, sub, seg):
    x = h_ref[0].astype(F32) + h_ref[1].astype(F32)
    hd = x.shape[1] // heads
    parts = []
    for h in range(heads):
        xh = x[:, h * hd:(h + 1) * hd]
        parts.append(xh * lax.rsqrt(jnp.mean(xh * xh, axis=-1, keepdims=True) + NORM_EPS))
    y = (jnp.concatenate(parts, axis=1) * hg_ref[...] * g_ref[...].astype(F32)).astype(BF16)
    _gated_residual_store(o_ref, s_ref, gm_ref, jnp.dot(y, w_ref[...], preferred_element_type=F32), sub, seg)


def post_mm_residual(h2, gate_arr, gate_block, s, head_g, gm, w_out, heads, geom):
    n, dm = s.shape
    seg = _seg_map(*geom)
    sub = _sub_tiles(n, most=2)
    tm = sub * ROW_TILE
    return pl.pallas_call(
        functools.partial(_post_kernel, heads=heads, sub=sub, seg=seg),
        grid=(n // tm,),
        in_specs=[pl.BlockSpec((2, tm, dm), lambda i: (0, i, 0)),
                  pl.BlockSpec((tm, dm), lambda i: (i, gate_block)),
                  pl.BlockSpec((tm, dm), lambda i: (i, 0)),
                  pl.BlockSpec((1, dm), lambda i: (0, 0)),
                  pl.BlockSpec(gm.shape, lambda i: (0, 0, 0)),
                  pl.BlockSpec((dm, dm), lambda i: (0, 0))],
        out_specs=pl.BlockSpec((tm, dm), lambda i: (i, 0)),
        out_shape=jax.ShapeDtypeStruct((n, dm), F32),
        compiler_params=pltpu.CompilerParams(dimension_semantics=("arbitrary",), vmem_limit_bytes=VMEM_LIMIT),
        name="post_mm_residual",
    )(h2, gate_arr, s, head_g.reshape(1, dm), gm, w_out.astype(BF16))


CONV_COLS = 512


def _conv_kernel(cur_ref, up_ref, dn_ref, w_ref, sc_ref, o_ref, *, tpb, nctx_t):
    ti = pl.program_id(0) % tpb
    is_ctx = ti < nctx_t
    no_up = jnp.logical_or(is_ctx, ti == nctx_t)
    no_dn = jnp.logical_or(is_ctx, ti == tpb - 1)
    x = cur_ref[...].astype(F32)
    up = jnp.where(no_up, 0.0, up_ref[...].astype(F32))
    dn = jnp.where(no_dn, 0.0, dn_ref[...].astype(F32))
    ext = jnp.concatenate([up, x, dn], axis=0)
    tpos = lax.broadcasted_iota(jnp.int32, (ROW_TILE, 1), 0)
    col = tpos % GRID_W
    left_ok = jnp.where(is_ctx, (tpos > 0).astype(F32), (col > 0).astype(F32))
    right_ok = jnp.where(is_ctx, (tpos < ROW_TILE - 1).astype(F32), (col < GRID_W - 1).astype(F32))
    vert = jnp.where(is_ctx, 0.0, 1.0)
    w = w_ref[...]
    sums = [None, None, None]
    for dr in (-1, 0, 1):
        base = GRID_W * (1 + dr)
        wr = w[3 * (dr + 1):3 * (dr + 2)] * (1.0 if dr == 0 else vert)
        for dc in range(3):
            term = ext[base:base + ROW_TILE] * wr[dc:dc + 1]
            sums[dc] = term if sums[dc] is None else sums[dc] + term
    acc = (sums[1] + left_ok * pltpu.roll(sums[0], 1, axis=0)
           + right_ok * pltpu.roll(sums[2], ROW_TILE - 1, axis=0))
    o_ref[...] = (acc * jax.nn.sigmoid(acc) * sc_ref[...]).astype(o_ref.dtype)


def conv_silu(z, conv_w, scale, width, geom):
    n = z.shape[0]
    tpb, nctx_t = geom
    assert nctx_t == 1 and ROW_TILE % GRID_W == 0
    hb = ROW_TILE // GRID_W
    last = n // GRID_W - 1
    return pl.pallas_call(
        functools.partial(_conv_kernel, tpb=tpb, nctx_t=nctx_t),
        grid=(n // ROW_TILE, width // CONV_COLS),
        in_specs=[pl.BlockSpec((ROW_TILE, CONV_COLS), lambda i, c: (i, c)),
                  pl.BlockSpec((GRID_W, CONV_COLS), lambda i, c: (jnp.maximum(i * hb - 1, 0), c)),
                  pl.BlockSpec((GRID_W, CONV_COLS), lambda i, c: (jnp.minimum((i + 1) * hb, last), c)),
                  pl.BlockSpec((9, CONV_COLS), lambda i, c: (0, c)),
                  pl.BlockSpec((1, CONV_COLS), lambda i, c: (0, c))],
        out_specs=pl.BlockSpec((ROW_TILE, CONV_COLS), lambda i, c: (i, c)),
        out_shape=jax.ShapeDtypeStruct((n, width), BF16),
        compiler_params=pltpu.CompilerParams(
            dimension_semantics=("arbitrary", "arbitrary"), vmem_limit_bytes=VMEM_LIMIT),
        name="conv_silu",
    )(z, z, z, conv_w.reshape(9, width).astype(F32), scale.reshape(1, width).astype(F32))


def _cummax_rows(x, reverse):
    n = x.shape[0]
    row = lax.broadcasted_iota(jnp.int32, x.shape, 0)
    s = 1
    while s < n:
        if reverse:
            x = jnp.maximum(x, jnp.where(row < n - s, pltpu.roll(x, n - s, axis=0), -jnp.inf))
        else:
            x = jnp.maximum(x, jnp.where(row >= s, pltpu.roll(x, s, axis=0), -jnp.inf))
        s *= 2
    return x


def _ml_scan_kernel(q_ref, k_ref, v_ref, gc_ref, gr_ref, o_ref, z_scr, m_scr, *, heads, nb):
    L = q_ref.shape[1]
    assert L == LANES
    d = pl.program_id(0)

    @pl.when(pl.program_id(2) == 0)
    def _():
        z_scr[...] = jnp.zeros_like(z_scr)
        m_scr[...] = jnp.zeros_like(m_scr)

    row = lax.broadcasted_iota(jnp.int32, (L, L), 0)
    col = lax.broadcasted_iota(jnp.int32, (L, L), 1)
    ones_blk = jnp.ones((L, LANES), BF16)

    def body(reverse):
        incl = (col >= row) if reverse else (col <= row)
        incl_t = (row >= col) if reverse else (row <= col)
        last = 0 if reverse else L - 1
        items = [(bb, h) for bb in range(nb) for h in range(heads)]
        idx = range(len(items))

        def sl(h):
            return slice(h * LANES, (h + 1) * LANES)

        qk = [_dot(q_ref[bb, :, sl(h)], k_ref[bb, :, sl(h)], NT) for bb, h in items]
        qz = [_dot(q_ref[bb, :, sl(h)], z_scr[bb, h]) for bb, h in items]
        cols, b_rows = [], []
        for bb in range(nb):
            b_cols = _dot_exact01(incl.astype(BF16), gc_ref[0, bb, :, heads:2 * heads], lhs01=True)
            b_rows.append(_dot_exact01(gr_ref[0, bb, heads:2 * heads, :], incl_t.astype(BF16)))
            cols.append(jnp.concatenate([b_cols, gc_ref[0, bb, :, 0:heads]], axis=1))
        pick = lax.broadcasted_iota(jnp.int32, (2 * heads, 2 * LANES), 0)
        lane2 = lax.broadcasted_iota(jnp.int32, (2 * heads, 2 * LANES), 1)
        stats = []
        for bb, h in items:
            sel = (pick == jnp.where(lane2 < LANES, h, heads + h)).astype(BF16)
            rep = _dot_exact01(cols[bb], sel)
            b_rep, ig_rep = rep[:, :LANES], rep[:, LANES:]
            ig_row = gr_ref[0, bb, h:h + 1, :]
            b_row = b_rows[bb][h:h + 1, :]
            m_prev = m_scr[bb, h:h + 1, :]
            cmax = _cummax_rows(ig_rep - b_rep, reverse)
            dmat = jnp.where(incl, b_rep - (b_row - ig_row), -jnp.inf)
            inter = b_rep + m_prev
            m_t = jnp.maximum(inter, b_rep + cmax)
            b_last = b_rep[last:last + 1, :]
            m_new = jnp.maximum(b_last + m_prev, b_last + cmax[last:last + 1, :])
            w_k = jnp.exp(b_last - b_rep + ig_rep - m_new)
            w_prev = jnp.exp(b_last + m_prev - m_new)
            stats.append((jnp.exp(dmat - m_t), jnp.exp(inter - m_t), jnp.exp(-m_t), w_k, w_prev, m_new))
        kv = []
        for i, (bb, h) in enumerate(items):
            w_k = stats[i][3]
            wv = jnp.concatenate([w_k * v_ref[bb, :, sl(h)].astype(F32), w_k], axis=1)
            kv.append(_dot(k_ref[bb, :, sl(h)].astype(F32).T, wv))
        s = [qk[i] * stats[i][0] for i in idx]
        sv = [_dot(s[i], jnp.concatenate([v_ref[bb, :, sl(h)], ones_blk], axis=1))
              for i, (bb, h) in enumerate(items)]
        for i, (bb, h) in enumerate(items):
            _, w_inter, floor, _, w_prev, m_new = stats[i]
            w2 = jnp.concatenate([w_inter, w_inter], axis=1)
            full = sv[i] + w2 * qz[i]
            den = full[:, LANES:]
            o_ref[0, bb, :, sl(h)] = (full[:, :LANES] / jnp.maximum(jnp.abs(den), floor)).astype(o_ref.dtype)
            z_scr[bb, h] = jnp.concatenate([w_prev, w_prev], axis=1) * z_scr[bb, h] + kv[i]
            m_scr[bb, h:h + 1, :] = m_new

    @pl.when(d == 0)
    def _():
        body(False)

    @pl.when(d == 1)
    def _():
        body(True)


def mlstm_scan(qk, z, gc, gr, dm, bsz, t):
    n = qk.shape[0]
    heads = dm // LANES
    L = ML_CHUNK
    nc, nctx = t // L, CTX_LEN // L
    nb = 2 if bsz % 2 == 0 else 1

    def chunk(d, p):
        return _scan_chunk_index(d, p, nctx, nc)

    qk3 = qk.reshape(bsz, t, 2 * dm)
    out = pl.pallas_call(
        functools.partial(_ml_scan_kernel, heads=heads, nb=nb),
        grid=(2, bsz // nb, nc),
        in_specs=[pl.BlockSpec((nb, L, dm), lambda d, b, p: (b, chunk(d, p), 0)),
                  pl.BlockSpec((nb, L, dm), lambda d, b, p: (b, chunk(d, p), 1)),
                  pl.BlockSpec((nb, L, dm), lambda d, b, p: (b, chunk(d, p), 2)),
                  pl.BlockSpec((1, nb, L, 2 * heads), lambda d, b, p: (d, b, chunk(d, p), 0)),
                  pl.BlockSpec((1, nb, 2 * heads, L), lambda d, b, p: (d, b, 0, chunk(d, p)))],
        out_specs=pl.BlockSpec((1, nb, L, dm), lambda d, b, p: (d, b, chunk(d, p), 0)),
        out_shape=jax.ShapeDtypeStruct((2, bsz, t, dm), BF16),
        scratch_shapes=[pltpu.VMEM((nb, heads, LANES, 2 * LANES), F32), pltpu.VMEM((nb, heads, LANES), F32)],
        compiler_params=pltpu.CompilerParams(
            dimension_semantics=("arbitrary", "arbitrary", "arbitrary"), vmem_limit_bytes=VMEM_LIMIT),
        name="mlstm_scan",
    )(qk3, qk3, z.reshape(bsz, t, 4 * dm), gc.reshape(2, bsz, t, 2 * heads), gr)
    return out.reshape(2, n, dm)


RW_STATE_PASSES = 3


def _rw_scan_kernel(lw0_ref, lw1_ref, kd0_ref, kd1_ref, a0_ref, a1_ref, r0_ref, r1_ref, v0_ref, v1_ref,
                    kk0_ref, kk1_ref, of_ref, ob_ref, h_scr, rdp_scr, o0_scr, m_scr, ha_scr, *, nchunk, npair):
    L = RW_CHUNK
    j = pl.program_id(2)

    @pl.when(j == 0)
    def _():
        for ref in (h_scr, rdp_scr, o0_scr, m_scr, ha_scr):
            ref[...] = jnp.zeros_like(ref)

    qls = [slice(q * LANES, (q + 1) * LANES) for q in range(npair)]
    hs = {(q, d): h_scr[q, d] for q in range(npair) for d in range(2)}

    def recurrence_step(k):
        for q in range(npair):
            for d, o_ref in ((0, of_ref), (1, ob_ref)):
                c = k if d == 0 else nchunk - 1 - k
                o_ref[pl.ds(c * L, L), qls[q]] = (_dot(rdp_scr[q, d, c], hs[q, d], passes=RW_STATE_PASSES)
                                                  + o0_scr[q, d, c]).astype(o_ref.dtype)
                hs[q, d] = _dot(m_scr[q, d, c], hs[q, d], passes=RW_STATE_PASSES) + ha_scr[q, d, c]

    pending = list(range(nchunk))

    half = LANES // 2
    row = lax.broadcasted_iota(jnp.int32, (L, LANES), 0)
    col = lax.broadcasted_iota(jnp.int32, (L, LANES), 1) % half
    eye2 = (row == col).astype(F32)
    lane = lax.broadcasted_iota(jnp.int32, (1, LANES), 1)
    m0 = (lane < half).astype(BF16)
    m1 = (lane >= half).astype(BF16)
    r2 = lax.broadcasted_iota(jnp.int32, (LANES, LANES), 0)
    c2 = lax.broadcasted_iota(jnp.int32, (LANES, LANES), 1)
    same_head = (r2 // half) == (c2 // half)

    def stack(x):
        xb = x.astype(BF16)
        return jnp.concatenate([xb * m0, xb * m1], axis=0)

    chains = [(q, d, c) for c in range(nchunk) for q in range(npair) for d in range(2)]
    st = {}
    for q, d, c in chains:
        reverse = d == 1
        sl = pl.ds(c * L, L)
        lw = (lw0_ref, lw1_ref)[d][sl, qls[q]]
        k = (kd0_ref, kd1_ref)[d][sl, qls[q]].astype(F32)
        kk = (kk0_ref, kk1_ref)[d][sl, qls[q]].astype(F32)
        akk = kk * (a0_ref, a1_ref)[d][sl, qls[q]].astype(F32)
        g = _cumsum_rows(lw, reverse)
        ieg = jnp.exp(-g)
        g_last = g[0:1] if reverse else g[L - 1:L]
        dl = jnp.exp(g_last - g)
        st[q, d, c] = dict(kd=kk * jnp.exp(g - lw), rd=(r0_ref, r1_ref)[d][sl, qls[q]].astype(F32) * jnp.exp(g),
                           ai=akk * ieg, ki=k * ieg, ad=akk * dl, kdd=k * dl, eg_last=jnp.exp(g_last),
                           v=(v0_ref, v1_ref)[d][sl, qls[q]].astype(F32))
    recurrence_step(pending.pop(0))
    for q, d, c in chains:
        s = st[q, d, c]
        reverse = d == 1
        incl = (col >= row) if reverse else (col <= row)
        strict = (col > row) if reverse else (col < row)
        x = jnp.concatenate([s["kd"], s["rd"]], axis=0)
        rhs = jnp.concatenate([stack(s["ai"]), stack(s["ki"])], axis=0)
        sc = _dot(x, rhs, NT)
        s["a_ab"] = jnp.where(strict, sc[:L, :LANES], 0.0)
        s["a_ak"] = jnp.where(strict, sc[:L, LANES:], 0.0)
        s["b_ra"] = jnp.where(incl, sc[L:, :LANES], 0.0)
        s["b_rk"] = jnp.where(incl, sc[L:, LANES:], 0.0)
        s["tinv"] = eye2 - s["a_ab"]
        s["pw"] = s["a_ab"]
    span = 2
    while span < L:
        for key in chains:
            s = st[key]
            s["pw"] = _dot(s["pw"], stack(s["pw"]))
        for key in chains:
            s = st[key]
            s["tinv"] = _dot(s["tinv"], stack(eye2 + s["pw"]))
        if pending:
            recurrence_step(pending.pop(0))
        span *= 2
    while pending:
        recurrence_step(pending.pop(0))
    for (q, d), h in hs.items():
        h_scr[q, d] = h
    for key in chains:
        s = st[key]
        s["w"] = -_dot(s["tinv"], stack(s["a_ak"]))
        s["kdp"] = _dot(s["tinv"], stack(s["kd"]))
    for key in chains:
        s = st[key]
        s["vst"] = stack(s["v"])
        s["u0"] = _dot(s["w"], s["vst"])
    for key in chains:
        s = st[key]
        lhs = jnp.concatenate([s["b_ra"], s["b_rk"]], axis=1)
        rhs = jnp.concatenate([stack(s["u0"]), s["vst"]], axis=0)
        o0_scr[key] = _dot(lhs, rhs)
        rdp_scr[key] = s["rd"] - _dot(s["b_ra"], stack(s["kdp"]))
        diag = jnp.where(r2 == c2, s["eg_last"], 0.0)
        m_scr[key] = jnp.where(same_head, diag - _dot(s["ad"].T, s["kdp"]), 0.0)
        at = jnp.concatenate([s["ad"], s["kdd"]], axis=0).T
        ha_scr[key] = jnp.where(same_head, _dot(at, jnp.concatenate([s["u0"], s["v"]], axis=0)), 0.0)


def rwkv_scan(lw, kda, rvkg, dm, bsz, t):
    n = lw.shape[0]
    pairs = dm // LANES
    L = RW_CHUNK
    nchunk = RW_PRE_CHUNKS
    tb = nchunk * L
    nblk, nctx = t // tb, CTX_LEN // tb

    def block(d, b, j):
        return b * nblk + _scan_chunk_index(d, jnp.minimum(j, nblk - 1), nctx, nblk)

    npair = RW_PAIRS_PER_STEP
    width = npair * LANES
    groups = pairs // npair

    def ispec(d, col):
        return pl.BlockSpec((tb, width), lambda b, p, j: (block(d, b, j), col * groups + p))

    def ospec(d):
        return pl.BlockSpec((tb, width), lambda b, p, j: (block(d, b, jnp.maximum(j - 1, 0)), p))

    return pl.pallas_call(
        functools.partial(_rw_scan_kernel, nchunk=nchunk, npair=npair),
        grid=(bsz, groups, nblk + 1),
        in_specs=[ispec(0, 0), ispec(1, 1), ispec(0, 0), ispec(1, 1), ispec(0, 2), ispec(1, 3),
                  ispec(0, 0), ispec(1, 0), ispec(0, 1), ispec(1, 1), ispec(0, 2), ispec(1, 2)],
        out_specs=[ospec(0), ospec(1)],
        out_shape=[jax.ShapeDtypeStruct((n, dm), BF16), jax.ShapeDtypeStruct((n, dm), BF16)],
        scratch_shapes=[pltpu.VMEM((npair, 2, LANES, LANES), F32), pltpu.VMEM((npair, 2, nchunk, L, LANES), F32),
                        pltpu.VMEM((npair, 2, nchunk, L, LANES), F32),
                        pltpu.VMEM((npair, 2, nchunk, LANES, LANES), F32),
                        pltpu.VMEM((npair, 2, nchunk, LANES, LANES), F32)],
        compiler_params=pltpu.CompilerParams(
            dimension_semantics=("arbitrary", "arbitrary", "arbitrary"), vmem_limit_bytes=VMEM_LIMIT),
        name="rwkv_scan",
    )(lw, lw, kda, kda, kda, kda, rvkg, rvkg, rvkg, rvkg, rvkg, rvkg)


HALO_ROWS = 8


def _group_sum(x, width):
    r = lax.broadcasted_iota(jnp.int32, (LANES, LANES), 0) // width
    c = lax.broadcasted_iota(jnp.int32, (LANES, LANES), 1) // width
    ones = (r == c).astype(BF16)
    hi = x.astype(BF16)
    lo = (x - hi.astype(F32)).astype(BF16)
    parts = []
    for j in range(x.shape[1] // LANES):
        sl = slice(j * LANES, (j + 1) * LANES)
        parts.append(jnp.dot(hi[:, sl], ones, preferred_element_type=F32)
                     + jnp.dot(lo[:, sl], ones, preferred_element_type=F32))
    return jnp.concatenate(parts, axis=1)


def _rw_proj_kernel(s_ref, up_ref, dn_ref, gain_ref, shift_ref, mu_ref, wrkv_ref, w1_ref, w2_ref, w0_ref,
                    a1_ref, a2_ref, a0_ref, g1_ref, g2_ref, kk_ref, ka_ref,
                    lw_ref, kda_ref, rvkg_ref, *, tpb, nctx_t):
    tm, dm = s_ref.shape
    ti = pl.program_id(0) % tpb
    is_ctx = ti < nctx_t
    has_up = jnp.logical_not(jnp.logical_or(is_ctx, ti == nctx_t))
    has_dn = jnp.logical_not(jnp.logical_or(is_ctx, ti == tpb - 1))
    gain = gain_ref[0]
    shift = shift_ref[0]
    u = _norm_mod(s_ref[...], gain, shift)
    u_up = jnp.where(has_up, _norm_mod(up_ref[HALO_ROWS - 1:HALO_ROWS, :], gain, shift), 0.0)
    u_dn = jnp.where(has_dn, _norm_mod(dn_ref[0:1, :], gain, shift), 0.0)
    row = lax.broadcasted_iota(jnp.int32, (tm, 1), 0)
    u_m = jnp.where(row == 0, u_up, pltpu.roll(u, 1, axis=0))
    u_p = jnp.where(row == tm - 1, u_dn, pltpu.roll(u, tm - 1, axis=0))
    du = 0.5 * (u_m + u_p) - u
    mu = mu_ref[...]

    def mix(i):
        return (u + du * mu[i:i + 1]).astype(BF16)

    def dot(a, b):
        return jnp.dot(a.astype(BF16), b, preferred_element_type=F32)

    r = dot(mix(0), wrkv_ref[0])
    k = dot(mix(1), wrkv_ref[1])
    v = dot(mix(2), wrkv_ref[2])
    w_pre = dot(jnp.tanh(dot(mix(3), w1_ref[...])), w2_ref[...]) + w0_ref[...]
    lw_ref[...] = -jnp.exp(-_softplus(-w_pre) - 0.5)
    a = jax.nn.sigmoid(dot(dot(mix(4), a1_ref[...]), a2_ref[...]) + a0_ref[...])
    g = dot(jax.nn.sigmoid(dot(mix(5), g1_ref[...])), g2_ref[...])
    kk = k * kk_ref[...]
    kk = kk * lax.rsqrt(jnp.maximum(_group_sum(kk * kk, RW_HEAD_DIM), 1e-24))
    ka = ka_ref[...]
    for d in range(2):
        kda_ref[:, d * dm:(d + 1) * dm] = (k * (1.0 + (a[:, d * dm:(d + 1) * dm] - 1.0) * ka)).astype(kda_ref.dtype)
    kda_ref[:, 2 * dm:] = a.astype(kda_ref.dtype)
    for j, val in enumerate((r, v, kk, g)):
        rvkg_ref[:, j * dm:(j + 1) * dm] = val.astype(rvkg_ref.dtype)


def rwkv_proj(s, gain, shift, geom, mu, w_rkv, w0, w1, w2, a0, a1, a2, g1, g2, k_k, k_a):
    n, dm = s.shape
    tpb, nctx_t = geom
    assert nctx_t == 1
    seg = _seg_map(tpb, nctx_t)
    hb = ROW_TILE // HALO_ROWS
    last = n // HALO_ROWS - 1
    lora = w1.shape[2]

    def blockdiag(w):
        z = jnp.zeros_like(w[0])
        return jnp.concatenate([jnp.concatenate([w[0], z], axis=1), jnp.concatenate([z, w[1]], axis=1)], axis=0)

    consts = [jnp.pad(mu, ((0, HALO_ROWS - mu.shape[0]), (0, 0))), w_rkv.astype(BF16),
              jnp.concatenate([w1[0], w1[1]], axis=1).astype(BF16), blockdiag(w2).astype(BF16),
              jnp.concatenate([w0[0], w0[1]])[None],
              jnp.concatenate([a1[0], a1[1]], axis=1).astype(BF16), blockdiag(a2).astype(BF16),
              jnp.concatenate([a0[0], a0[1]])[None],
              g1.astype(BF16), g2.astype(BF16), k_k[None], k_a[None]]

    def const_spec(x):
        nd = x.ndim
        return pl.BlockSpec(x.shape, lambda i: (0,) * nd)

    return pl.pallas_call(
        functools.partial(_rw_proj_kernel, tpb=tpb, nctx_t=nctx_t),
        grid=(n // ROW_TILE,),
        in_specs=[pl.BlockSpec((ROW_TILE, dm), lambda i: (i, 0)),
                  pl.BlockSpec((HALO_ROWS, dm), lambda i: (jnp.maximum(i * hb - 1, 0), 0)),
                  pl.BlockSpec((HALO_ROWS, dm), lambda i: (jnp.minimum((i + 1) * hb, last), 0)),
                  pl.BlockSpec((1, 1, dm), lambda i: (seg(i), 0, 0)),
                  pl.BlockSpec((1, 1, dm), lambda i: (seg(i), 0, 0))] + [const_spec(x) for x in consts],
        out_specs=[pl.BlockSpec((ROW_TILE, 2 * dm), lambda i: (i, 0)),
                   pl.BlockSpec((ROW_TILE, 4 * dm), lambda i: (i, 0)),
                   pl.BlockSpec((ROW_TILE, 4 * dm), lambda i: (i, 0))],
        out_shape=[jax.ShapeDtypeStruct((n, 2 * dm), F32), jax.ShapeDtypeStruct((n, 4 * dm), BF16),
                   jax.ShapeDtypeStruct((n, 4 * dm), BF16)],
        compiler_params=pltpu.CompilerParams(dimension_semantics=("arbitrary",), vmem_limit_bytes=VMEM_LIMIT_BIG),
        name="rwkv_proj",
    )(s, s, s, gain, shift, *consts)


def _rw_post_kernel(of_ref, ob_ref, r_ref, v_ref, g_ref, k0_ref, k1_ref, s_ref, lnw_ref, lnb_ref, rk_ref, gm_ref,
                    w_ref, out_ref, *, sub, seg):
    o = of_ref[...].astype(F32) + ob_ref[...].astype(F32)
    inv = 1.0 / RW_HEAD_DIM
    mean = _group_sum(o, RW_HEAD_DIM) * inv
    oc = o - mean
    var = _group_sum(oc * oc, RW_HEAD_DIM) * inv
    xn = oc * lax.rsqrt(var + RW_GN_EPS) * lnw_ref[...] + lnb_ref[...]
    r = r_ref[...].astype(F32)
    ksum = k0_ref[...].astype(F32) + k1_ref[...].astype(F32)
    bonus = _group_sum(r * ksum * rk_ref[...], RW_HEAD_DIM) * v_ref[...].astype(F32)
    y = ((xn + bonus) * g_ref[...].astype(F32)).astype(BF16)
    _gated_residual_store(out_ref, s_ref, gm_ref, jnp.dot(y, w_ref[...], preferred_element_type=F32), sub, seg)


def rwkv_post(o_f, o_b, rvkg, kda, s, ln_w, ln_b, r_k, gm, w_out, geom):
    n, dm = s.shape
    seg = _seg_map(*geom)
    sub = _sub_tiles(n, most=2)
    tm = sub * ROW_TILE

    def col(block):
        return pl.BlockSpec((tm, dm), lambda i: (i, block))

    vec = pl.BlockSpec((1, dm), lambda i: (0, 0))
    return pl.pallas_call(
        functools.partial(_rw_post_kernel, sub=sub, seg=seg),
        grid=(n // tm,),
        in_specs=[col(0), col(0), col(0), col(1), col(3), col(0), col(1),
                  col(0), vec, vec, vec, pl.BlockSpec(gm.shape, lambda i: (0, 0, 0)),
                  pl.BlockSpec((dm, dm), lambda i: (0, 0))],
        out_specs=pl.BlockSpec((tm, dm), lambda i: (i, 0)),
        out_shape=jax.ShapeDtypeStruct((n, dm), F32),
        compiler_params=pltpu.CompilerParams(dimension_semantics=("arbitrary",), vmem_limit_bytes=VMEM_LIMIT),
        name="rwkv_post",
    )(o_f, o_b, rvkg, rvkg, rvkg, kda, kda, s, ln_w[None], ln_b[None], r_k[None], gm, w_out.astype(BF16))


def _hg_scan_kernel(q_ref, v_ref, lf_ref, o_ref, st_scr, *, heads):
    C = q_ref.shape[0]
    d = pl.program_id(0)
    nsub = C // HG_SUB

    @pl.when(pl.program_id(2) == 0)
    def _():
        st_scr[...] = jnp.zeros_like(st_scr)

    def body(reverse):
        last = 0 if reverse else C - 1
        hs = range(heads)
        sls = [slice(h * LANES, (h + 1) * LANES) for h in hs]
        g = [lf_ref[:, sls[h]] for h in hs]
        b = [_cumsum_rows(g[h], reverse) for h in hs]
        k = [-jnp.tanh(0.5 * g[h]) * (jnp.exp(g[h]) + 1.0) for h in hs]
        o_inter = [_dot(q_ref[:, sls[h]].astype(F32) * jnp.exp(b[h]), st_scr[h], NT) for h in hs]
        parts = [[None] * nsub for _ in hs]
        for i in range(nsub):
            r0 = i * HG_SUB
            lo, hi = (r0, C) if reverse else (0, r0 + HG_SUB)
            first = r0 + HG_SUB - 1 if reverse else r0
            row = lax.broadcasted_iota(jnp.int32, (HG_SUB, hi - lo), 0) + r0
            col = lax.broadcasted_iota(jnp.int32, (HG_SUB, hi - lo), 1) + lo
            keep = (col >= row) if reverse else (col <= row)
            att = []
            for h in hs:
                rho = b[h][first:first + 1, :] - g[h][first:first + 1, :]
                qi = q_ref[r0:r0 + HG_SUB, sls[h]].astype(F32) * jnp.exp(b[h][r0:r0 + HG_SUB] - rho)
                ki = k[h][lo:hi] * jnp.exp(jnp.minimum(rho - b[h][lo:hi], HG_EXP_CLAMP))
                att.append(jnp.where(keep, _dot(qi, ki, NT), 0.0))
            for h in hs:
                parts[h][i] = _dot(att[h], v_ref[lo:hi, sls[h]])
        for h in hs:
            o_ref[0, :, sls[h]] = (o_inter[h] + jnp.concatenate(parts[h], axis=0)).astype(o_ref.dtype)
        upd = []
        for h in hs:
            b_last = b[h][last:last + 1, :]
            upd.append((jnp.exp(b_last),
                        _dot(v_ref[:, sls[h]].astype(F32).T, k[h] * jnp.exp(b_last - b[h]))))
        for h in hs:
            st_scr[h] = st_scr[h] * upd[h][0] + upd[h][1]

    @pl.when(d == 0)
    def _():
        body(False)

    @pl.when(d == 1)
    def _():
        body(True)


def hgrn_scan(z, logf, dm, bsz, t):
    n = z.shape[0]
    heads = dm // LANES
    C = HG_CHUNK
    nc, nctx = t // C, CTX_LEN // C

    def row(d, b, p):
        return b * nc + _scan_chunk_index(d, p, nctx, nc)

    return pl.pallas_call(
        functools.partial(_hg_scan_kernel, heads=heads),
        grid=(2, bsz, nc),
        in_specs=[pl.BlockSpec((C, dm), lambda d, b, p: (row(d, b, p), 0)),
                  pl.BlockSpec((C, dm), lambda d, b, p: (row(d, b, p), 1)),
                  pl.BlockSpec((C, dm), lambda d, b, p: (row(d, b, p), d))],
        out_specs=pl.BlockSpec((1, C, dm), lambda d, b, p: (d, row(d, b, p), 0)),
        out_shape=jax.ShapeDtypeStruct((2, n, dm), BF16),
        scratch_shapes=[pltpu.VMEM((heads, LANES, LANES), F32)],
        compiler_params=pltpu.CompilerParams(
            dimension_semantics=("arbitrary", "arbitrary", "arbitrary"), vmem_limit_bytes=VMEM_LIMIT),
        name="hgrn_scan",
    )(z, z, logf)


def _first_argmax(vals):
    best, idx = vals[0], jnp.zeros(vals[0].shape, jnp.int32)
    for i in range(1, len(vals)):
        better = vals[i] > best
        best = jnp.where(better, vals[i], best)
        idx = jnp.where(better, i, idx)
    return best, idx


def _router_kernel(s_ref, gain_ref, shift_ref, wt_ref, b_ref, e_ref, g_ref, rank_ref, cnt_ref, carry_scr, *,
                   n_groups, top_k):
    n_experts = wt_ref.shape[0]
    per = n_experts // n_groups
    h = _norm_mod(s_ref[...], gain_ref[0], shift_ref[0])
    aff = jax.nn.sigmoid(_dot(wt_ref[...], h, NT, passes=3))
    sel = aff + b_ref[...]
    a = [aff[e:e + 1, :] for e in range(n_experts)]
    s = [sel[e:e + 1, :] for e in range(n_experts)]
    neg = jnp.full_like(s[0], -jnp.inf)
    scores = []
    for g in range(n_groups):
        grp = s[g * per:(g + 1) * per]
        m1, i1 = _first_argmax(grp)
        m2, _ = _first_argmax([jnp.where(i1 == j, neg, grp[j]) for j in range(per)])
        scores.append(m1 + m2)
    _, best = _first_argmax(scores)

    def in_best(rows):
        out = []
        for j in range(per):
            x = rows[j]
            for g in range(1, n_groups):
                x = jnp.where(best == g, rows[g * per + j], x)
            out.append(x)
        return out

    sb, ab = in_best(s), in_best(a)
    picked, chosen = [], []
    cand = sb
    for _ in range(top_k):
        _, i = _first_argmax(cand)
        c = ab[0]
        for j in range(1, per):
            c = jnp.where(i == j, ab[j], c)
        picked.append(i)
        chosen.append(c)
        cand = [jnp.where(i == j, neg, cand[j]) for j in range(per)]
    total = functools.reduce(jnp.add, chosen)
    experts = [best * per + picked[kk_] for kk_ in range(top_k)]
    for kk_ in range(top_k):
        e_ref[kk_:kk_ + 1, :] = experts[kk_]
        g_ref[kk_:kk_ + 1, :] = chosen[kk_] / total

    @pl.when(pl.program_id(0) == 0)
    def _():
        carry_scr[...] = jnp.zeros_like(carry_scr)

    tm = s_ref.shape[0]
    sub = lax.broadcasted_iota(jnp.int32, (n_experts, tm), 0)
    onehots = [(sub == ex).astype(F32) for ex in experts]
    tot = functools.reduce(jnp.add, onehots)
    ri = lax.broadcasted_iota(jnp.int32, (tm, tm), 0)
    ci = lax.broadcasted_iota(jnp.int32, (tm, tm), 1)
    earlier = (ri < ci).astype(BF16)
    before = jnp.dot(tot.astype(BF16), earlier, preferred_element_type=F32) + carry_scr[:, 0:1]
    seen = before
    for kk_ in range(top_k):
        rank_ref[kk_:kk_ + 1, :] = jnp.sum(onehots[kk_] * seen, axis=0, keepdims=True).astype(jnp.int32)
        seen = seen + onehots[kk_]
    carry = carry_scr[...] + jnp.sum(tot, axis=1, keepdims=True)
    carry_scr[...] = carry
    cnt_ref[...] = carry.astype(jnp.int32)


def norm_route(s, gain, shift, geom, router_w, router_b):
    n, k = s.shape
    n_experts = router_w.shape[1]
    tm = ROW_TILE
    seg = _seg_map(*geom)
    return pl.pallas_call(
        functools.partial(_router_kernel, n_groups=N_GROUPS, top_k=TOP_K),
        grid=(n // tm,),
        in_specs=[pl.BlockSpec((tm, k), lambda i: (i, 0)),
                  pl.BlockSpec((1, 1, k), lambda i: (seg(i), 0, 0)),
                  pl.BlockSpec((1, 1, k), lambda i: (seg(i), 0, 0)),
                  pl.BlockSpec((n_experts, k), lambda i: (0, 0)),
                  pl.BlockSpec((n_experts, 1), lambda i: (0, 0))],
        out_specs=[pl.BlockSpec((TOP_K, tm), lambda i: (0, i)), pl.BlockSpec((TOP_K, tm), lambda i: (0, i)),
                   pl.BlockSpec((TOP_K, tm), lambda i: (0, i)), pl.BlockSpec((n_experts, LANES), lambda i: (0, 0))],
        out_shape=[jax.ShapeDtypeStruct((TOP_K, n), jnp.int32), jax.ShapeDtypeStruct((TOP_K, n), F32),
                   jax.ShapeDtypeStruct((TOP_K, n), jnp.int32), jax.ShapeDtypeStruct((n_experts, LANES), jnp.int32)],
        scratch_shapes=[pltpu.VMEM((n_experts, LANES), F32)],
        compiler_params=pltpu.CompilerParams(dimension_semantics=("arbitrary",), vmem_limit_bytes=VMEM_LIMIT),
        name="norm_route",
    )(s, gain, shift, router_w.T, router_b.reshape(n_experts, 1).astype(F32))


def _final_norm_kernel(s_ref, g_ref, o_ref):
    x = s_ref[...]
    o_ref[0] = x * lax.rsqrt(jnp.mean(x * x, axis=-1, keepdims=True) + NORM_EPS) * g_ref[...]


def final_norm(s, g, bsz, t, geom):
    dm = s.shape[1]
    tpb, nctx_t = geom
    return pl.pallas_call(
        _final_norm_kernel,
        grid=(bsz, tpb - nctx_t),
        in_specs=[pl.BlockSpec((ROW_TILE, dm), lambda b, i: (b * tpb + nctx_t + i, 0)),
                  pl.BlockSpec((1, dm), lambda b, i: (0, 0))],
        out_specs=pl.BlockSpec((1, ROW_TILE, dm), lambda b, i: (b, i, 0)),
        out_shape=jax.ShapeDtypeStruct((bsz, t - CTX_LEN, dm), F32),
        name="final_norm",
    )(s, g[None])


DMA_UNROLL = 8


def _row_copy_waits(src_row, dst_row, sem, count):
    def body(_, carry):
        pltpu.make_async_copy(src_row, dst_row, sem).wait()
        return carry
    lax.fori_loop(0, count, body, 0)


def _scatter_kernel(dest_ref, meta_ref, s_ref, gain_ref, shift_ref, xb_ref, hbuf, zrow, sems, zsem, *, n_experts):
    i = pl.program_id(0)
    last = pl.num_programs(0) - 1
    slot = i % 2
    per_tile = TOP_K * ROW_TILE

    def wait_tile(sl):
        for _ in range(TOP_K):
            pltpu.make_async_copy(hbuf.at[sl], xb_ref.at[pl.ds(0, ROW_TILE), :], sems.at[sl]).wait()

    @pl.when(i >= 2)
    def _():
        wait_tile(slot)

    hbuf[slot] = _norm_mod(s_ref[...], gain_ref[0], shift_ref[0])

    def issue(r8, carry):
        for u in range(DMA_UNROLL):
            r = r8 * DMA_UNROLL + u
            for k in range(TOP_K):
                d = dest_ref[0, 0, TOP_K * r + k]
                pltpu.make_async_copy(hbuf.at[slot, pl.ds(r, 1), :], xb_ref.at[pl.ds(d, 1), :],
                                      sems.at[slot]).start(priority=(TOP_K * u + k) % 2)
        return carry
    lax.fori_loop(0, ROW_TILE // DMA_UNROLL, issue, 0)

    @pl.when(i == last)
    def _():
        @pl.when(last >= 1)
        def _():
            wait_tile(1 - slot)
        wait_tile(slot)
        zrow[...] = jnp.zeros_like(zrow)
        for e in range(n_experts):
            lo = meta_ref[2, e] + meta_ref[0, e]
            hi = meta_ref[2, e] + meta_ref[1, e]

            def pad_start(q, carry):
                pltpu.make_async_copy(zrow.at[pl.ds(0, 1), :], xb_ref.at[pl.ds(q, 1), :], zsem.at[0]).start()
                return carry
            lax.fori_loop(lo, hi, pad_start, 0)
            _row_copy_waits(zrow.at[pl.ds(0, 1), :], xb_ref.at[pl.ds(0, 1), :], zsem.at[0], hi - lo)
        end = meta_ref[2, n_experts - 1] + meta_ref[1, n_experts - 1]

        def tail_start(q, carry):
            pltpu.make_async_copy(zrow.at[pl.ds(0, 1), :], xb_ref.at[pl.ds(q, 1), :], zsem.at[0]).start()
            return carry
        lax.fori_loop(end, xb_ref.shape[0], tail_start, 0)
        _row_copy_waits(zrow.at[pl.ds(0, 1), :], xb_ref.at[pl.ds(0, 1), :], zsem.at[0], xb_ref.shape[0] - end)


def moe_scatter(s, gain, shift, geom, dest, meta, n_slots):
    n, dm = s.shape
    seg = _seg_map(*geom)
    nt = n // ROW_TILE
    return pl.pallas_call(
        functools.partial(_scatter_kernel, n_experts=meta.shape[1]),
        grid=(nt,),
        in_specs=[pl.BlockSpec((1, 1, TOP_K * ROW_TILE), lambda i: (i, 0, 0), memory_space=pltpu.SMEM),
                  pl.BlockSpec(memory_space=pltpu.SMEM),
                  pl.BlockSpec((ROW_TILE, dm), lambda i: (i, 0)),
                  pl.BlockSpec((1, 1, dm), lambda i: (seg(i), 0, 0)),
                  pl.BlockSpec((1, 1, dm), lambda i: (seg(i), 0, 0))],
        out_specs=pl.BlockSpec(memory_space=pl.ANY),
        out_shape=jax.ShapeDtypeStruct((n_slots, dm), F32),
        scratch_shapes=[pltpu.VMEM((2, ROW_TILE, dm), F32), pltpu.VMEM((8, dm), F32),
                        pltpu.SemaphoreType.DMA((2,)), pltpu.SemaphoreType.DMA((1,))],
        compiler_params=pltpu.CompilerParams(dimension_semantics=("arbitrary",), vmem_limit_bytes=VMEM_LIMIT),
        name="moe_scatter",
    )(dest.reshape(nt, 1, TOP_K * ROW_TILE), meta, s, gain, shift)


def _gather_combine_kernel(dcur_ref, dnxt_ref, g_ref, s_ref, gm_ref, yb_ref, o_ref, ybuf, sems):
    i = pl.program_id(0)
    nsteps = pl.num_programs(0)
    slot = i % 2

    def start_tile(dref, sl):
        def issue(r8, carry):
            for u in range(DMA_UNROLL):
                r = r8 * DMA_UNROLL + u
                for k in range(TOP_K):
                    d = dref[0, 0, TOP_K * r + k]
                    pltpu.make_async_copy(yb_ref.at[pl.ds(d, 1), :], ybuf.at[sl, k, pl.ds(r, 1), :],
                                          sems.at[sl]).start(priority=(TOP_K * u + k) % 2)
            return carry
        lax.fori_loop(0, ROW_TILE // DMA_UNROLL, issue, 0)

    @pl.when(i == 0)
    def _():
        start_tile(dcur_ref, 0)

    @pl.when(i + 1 < nsteps)
    def _():
        start_tile(dnxt_ref, 1 - slot)

    for k in range(TOP_K):
        pltpu.make_async_copy(yb_ref.at[pl.ds(0, ROW_TILE), :], ybuf.at[slot, k], sems.at[slot]).wait()
    g = g_ref[...]
    y = sum(ybuf[slot, k] * g[:, k:k + 1] for k in range(TOP_K))
    o_ref[...] = s_ref[...] + gm_ref[0] * y


def moe_gather_combine(yb, dest, gate, s, gm, geom):
    n, dm = s.shape
    seg = _seg_map(*geom)
    nt = n // ROW_TILE
    d3 = dest.reshape(nt, 1, TOP_K * ROW_TILE)
    row = pl.BlockSpec((ROW_TILE, dm), lambda i: (i, 0))
    return pl.pallas_call(
        _gather_combine_kernel,
        grid=(nt,),
        in_specs=[pl.BlockSpec((1, 1, TOP_K * ROW_TILE), lambda i: (i, 0, 0), memory_space=pltpu.SMEM),
                  pl.BlockSpec((1, 1, TOP_K * ROW_TILE), lambda i: (jnp.minimum(i + 1, nt - 1), 0, 0),
                               memory_space=pltpu.SMEM),
                  pl.BlockSpec((ROW_TILE, TOP_K), lambda i: (i, 0)), row,
                  pl.BlockSpec((1, 1, dm), lambda i: (seg(i), 0, 0)),
                  pl.BlockSpec(memory_space=pl.ANY)],
        out_specs=row,
        out_shape=jax.ShapeDtypeStruct((n, dm), F32),
        scratch_shapes=[pltpu.VMEM((2, TOP_K, ROW_TILE, dm), F32), pltpu.SemaphoreType.DMA((2,))],
        compiler_params=pltpu.CompilerParams(dimension_semantics=("arbitrary",), vmem_limit_bytes=VMEM_LIMIT),
        name="moe_gather_combine",
    )(d3, d3, gate, s, gm, yb)


def _ffn_kernel(be_ref, x_ref, w1_ref, w3_ref, w2_ref, o_ref):
    del be_ref
    x = x_ref[...].astype(BF16)
    a = jnp.dot(x, w1_ref[0], preferred_element_type=F32)
    b = jnp.dot(x, w3_ref[0], preferred_element_type=F32)
    hid = (a * jax.nn.sigmoid(a) * b).astype(BF16)
    o_ref[...] = jnp.dot(hid, w2_ref[0], preferred_element_type=F32).astype(o_ref.dtype)


def expert_ffn(xb, block_expert, w1, w3, w2):
    nrows, dm = xb.shape
    f = w1.shape[2]
    nb = nrows // MOE_BLOCK
    return pl.pallas_call(
        _ffn_kernel,
        grid_spec=pltpu.PrefetchScalarGridSpec(
            num_scalar_prefetch=1,
            grid=(nb,),
            in_specs=[pl.BlockSpec((MOE_BLOCK, dm), lambda i, be: (i, 0)),
                      pl.BlockSpec((1, dm, f), lambda i, be: (be[i], 0, 0)),
                      pl.BlockSpec((1, dm, f), lambda i, be: (be[i], 0, 0)),
                      pl.BlockSpec((1, f, dm), lambda i, be: (be[i], 0, 0))],
            out_specs=pl.BlockSpec((MOE_BLOCK, dm), lambda i, be: (i, 0))),
        out_shape=jax.ShapeDtypeStruct((nrows, dm), F32),
        compiler_params=pltpu.CompilerParams(
            dimension_semantics=("arbitrary",), vmem_limit_bytes=VMEM_LIMIT),
        name="expert_ffn",
    )(block_expert, xb, w1.astype(BF16), w3.astype(BF16), w2.astype(BF16))


def _mlstm_layer(s, gain, shift, gate_mod, geom, bsz, t, w_in, w_gate, b_gate, conv, head_g, w_out):
    n, dm = s.shape
    heads = ML_HEADS
    z = norm_mod_mm(s, gain, shift, w_in, None, (None, None, None, "sigmoid"), geom)
    scale = jnp.concatenate([jnp.ones((dm,), F32), jnp.full((dm,), (dm // heads) ** -0.5, F32)])
    qk = conv_silu(z, conv, scale, 2 * dm, geom)
    ng = 4 * heads
    wg = jnp.pad(jnp.concatenate([w_gate[0], w_gate[1]], axis=1), ((0, 0), (0, LANES - ng)))
    bg = jnp.pad(jnp.concatenate([b_gate[0], b_gate[1]]), (0, LANES - ng))
    gates = norm_mod_mm(s, gain, shift, wg, bg, (None,), geom, out_dtype=F32)[:, :ng]
    gates = gates.reshape(bsz, t, 2, 2 * heads)
    gates = jnp.concatenate([gates[..., :heads], jax.nn.log_sigmoid(gates[..., heads:])], axis=-1)
    gc = jnp.moveaxis(gates, 2, 0).reshape(2, n, 2 * heads)
    gr = jnp.transpose(gates, (2, 0, 3, 1))
    h = mlstm_scan(qk, z, gc, gr, dm, bsz, t)
    return post_mm_residual(h, z, 3, s, head_g, gate_mod, w_out, heads, geom)


def _rwkv7_layer(s, gain, shift, gate_mod, geom, bsz, t, mu, w_rkv, w0, w1, w2, a0, a1, a2, g1, g2,
                 k_k, k_a, r_k, ln_w, ln_b, w_out):
    dm = s.shape[1]
    lw, kda, rvkg = rwkv_proj(s, gain, shift, geom, mu, w_rkv, w0, w1, w2, a0, a1, a2, g1, g2, k_k, k_a)
    o_f, o_b = rwkv_scan(lw, kda, rvkg, dm, bsz, t)
    return rwkv_post(o_f, o_b, rvkg, kda, s, ln_w, ln_b, r_k, gate_mod, w_out, geom)


def _hgrn2_layer(s, gain, shift, gate_mod, geom, bsz, t, layer_idx, w_in, w_f, b_f, lb_logits, head_g, w_out):
    dm = s.shape[1]
    z = norm_mod_mm(s, gain, shift, w_in, None, ("silu", None, "silu"), geom)
    p = jax.nn.softmax(lb_logits, axis=0)
    lb = jnp.cumsum(p, axis=0)[layer_idx] - p[0]
    aux = jnp.tile(jnp.stack([jnp.log(lb), jnp.log1p(-lb)]), (1, 2))
    log_f = norm_mod_mm(s, gain, shift, jnp.concatenate([w_f[0], w_f[1]], axis=1),
                        jnp.concatenate([b_f[0], b_f[1]]), ("logf", "logf"), geom, aux=aux, out_dtype=F32)
    o = hgrn_scan(z, log_f, dm, bsz, t)
    return post_mm_residual(o, z, 2, s, head_g, gate_mod, w_out, dm // HG_EXPAND, geom)


def _moe_layer(s, gain, shift, gate_mod, geom, router_w, router_b, w1, w3, w2):
    n_tok, d = s.shape
    n_experts = w1.shape[0]
    n_assign = n_tok * TOP_K
    e, g, rank, cnt = norm_route(s, gain, shift, geom, router_w, router_b)
    flat_e = e.T.reshape(n_assign)
    rank = rank.T.reshape(n_assign)
    counts = cnt[:, 0]
    padded = (counts + MOE_BLOCK - 1) // MOE_BLOCK * MOE_BLOCK
    end_pad = jnp.cumsum(padded)
    start_pad = end_pad - padded
    onehot = flat_e[:, None] == jnp.arange(n_experts, dtype=jnp.int32)[None, :]
    dest = jnp.sum(jnp.where(onehot, start_pad[None, :], 0), axis=1) + rank
    n_blocks = -(-n_assign // MOE_BLOCK) + n_experts
    block_start = jnp.arange(n_blocks, dtype=jnp.int32) * MOE_BLOCK
    block_expert = jnp.minimum(jnp.sum(end_pad[None, :] <= block_start[:, None], axis=1), n_experts - 1)
    meta = jnp.stack([counts, padded, start_pad]).astype(jnp.int32)
    xb = moe_scatter(s, gain, shift, geom, dest.astype(jnp.int32), meta, n_blocks * MOE_BLOCK)
    yb = expert_ffn(xb, block_expert.astype(jnp.int32), w1, w3, w2)
    return moe_gather_combine(yb, dest.astype(jnp.int32), g.T, s, gate_mod, geom)


def kernel(x, c, ctx, c_ctx, ada_w, ada_b, norm_mix, norm_ffn, norm_out, ml_w_in, ml_w_gate, ml_b_gate, ml_conv, ml_head_g, ml_w_out, rw_mu, rw_w_rkv, rw_w0, rw_w1, rw_w2, rw_a0, rw_a1, rw_a2, rw_g1, rw_g2, rw_k_k, rw_k_a, rw_r_k, rw_ln_w, rw_ln_b, rw_w_out, hg_w_in, hg_w_f, hg_b_f, hg_lb_logits, hg_head_g, hg_w_out, router_w, router_b, ex_w1, ex_w3, ex_w2):
    depth = ada_w.shape[0]
    bsz = x.shape[0]
    cond = jax.nn.silu(jnp.concatenate([c, c_ctx[None]], axis=0))
    cond = jnp.pad(cond, ((0, -(bsz + 1) % 8), (0, 0)))
    dm = x.shape[2]
    t = CTX_LEN + x.shape[1]
    n = bsz * t
    geom = (t // ROW_TILE, CTX_LEN // ROW_TILE)
    s = jnp.concatenate([ctx, x], axis=1).reshape(n, dm)
    for i in range(depth):
        mod = mm(cond, ada_w[i], bias=ada_b[i])
        mod_x = jnp.split(mod[:bsz, None, :], 6, axis=-1)
        mod_c = jnp.split(mod[bsz], 6, axis=-1)

        def table(idx):
            return jnp.stack([jnp.broadcast_to(mod_c[idx], (bsz, dm)), mod_x[idx][:, 0]], axis=1).reshape(2 * bsz, 1, dm)

        kind, j = i % N_MIXERS, i // N_MIXERS
        if kind == 2:
            s = _hgrn2_layer(s, norm_mix[i] * (1 + table(1)), table(0), table(2), geom, bsz, t, i,
                             hg_w_in[j], hg_w_f[j], hg_b_f[j], hg_lb_logits, hg_head_g[j], hg_w_out[j])
        elif kind == 0:
            s = _mlstm_layer(s, norm_mix[i] * (1 + table(1)), table(0), table(2), geom, bsz, t,
                             ml_w_in[j], ml_w_gate[j], ml_b_gate[j], ml_conv[j], ml_head_g[j], ml_w_out[j])
        else:
            s = _rwkv7_layer(s, norm_mix[i] * (1 + table(1)), table(0), table(2), geom, bsz, t,
                             rw_mu[j], rw_w_rkv[j], rw_w0[j], rw_w1[j], rw_w2[j], rw_a0[j],
                             rw_a1[j], rw_a2[j], rw_g1[j], rw_g2[j], rw_k_k[j], rw_k_a[j],
                             rw_r_k[j], rw_ln_w[j], rw_ln_b[j], rw_w_out[j])
        s = _moe_layer(s, norm_ffn[i] * (1 + table(4)), table(3), table(5), geom, router_w, router_b,
                       ex_w1[i], ex_w3[i], ex_w2[i])
    return final_norm(s, norm_out, bsz, t, geom)
```

```python
import functools

import jax
import jax.numpy as jnp
from jax import lax
from jax.experimental import pallas as pl
from jax.experimental.pallas import tpu as pltpu

F32 = jnp.float32
BF16 = jnp.bfloat16

GRID_W = 64
CTX_LEN = 256
N_MIXERS = 3
NORM_EPS = 1e-6
ML_HEADS = 8
RW_HEAD_DIM = 64
RW_GN_EPS = 64e-5
HG_EXPAND = 128
N_GROUPS = 4
TOP_K = 2
MOE_BLOCK = 512

LANES = 128
ML_CHUNK = 128
RW_CHUNK = 64
RW_PRE_CHUNKS = 4
RW_PAIRS_PER_STEP = 4
HG_CHUNK = 128
HG_SUB = 16
HG_EXP_CLAMP = 80.0
VMEM_LIMIT = 48 * 1024 * 1024
VMEM_LIMIT_BIG = 56 * 1024 * 1024

NT = (((1,), (1,)), ((), ()))
NN = (((1,), (0,)), ((), ()))


def _dot(a, b, dims=NN, passes=1):
    a_hi = a.astype(BF16)
    b_hi = b.astype(BF16)
    out = lax.dot_general(a_hi, b_hi, dims, preferred_element_type=F32)
    if passes == 3:
        a_lo = (a - a_hi.astype(F32)).astype(BF16)
        b_lo = (b - b_hi.astype(F32)).astype(BF16)
        out = out + lax.dot_general(a_hi, b_lo, dims, preferred_element_type=F32)
        out = out + lax.dot_general(a_lo, b_hi, dims, preferred_element_type=F32)
    return out


def _dot_exact01(a, b, lhs01=False):
    x = (b if lhs01 else a).astype(F32)
    out = None
    for _ in range(3):
        t = x.astype(BF16)
        x = x - t.astype(F32)
        p = lax.dot_general(a, t, NN, preferred_element_type=F32) if lhs01 else \
            lax.dot_general(t, b, NN, preferred_element_type=F32)
        out = p if out is None else out + p
    return out


def _cumsum_rows(x, reverse):
    n = x.shape[0]
    row = lax.broadcasted_iota(jnp.int32, x.shape, 0)
    s = 1
    while s < n:
        if reverse:
            x = x + jnp.where(row < n - s, pltpu.roll(x, n - s, axis=0), 0.0)
        else:
            x = x + jnp.where(row >= s, pltpu.roll(x, s, axis=0), 0.0)
        s *= 2
    return x


def _pick_tile(n, candidates):
    for c in candidates:
        if n % c == 0:
            return c
    raise ValueError(f"no tile for {n}")


def _scan_chunk_index(d, p, nctx, nc):
    rev = jnp.where(p < nctx, nctx - 1 - p, nc - 1 - (p - nctx))
    return jnp.where(d == 0, p, rev)


_ACTS = {
    None: lambda y: y,
    "sigmoid": jax.nn.sigmoid,
    "silu": lambda y: y * jax.nn.sigmoid(y),
    "tanh": jnp.tanh,
}


def _mm_kernel(x_ref, w_ref, b_ref, o_ref, *, act, precise):
    if precise:
        y = _dot(x_ref[...], w_ref[...], passes=3)
    else:
        y = jnp.dot(x_ref[...].astype(BF16), w_ref[...], preferred_element_type=F32)
    o_ref[...] = _ACTS[act](y + b_ref[...]).astype(o_ref.dtype)


def mm(x, w, bias=None, act=None, out_dtype=F32, precise=False):
    n, k = x.shape
    m = w.shape[1]
    tm = _pick_tile(n, (512, 256, 128, 64, 32, 16, 8))
    tn = m if m <= 1024 else _pick_tile(m, (1024, 512, 256, 128))
    if not precise:
        w = w.astype(BF16)
    if bias is None:
        bias = jnp.zeros((m,), F32)
    return pl.pallas_call(
        functools.partial(_mm_kernel, act=act, precise=precise),
        grid=(n // tm, m // tn),
        in_specs=[pl.BlockSpec((tm, k), lambda i, j: (i, 0)),
                  pl.BlockSpec((k, tn), lambda i, j: (0, j)),
                  pl.BlockSpec((1, tn), lambda i, j: (0, j))],
        out_specs=pl.BlockSpec((tm, tn), lambda i, j: (i, j)),
        out_shape=jax.ShapeDtypeStruct((n, m), out_dtype),
        compiler_params=pltpu.CompilerParams(vmem_limit_bytes=VMEM_LIMIT),
        name="mm",
    )(x, w, bias.reshape(1, m).astype(F32))


ROW_TILE = 256


def _log1p_exp_neg_abs(x):
    return jnp.log(1.0 + jnp.exp(-jnp.abs(x)))


def _log_sigmoid(y):
    return jnp.minimum(y, 0.0) - _log1p_exp_neg_abs(y)


def _softplus(x):
    return jnp.maximum(x, 0.0) + _log1p_exp_neg_abs(x)


def _logaddexp(a, b):
    return jnp.maximum(a, b) + _log1p_exp_neg_abs(a - b)


def _norm_mod(x, gain, shift):
    return x * lax.rsqrt(jnp.mean(x * x, axis=-1, keepdims=True) + NORM_EPS) * gain + shift


def _seg_map(tpb, nctx_t):
    def seg(i):
        return (i // tpb) * 2 + jnp.where(i % tpb < nctx_t, 0, 1)
    return seg


_EPILOGUES = {
    None: lambda y, aux: y,
    "sigmoid": lambda y, aux: jax.nn.sigmoid(y),
    "silu": lambda y, aux: y * jax.nn.sigmoid(y),
    "logf": lambda y, aux: _logaddexp(aux[0:1, :], aux[1:2, :] + _log_sigmoid(y)),
}


def _sub_tiles(n, most=4):
    return _pick_tile(n // ROW_TILE, tuple(range(most, 0, -1)))


def _nmm_kernel(s_ref, gain_ref, shift_ref, w_ref, b_ref, aux_ref, o_ref, h_scr, *, acts, sub, seg):
    j = pl.program_id(1)

    @pl.when(j == 0)
    def _():
        for k in range(sub):
            rows = pl.ds(k * ROW_TILE, ROW_TILE)
            sk = seg(pl.program_id(0) * sub + k)
            h_scr[rows, :] = _norm_mod(s_ref[rows, :], gain_ref[sk], shift_ref[sk]).astype(BF16)

    y = jnp.dot(h_scr[...], w_ref[...], preferred_element_type=F32) + b_ref[...]
    for act in sorted(set(acts), key=str):
        cols = [jj for jj, a in enumerate(acts) if a == act]
        if len(cols) == len(acts):
            o_ref[...] = _EPILOGUES[act](y, aux_ref[...]).astype(o_ref.dtype)
        else:
            @pl.when(functools.reduce(jnp.logical_or, [j == jj for jj in cols]))
            def _(act=act):
                o_ref[...] = _EPILOGUES[act](y, aux_ref[...]).astype(o_ref.dtype)


def norm_mod_mm(s, gain, shift, w, bias, acts, geom, aux=None, out_dtype=None):
    out_dtype = out_dtype or BF16
    n, k = s.shape
    m = w.shape[1]
    tn = m // len(acts)
    tpb, nctx_t = geom
    seg = _seg_map(tpb, nctx_t)
    sub = _sub_tiles(n)
    tm = sub * ROW_TILE
    if bias is None:
        bias = jnp.zeros((m,), F32)
    if aux is None:
        aux = jnp.zeros((2, m), F32)
    return pl.pallas_call(
        functools.partial(_nmm_kernel, acts=tuple(acts), sub=sub, seg=seg),
        grid=(n // tm, m // tn),
        in_specs=[pl.BlockSpec((tm, k), lambda i, j: (i, 0)),
                  pl.BlockSpec(gain.shape, lambda i, j: (0, 0, 0)),
                  pl.BlockSpec(shift.shape, lambda i, j: (0, 0, 0)),
                  pl.BlockSpec((k, tn), lambda i, j: (0, j)),
                  pl.BlockSpec((1, tn), lambda i, j: (0, j)),
                  pl.BlockSpec((2, tn), lambda i, j: (0, j))],
        out_specs=pl.BlockSpec((tm, tn), lambda i, j: (i, j)),
        out_shape=jax.ShapeDtypeStruct((n, m), out_dtype),
        scratch_shapes=[pltpu.VMEM((tm, k), BF16)],
        compiler_params=pltpu.CompilerParams(
            dimension_semantics=("arbitrary", "arbitrary"), vmem_limit_bytes=VMEM_LIMIT),
        name="norm_mod_mm",
    )(s, gain, shift, w.astype(BF16), bias.reshape(1, m).astype(F32), aux.astype(F32))


def _gated_residual_store(o_ref, s_ref, gm_ref, y, sub, seg):
    for k in range(sub):
        rows = pl.ds(k * ROW_TILE, ROW_TILE)
        gm = gm_ref[seg(pl.program_id(0) * sub + k)]
        o_ref[rows, :] = s_ref[rows, :] + gm * y[k * ROW_TILE:(k + 1) * ROW_TILE]


def _post_kernel(h_ref, g_ref, s_ref, hg_ref, gm_ref, w_ref, o_ref, *, heads, sub, seg):
    x = h_ref[0].astype(F32) + h_ref[1].astype(F32)
    hd = x.shape[1] // heads
    parts = []
    for h in range(heads):
        xh = x[:, h * hd:(h + 1) * hd]
        parts.append(xh * lax.rsqrt(jnp.mean(xh * xh, axis=-1, keepdims=True) + NORM_EPS))
    y = (jnp.concatenate(parts, axis=1) * hg_ref[...] * g_ref[...].astype(F32)).astype(BF16)
    _gated_residual_store(o_ref, s_ref, gm_ref, jnp.dot(y, w_ref[...], preferred_element_type=F32), sub, seg)


def post_mm_residual(h2, gate_arr, gate_block, s, head_g, gm, w_out, heads, geom):
    n, dm = s.shape
    seg = _seg_map(*geom)
    sub = _sub_tiles(n, most=2)
    tm = sub * ROW_TILE
    return pl.pallas_call(
        functools.partial(_post_kernel, heads=heads, sub=sub, seg=seg),
        grid=(n // tm,),
        in_specs=[pl.BlockSpec((2, tm, dm), lambda i: (0, i, 0)),
                  pl.BlockSpec((tm, dm), lambda i: (i, gate_block)),
                  pl.BlockSpec((tm, dm), lambda i: (i, 0)),
                  pl.BlockSpec((1, dm), lambda i: (0, 0)),
                  pl.BlockSpec(gm.shape, lambda i: (0, 0, 0)),
                  pl.BlockSpec((dm, dm), lambda i: (0, 0))],
        out_specs=pl.BlockSpec((tm, dm), lambda i: (i, 0)),
        out_shape=jax.ShapeDtypeStruct((n, dm), F32),
        compiler_params=pltpu.CompilerParams(dimension_semantics=("arbitrary",), vmem_limit_bytes=VMEM_LIMIT),
        name="post_mm_residual",
    )(h2, gate_arr, s, head_g.reshape(1, dm), gm, w_out.astype(BF16))


CONV_COLS = 512


def _conv_kernel(cur_ref, up_ref, dn_ref, w_ref, sc_ref, o_ref, *, tpb, nctx_t):
    ti = pl.program_id(0) % tpb
    is_ctx = ti < nctx_t
    no_up = jnp.logical_or(is_ctx, ti == nctx_t)
    no_dn = jnp.logical_or(is_ctx, ti == tpb - 1)
    x = cur_ref[...].astype(F32)
    up = jnp.where(no_up, 0.0, up_ref[...].astype(F32))
    dn = jnp.where(no_dn, 0.0, dn_ref[...].astype(F32))
    ext = jnp.concatenate([up, x, dn], axis=0)
    tpos = lax.broadcasted_iota(jnp.int32, (ROW_TILE, 1), 0)
    col = tpos % GRID_W
    left_ok = jnp.where(is_ctx, (tpos > 0).astype(F32), (col > 0).astype(F32))
    right_ok = jnp.where(is_ctx, (tpos < ROW_TILE - 1).astype(F32), (col < GRID_W - 1).astype(F32))
    vert = jnp.where(is_ctx, 0.0, 1.0)
    w = w_ref[...]
    sums = [None, None, None]
    for dr in (-1, 0, 1):
        base = GRID_W * (1 + dr)
        wr = w[3 * (dr + 1):3 * (dr + 2)] * (1.0 if dr == 0 else vert)
        for dc in range(3):
            term = ext[base:base + ROW_TILE] * wr[dc:dc + 1]
            sums[dc] = term if sums[dc] is None else sums[dc] + term
    acc = (sums[1] + left_ok * pltpu.roll(sums[0], 1, axis=0)
           + right_ok * pltpu.roll(sums[2], ROW_TILE - 1, axis=0))
    o_ref[...] = (acc * jax.nn.sigmoid(acc) * sc_ref[...]).astype(o_ref.dtype)


def conv_silu(z, conv_w, scale, width, geom):
    n = z.shape[0]
    tpb, nctx_t = geom
    assert nctx_t == 1 and ROW_TILE % GRID_W == 0
    hb = ROW_TILE // GRID_W
    last = n // GRID_W - 1
    return pl.pallas_call(
        functools.partial(_conv_kernel, tpb=tpb, nctx_t=nctx_t),
        grid=(n // ROW_TILE, width // CONV_COLS),
        in_specs=[pl.BlockSpec((ROW_TILE, CONV_COLS), lambda i, c: (i, c)),
                  pl.BlockSpec((GRID_W, CONV_COLS), lambda i, c: (jnp.maximum(i * hb - 1, 0), c)),
                  pl.BlockSpec((GRID_W, CONV_COLS), lambda i, c: (jnp.minimum((i + 1) * hb, last), c)),
                  pl.BlockSpec((9, CONV_COLS), lambda i, c: (0, c)),
                  pl.BlockSpec((1, CONV_COLS), lambda i, c: (0, c))],
        out_specs=pl.BlockSpec((ROW_TILE, CONV_COLS), lambda i, c: (i, c)),
        out_shape=jax.ShapeDtypeStruct((n, width), BF16),
        compiler_params=pltpu.CompilerParams(
            dimension_semantics=("arbitrary", "arbitrary"), vmem_limit_bytes=VMEM_LIMIT),
        name="conv_silu",
    )(z, z, z, conv_w.reshape(9, width).astype(F32), scale.reshape(1, width).astype(F32))


def _cummax_rows(x, reverse):
    n = x.shape[0]
    row = lax.broadcasted_iota(jnp.int32, x.shape, 0)
    s = 1
    while s < n:
        if reverse:
            x = jnp.maximum(x, jnp.where(row < n - s, pltpu.roll(x, n - s, axis=0), -jnp.inf))
        else:
            x = jnp.maximum(x, jnp.where(row >= s, pltpu.roll(x, s, axis=0), -jnp.inf))
        s *= 2
    return x


def _ml_scan_kernel(q_ref, k_ref, v_ref, gc_ref, gr_ref, o_ref, z_scr, m_scr, *, heads, nb):
    L = q_ref.shape[1]
    assert L == LANES
    d = pl.program_id(0)

    @pl.when(pl.program_id(2) == 0)
    def _():
        z_scr[...] = jnp.zeros_like(z_scr)
        m_scr[...] = jnp.zeros_like(m_scr)

    row = lax.broadcasted_iota(jnp.int32, (L, L), 0)
    col = lax.broadcasted_iota(jnp.int32, (L, L), 1)
    ones_blk = jnp.ones((L, LANES), BF16)

    def body(reverse):
        incl = (col >= row) if reverse else (col <= row)
        incl_t = (row >= col) if reverse else (row <= col)
        last = 0 if reverse else L - 1
        items = [(bb, h) for bb in range(nb) for h in range(heads)]
        idx = range(len(items))

        def sl(h):
            return slice(h * LANES, (h + 1) * LANES)

        qk = [_dot(q_ref[bb, :, sl(h)], k_ref[bb, :, sl(h)], NT) for bb, h in items]
        qz = [_dot(q_ref[bb, :, sl(h)], z_scr[bb, h]) for bb, h in items]
        cols, b_rows = [], []
        for bb in range(nb):
            b_cols = _dot_exact01(incl.astype(BF16), gc_ref[0, bb, :, heads:2 * heads], lhs01=True)
            b_rows.append(_dot_exact01(gr_ref[0, bb, heads:2 * heads, :], incl_t.astype(BF16)))
            cols.append(jnp.concatenate([b_cols, gc_ref[0, bb, :, 0:heads]], axis=1))
        pick = lax.broadcasted_iota(jnp.int32, (2 * heads, 2 * LANES), 0)
        lane2 = lax.broadcasted_iota(jnp.int32, (2 * heads, 2 * LANES), 1)
        stats = []
        for bb, h in items:
            sel = (pick == jnp.where(lane2 < LANES, h, heads + h)).astype(BF16)
            rep = _dot_exact01(cols[bb], sel)
            b_rep, ig_rep = rep[:, :LANES], rep[:, LANES:]
            ig_row = gr_ref[0, bb, h:h + 1, :]
            b_row = b_rows[bb][h:h + 1, :]
            m_prev = m_scr[bb, h:h + 1, :]
            cmax = _cummax_rows(ig_rep - b_rep, reverse)
            dmat = jnp.where(incl, b_rep - (b_row - ig_row), -jnp.inf)
            inter = b_rep + m_prev
            m_t = jnp.maximum(inter, b_rep + cmax)
            b_last = b_rep[last:last + 1, :]
            m_new = jnp.maximum(b_last + m_prev, b_last + cmax[last:last + 1, :])
            w_k = jnp.exp(b_last - b_rep + ig_rep - m_new)
            w_prev = jnp.exp(b_last + m_prev - m_new)
            stats.append((jnp.exp(dmat - m_t), jnp.exp(inter - m_t), jnp.exp(-m_t), w_k, w_prev, m_new))
        kv = []
        for i, (bb, h) in enumerate(items):
            w_k = stats[i][3]
            wv = jnp.concatenate([w_k * v_ref[bb, :, sl(h)].astype(F32), w_k], axis=1)
            kv.append(_dot(k_ref[bb, :, sl(h)].astype(F32).T, wv))
        s = [qk[i] * stats[i][0] for i in idx]
        sv = [_dot(s[i], jnp.concatenate([v_ref[bb, :, sl(h)], ones_blk], axis=1))
              for i, (bb, h) in enumerate(items)]
        for i, (bb, h) in enumerate(items):
            _, w_inter, floor, _, w_prev, m_new = stats[i]
            w2 = jnp.concatenate([w_inter, w_inter], axis=1)
            full = sv[i] + w2 * qz[i]
            den = full[:, LANES:]
            o_ref[0, bb, :, sl(h)] = (full[:, :LANES] / jnp.maximum(jnp.abs(den), floor)).astype(o_ref.dtype)
            z_scr[bb, h] = jnp.concatenate([w_prev, w_prev], axis=1) * z_scr[bb, h] + kv[i]
            m_scr[bb, h:h + 1, :] = m_new

    @pl.when(d == 0)
    def _():
        body(False)

    @pl.when(d == 1)
    def _():
        body(True)


def mlstm_scan(qk, z, gc, gr, dm, bsz, t):
    n = qk.shape[0]
    heads = dm // LANES
    L = ML_CHUNK
    nc, nctx = t // L, CTX_LEN // L
    nb = _pick_tile(bsz, (4, 2, 1))

    def chunk(d, p):
        return _scan_chunk_index(d, p, nctx, nc)

    qk3 = qk.reshape(bsz, t, 2 * dm)
    out = pl.pallas_call(
        functools.partial(_ml_scan_kernel, heads=heads, nb=nb),
        grid=(2, bsz // nb, nc),
        in_specs=[pl.BlockSpec((nb, L, dm), lambda d, b, p: (b, chunk(d, p), 0)),
                  pl.BlockSpec((nb, L, dm), lambda d, b, p: (b, chunk(d, p), 1)),
                  pl.BlockSpec((nb, L, dm), lambda d, b, p: (b, chunk(d, p), 2)),
                  pl.BlockSpec((1, nb, L, 2 * heads), lambda d, b, p: (d, b, chunk(d, p), 0)),
                  pl.BlockSpec((1, nb, 2 * heads, L), lambda d, b, p: (d, b, 0, chunk(d, p)))],
        out_specs=pl.BlockSpec((1, nb, L, dm), lambda d, b, p: (d, b, chunk(d, p), 0)),
        out_shape=jax.ShapeDtypeStruct((2, bsz, t, dm), BF16),
        scratch_shapes=[pltpu.VMEM((nb, heads, LANES, 2 * LANES), F32), pltpu.VMEM((nb, heads, LANES), F32)],
        compiler_params=pltpu.CompilerParams(
            dimension_semantics=("arbitrary", "arbitrary", "arbitrary"), vmem_limit_bytes=VMEM_LIMIT),
        name="mlstm_scan",
    )(qk3, qk3, z.reshape(bsz, t, 4 * dm), gc.reshape(2, bsz, t, 2 * heads), gr)
    return out.reshape(2, n, dm)


RW_STATE_PASSES = 3


def _rw_scan_kernel(lw0_ref, lw1_ref, kd0_ref, kd1_ref, a0_ref, a1_ref, r0_ref, r1_ref, v0_ref, v1_ref,
                    kk0_ref, kk1_ref, of_ref, ob_ref, h_scr, rdp_scr, o0_scr, m_scr, ha_scr, *, nchunk, npair):
    L = RW_CHUNK
    j = pl.program_id(2)

    @pl.when(j == 0)
    def _():
        for ref in (h_scr, rdp_scr, o0_scr, m_scr, ha_scr):
            ref[...] = jnp.zeros_like(ref)

    qls = [slice(q * LANES, (q + 1) * LANES) for q in range(npair)]
    hs = {(q, d): h_scr[q, d] for q in range(npair) for d in range(2)}

    def recurrence_step(k):
        for q in range(npair):
            for d, o_ref in ((0, of_ref), (1, ob_ref)):
                c = k if d == 0 else nchunk - 1 - k
                o_ref[pl.ds(c * L, L), qls[q]] = (_dot(rdp_scr[q, d, c], hs[q, d], passes=RW_STATE_PASSES)
                                                  + o0_scr[q, d, c]).astype(o_ref.dtype)
                hs[q, d] = _dot(m_scr[q, d, c], hs[q, d], passes=RW_STATE_PASSES) + ha_scr[q, d, c]

    pending = list(range(nchunk))

    half = LANES // 2
    row = lax.broadcasted_iota(jnp.int32, (L, LANES), 0)
    col = lax.broadcasted_iota(jnp.int32, (L, LANES), 1) % half
    eye2 = (row == col).astype(F32)
    lane = lax.broadcasted_iota(jnp.int32, (1, LANES), 1)
    m0 = (lane < half).astype(BF16)
    m1 = (lane >= half).astype(BF16)
    r2 = lax.broadcasted_iota(jnp.int32, (LANES, LANES), 0)
    c2 = lax.broadcasted_iota(jnp.int32, (LANES, LANES), 1)
    same_head = (r2 // half) == (c2 // half)

    def stack(x):
        xb = x.astype(BF16)
        return jnp.concatenate([xb * m0, xb * m1], axis=0)

    chains = [(q, d, c) for c in range(nchunk) for q in range(npair) for d in range(2)]
    st = {}
    for q, d, c in chains:
        reverse = d == 1
        sl = pl.ds(c * L, L)
        lw = (lw0_ref, lw1_ref)[d][sl, qls[q]]
        k = (kd0_ref, kd1_ref)[d][sl, qls[q]].astype(F32)
        kk = (kk0_ref, kk1_ref)[d][sl, qls[q]].astype(F32)
        akk = kk * (a0_ref, a1_ref)[d][sl, qls[q]].astype(F32)
        g = _cumsum_rows(lw, reverse)
        ieg = jnp.exp(-g)
        g_last = g[0:1] if reverse else g[L - 1:L]
        dl = jnp.exp(g_last - g)
        st[q, d, c] = dict(kd=kk * jnp.exp(g - lw), rd=(r0_ref, r1_ref)[d][sl, qls[q]].astype(F32) * jnp.exp(g),
                           ai=akk * ieg, ki=k * ieg, ad=akk * dl, kdd=k * dl, eg_last=jnp.exp(g_last),
                           v=(v0_ref, v1_ref)[d][sl, qls[q]].astype(F32))
    recurrence_step(pending.pop(0))
    for q, d, c in chains:
        s = st[q, d, c]
        reverse = d == 1
        incl = (col >= row) if reverse else (col <= row)
        strict = (col > row) if reverse else (col < row)
        x = jnp.concatenate([s["kd"], s["rd"]], axis=0)
        rhs = jnp.concatenate([stack(s["ai"]), stack(s["ki"])], axis=0)
        sc = _dot(x, rhs, NT)
        s["a_ab"] = jnp.where(strict, sc[:L, :LANES], 0.0)
        s["a_ak"] = jnp.where(strict, sc[:L, LANES:], 0.0)
        s["b_ra"] = jnp.where(incl, sc[L:, :LANES], 0.0)
        s["b_rk"] = jnp.where(incl, sc[L:, LANES:], 0.0)
        s["tinv"] = eye2 - s["a_ab"]
        s["pw"] = s["a_ab"]
    span = 2
    while span < L:
        for key in chains:
            s = st[key]
            s["pw"] = _dot(s["pw"], stack(s["pw"]))
        for key in chains:
            s = st[key]
            s["tinv"] = _dot(s["tinv"], stack(eye2 + s["pw"]))
        if pending:
            recurrence_step(pending.pop(0))
        span *= 2
    while pending:
        recurrence_step(pending.pop(0))
    for (q, d), h in hs.items():
        h_scr[q, d] = h
    for key in chains:
        s = st[key]
        s["w"] = -_dot(s["tinv"], stack(s["a_ak"]))
        s["kdp"] = _dot(s["tinv"], stack(s["kd"]))
    for key in chains:
        s = st[key]
        s["vst"] = stack(s["v"])
        s["u0"] = _dot(s["w"], s["vst"])
    for key in chains:
        s = st[key]
        lhs = jnp.concatenate([s["b_ra"], s["b_rk"]], axis=1)
        rhs = jnp.concatenate([stack(s["u0"]), s["vst"]], axis=0)
        o0_scr[key] = _dot(lhs, rhs)
        rdp_scr[key] = s["rd"] - _dot(s["b_ra"], stack(s["kdp"]))
        diag = jnp.where(r2 == c2, s["eg_last"], 0.0)
        m_scr[key] = jnp.where(same_head, diag - _dot(s["ad"].T, s["kdp"]), 0.0)
        at = jnp.concatenate([s["ad"], s["kdd"]], axis=0).T
        ha_scr[key] = jnp.where(same_head, _dot(at, jnp.concatenate([s["u0"], s["v"]], axis=0)), 0.0)


def rwkv_scan(lw, kda, rvkg, dm, bsz, t):
    n = lw.shape[0]
    pairs = dm // LANES
    L = RW_CHUNK
    nchunk = RW_PRE_CHUNKS
    tb = nchunk * L
    nblk, nctx = t // tb, CTX_LEN // tb

    def block(d, b, j):
        return b * nblk + _scan_chunk_index(d, jnp.minimum(j, nblk - 1), nctx, nblk)

    npair = RW_PAIRS_PER_STEP
    width = npair * LANES
    groups = pairs // npair

    def ispec(d, col):
        return pl.BlockSpec((tb, width), lambda b, p, j: (block(d, b, j), col * groups + p))

    def ospec(d):
        return pl.BlockSpec((tb, width), lambda b, p, j: (block(d, b, jnp.maximum(j - 1, 0)), p))

    return pl.pallas_call(
        functools.partial(_rw_scan_kernel, nchunk=nchunk, npair=npair),
        grid=(bsz, groups, nblk + 1),
        in_specs=[ispec(0, 0), ispec(1, 1), ispec(0, 0), ispec(1, 1), ispec(0, 2), ispec(1, 3),
                  ispec(0, 0), ispec(1, 0), ispec(0, 1), ispec(1, 1), ispec(0, 2), ispec(1, 2)],
        out_specs=[ospec(0), ospec(1)],
        out_shape=[jax.ShapeDtypeStruct((n, dm), BF16), jax.ShapeDtypeStruct((n, dm), BF16)],
        scratch_shapes=[pltpu.VMEM((npair, 2, LANES, LANES), F32), pltpu.VMEM((npair, 2, nchunk, L, LANES), F32),
                        pltpu.VMEM((npair, 2, nchunk, L, LANES), F32),
                        pltpu.VMEM((npair, 2, nchunk, LANES, LANES), F32),
                        pltpu.VMEM((npair, 2, nchunk, LANES, LANES), F32)],
        compiler_params=pltpu.CompilerParams(
            dimension_semantics=("arbitrary", "arbitrary", "arbitrary"), vmem_limit_bytes=VMEM_LIMIT),
        name="rwkv_scan",
    )(lw, lw, kda, kda, kda, kda, rvkg, rvkg, rvkg, rvkg, rvkg, rvkg)


HALO_ROWS = 8


def _group_sum(x, width):
    r = lax.broadcasted_iota(jnp.int32, (LANES, LANES), 0) // width
    c = lax.broadcasted_iota(jnp.int32, (LANES, LANES), 1) // width
    ones = (r == c).astype(BF16)
    hi = x.astype(BF16)
    lo = (x - hi.astype(F32)).astype(BF16)
    parts = []
    for j in range(x.shape[1] // LANES):
        sl = slice(j * LANES, (j + 1) * LANES)
        parts.append(jnp.dot(hi[:, sl], ones, preferred_element_type=F32)
                     + jnp.dot(lo[:, sl], ones, preferred_element_type=F32))
    return jnp.concatenate(parts, axis=1)


def _rw_proj_kernel(s_ref, up_ref, dn_ref, gain_ref, shift_ref, mu_ref, wrkv_ref, w1_ref, w2_ref, w0_ref,
                    a1_ref, a2_ref, a0_ref, g1_ref, g2_ref, kk_ref, ka_ref,
                    lw_ref, kda_ref, rvkg_ref, *, tpb, nctx_t):
    tm, dm = s_ref.shape
    ti = pl.program_id(0) % tpb
    is_ctx = ti < nctx_t
    has_up = jnp.logical_not(jnp.logical_or(is_ctx, ti == nctx_t))
    has_dn = jnp.logical_not(jnp.logical_or(is_ctx, ti == tpb - 1))
    gain = gain_ref[0]
    shift = shift_ref[0]
    u = _norm_mod(s_ref[...], gain, shift)
    u_up = jnp.where(has_up, _norm_mod(up_ref[HALO_ROWS - 1:HALO_ROWS, :], gain, shift), 0.0)
    u_dn = jnp.where(has_dn, _norm_mod(dn_ref[0:1, :], gain, shift), 0.0)
    row = lax.broadcasted_iota(jnp.int32, (tm, 1), 0)
    u_m = jnp.where(row == 0, u_up, pltpu.roll(u, 1, axis=0))
    u_p = jnp.where(row == tm - 1, u_dn, pltpu.roll(u, tm - 1, axis=0))
    du = 0.5 * (u_m + u_p) - u
    mu = mu_ref[...]

    def mix(i):
        return (u + du * mu[i:i + 1]).astype(BF16)

    def dot(a, b):
        return jnp.dot(a.astype(BF16), b, preferred_element_type=F32)

    r = dot(mix(0), wrkv_ref[0])
    k = dot(mix(1), wrkv_ref[1])
    v = dot(mix(2), wrkv_ref[2])
    w_pre = dot(jnp.tanh(dot(mix(3), w1_ref[...])), w2_ref[...]) + w0_ref[...]
    lw_ref[...] = -jnp.exp(-_softplus(-w_pre) - 0.5)
    a = jax.nn.sigmoid(dot(dot(mix(4), a1_ref[...]), a2_ref[...]) + a0_ref[...])
    g = dot(jax.nn.sigmoid(dot(mix(5), g1_ref[...])), g2_ref[...])
    kk = k * kk_ref[...]
    kk = kk * lax.rsqrt(jnp.maximum(_group_sum(kk * kk, RW_HEAD_DIM), 1e-24))
    ka = ka_ref[...]
    for d in range(2):
        kda_ref[:, d * dm:(d + 1) * dm] = (k * (1.0 + (a[:, d * dm:(d + 1) * dm] - 1.0) * ka)).astype(kda_ref.dtype)
    kda_ref[:, 2 * dm:] = a.astype(kda_ref.dtype)
    for j, val in enumerate((r, v, kk, g)):
        rvkg_ref[:, j * dm:(j + 1) * dm] = val.astype(rvkg_ref.dtype)


def rwkv_proj(s, gain, shift, geom, mu, w_rkv, w0, w1, w2, a0, a1, a2, g1, g2, k_k, k_a):
    n, dm = s.shape
    tpb, nctx_t = geom
    assert nctx_t == 1
    seg = _seg_map(tpb, nctx_t)
    hb = ROW_TILE // HALO_ROWS
    last = n // HALO_ROWS - 1
    lora = w1.shape[2]

    def blockdiag(w):
        z = jnp.zeros_like(w[0])
        return jnp.concatenate([jnp.concatenate([w[0], z], axis=1), jnp.concatenate([z, w[1]], axis=1)], axis=0)

    consts = [jnp.pad(mu, ((0, HALO_ROWS - mu.shape[0]), (0, 0))), w_rkv.astype(BF16),
              jnp.concatenate([w1[0], w1[1]], axis=1).astype(BF16), blockdiag(w2).astype(BF16),
              jnp.concatenate([w0[0], w0[1]])[None],
              jnp.concatenate([a1[0], a1[1]], axis=1).astype(BF16), blockdiag(a2).astype(BF16),
              jnp.concatenate([a0[0], a0[1]])[None],
              g1.astype(BF16), g2.astype(BF16), k_k[None], k_a[None]]

    def const_spec(x):
        nd = x.ndim
        return pl.BlockSpec(x.shape, lambda i: (0,) * nd)

    return pl.pallas_call(
        functools.partial(_rw_proj_kernel, tpb=tpb, nctx_t=nctx_t),
        grid=(n // ROW_TILE,),
        in_specs=[pl.BlockSpec((ROW_TILE, dm), lambda i: (i, 0)),
                  pl.BlockSpec((HALO_ROWS, dm), lambda i: (jnp.maximum(i * hb - 1, 0), 0)),
                  pl.BlockSpec((HALO_ROWS, dm), lambda i: (jnp.minimum((i + 1) * hb, last), 0)),
                  pl.BlockSpec((1, 1, dm), lambda i: (seg(i), 0, 0)),
                  pl.BlockSpec((1, 1, dm), lambda i: (seg(i), 0, 0))] + [const_spec(x) for x in consts],
        out_specs=[pl.BlockSpec((ROW_TILE, 2 * dm), lambda i: (i, 0)),
                   pl.BlockSpec((ROW_TILE, 4 * dm), lambda i: (i, 0)),
                   pl.BlockSpec((ROW_TILE, 4 * dm), lambda i: (i, 0))],
        out_shape=[jax.ShapeDtypeStruct((n, 2 * dm), F32), jax.ShapeDtypeStruct((n, 4 * dm), BF16),
                   jax.ShapeDtypeStruct((n, 4 * dm), BF16)],
        compiler_params=pltpu.CompilerParams(dimension_semantics=("arbitrary",), vmem_limit_bytes=VMEM_LIMIT_BIG),
        name="rwkv_proj",
    )(s, s, s, gain, shift, *consts)


def _rw_post_kernel(of_ref, ob_ref, r_ref, v_ref, g_ref, k0_ref, k1_ref, s_ref, lnw_ref, lnb_ref, rk_ref, gm_ref,
                    w_ref, out_ref, *, sub, seg):
    o = of_ref[...].astype(F32) + ob_ref[...].astype(F32)
    inv = 1.0 / RW_HEAD_DIM
    mean = _group_sum(o, RW_HEAD_DIM) * inv
    oc = o - mean
    var = _group_sum(oc * oc, RW_HEAD_DIM) * inv
    xn = oc * lax.rsqrt(var + RW_GN_EPS) * lnw_ref[...] + lnb_ref[...]
    r = r_ref[...].astype(F32)
    ksum = k0_ref[...].astype(F32) + k1_ref[...].astype(F32)
    bonus = _group_sum(r * ksum * rk_ref[...], RW_HEAD_DIM) * v_ref[...].astype(F32)
    y = ((xn + bonus) * g_ref[...].astype(F32)).astype(BF16)
    _gated_residual_store(out_ref, s_ref, gm_ref, jnp.dot(y, w_ref[...], preferred_element_type=F32), sub, seg)


def rwkv_post(o_f, o_b, rvkg, kda, s, ln_w, ln_b, r_k, gm, w_out, geom):
    n, dm = s.shape
    seg = _seg_map(*geom)
    sub = _sub_tiles(n, most=2)
    tm = sub * ROW_TILE

    def col(block):
        return pl.BlockSpec((tm, dm), lambda i: (i, block))

    vec = pl.BlockSpec((1, dm), lambda i: (0, 0))
    return pl.pallas_call(
        functools.partial(_rw_post_kernel, sub=sub, seg=seg),
        grid=(n // tm,),
        in_specs=[col(0), col(0), col(0), col(1), col(3), col(0), col(1),
                  col(0), vec, vec, vec, pl.BlockSpec(gm.shape, lambda i: (0, 0, 0)),
                  pl.BlockSpec((dm, dm), lambda i: (0, 0))],
        out_specs=pl.BlockSpec((tm, dm), lambda i: (i, 0)),
        out_shape=jax.ShapeDtypeStruct((n, dm), F32),
        compiler_params=pltpu.CompilerParams(dimension_semantics=("arbitrary",), vmem_limit_bytes=VMEM_LIMIT),
        name="rwkv_post",
    )(o_f, o_b, rvkg, rvkg, rvkg, kda, kda, s, ln_w[None], ln_b[None], r_k[None], gm, w_out.astype(BF16))


def _hg_scan_kernel(q_ref, v_ref, lf_ref, o_ref, st_scr, *, heads):
    C = q_ref.shape[0]
    d = pl.program_id(0)
    nsub = C // HG_SUB

    @pl.when(pl.program_id(2) == 0)
    def _():
        st_scr[...] = jnp.zeros_like(st_scr)

    def body(reverse):
        last = 0 if reverse else C - 1
        hs = range(heads)
        sls = [slice(h * LANES, (h + 1) * LANES) for h in hs]
        g = [lf_ref[:, sls[h]] for h in hs]
        b = [_cumsum_rows(g[h], reverse) for h in hs]
        k = [-jnp.tanh(0.5 * g[h]) * (jnp.exp(g[h]) + 1.0) for h in hs]
        o_inter = [_dot(q_ref[:, sls[h]].astype(F32) * jnp.exp(b[h]), st_scr[h], NT) for h in hs]
        parts = [[None] * nsub for _ in hs]
        for i in range(nsub):
            r0 = i * HG_SUB
            lo, hi = (r0, C) if reverse else (0, r0 + HG_SUB)
            first = r0 + HG_SUB - 1 if reverse else r0
            row = lax.broadcasted_iota(jnp.int32, (HG_SUB, hi - lo), 0) + r0
            col = lax.broadcasted_iota(jnp.int32, (HG_SUB, hi - lo), 1) + lo
            keep = (col >= row) if reverse else (col <= row)
            att = []
            for h in hs:
                rho = b[h][first:first + 1, :] - g[h][first:first + 1, :]
                qi = q_ref[r0:r0 + HG_SUB, sls[h]].astype(F32) * jnp.exp(b[h][r0:r0 + HG_SUB] - rho)
                ki = k[h][lo:hi] * jnp.exp(jnp.minimum(rho - b[h][lo:hi], HG_EXP_CLAMP))
                att.append(jnp.where(keep, _dot(qi, ki, NT), 0.0))
            for h in hs:
                parts[h][i] = _dot(att[h], v_ref[lo:hi, sls[h]])
        for h in hs:
            o_ref[0, :, sls[h]] = (o_inter[h] + jnp.concatenate(parts[h], axis=0)).astype(o_ref.dtype)
        upd = []
        for h in hs:
            b_last = b[h][last:last + 1, :]
            upd.append((jnp.exp(b_last),
                        _dot(v_ref[:, sls[h]].astype(F32).T, k[h] * jnp.exp(b_last - b[h]))))
        for h in hs:
            st_scr[h] = st_scr[h] * upd[h][0] + upd[h][1]

    @pl.when(d == 0)
    def _():
        body(False)

    @pl.when(d == 1)
    def _():
        body(True)


def hgrn_scan(z, logf, dm, bsz, t):
    n = z.shape[0]
    heads = dm // LANES
    C = HG_CHUNK
    nc, nctx = t // C, CTX_LEN // C

    def row(d, b, p):
        return b * nc + _scan_chunk_index(d, p, nctx, nc)

    return pl.pallas_call(
        functools.partial(_hg_scan_kernel, heads=heads),
        grid=(2, bsz, nc),
        in_specs=[pl.BlockSpec((C, dm), lambda d, b, p: (row(d, b, p), 0)),
                  pl.BlockSpec((C, dm), lambda d, b, p: (row(d, b, p), 1)),
                  pl.BlockSpec((C, dm), lambda d, b, p: (row(d, b, p), d))],
        out_specs=pl.BlockSpec((1, C, dm), lambda d, b, p: (d, row(d, b, p), 0)),
        out_shape=jax.ShapeDtypeStruct((2, n, dm), BF16),
        scratch_shapes=[pltpu.VMEM((heads, LANES, LANES), F32)],
        compiler_params=pltpu.CompilerParams(
            dimension_semantics=("arbitrary", "arbitrary", "arbitrary"), vmem_limit_bytes=VMEM_LIMIT),
        name="hgrn_scan",
    )(z, z, logf)


def _first_argmax(vals):
    best, idx = vals[0], jnp.zeros(vals[0].shape, jnp.int32)
    for i in range(1, len(vals)):
        better = vals[i] > best
        best = jnp.where(better, vals[i], best)
        idx = jnp.where(better, i, idx)
    return best, idx


def _router_kernel(s_ref, gain_ref, shift_ref, wt_ref, b_ref, e_ref, g_ref, rank_ref, cnt_ref, carry_scr, *,
                   n_groups, top_k):
    n_experts = wt_ref.shape[0]
    per = n_experts // n_groups
    h = _norm_mod(s_ref[...], gain_ref[0], shift_ref[0])
    aff = jax.nn.sigmoid(_dot(wt_ref[...], h, NT, passes=3))
    sel = aff + b_ref[...]
    a = [aff[e:e + 1, :] for e in range(n_experts)]
    s = [sel[e:e + 1, :] for e in range(n_experts)]
    neg = jnp.full_like(s[0], -jnp.inf)
    scores = []
    for g in range(n_groups):
        grp = s[g * per:(g + 1) * per]
        m1, i1 = _first_argmax(grp)
        m2, _ = _first_argmax([jnp.where(i1 == j, neg, grp[j]) for j in range(per)])
        scores.append(m1 + m2)
    _, best = _first_argmax(scores)

    def in_best(rows):
        out = []
        for j in range(per):
            x = rows[j]
            for g in range(1, n_groups):
                x = jnp.where(best == g, rows[g * per + j], x)
            out.append(x)
        return out

    sb, ab = in_best(s), in_best(a)
    picked, chosen = [], []
    cand = sb
    for _ in range(top_k):
        _, i = _first_argmax(cand)
        c = ab[0]
        for j in range(1, per):
            c = jnp.where(i == j, ab[j], c)
        picked.append(i)
        chosen.append(c)
        cand = [jnp.where(i == j, neg, cand[j]) for j in range(per)]
    total = functools.reduce(jnp.add, chosen)
    experts = [best * per + picked[kk_] for kk_ in range(top_k)]
    for kk_ in range(top_k):
        e_ref[kk_:kk_ + 1, :] = experts[kk_]
        g_ref[kk_:kk_ + 1, :] = chosen[kk_] / total

    @pl.when(pl.program_id(0) == 0)
    def _():
        carry_scr[...] = jnp.zeros_like(carry_scr)

    tm = s_ref.shape[0]
    sub = lax.broadcasted_iota(jnp.int32, (n_experts, tm), 0)
    onehots = [(sub == ex).astype(F32) for ex in experts]
    tot = functools.reduce(jnp.add, onehots)
    ri = lax.broadcasted_iota(jnp.int32, (tm, tm), 0)
    ci = lax.broadcasted_iota(jnp.int32, (tm, tm), 1)
    earlier = (ri < ci).astype(BF16)
    before = jnp.dot(tot.astype(BF16), earlier, preferred_element_type=F32) + carry_scr[:, 0:1]
    seen = before
    for kk_ in range(top_k):
        rank_ref[kk_:kk_ + 1, :] = jnp.sum(onehots[kk_] * seen, axis=0, keepdims=True).astype(jnp.int32)
        seen = seen + onehots[kk_]
    carry = carry_scr[...] + jnp.sum(tot, axis=1, keepdims=True)
    carry_scr[...] = carry
    cnt_ref[...] = carry.astype(jnp.int32)


def norm_route(s, gain, shift, geom, router_w, router_b):
    n, k = s.shape
    n_experts = router_w.shape[1]
    tm = ROW_TILE
    seg = _seg_map(*geom)
    return pl.pallas_call(
        functools.partial(_router_kernel, n_groups=N_GROUPS, top_k=TOP_K),
        grid=(n // tm,),
        in_specs=[pl.BlockSpec((tm, k), lambda i: (i, 0)),
                  pl.BlockSpec((1, 1, k), lambda i: (seg(i), 0, 0)),
                  pl.BlockSpec((1, 1, k), lambda i: (seg(i), 0, 0)),
                  pl.BlockSpec((n_experts, k), lambda i: (0, 0)),
                  pl.BlockSpec((n_experts, 1), lambda i: (0, 0))],
        out_specs=[pl.BlockSpec((TOP_K, tm), lambda i: (0, i)), pl.BlockSpec((TOP_K, tm), lambda i: (0, i)),
                   pl.BlockSpec((TOP_K, tm), lambda i: (0, i)), pl.BlockSpec((n_experts, LANES), lambda i: (0, 0))],
        out_shape=[jax.ShapeDtypeStruct((TOP_K, n), jnp.int32), jax.ShapeDtypeStruct((TOP_K, n), F32),
                   jax.ShapeDtypeStruct((TOP_K, n), jnp.int32), jax.ShapeDtypeStruct((n_experts, LANES), jnp.int32)],
        scratch_shapes=[pltpu.VMEM((n_experts, LANES), F32)],
        compiler_params=pltpu.CompilerParams(dimension_semantics=("arbitrary",), vmem_limit_bytes=VMEM_LIMIT),
        name="norm_route",
    )(s, gain, shift, router_w.T, router_b.reshape(n_experts, 1).astype(F32))


def _final_norm_kernel(s_ref, g_ref, o_ref):
    x = s_ref[...]
    o_ref[0] = x * lax.rsqrt(jnp.mean(x * x, axis=-1, keepdims=True) + NORM_EPS) * g_ref[...]


def final_norm(s, g, bsz, t, geom):
    dm = s.shape[1]
    tpb, nctx_t = geom
    return pl.pallas_call(
        _final_norm_kernel,
        grid=(bsz, tpb - nctx_t),
        in_specs=[pl.BlockSpec((ROW_TILE, dm), lambda b, i: (b * tpb + nctx_t + i, 0)),
                  pl.BlockSpec((1, dm), lambda b, i: (0, 0))],
        out_specs=pl.BlockSpec((1, ROW_TILE, dm), lambda b, i: (b, i, 0)),
        out_shape=jax.ShapeDtypeStruct((bsz, t - CTX_LEN, dm), F32),
        name="final_norm",
    )(s, g[None])


DMA_UNROLL = 8


def _row_copy_waits(src_row, dst_row, sem, count):
    def body(_, carry):
        pltpu.make_async_copy(src_row, dst_row, sem).wait()
        return carry
    lax.fori_loop(0, count, body, 0)


def _scatter_kernel(dest_ref, meta_ref, s_ref, gain_ref, shift_ref, xb_ref, hbuf, zrow, sems, zsem, *, n_experts):
    i = pl.program_id(0)
    last = pl.num_programs(0) - 1
    slot = i % 2
    per_tile = TOP_K * ROW_TILE

    def wait_tile(sl):
        for _ in range(TOP_K):
            pltpu.make_async_copy(hbuf.at[sl], xb_ref.at[pl.ds(0, ROW_TILE), :], sems.at[sl]).wait()

    @pl.when(i >= 2)
    def _():
        wait_tile(slot)

    hbuf[slot] = _norm_mod(s_ref[...], gain_ref[0], shift_ref[0])

    def issue(r8, carry):
        for u in range(DMA_UNROLL):
            r = r8 * DMA_UNROLL + u
            for k in range(TOP_K):
                d = dest_ref[0, 0, TOP_K * r + k]
                pltpu.make_async_copy(hbuf.at[slot, pl.ds(r, 1), :], xb_ref.at[pl.ds(d, 1), :],
                                      sems.at[slot]).start(priority=(TOP_K * u + k) % 2)
        return carry
    lax.fori_loop(0, ROW_TILE // DMA_UNROLL, issue, 0)

    @pl.when(i == last)
    def _():
        @pl.when(last >= 1)
        def _():
            wait_tile(1 - slot)
        wait_tile(slot)
        zrow[...] = jnp.zeros_like(zrow)
        for e in range(n_experts):
            lo = meta_ref[2, e] + meta_ref[0, e]
            hi = meta_ref[2, e] + meta_ref[1, e]

            def pad_start(q, carry):
                pltpu.make_async_copy(zrow.at[pl.ds(0, 1), :], xb_ref.at[pl.ds(q, 1), :], zsem.at[0]).start()
                return carry
            lax.fori_loop(lo, hi, pad_start, 0)
            _row_copy_waits(zrow.at[pl.ds(0, 1), :], xb_ref.at[pl.ds(0, 1), :], zsem.at[0], hi - lo)
        end = meta_ref[2, n_experts - 1] + meta_ref[1, n_experts - 1]

        def tail_start(q, carry):
            pltpu.make_async_copy(zrow.at[pl.ds(0, 1), :], xb_ref.at[pl.ds(q, 1), :], zsem.at[0]).start()
            return carry
        lax.fori_loop(end, xb_ref.shape[0], tail_start, 0)
        _row_copy_waits(zrow.at[pl.ds(0, 1), :], xb_ref.at[pl.ds(0, 1), :], zsem.at[0], xb_ref.shape[0] - end)


def moe_scatter(s, gain, shift, geom, dest, meta, n_slots):
    n, dm = s.shape
    seg = _seg_map(*geom)
    nt = n // ROW_TILE
    return pl.pallas_call(
        functools.partial(_scatter_kernel, n_experts=meta.shape[1]),
        grid=(nt,),
        in_specs=[pl.BlockSpec((1, 1, TOP_K * ROW_TILE), lambda i: (i, 0, 0), memory_space=pltpu.SMEM),
                  pl.BlockSpec(memory_space=pltpu.SMEM),
                  pl.BlockSpec((ROW_TILE, dm), lambda i: (i, 0)),
                  pl.BlockSpec((1, 1, dm), lambda i: (seg(i), 0, 0)),
                  pl.BlockSpec((1, 1, dm), lambda i: (seg(i), 0, 0))],
        out_specs=pl.BlockSpec(memory_space=pl.ANY),
        out_shape=jax.ShapeDtypeStruct((n_slots, dm), F32),
        scratch_shapes=[pltpu.VMEM((2, ROW_TILE, dm), F32), pltpu.VMEM((8, dm), F32),
                        pltpu.SemaphoreType.DMA((2,)), pltpu.SemaphoreType.DMA((1,))],
        compiler_params=pltpu.CompilerParams(dimension_semantics=("arbitrary",), vmem_limit_bytes=VMEM_LIMIT),
        name="moe_scatter",
    )(dest.reshape(nt, 1, TOP_K * ROW_TILE), meta, s, gain, shift)


def _gather_combine_kernel(dcur_ref, dnxt_ref, g_ref, s_ref, gm_ref, yb_ref, o_ref, ybuf, sems):
    i = pl.program_id(0)
    nsteps = pl.num_programs(0)
    slot = i % 2

    def start_tile(dref, sl):
        def issue(r8, carry):
            for u in range(DMA_UNROLL):
                r = r8 * DMA_UNROLL + u
                for k in range(TOP_K):
                    d = dref[0, 0, TOP_K * r + k]
                    pltpu.make_async_copy(yb_ref.at[pl.ds(d, 1), :], ybuf.at[sl, k, pl.ds(r, 1), :],
                                          sems.at[sl]).start(priority=(TOP_K * u + k) % 2)
            return carry
        lax.fori_loop(0, ROW_TILE // DMA_UNROLL, issue, 0)

    @pl.when(i == 0)
    def _():
        start_tile(dcur_ref, 0)

    @pl.when(i + 1 < nsteps)
    def _():
        start_tile(dnxt_ref, 1 - slot)

    for k in range(TOP_K):
        pltpu.make_async_copy(yb_ref.at[pl.ds(0, ROW_TILE), :], ybuf.at[slot, k], sems.at[slot]).wait()
    g = g_ref[...]
    y = sum(ybuf[slot, k] * g[:, k:k + 1] for k in range(TOP_K))
    o_ref[...] = s_ref[...] + gm_ref[0] * y


def moe_gather_combine(yb, dest, gate, s, gm, geom):
    n, dm = s.shape
    seg = _seg_map(*geom)
    nt = n // ROW_TILE
    d3 = dest.reshape(nt, 1, TOP_K * ROW_TILE)
    row = pl.BlockSpec((ROW_TILE, dm), lambda i: (i, 0))
    return pl.pallas_call(
        _gather_combine_kernel,
        grid=(nt,),
        in_specs=[pl.BlockSpec((1, 1, TOP_K * ROW_TILE), lambda i: (i, 0, 0), memory_space=pltpu.SMEM),
                  pl.BlockSpec((1, 1, TOP_K * ROW_TILE), lambda i: (jnp.minimum(i + 1, nt - 1), 0, 0),
                               memory_space=pltpu.SMEM),
                  pl.BlockSpec((ROW_TILE, TOP_K), lambda i: (i, 0)), row,
                  pl.BlockSpec((1, 1, dm), lambda i: (seg(i), 0, 0)),
                  pl.BlockSpec(memory_space=pl.ANY)],
        out_specs=row,
        out_shape=jax.ShapeDtypeStruct((n, dm), F32),
        scratch_shapes=[pltpu.VMEM((2, TOP_K, ROW_TILE, dm), F32), pltpu.SemaphoreType.DMA((2,))],
        compiler_params=pltpu.CompilerParams(dimension_semantics=("arbitrary",), vmem_limit_bytes=VMEM_LIMIT),
        name="moe_gather_combine",
    )(d3, d3, gate, s, gm, yb)


def _ffn_kernel(be_ref, x_ref, w1_ref, w3_ref, w2_ref, o_ref):
    del be_ref
    x = x_ref[...].astype(BF16)
    a = jnp.dot(x, w1_ref[0], preferred_element_type=F32)
    b = jnp.dot(x, w3_ref[0], preferred_element_type=F32)
    hid = (a * jax.nn.sigmoid(a) * b).astype(BF16)
    o_ref[...] = jnp.dot(hid, w2_ref[0], preferred_element_type=F32).astype(o_ref.dtype)


def expert_ffn(xb, block_expert, w1, w3, w2):
    nrows, dm = xb.shape
    f = w1.shape[2]
    nb = nrows // MOE_BLOCK
    return pl.pallas_call(
        _ffn_kernel,
        grid_spec=pltpu.PrefetchScalarGridSpec(
            num_scalar_prefetch=1,
            grid=(nb,),
            in_specs=[pl.BlockSpec((MOE_BLOCK, dm), lambda i, be: (i, 0)),
                      pl.BlockSpec((1, dm, f), lambda i, be: (be[i], 0, 0)),
                      pl.BlockSpec((1, dm, f), lambda i, be: (be[i], 0, 0)),
                      pl.BlockSpec((1, f, dm), lambda i, be: (be[i], 0, 0))],
            out_specs=pl.BlockSpec((MOE_BLOCK, dm), lambda i, be: (i, 0))),
        out_shape=jax.ShapeDtypeStruct((nrows, dm), F32),
        compiler_params=pltpu.CompilerParams(
            dimension_semantics=("arbitrary",), vmem_limit_bytes=VMEM_LIMIT),
        name="expert_ffn",
    )(block_expert, xb, w1.astype(BF16), w3.astype(BF16), w2.astype(BF16))


def _mlstm_layer(s, gain, shift, gate_mod, geom, bsz, t, w_in, w_gate, b_gate, conv, head_g, w_out):
    n, dm = s.shape
    heads = ML_HEADS
    z = norm_mod_mm(s, gain, shift, w_in, None, (None, None, None, "sigmoid"), geom)
    scale = jnp.concatenate([jnp.ones((dm,), F32), jnp.full((dm,), (dm // heads) ** -0.5, F32)])
    qk = conv_silu(z, conv, scale, 2 * dm, geom)
    ng = 4 * heads
    wg = jnp.pad(jnp.concatenate([w_gate[0], w_gate[1]], axis=1), ((0, 0), (0, LANES - ng)))
    bg = jnp.pad(jnp.concatenate([b_gate[0], b_gate[1]]), (0, LANES - ng))
    gates = norm_mod_mm(s, gain, shift, wg, bg, (None,), geom, out_dtype=F32)[:, :ng]
    gates = gates.reshape(bsz, t, 2, 2 * heads)
    gates = jnp.concatenate([gates[..., :heads], jax.nn.log_sigmoid(gates[..., heads:])], axis=-1)
    gc = jnp.moveaxis(gates, 2, 0).reshape(2, n, 2 * heads)
    gr = jnp.transpose(gates, (2, 0, 3, 1))
    h = mlstm_scan(qk, z, gc, gr, dm, bsz, t)
    return post_mm_residual(h, z, 3, s, head_g, gate_mod, w_out, heads, geom)


def _rwkv7_layer(s, gain, shift, gate_mod, geom, bsz, t, mu, w_rkv, w0, w1, w2, a0, a1, a2, g1, g2,
                 k_k, k_a, r_k, ln_w, ln_b, w_out):
    dm = s.shape[1]
    lw, kda, rvkg = rwkv_proj(s, gain, shift, geom, mu, w_rkv, w0, w1, w2, a0, a1, a2, g1, g2, k_k, k_a)
    o_f, o_b = rwkv_scan(lw, kda, rvkg, dm, bsz, t)
    return rwkv_post(o_f, o_b, rvkg, kda, s, ln_w, ln_b, r_k, gate_mod, w_out, geom)


def _hgrn2_layer(s, gain, shift, gate_mod, geom, bsz, t, layer_idx, w_in, w_f, b_f, lb_logits, head_g, w_out):
    dm = s.shape[1]
    z = norm_mod_mm(s, gain, shift, w_in, None, ("silu", None, "silu"), geom)
    p = jax.nn.softmax(lb_logits, axis=0)
    lb = jnp.cumsum(p, axis=0)[layer_idx] - p[0]
    aux = jnp.tile(jnp.stack([jnp.log(lb), jnp.log1p(-lb)]), (1, 2))
    log_f = norm_mod_mm(s, gain, shift, jnp.concatenate([w_f[0], w_f[1]], axis=1),
                        jnp.concatenate([b_f[0], b_f[1]]), ("logf", "logf"), geom, aux=aux, out_dtype=F32)
    o = hgrn_scan(z, log_f, dm, bsz, t)
    return post_mm_residual(o, z, 2, s, head_g, gate_mod, w_out, dm // HG_EXPAND, geom)


def _moe_layer(s, gain, shift, gate_mod, geom, router_w, router_b, w1, w3, w2):
    n_tok, d = s.shape
    n_experts = w1.shape[0]
    n_assign = n_tok * TOP_K
    e, g, rank, cnt = norm_route(s, gain, shift, geom, router_w, router_b)
    flat_e = e.T.reshape(n_assign)
    rank = rank.T.reshape(n_assign)
    counts = cnt[:, 0]
    padded = (counts + MOE_BLOCK - 1) // MOE_BLOCK * MOE_BLOCK
    end_pad = jnp.cumsum(padded)
    start_pad = end_pad - padded
    onehot = flat_e[:, None] == jnp.arange(n_experts, dtype=jnp.int32)[None, :]
    dest = jnp.sum(jnp.where(onehot, start_pad[None, :], 0), axis=1) + rank
    n_blocks = -(-n_assign // MOE_BLOCK) + n_experts
    block_start = jnp.arange(n_blocks, dtype=jnp.int32) * MOE_BLOCK
    block_expert = jnp.minimum(jnp.sum(end_pad[None, :] <= block_start[:, None], axis=1), n_experts - 1)
    meta = jnp.stack([counts, padded, start_pad]).astype(jnp.int32)
    xb = moe_scatter(s, gain, shift, geom, dest.astype(jnp.int32), meta, n_blocks * MOE_BLOCK)
    yb = expert_ffn(xb, block_expert.astype(jnp.int32), w1, w3, w2)
    return moe_gather_combine(yb, dest.astype(jnp.int32), g.T, s, gate_mod, geom)


def kernel(x, c, ctx, c_ctx, ada_w, ada_b, norm_mix, norm_ffn, norm_out, ml_w_in, ml_w_gate, ml_b_gate, ml_conv, ml_head_g, ml_w_out, rw_mu, rw_w_rkv, rw_w0, rw_w1, rw_w2, rw_a0, rw_a1, rw_a2, rw_g1, rw_g2, rw_k_k, rw_k_a, rw_r_k, rw_ln_w, rw_ln_b, rw_w_out, hg_w_in, hg_w_f, hg_b_f, hg_lb_logits, hg_head_g, hg_w_out, router_w, router_b, ex_w1, ex_w3, ex_w2):
    depth = ada_w.shape[0]
    bsz = x.shape[0]
    cond = jax.nn.silu(jnp.concatenate([c, c_ctx[None]], axis=0))
    cond = jnp.pad(cond, ((0, -(bsz + 1) % 8), (0, 0)))
    dm = x.shape[2]
    t = CTX_LEN + x.shape[1]
    n = bsz * t
    geom = (t // ROW_TILE, CTX_LEN // ROW_TILE)
    s = jnp.concatenate([ctx, x], axis=1).reshape(n, dm)
    for i in range(depth):
        mod = mm(cond, ada_w[i], bias=ada_b[i])
        mod_x = jnp.split(mod[:bsz, None, :], 6, axis=-1)
        mod_c = jnp.split(mod[bsz], 6, axis=-1)

        def table(idx):
            return jnp.stack([jnp.broadcast_to(mod_c[idx], (bsz, dm)), mod_x[idx][:, 0]], axis=1).reshape(2 * bsz, 1, dm)

        kind, j = i % N_MIXERS, i // N_MIXERS
        if kind == 2:
            s = _hgrn2_layer(s, norm_mix[i] * (1 + table(1)), table(0), table(2), geom, bsz, t, i,
                             hg_w_in[j], hg_w_f[j], hg_b_f[j], hg_lb_logits, hg_head_g[j], hg_w_out[j])
        elif kind == 0:
            s = _mlstm_layer(s, norm_mix[i] * (1 + table(1)), table(0), table(2), geom, bsz, t,
                             ml_w_in[j], ml_w_gate[j], ml_b_gate[j], ml_conv[j], ml_head_g[j], ml_w_out[j])
        else:
            s = _rwkv7_layer(s, norm_mix[i] * (1 + table(1)), table(0), table(2), geom, bsz, t,
                             rw_mu[j], rw_w_rkv[j], rw_w0[j], rw_w1[j], rw_w2[j], rw_a0[j],
                             rw_a1[j], rw_a2[j], rw_g1[j], rw_g2[j], rw_k_k[j], rw_k_a[j],
                             rw_r_k[j], rw_ln_w[j], rw_ln_b[j], rw_w_out[j])
        s = _moe_layer(s, norm_ffn[i] * (1 + table(4)), table(3), table(5), geom, router_w, router_b,
                       ex_w1[i], ex_w3[i], ex_w2[i])
    return final_norm(s, norm_out, bsz, t, geom)
```
